```python
import jax, jax.numpy as jnp
from jax import lax
import numpy as np

D_MODEL = 1024
BATCH = 4
SEQ = 4096
DEPTH = 1
DEC_BATCH = 32
DEC_SEQ = 8
PAST_LEN = 8192
PAGE_SIZE = 128

D_MIX = D_MODEL
D_CONV = D_MIX // 2
N_CONV_GROUPS = 8
CONV_W = 3
N_HEADS = 8
HEAD_DIM = 64
D_ATT = N_HEADS * HEAD_DIM
D_IN = 3 * D_CONV + 3 * D_ATT
D_FF = 2816
MOBA_BLOCK = 256
MOBA_TOPK = 3
Q_CHUNK = 64
EPS = 1e-5
NEG_INF = -1e30

kernel_name = "hymba_conv_moba_macaron_step"


def rms_norm(x, g):
    xf = x.astype(jnp.float32)
    y = xf * lax.rsqrt(jnp.mean(xf * xf, axis=-1, keepdims=True) + EPS)
    return (y * g.astype(jnp.float32)).astype(x.dtype)


def group_rms_norm(x, g, n_groups):
    xf = x.astype(jnp.float32).reshape(*x.shape[:-1], n_groups, -1)
    y = xf * lax.rsqrt(jnp.mean(xf * xf, axis=-1, keepdims=True) + EPS)
    return (y.reshape(x.shape) * g.astype(jnp.float32)).astype(x.dtype)


def swiglu_ffn(x, w_gu, w_down):
    g, u = jnp.split(x @ w_gu, 2, axis=-1)
    return (jax.nn.silu(g) * u) @ w_down


def alibi_slopes():
    return 2.0 ** (-(8.0 / N_HEADS) * jnp.arange(1, N_HEADS + 1, dtype=jnp.float32))


def moba_attention(q, k, v):
    Bn, T = q.shape[0], q.shape[1]
    L = k.shape[1]
    nb = -(-L // MOBA_BLOCK)
    pad = nb * MOBA_BLOCK - L
    kp = jnp.pad(k, ((0, 0), (0, pad), (0, 0), (0, 0)))
    vp = jnp.pad(v, ((0, 0), (0, pad), (0, 0), (0, 0)))
    kb = kp.reshape(Bn, nb, MOBA_BLOCK, N_HEADS, HEAD_DIM).transpose(0, 3, 1, 2, 4)
    vb = vp.reshape(Bn, nb, MOBA_BLOCK, N_HEADS, HEAD_DIM).transpose(0, 3, 1, 2, 4)
    means = jnp.mean(kb.astype(jnp.float32), axis=3)
    top = min(MOBA_TOPK, nb)
    slopes = alibi_slopes()
    scale = HEAD_DIM ** -0.5
    bi = jnp.arange(Bn)[:, None, None, None]
    hi = jnp.arange(N_HEADS)[None, :, None, None]
    blk_off = jnp.arange(MOBA_BLOCK, dtype=jnp.int32)

    def attend_chunk(args):
        qc, pos = args
        qc_len = qc.shape[1]
        qf = qc.astype(jnp.float32).transpose(0, 2, 1, 3)
        own = pos // MOBA_BLOCK
        gate = jnp.einsum('bhqd,bhnd->bhqn', qf, means)
        past_ok = jnp.arange(nb, dtype=jnp.int32)[None, :] < own[:, None]
        gate = jnp.where(past_ok, gate, NEG_INF)
        _, sel = lax.top_k(gate, top)
        sel_ok = sel < own[:, None]
        own_b = jnp.broadcast_to(own[:, None], (Bn, N_HEADS, qc_len, 1))
        blocks = jnp.concatenate([sel, own_b], axis=-1)
        ok = jnp.concatenate([sel_ok, jnp.ones_like(own_b, dtype=bool)], axis=-1)
        kg = kb[bi, hi, blocks].astype(jnp.float32)
        vg = vb[bi, hi, blocks].astype(jnp.float32)
        key_pos = blocks[..., None] * MOBA_BLOCK + blk_off
        dist = pos[None, None, :, None, None] - key_pos
        valid = ok[..., None] & (dist >= 0)
        s = (jnp.einsum('bhqd,bhqnkd->bhqnk', qf, kg) * scale
             - slopes[None, :, None, None, None] * dist.astype(jnp.float32))
        s = jnp.where(valid, s, NEG_INF)
        p = jax.nn.softmax(s.reshape(Bn, N_HEADS, qc_len, -1), axis=-1).reshape(s.shape)
        o = jnp.einsum('bhqnk,bhqnkd->bqhd', p, vg)
        return o.astype(v.dtype)

    pos_all = (L - T) + jnp.arange(T, dtype=jnp.int32)
    if T % Q_CHUNK == 0 and T > Q_CHUNK:
        n = T // Q_CHUNK
        qs = q.reshape(Bn, n, Q_CHUNK, N_HEADS, HEAD_DIM).transpose(1, 0, 2, 3, 4)
        out = lax.map(attend_chunk, (qs, pos_all.reshape(n, Q_CHUNK)))
        return out.transpose(1, 0, 2, 3, 4).reshape(Bn, T, N_HEADS, HEAD_DIM)
    return attend_chunk((q, pos_all))


def mixer_layer(x, conv_prev, k_prev, v_prev, ffn1_norm, ffn1_w_gu, ffn1_w_down, mix_norm, w_in,
                conv_w, conv_out_norm, attn_out_norm, w_out, ffn2_norm, ffn2_w_gu, ffn2_w_down):
    Bn, T, _ = x.shape
    x = x + 0.5 * swiglu_ffn(rms_norm(x, ffn1_norm), ffn1_w_gu, ffn1_w_down)
    h = rms_norm(x, mix_norm)
    proj = h @ w_in
    hc, bg, cg, q, k, v = jnp.split(
        proj, [D_CONV, 2 * D_CONV, 3 * D_CONV, 3 * D_CONV + D_ATT, 3 * D_CONV + 2 * D_ATT], axis=-1)
    u = cg * hc
    u_ext = jnp.concatenate([conv_prev.astype(u.dtype), u], axis=1)
    conv = u_ext[:, 0:T] * conv_w[0]
    for i in range(1, CONV_W):
        conv = conv + u_ext[:, i:i + T] * conv_w[i]
    y_conv = group_rms_norm(bg * conv, conv_out_norm, N_CONV_GROUPS)
    conv_new = u_ext[:, -(CONV_W - 1):]
    q = q.reshape(Bn, T, N_HEADS, HEAD_DIM)
    k_new = k.reshape(Bn, T, N_HEADS, HEAD_DIM)
    v_new = v.reshape(Bn, T, N_HEADS, HEAD_DIM)
    k_all = jnp.concatenate([k_prev.astype(k_new.dtype), k_new], axis=1)
    v_all = jnp.concatenate([v_prev.astype(v_new.dtype), v_new], axis=1)
    att = moba_attention(q, k_all, v_all).reshape(Bn, T, D_ATT)
    y_att = group_rms_norm(att, attn_out_norm, N_HEADS)
    x = x + jnp.concatenate([y_conv, y_att], axis=-1) @ w_out
    x = x + 0.5 * swiglu_ffn(rms_norm(x, ffn2_norm), ffn2_w_gu, ffn2_w_down)
    return x, k_new, v_new, conv_new


def setup_inputs(seed: int = 0) -> dict:
    key = jax.random.key(seed)
    ks = jax.random.split(key, 20)
    n_pages = PAST_LEN // PAGE_SIZE
    n_used = DEC_BATCH * n_pages
    n_pool = (n_used * 5) // 4
    nrm = jax.random.normal

    def gain(k, shape):
        return 1.0 + 0.02 * nrm(k, shape, jnp.float32)

    return {
        'x_prompt': nrm(ks[0], (BATCH, SEQ, D_MODEL), jnp.float32),
        'x_sample': nrm(ks[1], (DEC_BATCH, DEC_SEQ, D_MODEL), jnp.float32),
        'cache_k': nrm(ks[2], (DEPTH, n_pool, PAGE_SIZE, N_HEADS, HEAD_DIM), jnp.float32),
        'cache_v': nrm(ks[3], (DEPTH, n_pool, PAGE_SIZE, N_HEADS, HEAD_DIM), jnp.float32),
        'state_conv': nrm(ks[4], (DEPTH, DEC_BATCH, CONV_W - 1, D_CONV), jnp.float32),
        'page_table': jax.random.permutation(ks[5], n_pool)[:n_used].reshape(DEC_BATCH, n_pages).astype(jnp.int32),
        'ffn1_norm': gain(ks[6], (DEPTH, D_MODEL)),
        'ffn1_w_gu': nrm(ks[7], (DEPTH, D_MODEL, 2 * D_FF), jnp.float32) * D_MODEL ** -0.5,
        'ffn1_w_down': nrm(ks[8], (DEPTH, D_FF, D_MODEL), jnp.float32) * D_FF ** -0.5,
        'mix_norm': gain(ks[9], (DEPTH, D_MODEL)),
        'w_in': nrm(ks[10], (DEPTH, D_MODEL, D_IN), jnp.float32) * D_MODEL ** -0.5,
        'conv_w': nrm(ks[11], (DEPTH, CONV_W, D_CONV), jnp.float32) * CONV_W ** -0.5,
        'conv_out_norm': gain(ks[12], (DEPTH, D_CONV)),
        'attn_out_norm': gain(ks[13], (DEPTH, D_ATT)),
        'w_out': nrm(ks[14], (DEPTH, D_MIX, D_MODEL), jnp.float32) * D_MIX ** -0.5,
        'ffn2_norm': gain(ks[15], (DEPTH, D_MODEL)),
        'ffn2_w_gu': nrm(ks[16], (DEPTH, D_MODEL, 2 * D_FF), jnp.float32) * D_MODEL ** -0.5,
        'ffn2_w_down': nrm(ks[17], (DEPTH, D_FF, D_MODEL), jnp.float32) * D_FF ** -0.5,
        'final_norm': gain(ks[18], (D_MODEL,)),
    }


def reference(x_prompt, x_sample, cache_k, cache_v, state_conv, page_table, ffn1_norm, ffn1_w_gu,
              ffn1_w_down, mix_norm, w_in, conv_w, conv_out_norm, attn_out_norm, w_out, ffn2_norm,
              ffn2_w_gu, ffn2_w_down, final_norm):
    bp = x_prompt.shape[0]
    bs = x_sample.shape[0]
    xp, xs = x_prompt, x_sample
    kp_l, vp_l, cp_l, ks_l, vs_l, cs_l = [], [], [], [], [], []
    for l in range(DEPTH):
        w = (ffn1_norm[l], ffn1_w_gu[l], ffn1_w_down[l], mix_norm[l], w_in[l], conv_w[l],
             conv_out_norm[l], attn_out_norm[l], w_out[l], ffn2_norm[l], ffn2_w_gu[l], ffn2_w_down[l])
        conv0 = jnp.zeros((bp, CONV_W - 1, D_CONV), xp.dtype)
        kv0 = jnp.zeros((bp, 0, N_HEADS, HEAD_DIM), xp.dtype)
        xp, kp, vp, cp = mixer_layer(xp, conv0, kv0, kv0, *w)
        k_past = cache_k[l][page_table].reshape(bs, -1, N_HEADS, HEAD_DIM)
        v_past = cache_v[l][page_table].reshape(bs, -1, N_HEADS, HEAD_DIM)
        xs, ksn, vsn, csn = mixer_layer(xs, state_conv[l], k_past, v_past, *w)
        kp_l.append(kp); vp_l.append(vp); cp_l.append(cp)
        ks_l.append(ksn); vs_l.append(vsn); cs_l.append(csn)
    y_prompt = rms_norm(xp, final_norm)
    y_sample = rms_norm(xs, final_norm)
    k_prompt = jnp.stack(kp_l)
    v_prompt = jnp.stack(vp_l)
    conv_prompt = jnp.stack(cp_l)
    k_sample = jnp.stack(ks_l)
    v_sample = jnp.stack(vs_l)
    conv_sample = jnp.stack(cs_l)
    return (y_prompt, y_sample, k_prompt, v_prompt, conv_prompt, k_sample, v_sample, conv_sample)
```

```python
import functools

import jax
import jax.numpy as jnp
from jax import lax
from jax.experimental import pallas as pl
from jax.experimental.pallas import tpu as pltpu

F32 = jnp.float32
BF16 = jnp.bfloat16

N_HEADS = 8
HEAD_DIM = 64
D_ATT = N_HEADS * HEAD_DIM
N_CONV_GROUPS = 8
CONV_W = 3
MOBA_BLOCK = 256
MOBA_TOPK = 3
EPS = 1e-5
NEG_INF = -1e30
SCALE = HEAD_DIM ** -0.5

LANES = 128
HEADS_PER_SLAB = LANES // HEAD_DIM
N_SLABS = D_ATT // LANES
VMEM_LIMIT = 56 * 1024 * 1024

NT_DIMS = (((1,), (1,)), ((), ()))


def _slope(h):
    return 2.0 ** (-(8.0 / N_HEADS) * (h + 1))


def _dot(a, b):
    return jnp.dot(a, b, preferred_element_type=F32)


def _dot_nt(a, b, precision=None):
    return lax.dot_general(a, b, NT_DIMS, precision=precision, preferred_element_type=F32)


def _rms(x, g):
    ms = jnp.mean(x * x, axis=-1, keepdims=True)
    return x * lax.rsqrt(ms + EPS) * g


def _group_sumsq(y, bd_ref):
    y2 = y * y
    hi = y2.astype(BF16)
    lo = (y2 - hi.astype(F32)).astype(BF16)
    bd = bd_ref[...]
    return _dot(hi, bd) + _dot(lo, bd)


def _group_rms(y, g, bd_ref):
    ms = _group_sumsq(y, bd_ref) * (1.0 / HEAD_DIM)
    return y * lax.rsqrt(ms + EPS) * g


def _const_spec(shape):
    nd = len(shape)
    return pl.BlockSpec(shape, lambda *_: (0,) * nd, pipeline_mode=pl.Buffered(1))


def _ffn_body(*refs, mix, final, n_chunks):
    it = iter(refs)
    x_ref = next(it)
    if mix:
        yc_ref, ya_ref, woc_ref, woa_ref = next(it), next(it), next(it), next(it)
    g_ref, wg_ref, wu_ref, wd_ref = next(it), next(it), next(it), next(it)
    gf_ref = next(it) if final else None
    o_ref = next(it)

    x = x_ref[...]
    if mix:
        x = x + _dot(yc_ref[...], woc_ref[...]) + _dot(ya_ref[...], woa_ref[...])
    h = _rms(x, g_ref[...]).astype(BF16)
    d_ff = wg_ref.shape[1]
    cw = d_ff // n_chunks
    acc = jnp.zeros(x.shape, F32)
    for c in range(n_chunks):
        gate = _dot(h, wg_ref[:, c * cw:(c + 1) * cw])
        up = _dot(h, wu_ref[:, c * cw:(c + 1) * cw])
        act = (gate * jax.nn.sigmoid(gate) * up).astype(BF16)
        acc = acc + _dot(act, wd_ref[c * cw:(c + 1) * cw, :])
    x = x + 0.5 * acc
    if final:
        x = _rms(x, gf_ref[...])
    o_ref[...] = x


def _ffn_call(x, g, wg, wu, wd, *, tm, mix=None, final=None, name):
    n, d = x.shape
    d_ff = wg.shape[1]
    row = lambda w: pl.BlockSpec((tm, w), lambda i: (i, 0))
    ins, specs = [x], [row(d)]
    if mix is not None:
        yc, ya, woc, woa = mix
        ins += [yc, ya, woc, woa]
        specs += [row(yc.shape[1]), row(ya.shape[1]), _const_spec(woc.shape), _const_spec(woa.shape)]
    ins += [g, wg, wu, wd]
    specs += [_const_spec(g.shape), _const_spec(wg.shape), _const_spec(wu.shape), _const_spec(wd.shape)]
    if final is not None:
        ins.append(final)
        specs.append(_const_spec(final.shape))
    n_chunks = 2
    assert d_ff % (n_chunks * LANES) == 0
    body = functools.partial(_ffn_body, mix=mix is not None, final=final is not None, n_chunks=n_chunks)
    return pl.pallas_call(
        body,
        grid=(n // tm,),
        in_specs=specs,
        out_specs=row(d),
        out_shape=jax.ShapeDtypeStruct((n, d), F32),
        compiler_params=pltpu.CompilerParams(dimension_semantics=("arbitrary",),
                                             vmem_limit_bytes=VMEM_LIMIT),
        name=name,
    )(*ins)


def _inproj_body(*refs, tm, tiles_per_seq, sample):
    it = iter(refs)
    x_ref, g_ref, win_ref, cw_ref, cn_ref, bd_ref = (next(it) for _ in range(6))
    if sample:
        s1_ref, s2_ref = next(it), next(it)
        q_ref, k_ref, v_ref, yc_ref, u_ref = (next(it) for _ in range(5))
    else:
        q_ref, k_ref, v_ref, kb_ref, vb_ref, yc_ref, mean_ref, cnew_ref = (next(it) for _ in range(8))
    ubuf = next(it)

    dc = yc_ref.shape[1]
    h = _rms(x_ref[...], g_ref[...]).astype(BF16)
    piece = lambda c, w: _dot(h, win_ref[:, c:c + w])
    hc = piece(0, dc)
    bg = piece(dc, dc)
    cg = piece(2 * dc, dc)
    q_ref[...] = piece(3 * dc, D_ATT)
    k = piece(3 * dc + D_ATT, D_ATT)
    v = piece(3 * dc + 2 * D_ATT, D_ATT)
    k_ref[...] = k
    v_ref[...] = v

    u = cg * hc
    if sample:
        ubuf[0:8, :] = jnp.zeros((8, dc), F32)
    else:
        first = (pl.program_id(0) % tiles_per_seq) == 0

        @pl.when(first)
        def _():
            ubuf[0:8, :] = jnp.zeros((8, dc), F32)

        @pl.when(jnp.logical_not(first))
        def _():
            ubuf[0:8, :] = ubuf[tm:tm + 8, :]

    ubuf[8:tm + 8, :] = u
    um1 = ubuf[7:tm + 7, :]
    um2 = ubuf[6:tm + 6, :]
    if sample:
        t = lax.broadcasted_iota(jnp.int32, (tm, dc), 0) % u_ref.shape[1]
        um1 = jnp.where(t >= 1, um1, s1_ref[...])
        um2 = jnp.where(t >= 2, um2, s2_ref[...])
    cw = cw_ref[...]
    conv = um2 * cw[0:1, :] + um1 * cw[1:2, :] + u * cw[2:3, :]
    yc_ref[...] = _group_rms(bg * conv, cn_ref[...], bd_ref).astype(BF16)

    if sample:
        u_ref[...] = u.reshape(u_ref.shape)
    else:
        kb_ref[...] = k.astype(BF16)
        vb_ref[...] = v.astype(BF16)
        nblk = tm // MOBA_BLOCK
        mean_ref[0] = jnp.sum(k.reshape(nblk, MOBA_BLOCK, D_ATT), axis=1) * (1.0 / MOBA_BLOCK)
        cnew_ref[0] = ubuf[tm + 6:tm + 8, :]


def _inproj_prompt(x, g, w_in, conv_w, conv_norm, bd, *, batch, tm):
    n, d = x.shape
    dc = conv_w.shape[1]
    seq = n // batch
    tps = seq // tm
    nblk = tm // MOBA_BLOCK
    row = lambda w: pl.BlockSpec((tm, w), lambda i: (i, 0))
    f = lambda w, dt: jax.ShapeDtypeStruct((n, w), dt)
    body = functools.partial(_inproj_body, tm=tm, tiles_per_seq=tps, sample=False)
    return pl.pallas_call(
        body,
        grid=(n // tm,),
        in_specs=[row(d), _const_spec(g.shape), _const_spec(w_in.shape), _const_spec(conv_w.shape),
                  _const_spec(conv_norm.shape), _const_spec(bd.shape)],
        out_specs=[row(D_ATT), row(D_ATT), row(D_ATT), row(D_ATT), row(D_ATT), row(dc),
                   pl.BlockSpec((1, nblk, D_ATT), lambda i: (i, 0, 0)),
                   pl.BlockSpec((1, CONV_W - 1, dc), lambda i: (i // tps, 0, 0))],
        out_shape=[f(D_ATT, F32), f(D_ATT, F32), f(D_ATT, F32), f(D_ATT, BF16), f(D_ATT, BF16), f(dc, BF16),
                   jax.ShapeDtypeStruct((n // tm, nblk, D_ATT), F32),
                   jax.ShapeDtypeStruct((batch, CONV_W - 1, dc), F32)],
        scratch_shapes=[pltpu.VMEM((tm + 8, dc), F32)],
        compiler_params=pltpu.CompilerParams(dimension_semantics=("arbitrary",),
                                             vmem_limit_bytes=VMEM_LIMIT),
        name="inproj_prompt",
    )(x, g, w_in, conv_w, conv_norm, bd)


def _inproj_sample(x, g, w_in, conv_w, conv_norm, bd, s1, s2, *, seq):
    n, d = x.shape
    dc = conv_w.shape[1]
    tm = n
    full = lambda shape: pl.BlockSpec(shape, lambda i: (0,) * len(shape))
    f = lambda w, dt: jax.ShapeDtypeStruct((n, w), dt)
    body = functools.partial(_inproj_body, tm=tm, tiles_per_seq=1, sample=True)
    return pl.pallas_call(
        body,
        grid=(1,),
        in_specs=[full((tm, d)), full(g.shape), full(w_in.shape), full(conv_w.shape),
                  full(conv_norm.shape), full(bd.shape), full((tm, dc)), full((tm, dc))],
        out_specs=[full((tm, D_ATT)), full((tm, D_ATT)), full((tm, D_ATT)), full((tm, dc)),
                   full((n // seq, seq, dc))],
        out_shape=[f(D_ATT, F32), f(D_ATT, F32), f(D_ATT, F32), f(dc, BF16),
                   jax.ShapeDtypeStruct((n // seq, seq, dc), F32)],
        scratch_shapes=[pltpu.VMEM((tm + 8, dc), F32)],
        compiler_params=pltpu.CompilerParams(dimension_semantics=("arbitrary",),
                                             vmem_limit_bytes=VMEM_LIMIT),
        name="inproj_sample",
    )(x, g, w_in, conv_w, conv_norm, bd, s1, s2)


def _attn_prompt_body(q_ref, kb_ref, vb_ref, mean_ref, gain_ref, bd_ref, o_ref,
                      bias_own, bias_past, sn_ref, m_ref, l_ref, acc_ref, oall_ref):
    blk = MOBA_BLOCK
    b = pl.program_id(0)
    j = pl.program_id(1)
    nb = mean_ref.shape[1]
    group = LANES // N_HEADS

    @pl.when((b == 0) & (j == 0))
    def _init_bias():
        r = lax.broadcasted_iota(jnp.int32, (HEADS_PER_SLAB * blk, blk), 0)
        c = lax.broadcasted_iota(jnp.int32, (HEADS_PER_SLAB * blk, blk), 1)
        d = ((r % blk) - c).astype(F32)
        for p in range(N_SLABS):
            slope = jnp.where(r < blk, _slope(HEADS_PER_SLAB * p), _slope(HEADS_PER_SLAB * p + 1))
            bias_past[p] = -slope * d
            bias_own[p] = jnp.where(d >= 0, -slope * d, NEG_INF)

    q = q_ref[...]
    means = mean_ref[0]
    if nb < group:
        means = jnp.concatenate([means, jnp.zeros((group - nb, D_ATT), F32)], axis=0)
    mt = jnp.concatenate([means] * N_HEADS, axis=0)
    rh = lax.broadcasted_iota(jnp.int32, mt.shape, 0) // group
    ch = lax.broadcasted_iota(jnp.int32, mt.shape, 1) // HEAD_DIM
    mbd = jnp.where(rh == ch, mt, 0.0)
    gate = _dot_nt(q, mbd, precision=lax.Precision.HIGHEST)

    n_idx = lax.broadcasted_iota(jnp.int32, gate.shape, 1) % group
    valid = n_idx < j
    gm = jnp.where(valid, gate, NEG_INF)
    rank = jnp.zeros(gate.shape, jnp.int32)
    for s in range(1, group):
        ra = pltpu.roll(gm, s, axis=1)
        rb = pltpu.roll(gm, LANES - group + s, axis=1)
        nowrap = n_idx >= s
        other = jnp.where(nowrap, ra, rb)
        beats = (other > gm) | ((other == gm) & nowrap)
        rank = rank + beats.astype(jnp.int32)
    colb = jnp.where((rank < MOBA_TOPK) & valid, 0.0, NEG_INF)
    for n in range(nb):
        sn_ref[n] = colb if n == 0 else pltpu.roll(colb, LANES - n, axis=1)

    lane = lax.broadcasted_iota(jnp.int32, (blk, LANES), 1)
    qabs = []
    for p in range(N_SLABS):
        q2 = q[:, p * LANES:(p + 1) * LANES] * SCALE
        qa = jnp.where(lane < HEAD_DIM, q2, 0.0)
        qb = jnp.where(lane >= HEAD_DIM, q2, 0.0)
        qabs.append(jnp.concatenate([qa, qb], axis=0).astype(BF16))

    own = pl.multiple_of(j * blk, blk)
    for p in range(N_SLABS):
        ls = slice(p * LANES, (p + 1) * LANES)
        s = _dot_nt(qabs[p], kb_ref[pl.ds(own, blk), ls]) + bias_own[p]
        m = jnp.max(s, axis=1, keepdims=True)
        e = jnp.exp(s - m)
        m_ref[p] = m
        l_ref[p] = jnp.sum(e, axis=1, keepdims=True)
        acc_ref[p] = _dot(e.astype(BF16), vb_ref[pl.ds(own, blk), ls])

    def past_block(n, carry):
        off = pl.multiple_of(n * blk, blk)
        sn = sn_ref[n]
        shift = ((j - n) * blk).astype(F32)
        for p in range(N_SLABS):
            ls = slice(p * LANES, (p + 1) * LANES)
            ha, hb = HEADS_PER_SLAB * p, HEADS_PER_SLAB * p + 1
            col = jnp.concatenate([sn[:, ha * group:ha * group + 1] - _slope(ha) * shift,
                                   sn[:, hb * group:hb * group + 1] - _slope(hb) * shift], axis=0)
            s = _dot_nt(qabs[p], kb_ref[pl.ds(off, blk), ls]) + bias_past[p] + col
            m_prev = m_ref[p]
            m_new = jnp.maximum(m_prev, jnp.max(s, axis=1, keepdims=True))
            alpha = jnp.exp(m_prev - m_new)
            e = jnp.exp(s - m_new)
            m_ref[p] = m_new
            l_ref[p] = alpha * l_ref[p] + jnp.sum(e, axis=1, keepdims=True)
            acc_ref[p] = alpha * acc_ref[p] + _dot(e.astype(BF16), vb_ref[pl.ds(off, blk), ls])
        return carry

    lax.fori_loop(0, j, past_block, 0)

    for p in range(N_SLABS):
        o = acc_ref[p] / l_ref[p]
        oall_ref[:, p * LANES:(p + 1) * LANES] = jnp.where(lane < HEAD_DIM, o[:blk], o[blk:])
    o_ref[...] = _group_rms(oall_ref[...], gain_ref[...], bd_ref).astype(BF16)


def _attn_prompt(q, kb, vb, means, gain, bd, *, batch):
    n = q.shape[0]
    seq = n // batch
    nb = seq // MOBA_BLOCK
    assert nb * N_HEADS <= LANES and seq % MOBA_BLOCK == 0
    means = means.reshape(batch, nb, D_ATT)
    rows2 = HEADS_PER_SLAB * MOBA_BLOCK
    return pl.pallas_call(
        _attn_prompt_body,
        grid=(batch, nb),
        in_specs=[pl.BlockSpec((MOBA_BLOCK, D_ATT), lambda b, j: (b * nb + j, 0)),
                  pl.BlockSpec((seq, D_ATT), lambda b, j: (b, 0)),
                  pl.BlockSpec((seq, D_ATT), lambda b, j: (b, 0)),
                  pl.BlockSpec((1, nb, D_ATT), lambda b, j: (b, 0, 0)),
                  pl.BlockSpec(gain.shape, lambda b, j: (0, 0)),
                  pl.BlockSpec(bd.shape, lambda b, j: (0, 0))],
        out_specs=pl.BlockSpec((MOBA_BLOCK, D_ATT), lambda b, j: (b * nb + j, 0)),
        out_shape=jax.ShapeDtypeStruct((n, D_ATT), BF16),
        scratch_shapes=[pltpu.VMEM((N_SLABS, rows2, MOBA_BLOCK), F32),
                        pltpu.VMEM((N_SLABS, rows2, MOBA_BLOCK), F32),
                        pltpu.VMEM((nb, MOBA_BLOCK, LANES), F32),
                        pltpu.VMEM((N_SLABS, rows2, 1), F32),
                        pltpu.VMEM((N_SLABS, rows2, 1), F32),
                        pltpu.VMEM((N_SLABS, rows2, LANES), F32),
                        pltpu.VMEM((MOBA_BLOCK, D_ATT), F32)],
        compiler_params=pltpu.CompilerParams(dimension_semantics=("arbitrary", "arbitrary"),
                                             vmem_limit_bytes=VMEM_LIMIT),
        name="attn_prompt",
    )(q, kb, vb, means, gain, bd)


def _attn_sample_body(pt_ref, q_ref, kn_ref, vn_ref, gain_ref, bd_ref, ck_hbm, cv_hbm, o_ref,
                      kbuf, vbuf, s_ref, ksem, vsem, *, past_len, page, page_base, chunk):
    b = pl.program_id(0)
    nbat = pl.num_programs(0)
    n_pages = past_len // page
    nb = past_len // MOBA_BLOCK
    t_new = q_ref.shape[1]
    rows = N_HEADS * t_new
    pad = LANES

    def page_copy(hbm, buf, sem, bb, pg):
        return pltpu.make_async_copy(hbm.at[page_base + pt_ref[bb, pg]],
                                     buf.at[pl.ds(pl.multiple_of(pg * page, page), page)], sem)

    def start_all(hbm, buf, sem, bb):
        def body(pg, c):
            page_copy(hbm, buf, sem, bb, pg).start()
            return c
        lax.fori_loop(0, n_pages, body, 0)

    def wait_all(hbm, buf, sem, bb):
        def body(pg, c):
            page_copy(hbm, buf, sem, bb, pg).wait()
            return c
        lax.fori_loop(0, n_pages, body, 0)

    @pl.when(b == 0)
    def _():
        start_all(ck_hbm, kbuf, ksem, b)
        start_all(cv_hbm, vbuf, vsem, b)

    q = q_ref[0]
    qt = jnp.concatenate([q] * N_HEADS, axis=0)
    rh = lax.broadcasted_iota(jnp.int32, qt.shape, 0) // t_new
    ch = lax.broadcasted_iota(jnp.int32, qt.shape, 1) // HEAD_DIM
    qexp = jnp.where(rh == ch, qt, 0.0)
    qs = (qexp * SCALE).astype(BF16)

    r1 = lax.broadcasted_iota(jnp.int32, (rows, 1), 0)
    tq = r1 % t_new
    hq = r1 // t_new
    slope = jnp.zeros((rows, 1), F32)
    for h in range(N_HEADS):
        slope = jnp.where(hq == h, _slope(h), slope)

    wait_all(ck_hbm, kbuf, ksem, b)
    sums = []
    for c in range(past_len // chunk):
        kc = kbuf[c * chunk:(c + 1) * chunk, :]
        s_ref[:, c * chunk:(c + 1) * chunk] = _dot_nt(qs, kc.astype(BF16))
        sums.append(jnp.sum(kc.reshape(chunk // MOBA_BLOCK, MOBA_BLOCK, D_ATT), axis=1))
    sums.append(jnp.zeros((LANES - nb, D_ATT), F32))
    means = jnp.concatenate(sums, axis=0) * (1.0 / MOBA_BLOCK)

    @pl.when(b + 1 < nbat)
    def _():
        start_all(ck_hbm, kbuf, ksem, b + 1)

    gate = _dot_nt(qexp, means, precision=lax.Precision.HIGHEST)
    n_idx = lax.broadcasted_iota(jnp.int32, gate.shape, 1)
    rank = jnp.zeros(gate.shape, jnp.int32)
    for m in range(nb):
        gcol = gate[:, m:m + 1]
        beats = (gcol > gate) | ((gcol == gate) & (n_idx > m))
        rank = rank + beats.astype(jnp.int32)
    colb = jnp.where(rank < MOBA_TOPK, 0.0, NEG_INF)

    kn = jnp.concatenate([kn_ref[0], jnp.zeros((pad - t_new, D_ATT), F32)], axis=0)
    vn = jnp.concatenate([vn_ref[0], jnp.zeros((pad - t_new, D_ATT), F32)], axis=0)
    tk = lax.broadcasted_iota(jnp.int32, (rows, pad), 1)
    s_own = _dot_nt(qs, kn.astype(BF16))
    s_own = jnp.where(tk <= tq, s_own - slope * (tq - tk).astype(F32), NEG_INF)
    m = jnp.max(s_own, axis=1, keepdims=True)

    kpos = lax.broadcasted_iota(jnp.int32, (rows, MOBA_BLOCK), 1)
    for n in range(nb):
        dist = (past_len - n * MOBA_BLOCK + tq - kpos).astype(F32)
        sn = s_ref[:, n * MOBA_BLOCK:(n + 1) * MOBA_BLOCK] - slope * dist + colb[:, n:n + 1]
        s_ref[:, n * MOBA_BLOCK:(n + 1) * MOBA_BLOCK] = sn
        m = jnp.maximum(m, jnp.max(sn, axis=1, keepdims=True))

    e_own = jnp.exp(s_own - m)
    l = jnp.sum(e_own, axis=1, keepdims=True)
    acc = _dot(e_own.astype(BF16), vn.astype(BF16))

    wait_all(cv_hbm, vbuf, vsem, b)
    for c in range(past_len // chunk):
        e = jnp.exp(s_ref[:, c * chunk:(c + 1) * chunk] - m)
        l = l + jnp.sum(e, axis=1, keepdims=True)
        acc = acc + _dot(e.astype(BF16), vbuf[c * chunk:(c + 1) * chunk, :].astype(BF16))

    @pl.when(b + 1 < nbat)
    def _():
        start_all(cv_hbm, vbuf, vsem, b + 1)

    accn = acc / l
    ch8 = lax.broadcasted_iota(jnp.int32, (t_new, D_ATT), 1) // HEAD_DIM
    out = jnp.zeros((t_new, D_ATT), F32)
    for h in range(N_HEADS):
        out = jnp.where(ch8 == h, accn[h * t_new:(h + 1) * t_new, :], out)
    o_ref[0] = _group_rms(out, gain_ref[...], bd_ref).astype(BF16)


def _attn_sample(page_table, q, kn, vn, gain, bd, cache_k, cache_v, *, page_base, past_len):
    nbat, t_new, _ = q.shape
    page = cache_k.shape[1]
    assert past_len % MOBA_BLOCK == 0 and t_new <= LANES and t_new % 8 == 0
    chunk = 2048 if past_len % 2048 == 0 else MOBA_BLOCK
    rows = N_HEADS * t_new
    body = functools.partial(_attn_sample_body, past_len=past_len, page=page, page_base=page_base,
                             chunk=chunk)
    per_b = pl.BlockSpec((1, t_new, D_ATT), lambda b, pt: (b, 0, 0))
    grid_spec = pltpu.PrefetchScalarGridSpec(
        num_scalar_prefetch=1,
        grid=(nbat,),
        in_specs=[per_b, per_b, per_b,
                  pl.BlockSpec(gain.shape, lambda b, pt: (0, 0)),
                  pl.BlockSpec(bd.shape, lambda b, pt: (0, 0)),
                  pl.BlockSpec(memory_space=pl.ANY),
                  pl.BlockSpec(memory_space=pl.ANY)],
        out_specs=per_b,
        scratch_shapes=[pltpu.VMEM((past_len, D_ATT), F32),
                        pltpu.VMEM((past_len, D_ATT), F32),
                        pltpu.VMEM((rows, past_len), F32),
                        pltpu.SemaphoreType.DMA(()),
                        pltpu.SemaphoreType.DMA(())],
    )
    return pl.pallas_call(
        body,
        grid_spec=grid_spec,
        out_shape=jax.ShapeDtypeStruct((nbat, t_new, D_ATT), BF16),
        compiler_params=pltpu.CompilerParams(dimension_semantics=("arbitrary",),
                                             vmem_limit_bytes=VMEM_LIMIT),
        name="attn_sample",
    )(page_table, q, kn, vn, gain, bd, cache_k, cache_v)


def kernel(x_prompt, x_sample, cache_k, cache_v, state_conv, page_table, ffn1_norm, ffn1_w_gu, ffn1_w_down,
           mix_norm, w_in, conv_w, conv_out_norm, attn_out_norm, w_out, ffn2_norm, ffn2_w_gu, ffn2_w_down,
           final_norm):
    bp, seq, d = x_prompt.shape
    bs, dseq, _ = x_sample.shape
    depth, n_pool, page = cache_k.shape[:3]
    d_ff = ffn1_w_down.shape[1]
    dc = conv_w.shape[2]
    past_len = page_table.shape[1] * page

    ck = cache_k.reshape(depth * n_pool, page, D_ATT)
    cv = cache_v.reshape(depth * n_pool, page, D_ATT)
    gi = lax.broadcasted_iota(jnp.int32, (D_ATT, D_ATT), 0) // HEAD_DIM
    gj = lax.broadcasted_iota(jnp.int32, (D_ATT, D_ATT), 1) // HEAD_DIM
    bd = (gi == gj).astype(BF16)

    xp = x_prompt.reshape(bp * seq, d)
    xs = x_sample.reshape(bs * dseq, d)
    tm_p = 512
    row = lambda a: a.reshape(1, -1)
    outs = [[] for _ in range(6)]
    for l in range(depth):
        wg1, wu1 = ffn1_w_gu[l][:, :d_ff].astype(BF16), ffn1_w_gu[l][:, d_ff:].astype(BF16)
        wd1 = ffn1_w_down[l].astype(BF16)
        wg2, wu2 = ffn2_w_gu[l][:, :d_ff].astype(BF16), ffn2_w_gu[l][:, d_ff:].astype(BF16)
        wd2 = ffn2_w_down[l].astype(BF16)
        win = w_in[l].astype(BF16)
        woc, woa = w_out[l][:dc].astype(BF16), w_out[l][dc:].astype(BF16)
        g1, gm, g2 = row(ffn1_norm[l]), row(mix_norm[l]), row(ffn2_norm[l])
        gc, ga = row(conv_out_norm[l]), row(attn_out_norm[l])
        last = l == depth - 1
        gfin = row(final_norm) if last else None

        x1 = _ffn_call(xp, g1, wg1, wu1, wd1, tm=tm_p, name="ffn1_prompt")
        q, k, v, kb, vb, yc, means, cnew = _inproj_prompt(x1, gm, win, conv_w[l], gc, bd, batch=bp, tm=tm_p)
        ya = _attn_prompt(q, kb, vb, means, ga, bd, batch=bp)
        xp = _ffn_call(x1, g2, wg2, wu2, wd2, tm=tm_p, mix=(yc, ya, woc, woa), final=gfin, name="ffn2_prompt")
        outs[0].append(k.reshape(bp, seq, N_HEADS, HEAD_DIM))
        outs[1].append(v.reshape(bp, seq, N_HEADS, HEAD_DIM))
        outs[2].append(cnew)

        st = state_conv[l]
        zpad = jnp.zeros((bs, dseq - (CONV_W - 1), dc), F32)
        s2 = jnp.concatenate([st, zpad], axis=1).reshape(bs * dseq, dc)
        s1 = jnp.concatenate([st[:, 1:2], jnp.zeros((bs, dseq - 1, dc), F32)], axis=1).reshape(bs * dseq, dc)
        x1s = _ffn_call(xs, g1, wg1, wu1, wd1, tm=bs * dseq, name="ffn1_sample")
        qs, ks, vs, ycs, us = _inproj_sample(x1s, gm, win, conv_w[l], gc, bd, s1, s2, seq=dseq)
        r3 = lambda a: a.reshape(bs, dseq, D_ATT)
        yas = _attn_sample(page_table, r3(qs), r3(ks), r3(vs), ga, bd, ck, cv,
                           page_base=l * n_pool, past_len=past_len)
        xs = _ffn_call(x1s, g2, wg2, wu2, wd2, tm=bs * dseq,
                       mix=(ycs, yas.reshape(bs * dseq, D_ATT), woc, woa), final=gfin, name="ffn2_sample")
        outs[3].append(ks.reshape(bs, dseq, N_HEADS, HEAD_DIM))
        outs[4].append(vs.reshape(bs, dseq, N_HEADS, HEAD_DIM))
        outs[5].append(us[:, dseq - (CONV_W - 1):, :])

    y_prompt = xp.reshape(bp, seq, d)
    y_sample = xs.reshape(bs, dseq, d)
    kp, vp, cp, ksn, vsn, csn = (jnp.stack(o) for o in outs)
    return (y_prompt, y_sample, kp, vp, cp, ksn, vsn, csn)
```

```python
import functools

import jax
import jax.numpy as jnp
from jax import lax
from jax.experimental import pallas as pl
from jax.experimental.pallas import tpu as pltpu

F32 = jnp.float32
BF16 = jnp.bfloat16

N_HEADS = 8
HEAD_DIM = 64
D_ATT = N_HEADS * HEAD_DIM
N_CONV_GROUPS = 8
CONV_W = 3
MOBA_BLOCK = 256
MOBA_TOPK = 3
EPS = 1e-5
NEG_INF = -1e30
SCALE = HEAD_DIM ** -0.5
LOG2E = 1.4426950408889634

LANES = 128
HEADS_PER_SLAB = LANES // HEAD_DIM
N_SLABS = D_ATT // LANES
VMEM_LIMIT = 56 * 1024 * 1024

NT_DIMS = (((1,), (1,)), ((), ()))


def _slope(h):
    return 2.0 ** (-(8.0 / N_HEADS) * (h + 1))


def _dot(a, b):
    return jnp.dot(a, b, preferred_element_type=F32)


def _dot_nt(a, b, precision=None):
    return lax.dot_general(a, b, NT_DIMS, precision=precision, preferred_element_type=F32)


def _rms(x, g):
    ms = jnp.mean(x * x, axis=-1, keepdims=True)
    return x * lax.rsqrt(ms + EPS) * g


def _group_sumsq(y, bd_ref):
    y2 = y * y
    hi = y2.astype(BF16)
    lo = (y2 - hi.astype(F32)).astype(BF16)
    bd = bd_ref[...]
    return _dot(hi, bd) + _dot(lo, bd)


def _group_rms(y, g, bd_ref):
    ms = _group_sumsq(y, bd_ref) * (1.0 / HEAD_DIM)
    return y * lax.rsqrt(ms + EPS) * g


def _const_spec(shape):
    nd = len(shape)
    return pl.BlockSpec(shape, lambda *_: (0,) * nd, pipeline_mode=pl.Buffered(1))


def _ffn_body(*refs, mix, final, n_chunks):
    it = iter(refs)
    x_ref = next(it)
    if mix:
        yc_ref, ya_ref, woc_ref, woa_ref = next(it), next(it), next(it), next(it)
    g_ref, wg_ref, wu_ref, wd_ref = next(it), next(it), next(it), next(it)
    gf_ref = next(it) if final else None
    o_ref = next(it)

    x = x_ref[...]
    if mix:
        x = x + _dot(yc_ref[...], woc_ref[...]) + _dot(ya_ref[...], woa_ref[...])
    h = _rms(x, g_ref[...]).astype(BF16)
    d_ff = wg_ref.shape[1]
    cw = d_ff // n_chunks
    acc = jnp.zeros(x.shape, F32)
    for c in range(n_chunks):
        gate = _dot(h, wg_ref[:, c * cw:(c + 1) * cw])
        up = _dot(h, wu_ref[:, c * cw:(c + 1) * cw])
        act = (gate * jax.nn.sigmoid(gate) * up).astype(BF16)
        acc = acc + _dot(act, wd_ref[c * cw:(c + 1) * cw, :])
    x = x + 0.5 * acc
    if final:
        x = _rms(x, gf_ref[...])
    o_ref[...] = x


def _ffn_call(x, g, wg, wu, wd, *, tm, mix=None, final=None, name):
    n, d = x.shape
    d_ff = wg.shape[1]
    row = lambda w: pl.BlockSpec((tm, w), lambda i: (i, 0))
    ins, specs = [x], [row(d)]
    if mix is not None:
        yc, ya, woc, woa = mix
        ins += [yc, ya, woc, woa]
        specs += [row(yc.shape[1]), row(ya.shape[1]), _const_spec(woc.shape), _const_spec(woa.shape)]
    ins += [g, wg, wu, wd]
    specs += [_const_spec(g.shape), _const_spec(wg.shape), _const_spec(wu.shape), _const_spec(wd.shape)]
    if final is not None:
        ins.append(final)
        specs.append(_const_spec(final.shape))
    n_chunks = 2
    assert d_ff % (n_chunks * LANES) == 0
    body = functools.partial(_ffn_body, mix=mix is not None, final=final is not None, n_chunks=n_chunks)
    return pl.pallas_call(
        body,
        grid=(n // tm,),
        in_specs=specs,
        out_specs=row(d),
        out_shape=jax.ShapeDtypeStruct((n, d), F32),
        compiler_params=pltpu.CompilerParams(dimension_semantics=("arbitrary",),
                                             vmem_limit_bytes=VMEM_LIMIT),
        name=name,
    )(*ins)


def _inproj_body(*refs, tm, tiles_per_seq, sample):
    it = iter(refs)
    x_ref, g_ref, win_ref, cw_ref, cn_ref, bd_ref = (next(it) for _ in range(6))
    if sample:
        s1_ref, s2_ref = next(it), next(it)
        q_ref, k_ref, v_ref, yc_ref, u_ref = (next(it) for _ in range(5))
    else:
        q_ref, k_ref, v_ref, kb_ref, vt_ref, yc_ref, mean_ref, cnew_ref = (next(it) for _ in range(8))
    ubuf = next(it)

    dc = yc_ref.shape[1]
    h = _rms(x_ref[...], g_ref[...]).astype(BF16)
    piece = lambda c, w: _dot(h, win_ref[:, c:c + w])
    hc = piece(0, dc)
    bg = piece(dc, dc)
    cg = piece(2 * dc, dc)
    q_ref[...] = piece(3 * dc, D_ATT)
    k = piece(3 * dc + D_ATT, D_ATT)
    v = piece(3 * dc + 2 * D_ATT, D_ATT)
    k_ref[...] = k
    v_ref[...] = v

    u = cg * hc
    if sample:
        ubuf[0:8, :] = jnp.zeros((8, dc), F32)
    else:
        first = (pl.program_id(0) % tiles_per_seq) == 0

        @pl.when(first)
        def _():
            ubuf[0:8, :] = jnp.zeros((8, dc), F32)

        @pl.when(jnp.logical_not(first))
        def _():
            ubuf[0:8, :] = ubuf[tm:tm + 8, :]

    ubuf[8:tm + 8, :] = u
    um1 = ubuf[7:tm + 7, :]
    um2 = ubuf[6:tm + 6, :]
    if sample:
        t = lax.broadcasted_iota(jnp.int32, (tm, dc), 0) % u_ref.shape[1]
        um1 = jnp.where(t >= 1, um1, s1_ref[...])
        um2 = jnp.where(t >= 2, um2, s2_ref[...])
    cw = cw_ref[...]
    conv = um2 * cw[0:1, :] + um1 * cw[1:2, :] + u * cw[2:3, :]
    yc_ref[...] = _group_rms(bg * conv, cn_ref[...], bd_ref).astype(BF16)

    if sample:
        u_ref[...] = u.reshape(u_ref.shape)
    else:
        kb_ref[...] = k.astype(BF16)
        nblk = tm // MOBA_BLOCK
        for i in range(nblk):
            vt_ref[i] = v[i * MOBA_BLOCK:(i + 1) * MOBA_BLOCK, :].T.astype(BF16)
        mean_ref[0] = jnp.sum(k.reshape(nblk, MOBA_BLOCK, D_ATT), axis=1) * (1.0 / MOBA_BLOCK)
        cnew_ref[0] = ubuf[tm + 6:tm + 8, :]


def _inproj_prompt(x, g, w_in, conv_w, conv_norm, bd, *, batch, tm):
    n, d = x.shape
    dc = conv_w.shape[1]
    seq = n // batch
    tps = seq // tm
    nblk = tm // MOBA_BLOCK
    row = lambda w: pl.BlockSpec((tm, w), lambda i: (i, 0))
    f = lambda w, dt: jax.ShapeDtypeStruct((n, w), dt)
    body = functools.partial(_inproj_body, tm=tm, tiles_per_seq=tps, sample=False)
    return pl.pallas_call(
        body,
        grid=(n // tm,),
        in_specs=[row(d), _const_spec(g.shape), _const_spec(w_in.shape), _const_spec(conv_w.shape),
                  _const_spec(conv_norm.shape), _const_spec(bd.shape)],
        out_specs=[row(D_ATT), row(D_ATT), row(D_ATT), row(D_ATT),
                   pl.BlockSpec((nblk, D_ATT, MOBA_BLOCK), lambda i: (i, 0, 0)), row(dc),
                   pl.BlockSpec((1, nblk, D_ATT), lambda i: (i, 0, 0)),
                   pl.BlockSpec((1, CONV_W - 1, dc), lambda i: (i // tps, 0, 0))],
        out_shape=[f(D_ATT, F32), f(D_ATT, F32), f(D_ATT, F32), f(D_ATT, BF16),
                   jax.ShapeDtypeStruct((n // MOBA_BLOCK, D_ATT, MOBA_BLOCK), BF16), f(dc, BF16),
                   jax.ShapeDtypeStruct((n // tm, nblk, D_ATT), F32),
                   jax.ShapeDtypeStruct((batch, CONV_W - 1, dc), F32)],
        scratch_shapes=[pltpu.VMEM((tm + 8, dc), F32)],
        compiler_params=pltpu.CompilerParams(dimension_semantics=("arbitrary",),
                                             vmem_limit_bytes=VMEM_LIMIT),
        name="inproj_prompt",
    )(x, g, w_in, conv_w, conv_norm, bd)


def _inproj_sample(x, g, w_in, conv_w, conv_norm, bd, s1, s2, *, seq):
    n, d = x.shape
    dc = conv_w.shape[1]
    tm = n
    full = lambda shape: pl.BlockSpec(shape, lambda i: (0,) * len(shape))
    f = lambda w, dt: jax.ShapeDtypeStruct((n, w), dt)
    body = functools.partial(_inproj_body, tm=tm, tiles_per_seq=1, sample=True)
    return pl.pallas_call(
        body,
        grid=(1,),
        in_specs=[full((tm, d)), full(g.shape), full(w_in.shape), full(conv_w.shape),
                  full(conv_norm.shape), full(bd.shape), full((tm, dc)), full((tm, dc))],
        out_specs=[full((tm, D_ATT)), full((tm, D_ATT)), full((tm, D_ATT)), full((tm, dc)),
                   full((n // seq, seq, dc))],
        out_shape=[f(D_ATT, F32), f(D_ATT, F32), f(D_ATT, F32), f(dc, BF16),
                   jax.ShapeDtypeStruct((n // seq, seq, dc), F32)],
        scratch_shapes=[pltpu.VMEM((tm + 8, dc), F32)],
        compiler_params=pltpu.CompilerParams(dimension_semantics=("arbitrary",),
                                             vmem_limit_bytes=VMEM_LIMIT),
        name="inproj_sample",
    )(x, g, w_in, conv_w, conv_norm, bd, s1, s2)


def _attn_prompt_body(q_ref, kb_ref, vt_ref, mean_ref, gain_ref, o_ref,
                      bias_own, bias_past, qabt_ref, colb_ref, so_ref, sa_ref, sb_ref, m_ref, l_ref, acc_ref):
    blk = MOBA_BLOCK
    b = pl.program_id(0)
    j = pl.program_id(1)
    nb = mean_ref.shape[1]
    group = LANES // N_HEADS
    qcols = HEADS_PER_SLAB * blk
    lane_q = lax.broadcasted_iota(jnp.int32, (1, qcols), 1)

    def slope_row(p):
        return jnp.where(lane_q < blk, LOG2E * _slope(HEADS_PER_SLAB * p), LOG2E * _slope(HEADS_PER_SLAB * p + 1))

    @pl.when((b == 0) & (j == 0))
    def _init_bias():
        kk = lax.broadcasted_iota(jnp.int32, (blk, qcols), 0)
        qq = lax.broadcasted_iota(jnp.int32, (blk, qcols), 1)
        d = ((qq % blk) - kk).astype(F32)
        for p in range(N_SLABS):
            bias_past[p] = -slope_row(p) * d
            bias_own[p] = jnp.where(d >= 0, -slope_row(p) * d, NEG_INF)

    qt = q_ref[...].T
    means = mean_ref[0]
    if nb < group:
        means = jnp.concatenate([means, jnp.zeros((group - nb, D_ATT), F32)], axis=0)
    mt = jnp.concatenate([means] * N_HEADS, axis=0)
    rh = lax.broadcasted_iota(jnp.int32, mt.shape, 0) // group
    ch = lax.broadcasted_iota(jnp.int32, mt.shape, 1) // HEAD_DIM
    mbd = jnp.where(rh == ch, mt, 0.0)
    gate_t = jnp.dot(mbd, qt, precision=lax.Precision.HIGHEST, preferred_element_type=F32)
    gate = jnp.concatenate([gate_t[h * group:(h + 1) * group, :] for h in range(N_HEADS)], axis=1)

    n_idx = lax.broadcasted_iota(jnp.int32, gate.shape, 0)
    valid = n_idx < j
    gm = jnp.where(valid, gate, NEG_INF)
    rank = jnp.zeros(gate.shape, jnp.int32)
    for m in range(nb):
        other = gm[m:m + 1, :]
        beats = (other > gm) | ((other == gm) & (n_idx > m))
        rank = rank + beats.astype(jnp.int32)
    colb_ref[...] = jnp.where((rank < MOBA_TOPK) & valid, 0.0, NEG_INF)

    row_d = lax.broadcasted_iota(jnp.int32, (LANES, blk), 0)
    for p in range(N_SLABS):
        qs = qt[p * LANES:(p + 1) * LANES, :] * (SCALE * LOG2E)
        qa = jnp.where(row_d < HEAD_DIM, qs, 0.0)
        qb = jnp.where(row_d >= HEAD_DIM, qs, 0.0)
        qabt_ref[p] = jnp.concatenate([qa, qb], axis=1).astype(BF16)

    slabs = [slice(p * LANES, (p + 1) * LANES) for p in range(N_SLABS)]

    def park_scores(n, dst):
        off = pl.multiple_of(n * blk, blk)
        for p in range(N_SLABS):
            dst[p] = _dot(kb_ref[pl.ds(off, blk), slabs[p]], qabt_ref[p])

    def reduce_past(n, src):
        shift = ((j - n) * blk).astype(F32)
        for p in range(N_SLABS):
            crow = colb_ref[pl.ds(n, 1), p * qcols:(p + 1) * qcols] - slope_row(p) * shift
            sb = src[p] + bias_past[p]
            m_prev = m_ref[p]
            m_new = jnp.maximum(m_prev, jnp.max(sb, axis=0, keepdims=True) + crow)
            alpha = jnp.exp2(m_prev - m_new)
            e = jnp.exp2(sb - (m_new - crow))
            m_ref[p] = m_new
            l_ref[p] = alpha * l_ref[p] + jnp.sum(e, axis=0, keepdims=True)
            acc_ref[p] = alpha * acc_ref[p] + _dot(vt_ref[n, slabs[p], :], e.astype(BF16))

    park_scores(j, so_ref)
    park_scores(0, sa_ref)
    for p in range(N_SLABS):
        sb = so_ref[p] + bias_own[p]
        m = jnp.max(sb, axis=0, keepdims=True)
        e = jnp.exp2(sb - m)
        m_ref[p] = m
        l_ref[p] = jnp.sum(e, axis=0, keepdims=True)
        acc_ref[p] = _dot(vt_ref[j, slabs[p], :], e.astype(BF16))

    def two_blocks(i, carry):
        n0 = 2 * i
        park_scores(n0 + 1, sb_ref)
        reduce_past(n0, sa_ref)
        park_scores(jnp.minimum(n0 + 2, j), sa_ref)
        reduce_past(n0 + 1, sb_ref)
        return carry

    lax.fori_loop(0, j // 2, two_blocks, 0)

    @pl.when(j % 2 == 1)
    def _odd_tail():
        reduce_past(j - 1, sa_ref)

    for p in range(N_SLABS):
        o = acc_ref[p] / l_ref[p]
        o2 = jnp.where(row_d < HEAD_DIM, o[:, :blk], o[:, blk:])
        sq = o2 * o2
        ms_a = jnp.sum(sq[:HEAD_DIM], axis=0, keepdims=True) * (1.0 / HEAD_DIM)
        ms_b = jnp.sum(sq[HEAD_DIM:], axis=0, keepdims=True) * (1.0 / HEAD_DIM)
        inv = jnp.where(row_d < HEAD_DIM, lax.rsqrt(ms_a + EPS), lax.rsqrt(ms_b + EPS))
        ls = slice(p * LANES, (p + 1) * LANES)
        o_ref[:, ls] = ((o2 * inv).T * gain_ref[:, ls]).astype(BF16)


def _attn_prompt(q, kb, vt, means, gain, *, batch):
    n = q.shape[0]
    seq = n // batch
    nb = seq // MOBA_BLOCK
    group = LANES // N_HEADS
    assert nb <= group and seq % MOBA_BLOCK == 0
    means = means.reshape(batch, nb, D_ATT)
    qcols = HEADS_PER_SLAB * MOBA_BLOCK
    return pl.pallas_call(
        _attn_prompt_body,
        grid=(batch, nb),
        in_specs=[pl.BlockSpec((MOBA_BLOCK, D_ATT), lambda b, j: (b * nb + j, 0)),
                  pl.BlockSpec((seq, D_ATT), lambda b, j: (b, 0)),
                  pl.BlockSpec((nb, D_ATT, MOBA_BLOCK), lambda b, j: (b, 0, 0)),
                  pl.BlockSpec((1, nb, D_ATT), lambda b, j: (b, 0, 0)),
                  pl.BlockSpec(gain.shape, lambda b, j: (0, 0))],
        out_specs=pl.BlockSpec((MOBA_BLOCK, D_ATT), lambda b, j: (b * nb + j, 0)),
        out_shape=jax.ShapeDtypeStruct((n, D_ATT), BF16),
        scratch_shapes=[pltpu.VMEM((N_SLABS, MOBA_BLOCK, qcols), F32),
                        pltpu.VMEM((N_SLABS, MOBA_BLOCK, qcols), F32),
                        pltpu.VMEM((N_SLABS, LANES, qcols), BF16),
                        pltpu.VMEM((group, N_HEADS * MOBA_BLOCK), F32),
                        pltpu.VMEM((N_SLABS, MOBA_BLOCK, qcols), F32),
                        pltpu.VMEM((N_SLABS, MOBA_BLOCK, qcols), F32),
                        pltpu.VMEM((N_SLABS, MOBA_BLOCK, qcols), F32),
                        pltpu.VMEM((N_SLABS, 1, qcols), F32),
                        pltpu.VMEM((N_SLABS, 1, qcols), F32),
                        pltpu.VMEM((N_SLABS, LANES, qcols), F32)],
        compiler_params=pltpu.CompilerParams(dimension_semantics=("arbitrary", "arbitrary"),
                                             vmem_limit_bytes=VMEM_LIMIT),
        name="attn_prompt",
    )(q, kb, vt, means, gain)


def _attn_sample_body(pt_ref, q_ref, kn_ref, vn_ref, gain_ref, bd_ref, ck_hbm, cv_hbm, o_ref,
                      kbuf, vbuf, s_ref, ksem, vsem, *, past_len, page, page_base, chunk):
    b = pl.program_id(0)
    nbat = pl.num_programs(0)
    n_pages = past_len // page
    nb = past_len // MOBA_BLOCK
    t_new = q_ref.shape[1]
    rows = N_HEADS * t_new
    pad = LANES

    def page_copy(hbm, buf, sem, bb, pg):
        return pltpu.make_async_copy(hbm.at[page_base + pt_ref[bb, pg]],
                                     buf.at[pl.ds(pl.multiple_of(pg * page, page), page)], sem)

    def start_all(hbm, buf, sem, bb):
        def body(pg, c):
            page_copy(hbm, buf, sem, bb, pg).start()
            return c
        lax.fori_loop(0, n_pages, body, 0)

    def wait_all(hbm, buf, sem, bb):
        def body(pg, c):
            page_copy(hbm, buf, sem, bb, pg).wait()
            return c
        lax.fori_loop(0, n_pages, body, 0)

    @pl.when(b == 0)
    def _():
        start_all(ck_hbm, kbuf, ksem, b)
        start_all(cv_hbm, vbuf, vsem, b)

    q = q_ref[0]
    qt = jnp.concatenate([q] * N_HEADS, axis=0)
    rh = lax.broadcasted_iota(jnp.int32, qt.shape, 0) // t_new
    ch = lax.broadcasted_iota(jnp.int32, qt.shape, 1) // HEAD_DIM
    qexp = jnp.where(rh == ch, qt, 0.0)
    qs = (qexp * SCALE).astype(BF16)

    r1 = lax.broadcasted_iota(jnp.int32, (rows, 1), 0)
    tq = r1 % t_new
    hq = r1 // t_new
    slope = jnp.zeros((rows, 1), F32)
    for h in range(N_HEADS):
        slope = jnp.where(hq == h, _slope(h), slope)

    wait_all(ck_hbm, kbuf, ksem, b)
    sums = []
    for c in range(past_len // chunk):
        kc = kbuf[c * chunk:(c + 1) * chunk, :]
        s_ref[:, c * chunk:(c + 1) * chunk] = _dot_nt(qs, kc.astype(BF16))
        sums.append(jnp.sum(kc.reshape(chunk // MOBA_BLOCK, MOBA_BLOCK, D_ATT), axis=1))
    sums.append(jnp.zeros((LANES - nb, D_ATT), F32))
    means = jnp.concatenate(sums, axis=0) * (1.0 / MOBA_BLOCK)

    @pl.when(b + 1 < nbat)
    def _():
        start_all(ck_hbm, kbuf, ksem, b + 1)

    gate = _dot_nt(qexp, means, precision=lax.Precision.HIGHEST)
    n_idx = lax.broadcasted_iota(jnp.int32, gate.shape, 1)
    rank = jnp.zeros(gate.shape, jnp.int32)
    for m in range(nb):
        gcol = gate[:, m:m + 1]
        beats = (gcol > gate) | ((gcol == gate) & (n_idx > m))
        rank = rank + beats.astype(jnp.int32)
    colb = jnp.where(rank < MOBA_TOPK, 0.0, NEG_INF)

    kn = jnp.concatenate([kn_ref[0], jnp.zeros((pad - t_new, D_ATT), F32)], axis=0)
    vn = jnp.concatenate([vn_ref[0], jnp.zeros((pad - t_new, D_ATT), F32)], axis=0)
    tk = lax.broadcasted_iota(jnp.int32, (rows, pad), 1)
    s_own = _dot_nt(qs, kn.astype(BF16))
    s_own = jnp.where(tk <= tq, s_own - slope * (tq - tk).astype(F32), NEG_INF)
    m = jnp.max(s_own, axis=1, keepdims=True)

    kpos = lax.broadcasted_iota(jnp.int32, (rows, MOBA_BLOCK), 1)
    for n in range(nb):
        dist = (past_len - n * MOBA_BLOCK + tq - kpos).astype(F32)
        sn = s_ref[:, n * MOBA_BLOCK:(n + 1) * MOBA_BLOCK] - slope * dist + colb[:, n:n + 1]
        s_ref[:, n * MOBA_BLOCK:(n + 1) * MOBA_BLOCK] = sn
        m = jnp.maximum(m, jnp.max(sn, axis=1, keepdims=True))

    e_own = jnp.exp(s_own - m)
    l = jnp.sum(e_own, axis=1, keepdims=True)
    acc = _dot(e_own.astype(BF16), vn.astype(BF16))

    wait_all(cv_hbm, vbuf, vsem, b)
    for c in range(past_len // chunk):
        e = jnp.exp(s_ref[:, c * chunk:(c + 1) * chunk] - m)
        l = l + jnp.sum(e, axis=1, keepdims=True)
        acc = acc + _dot(e.astype(BF16), vbuf[c * chunk:(c + 1) * chunk, :].astype(BF16))

    @pl.when(b + 1 < nbat)
    def _():
        start_all(cv_hbm, vbuf, vsem, b + 1)

    accn = acc / l
    ch8 = lax.broadcasted_iota(jnp.int32, (t_new, D_ATT), 1) // HEAD_DIM
    out = jnp.zeros((t_new, D_ATT), F32)
    for h in range(N_HEADS):
        out = jnp.where(ch8 == h, accn[h * t_new:(h + 1) * t_new, :], out)
    o_ref[0] = _group_rms(out, gain_ref[...], bd_ref).astype(BF16)


def _attn_sample(page_table, q, kn, vn, gain, bd, cache_k, cache_v, *, page_base, past_len):
    nbat, t_new, _ = q.shape
    page = cache_k.shape[1]
    assert past_len % MOBA_BLOCK == 0 and t_new <= LANES and t_new % 8 == 0
    chunk = 2048 if past_len % 2048 == 0 else MOBA_BLOCK
    rows = N_HEADS * t_new
    body = functools.partial(_attn_sample_body, past_len=past_len, page=page, page_base=page_base,
                             chunk=chunk)
    per_b = pl.BlockSpec((1, t_new, D_ATT), lambda b, pt: (b, 0, 0))
    grid_spec = pltpu.PrefetchScalarGridSpec(
        num_scalar_prefetch=1,
        grid=(nbat,),
        in_specs=[per_b, per_b, per_b,
                  pl.BlockSpec(gain.shape, lambda b, pt: (0, 0)),
                  pl.BlockSpec(bd.shape, lambda b, pt: (0, 0)),
                  pl.BlockSpec(memory_space=pl.ANY),
                  pl.BlockSpec(memory_space=pl.ANY)],
        out_specs=per_b,
        scratch_shapes=[pltpu.VMEM((past_len, D_ATT), F32),
                        pltpu.VMEM((past_len, D_ATT), F32),
                        pltpu.VMEM((rows, past_len), F32),
                        pltpu.SemaphoreType.DMA(()),
                        pltpu.SemaphoreType.DMA(())],
    )
    return pl.pallas_call(
        body,
        grid_spec=grid_spec,
        out_shape=jax.ShapeDtypeStruct((nbat, t_new, D_ATT), BF16),
        compiler_params=pltpu.CompilerParams(dimension_semantics=("arbitrary",),
                                             vmem_limit_bytes=VMEM_LIMIT),
        name="attn_sample",
    )(page_table, q, kn, vn, gain, bd, cache_k, cache_v)


def kernel(x_prompt, x_sample, cache_k, cache_v, state_conv, page_table, ffn1_norm, ffn1_w_gu, ffn1_w_down,
           mix_norm, w_in, conv_w, conv_out_norm, attn_out_norm, w_out, ffn2_norm, ffn2_w_gu, ffn2_w_down,
           final_norm):
    bp, seq, d = x_prompt.shape
    bs, dseq, _ = x_sample.shape
    depth, n_pool, page = cache_k.shape[:3]
    d_ff = ffn1_w_down.shape[1]
    dc = conv_w.shape[2]
    past_len = page_table.shape[1] * page

    ck = cache_k.reshape(depth * n_pool, page, D_ATT)
    cv = cache_v.reshape(depth * n_pool, page, D_ATT)
    gi = lax.broadcasted_iota(jnp.int32, (D_ATT, D_ATT), 0) // HEAD_DIM
    gj = lax.broadcasted_iota(jnp.int32, (D_ATT, D_ATT), 1) // HEAD_DIM
    bd = (gi == gj).astype(BF16)

    xp = x_prompt.reshape(bp * seq, d)
    xs = x_sample.reshape(bs * dseq, d)
    tm_p = 512
    row = lambda a: a.reshape(1, -1)
    outs = [[] for _ in range(6)]
    for l in range(depth):
        wg1, wu1 = ffn1_w_gu[l][:, :d_ff].astype(BF16), ffn1_w_gu[l][:, d_ff:].astype(BF16)
        wd1 = ffn1_w_down[l].astype(BF16)
        wg2, wu2 = ffn2_w_gu[l][:, :d_ff].astype(BF16), ffn2_w_gu[l][:, d_ff:].astype(BF16)
        wd2 = ffn2_w_down[l].astype(BF16)
        win = w_in[l].astype(BF16)
        woc, woa = w_out[l][:dc].astype(BF16), w_out[l][dc:].astype(BF16)
        g1, gm, g2 = row(ffn1_norm[l]), row(mix_norm[l]), row(ffn2_norm[l])
        gc, ga = row(conv_out_norm[l]), row(attn_out_norm[l])
        last = l == depth - 1
        gfin = row(final_norm) if last else None

        x1 = _ffn_call(xp, g1, wg1, wu1, wd1, tm=tm_p, name="ffn1_prompt")
        q, k, v, kb, vt, yc, means, cnew = _inproj_prompt(x1, gm, win, conv_w[l], gc, bd, batch=bp, tm=tm_p)
        ya = _attn_prompt(q, kb, vt, means, ga, batch=bp)
        xp = _ffn_call(x1, g2, wg2, wu2, wd2, tm=tm_p, mix=(yc, ya, woc, woa), final=gfin, name="ffn2_prompt")
        outs[0].append(k.reshape(bp, seq, N_HEADS, HEAD_DIM))
        outs[1].append(v.reshape(bp, seq, N_HEADS, HEAD_DIM))
        outs[2].append(cnew)

        st = state_conv[l]
        zpad = jnp.zeros((bs, dseq - (CONV_W - 1), dc), F32)
        s2 = jnp.concatenate([st, zpad], axis=1).reshape(bs * dseq, dc)
        s1 = jnp.concatenate([st[:, 1:2], jnp.zeros((bs, dseq - 1, dc), F32)], axis=1).reshape(bs * dseq, dc)
        x1s = _ffn_call(xs, g1, wg1, wu1, wd1, tm=bs * dseq, name="ffn1_sample")
        qs, ks, vs, ycs, us = _inproj_sample(x1s, gm, win, conv_w[l], gc, bd, s1, s2, seq=dseq)
        r3 = lambda a: a.reshape(bs, dseq, D_ATT)
        yas = _attn_sample(page_table, r3(qs), r3(ks), r3(vs), ga, bd, ck, cv,
                           page_base=l * n_pool, past_len=past_len)
        xs = _ffn_call(x1s, g2, wg2, wu2, wd2, tm=bs * dseq,
                       mix=(ycs, yas.reshape(bs * dseq, D_ATT), woc, woa), final=gfin, name="ffn2_sample")
        outs[3].append(ks.reshape(bs, dseq, N_HEADS, HEAD_DIM))
        outs[4].append(vs.reshape(bs, dseq, N_HEADS, HEAD_DIM))
        outs[5].append(us[:, dseq - (CONV_W - 1):, :])

    y_prompt = xp.reshape(bp, seq, d)
    y_sample = xs.reshape(bs, dseq, d)
    kp, vp, cp, ksn, vsn, csn = (jnp.stack(o) for o in outs)
    return (y_prompt, y_sample, kp, vp, cp, ksn, vsn, csn)
```

```python
import functools

import jax
import jax.numpy as jnp
from jax import lax
from jax.experimental import pallas as pl
from jax.experimental.pallas import tpu as pltpu

F32 = jnp.float32
BF16 = jnp.bfloat16

N_HEADS = 8
HEAD_DIM = 64
D_ATT = N_HEADS * HEAD_DIM
N_CONV_GROUPS = 8
CONV_W = 3
MOBA_BLOCK = 256
MOBA_TOPK = 3
EPS = 1e-5
NEG_INF = -1e30
SCALE = HEAD_DIM ** -0.5
LOG2E = 1.4426950408889634

LANES = 128
HEADS_PER_SLAB = LANES // HEAD_DIM
N_SLABS = D_ATT // LANES
VMEM_LIMIT = 56 * 1024 * 1024

NT_DIMS = (((1,), (1,)), ((), ()))


def _slope(h):
    return 2.0 ** (-(8.0 / N_HEADS) * (h + 1))


def _dot(a, b):
    return jnp.dot(a, b, preferred_element_type=F32)


def _dot_nt(a, b, precision=None):
    return lax.dot_general(a, b, NT_DIMS, precision=precision, preferred_element_type=F32)


def _rms(x, g):
    ms = jnp.mean(x * x, axis=-1, keepdims=True)
    return x * lax.rsqrt(ms + EPS) * g


def _group_sumsq(y, bd_ref):
    y2 = y * y
    hi = y2.astype(BF16)
    lo = (y2 - hi.astype(F32)).astype(BF16)
    bd = bd_ref[...]
    return _dot(hi, bd) + _dot(lo, bd)


def _group_rms(y, g, bd_ref):
    ms = _group_sumsq(y, bd_ref) * (1.0 / HEAD_DIM)
    return y * lax.rsqrt(ms + EPS) * g


def _const_spec(shape):
    nd = len(shape)
    return pl.BlockSpec(shape, lambda *_: (0,) * nd, pipeline_mode=pl.Buffered(1))


def _ffn_body(*refs, mix, final, n_chunks):
    it = iter(refs)
    x_ref = next(it)
    if mix:
        yc_ref, ya_ref, woc_ref, woa_ref = next(it), next(it), next(it), next(it)
    g_ref, wg_ref, wu_ref, wd_ref = next(it), next(it), next(it), next(it)
    gf_ref = next(it) if final else None
    o_ref = next(it)

    x = x_ref[...]
    if mix:
        x = x + _dot(yc_ref[...], woc_ref[...]) + _dot(ya_ref[...], woa_ref[...])
    h = _rms(x, g_ref[...]).astype(BF16)
    d_ff = wg_ref.shape[1]
    cw = d_ff // n_chunks
    acc = jnp.zeros(x.shape, F32)
    for c in range(n_chunks):
        gate = _dot(h, wg_ref[:, c * cw:(c + 1) * cw])
        up = _dot(h, wu_ref[:, c * cw:(c + 1) * cw])
        act = (gate * jax.nn.sigmoid(gate) * up).astype(BF16)
        acc = acc + _dot(act, wd_ref[c * cw:(c + 1) * cw, :])
    x = x + 0.5 * acc
    if final:
        x = _rms(x, gf_ref[...])
    o_ref[...] = x


def _ffn_call(x, g, wg, wu, wd, *, tm, mix=None, final=None, name):
    n, d = x.shape
    d_ff = wg.shape[1]
    row = lambda w: pl.BlockSpec((tm, w), lambda i: (i, 0))
    ins, specs = [x], [row(d)]
    if mix is not None:
        yc, ya, woc, woa = mix
        ins += [yc, ya, woc, woa]
        specs += [row(yc.shape[1]), row(ya.shape[1]), _const_spec(woc.shape), _const_spec(woa.shape)]
    ins += [g, wg, wu, wd]
    specs += [_const_spec(g.shape), _const_spec(wg.shape), _const_spec(wu.shape), _const_spec(wd.shape)]
    if final is not None:
        ins.append(final)
        specs.append(_const_spec(final.shape))
    n_chunks = 2
    assert d_ff % (n_chunks * LANES) == 0
    body = functools.partial(_ffn_body, mix=mix is not None, final=final is not None, n_chunks=n_chunks)
    return pl.pallas_call(
        body,
        grid=(n // tm,),
        in_specs=specs,
        out_specs=row(d),
        out_shape=jax.ShapeDtypeStruct((n, d), F32),
        compiler_params=pltpu.CompilerParams(dimension_semantics=("arbitrary",),
                                             vmem_limit_bytes=VMEM_LIMIT),
        name=name,
    )(*ins)


def _inproj_body(*refs, tm, tiles_per_seq, sample):
    it = iter(refs)
    x_ref, g_ref, win_ref, cw_ref, cn_ref, bd_ref = (next(it) for _ in range(6))
    if sample:
        s1_ref, s2_ref = next(it), next(it)
        q_ref, k_ref, v_ref, yc_ref, u_ref = (next(it) for _ in range(5))
    else:
        q_ref, k_ref, v_ref, kb_ref, vt_ref, yc_ref, mean_ref, cnew_ref = (next(it) for _ in range(8))
    ubuf = next(it)

    dc = yc_ref.shape[1]
    h = _rms(x_ref[...], g_ref[...]).astype(BF16)
    piece = lambda c, w: _dot(h, win_ref[:, c:c + w])
    hc = piece(0, dc)
    bg = piece(dc, dc)
    cg = piece(2 * dc, dc)
    q_ref[...] = piece(3 * dc, D_ATT)
    k = piece(3 * dc + D_ATT, D_ATT)
    v = piece(3 * dc + 2 * D_ATT, D_ATT)
    k_ref[...] = k
    v_ref[...] = v

    u = cg * hc
    if sample:
        ubuf[0:8, :] = jnp.zeros((8, dc), F32)
    else:
        first = (pl.program_id(0) % tiles_per_seq) == 0

        @pl.when(first)
        def _():
            ubuf[0:8, :] = jnp.zeros((8, dc), F32)

        @pl.when(jnp.logical_not(first))
        def _():
            ubuf[0:8, :] = ubuf[tm:tm + 8, :]

    ubuf[8:tm + 8, :] = u
    um1 = ubuf[7:tm + 7, :]
    um2 = ubuf[6:tm + 6, :]
    if sample:
        t = lax.broadcasted_iota(jnp.int32, (tm, dc), 0) % u_ref.shape[1]
        um1 = jnp.where(t >= 1, um1, s1_ref[...])
        um2 = jnp.where(t >= 2, um2, s2_ref[...])
    cw = cw_ref[...]
    conv = um2 * cw[0:1, :] + um1 * cw[1:2, :] + u * cw[2:3, :]
    yc_ref[...] = _group_rms(bg * conv, cn_ref[...], bd_ref).astype(BF16)

    if sample:
        u_ref[...] = u.reshape(u_ref.shape)
    else:
        kb_ref[...] = k.astype(BF16)
        nblk = tm // MOBA_BLOCK
        for i in range(nblk):
            vt_ref[i] = v[i * MOBA_BLOCK:(i + 1) * MOBA_BLOCK, :].T.astype(BF16)
        mean_ref[0] = jnp.sum(k.reshape(nblk, MOBA_BLOCK, D_ATT), axis=1) * (1.0 / MOBA_BLOCK)
        cnew_ref[0] = ubuf[tm + 6:tm + 8, :]


def _inproj_prompt(x, g, w_in, conv_w, conv_norm, bd, *, batch, tm):
    n, d = x.shape
    dc = conv_w.shape[1]
    seq = n // batch
    tps = seq // tm
    nblk = tm // MOBA_BLOCK
    row = lambda w: pl.BlockSpec((tm, w), lambda i: (i, 0))
    f = lambda w, dt: jax.ShapeDtypeStruct((n, w), dt)
    body = functools.partial(_inproj_body, tm=tm, tiles_per_seq=tps, sample=False)
    return pl.pallas_call(
        body,
        grid=(n // tm,),
        in_specs=[row(d), _const_spec(g.shape), _const_spec(w_in.shape), _const_spec(conv_w.shape),
                  _const_spec(conv_norm.shape), _const_spec(bd.shape)],
        out_specs=[row(D_ATT), row(D_ATT), row(D_ATT), row(D_ATT),
                   pl.BlockSpec((nblk, D_ATT, MOBA_BLOCK), lambda i: (i, 0, 0)), row(dc),
                   pl.BlockSpec((1, nblk, D_ATT), lambda i: (i, 0, 0)),
                   pl.BlockSpec((1, CONV_W - 1, dc), lambda i: (i // tps, 0, 0))],
        out_shape=[f(D_ATT, F32), f(D_ATT, F32), f(D_ATT, F32), f(D_ATT, BF16),
                   jax.ShapeDtypeStruct((n // MOBA_BLOCK, D_ATT, MOBA_BLOCK), BF16), f(dc, BF16),
                   jax.ShapeDtypeStruct((n // tm, nblk, D_ATT), F32),
                   jax.ShapeDtypeStruct((batch, CONV_W - 1, dc), F32)],
        scratch_shapes=[pltpu.VMEM((tm + 8, dc), F32)],
        compiler_params=pltpu.CompilerParams(dimension_semantics=("arbitrary",),
                                             vmem_limit_bytes=VMEM_LIMIT),
        name="inproj_prompt",
    )(x, g, w_in, conv_w, conv_norm, bd)


def _inproj_sample(x, g, w_in, conv_w, conv_norm, bd, s1, s2, *, seq):
    n, d = x.shape
    dc = conv_w.shape[1]
    tm = n
    full = lambda shape: pl.BlockSpec(shape, lambda i: (0,) * len(shape))
    f = lambda w, dt: jax.ShapeDtypeStruct((n, w), dt)
    body = functools.partial(_inproj_body, tm=tm, tiles_per_seq=1, sample=True)
    return pl.pallas_call(
        body,
        grid=(1,),
        in_specs=[full((tm, d)), full(g.shape), full(w_in.shape), full(conv_w.shape),
                  full(conv_norm.shape), full(bd.shape), full((tm, dc)), full((tm, dc))],
        out_specs=[full((tm, D_ATT)), full((tm, D_ATT)), full((tm, D_ATT)), full((tm, dc)),
                   full((n // seq, seq, dc))],
        out_shape=[f(D_ATT, F32), f(D_ATT, F32), f(D_ATT, F32), f(dc, BF16),
                   jax.ShapeDtypeStruct((n // seq, seq, dc), F32)],
        scratch_shapes=[pltpu.VMEM((tm + 8, dc), F32)],
        compiler_params=pltpu.CompilerParams(dimension_semantics=("arbitrary",),
                                             vmem_limit_bytes=VMEM_LIMIT),
        name="inproj_sample",
    )(x, g, w_in, conv_w, conv_norm, bd, s1, s2)


def _attn_prompt_body(q_ref, kb_ref, vt_ref, mean_ref, gain_ref, o_ref,
                      bias_own, bias_past, qabt_ref, colb_ref, so_ref, sa_ref, sb_ref, m_ref, l_ref, acc_ref):
    blk = MOBA_BLOCK
    b = pl.program_id(0)
    j = pl.program_id(1)
    nb = mean_ref.shape[1]
    group = LANES // N_HEADS
    qcols = HEADS_PER_SLAB * blk
    lane_q = lax.broadcasted_iota(jnp.int32, (1, qcols), 1)

    def slope_row(p):
        return jnp.where(lane_q < blk, LOG2E * _slope(HEADS_PER_SLAB * p), LOG2E * _slope(HEADS_PER_SLAB * p + 1))

    @pl.when((b == 0) & (j == 0))
    def _init_bias():
        kk = lax.broadcasted_iota(jnp.int32, (blk, qcols), 0)
        qq = lax.broadcasted_iota(jnp.int32, (blk, qcols), 1)
        d = ((qq % blk) - kk).astype(F32)
        for p in range(N_SLABS):
            bias_past[p] = -slope_row(p) * d
            bias_own[p] = jnp.where(d >= 0, -slope_row(p) * d, NEG_INF)

    qt = q_ref[...].T
    means = mean_ref[0]
    if nb < group:
        means = jnp.concatenate([means, jnp.zeros((group - nb, D_ATT), F32)], axis=0)
    mt = jnp.concatenate([means] * N_HEADS, axis=0)
    rh = lax.broadcasted_iota(jnp.int32, mt.shape, 0) // group
    ch = lax.broadcasted_iota(jnp.int32, mt.shape, 1) // HEAD_DIM
    mbd = jnp.where(rh == ch, mt, 0.0)
    gate_t = jnp.dot(mbd, qt, precision=lax.Precision.HIGHEST, preferred_element_type=F32)
    gate = jnp.concatenate([gate_t[h * group:(h + 1) * group, :] for h in range(N_HEADS)], axis=1)

    n_idx = lax.broadcasted_iota(jnp.int32, gate.shape, 0)
    valid = n_idx < j
    gm = jnp.where(valid, gate, NEG_INF)
    rank = jnp.zeros(gate.shape, jnp.int32)
    for m in range(nb):
        other = gm[m:m + 1, :]
        beats = (other > gm) | ((other == gm) & (n_idx > m))
        rank = rank + beats.astype(jnp.int32)
    colb_ref[...] = jnp.where((rank < MOBA_TOPK) & valid, 0.0, NEG_INF)

    row_d = lax.broadcasted_iota(jnp.int32, (LANES, blk), 0)
    for p in range(N_SLABS):
        qs = qt[p * LANES:(p + 1) * LANES, :] * (SCALE * LOG2E)
        qa = jnp.where(row_d < HEAD_DIM, qs, 0.0)
        qb = jnp.where(row_d >= HEAD_DIM, qs, 0.0)
        qabt_ref[p] = jnp.concatenate([qa, qb], axis=1).astype(BF16)

    slabs = [slice(p * LANES, (p + 1) * LANES) for p in range(N_SLABS)]

    def park_scores(n, dst):
        off = pl.multiple_of(n * blk, blk)
        for p in range(N_SLABS):
            dst[p] = _dot(kb_ref[pl.ds(off, blk), slabs[p]], qabt_ref[p])

    def reduce_past(n, src):
        shift = ((j - n) * blk).astype(F32)
        for p in range(N_SLABS):
            crow = colb_ref[pl.ds(n, 1), p * qcols:(p + 1) * qcols] - slope_row(p) * shift
            sb = src[p] + bias_past[p]
            m_prev = m_ref[p]
            m_new = jnp.maximum(m_prev, jnp.max(sb, axis=0, keepdims=True) + crow)
            alpha = jnp.exp2(m_prev - m_new)
            e = jnp.exp2(sb - (m_new - crow))
            m_ref[p] = m_new
            l_ref[p] = alpha * l_ref[p] + jnp.sum(e, axis=0, keepdims=True)
            acc_ref[p] = alpha * acc_ref[p] + _dot(vt_ref[n, slabs[p], :], e.astype(BF16))

    park_scores(j, so_ref)
    park_scores(0, sa_ref)
    for p in range(N_SLABS):
        sb = so_ref[p] + bias_own[p]
        m = jnp.max(sb, axis=0, keepdims=True)
        e = jnp.exp2(sb - m)
        m_ref[p] = m
        l_ref[p] = jnp.sum(e, axis=0, keepdims=True)
        acc_ref[p] = _dot(vt_ref[j, slabs[p], :], e.astype(BF16))

    def two_blocks(i, carry):
        n0 = 2 * i
        park_scores(n0 + 1, sb_ref)
        reduce_past(n0, sa_ref)
        park_scores(jnp.minimum(n0 + 2, j), sa_ref)
        reduce_past(n0 + 1, sb_ref)
        return carry

    lax.fori_loop(0, j // 2, two_blocks, 0)

    @pl.when(j % 2 == 1)
    def _odd_tail():
        reduce_past(j - 1, sa_ref)

    for p in range(N_SLABS):
        o = acc_ref[p] / l_ref[p]
        o2 = jnp.where(row_d < HEAD_DIM, o[:, :blk], o[:, blk:])
        sq = o2 * o2
        ms_a = jnp.sum(sq[:HEAD_DIM], axis=0, keepdims=True) * (1.0 / HEAD_DIM)
        ms_b = jnp.sum(sq[HEAD_DIM:], axis=0, keepdims=True) * (1.0 / HEAD_DIM)
        inv = jnp.where(row_d < HEAD_DIM, lax.rsqrt(ms_a + EPS), lax.rsqrt(ms_b + EPS))
        ls = slice(p * LANES, (p + 1) * LANES)
        o_ref[:, ls] = ((o2 * inv).T * gain_ref[:, ls]).astype(BF16)


def _attn_prompt(q, kb, vt, means, gain, *, batch):
    n = q.shape[0]
    seq = n // batch
    nb = seq // MOBA_BLOCK
    group = LANES // N_HEADS
    assert nb <= group and seq % MOBA_BLOCK == 0
    means = means.reshape(batch, nb, D_ATT)
    qcols = HEADS_PER_SLAB * MOBA_BLOCK
    return pl.pallas_call(
        _attn_prompt_body,
        grid=(batch, nb),
        in_specs=[pl.BlockSpec((MOBA_BLOCK, D_ATT), lambda b, j: (b * nb + j, 0)),
                  pl.BlockSpec((seq, D_ATT), lambda b, j: (b, 0)),
                  pl.BlockSpec((nb, D_ATT, MOBA_BLOCK), lambda b, j: (b, 0, 0)),
                  pl.BlockSpec((1, nb, D_ATT), lambda b, j: (b, 0, 0)),
                  pl.BlockSpec(gain.shape, lambda b, j: (0, 0))],
        out_specs=pl.BlockSpec((MOBA_BLOCK, D_ATT), lambda b, j: (b * nb + j, 0)),
        out_shape=jax.ShapeDtypeStruct((n, D_ATT), BF16),
        scratch_shapes=[pltpu.VMEM((N_SLABS, MOBA_BLOCK, qcols), F32),
                        pltpu.VMEM((N_SLABS, MOBA_BLOCK, qcols), F32),
                        pltpu.VMEM((N_SLABS, LANES, qcols), BF16),
                        pltpu.VMEM((group, N_HEADS * MOBA_BLOCK), F32),
                        pltpu.VMEM((N_SLABS, MOBA_BLOCK, qcols), F32),
                        pltpu.VMEM((N_SLABS, MOBA_BLOCK, qcols), F32),
                        pltpu.VMEM((N_SLABS, MOBA_BLOCK, qcols), F32),
                        pltpu.VMEM((N_SLABS, 1, qcols), F32),
                        pltpu.VMEM((N_SLABS, 1, qcols), F32),
                        pltpu.VMEM((N_SLABS, LANES, qcols), F32)],
        compiler_params=pltpu.CompilerParams(dimension_semantics=("arbitrary", "arbitrary"),
                                             vmem_limit_bytes=VMEM_LIMIT),
        name="attn_prompt",
    )(q, kb, vt, means, gain)


def _attn_sample_t_body(pt_ref, q_ref, kn_ref, vn_ref, gain_ref, ckt_hbm, cvt_hbm, o_ref,
                        kbuf, vbuf, s_ref, ksem, vsem, *, past_len, page, page_base):
    b = pl.program_id(0)
    nbat = pl.num_programs(0)
    n_pages = past_len // page
    ppb = MOBA_BLOCK // page
    nb = past_len // MOBA_BLOCK
    t_new = q_ref.shape[2]
    hts = (N_HEADS, t_new, 1)

    def page_copy(hbm, buf, sem, bb, pg):
        return pltpu.make_async_copy(hbm.at[page_base + pt_ref[bb, pg]], buf.at[pg], sem.at[pg])

    def start_all(hbm, buf, sem, bb):
        def body(pg, c):
            page_copy(hbm, buf, sem, bb, pg).start()
            return c
        lax.fori_loop(0, n_pages, body, 0)

    @pl.when(b == 0)
    def _():
        start_all(ckt_hbm, kbuf, ksem, b)
        start_all(cvt_hbm, vbuf, vsem, b)

    qs = q_ref[0] * SCALE
    q_hi = qs.astype(BF16)
    q_lo = (qs - q_hi.astype(F32)).astype(BF16)
    qq = jnp.concatenate([q_hi, q_lo], axis=1)

    def qk(kt):
        s2 = lax.dot_general(qq, kt, (((2,), (1,)), ((0,), (0,))), preferred_element_type=F32)
        return s2[:, :t_new] + s2[:, t_new:]

    def k_page(pg, c):
        page_copy(ckt_hbm, kbuf, ksem, b, pg).wait()
        s_ref[pg] = qk(kbuf[pg].astype(BF16))
        return c
    lax.fori_loop(0, n_pages, k_page, 0)

    @pl.when(b + 1 < nbat)
    def _():
        start_all(ckt_hbm, kbuf, ksem, b + 1)

    lane = lax.broadcasted_iota(jnp.int32, (N_HEADS, t_new, LANES), 2)
    gate = jnp.zeros((N_HEADS, t_new, LANES), F32)
    gcols = []
    for n in range(nb):
        tot = s_ref[n * ppb]
        for i in range(1, ppb):
            tot = tot + s_ref[n * ppb + i]
        g = jnp.sum(tot, axis=2, keepdims=True)
        gcols.append(g)
        gate = jnp.where(lane == n, g, gate)
    rank = jnp.zeros(gate.shape, jnp.int32)
    for m in range(nb):
        beats = (gcols[m] > gate) | ((gcols[m] == gate) & (lane > m))
        rank = rank + beats.astype(jnp.int32)
    colb = jnp.where(rank < MOBA_TOPK, 0.0, NEG_INF)

    hidx = lax.broadcasted_iota(jnp.int32, hts, 0)
    slope = jnp.zeros(hts, F32)
    for h in range(N_HEADS):
        slope = jnp.where(hidx == h, _slope(h), slope)
    tq = lax.broadcasted_iota(jnp.int32, (1, t_new, LANES), 1)
    ln = lax.broadcasted_iota(jnp.int32, (1, t_new, LANES), 2)

    zpad = jnp.zeros((N_HEADS, LANES - t_new, HEAD_DIM), F32)
    kn = jnp.concatenate([kn_ref[0], zpad], axis=1).astype(BF16)
    vn = jnp.concatenate([vn_ref[0], zpad], axis=1).astype(BF16)
    s2 = lax.dot_general(qq, kn, (((2,), (2,)), ((0,), (0,))), preferred_element_type=F32)
    s_own = s2[:, :t_new] + s2[:, t_new:]
    s_own = jnp.where(ln <= tq, s_own - slope * (tq - ln).astype(F32), NEG_INF)

    mrun = s_own
    for n in range(nb):
        mask_n = jnp.sum(jnp.where(lane == n, colb, 0.0), axis=2, keepdims=True)
        for i in range(ppb):
            pg = n * ppb + i
            dist = (past_len - pg * page + tq - ln).astype(F32)
            sn = s_ref[pg] - slope * dist + mask_n
            s_ref[pg] = sn
            mrun = jnp.maximum(mrun, sn)
    m = jnp.max(mrun, axis=2, keepdims=True)

    e_own = jnp.exp(s_own - m)
    lrun = e_own
    acc = lax.dot_general(e_own.astype(BF16), vn, (((2,), (1,)), ((0,), (0,))), preferred_element_type=F32)
    for pg in range(n_pages):
        e = jnp.exp(s_ref[pg] - m)
        s_ref[pg] = e
        lrun = lrun + e
    l = jnp.sum(lrun, axis=2, keepdims=True)

    def v_page(pg, acc):
        page_copy(cvt_hbm, vbuf, vsem, b, pg).wait()
        p = s_ref[pg].astype(BF16)
        vt = vbuf[pg].astype(BF16)
        return acc + lax.dot_general(p, vt, (((2,), (2,)), ((0,), (0,))), preferred_element_type=F32)
    acc = lax.fori_loop(0, n_pages, v_page, acc)

    @pl.when(b + 1 < nbat)
    def _():
        start_all(cvt_hbm, vbuf, vsem, b + 1)

    out = acc / l
    ms = jnp.mean(out * out, axis=2, keepdims=True)
    o_ref[0] = (out * lax.rsqrt(ms + EPS) * gain_ref[...]).astype(BF16)


def _attn_sample_t(page_table, q, kn, vn, gain, cache_kt, cache_vt, *, page_base, past_len):
    nbat, _, t_new, _ = q.shape
    page = cache_kt.shape[3]
    n_pages = past_len // page
    assert past_len % MOBA_BLOCK == 0 and MOBA_BLOCK % page == 0 and page == LANES
    assert t_new <= LANES and t_new % 8 == 0 and past_len // MOBA_BLOCK <= LANES
    body = functools.partial(_attn_sample_t_body, past_len=past_len, page=page, page_base=page_base)
    per_b = pl.BlockSpec((1, N_HEADS, t_new, HEAD_DIM), lambda b, pt: (b, 0, 0, 0))
    grid_spec = pltpu.PrefetchScalarGridSpec(
        num_scalar_prefetch=1,
        grid=(nbat,),
        in_specs=[per_b, per_b, per_b,
                  pl.BlockSpec(gain.shape, lambda b, pt: (0, 0, 0)),
                  pl.BlockSpec(memory_space=pl.ANY),
                  pl.BlockSpec(memory_space=pl.ANY)],
        out_specs=per_b,
        scratch_shapes=[pltpu.VMEM((n_pages, N_HEADS, HEAD_DIM, page), F32),
                        pltpu.VMEM((n_pages, N_HEADS, HEAD_DIM, page), F32),
                        pltpu.VMEM((n_pages, N_HEADS, t_new, page), F32),
                        pltpu.SemaphoreType.DMA((n_pages,)),
                        pltpu.SemaphoreType.DMA((n_pages,))],
    )
    return pl.pallas_call(
        body,
        grid_spec=grid_spec,
        out_shape=jax.ShapeDtypeStruct((nbat, N_HEADS, t_new, HEAD_DIM), BF16),
        compiler_params=pltpu.CompilerParams(dimension_semantics=("arbitrary",),
                                             vmem_limit_bytes=VMEM_LIMIT),
        name="attn_sample",
    )(page_table, q, kn, vn, gain, cache_kt, cache_vt)


def kernel(x_prompt, x_sample, cache_k, cache_v, state_conv, page_table, ffn1_norm, ffn1_w_gu, ffn1_w_down,
           mix_norm, w_in, conv_w, conv_out_norm, attn_out_norm, w_out, ffn2_norm, ffn2_w_gu, ffn2_w_down,
           final_norm):
    bp, seq, d = x_prompt.shape
    bs, dseq, _ = x_sample.shape
    depth, n_pool, page = cache_k.shape[:3]
    d_ff = ffn1_w_down.shape[1]
    dc = conv_w.shape[2]
    past_len = page_table.shape[1] * page

    ck = jnp.transpose(cache_k, (0, 1, 3, 4, 2)).reshape(depth * n_pool, N_HEADS, HEAD_DIM, page)
    cv = jnp.transpose(cache_v, (0, 1, 3, 4, 2)).reshape(depth * n_pool, N_HEADS, HEAD_DIM, page)
    gi = lax.broadcasted_iota(jnp.int32, (D_ATT, D_ATT), 0) // HEAD_DIM
    gj = lax.broadcasted_iota(jnp.int32, (D_ATT, D_ATT), 1) // HEAD_DIM
    bd = (gi == gj).astype(BF16)

    xp = x_prompt.reshape(bp * seq, d)
    xs = x_sample.reshape(bs * dseq, d)
    tm_p = 512
    row = lambda a: a.reshape(1, -1)
    outs = [[] for _ in range(6)]
    for l in range(depth):
        wg1, wu1 = ffn1_w_gu[l][:, :d_ff].astype(BF16), ffn1_w_gu[l][:, d_ff:].astype(BF16)
        wd1 = ffn1_w_down[l].astype(BF16)
        wg2, wu2 = ffn2_w_gu[l][:, :d_ff].astype(BF16), ffn2_w_gu[l][:, d_ff:].astype(BF16)
        wd2 = ffn2_w_down[l].astype(BF16)
        win = w_in[l].astype(BF16)
        woc, woa = w_out[l][:dc].astype(BF16), w_out[l][dc:].astype(BF16)
        g1, gm, g2 = row(ffn1_norm[l]), row(mix_norm[l]), row(ffn2_norm[l])
        gc, ga = row(conv_out_norm[l]), row(attn_out_norm[l])
        last = l == depth - 1
        gfin = row(final_norm) if last else None

        x1 = _ffn_call(xp, g1, wg1, wu1, wd1, tm=tm_p, name="ffn1_prompt")
        q, k, v, kb, vt, yc, means, cnew = _inproj_prompt(x1, gm, win, conv_w[l], gc, bd, batch=bp, tm=tm_p)
        ya = _attn_prompt(q, kb, vt, means, ga, batch=bp)
        xp = _ffn_call(x1, g2, wg2, wu2, wd2, tm=tm_p, mix=(yc, ya, woc, woa), final=gfin, name="ffn2_prompt")
        outs[0].append(k.reshape(bp, seq, N_HEADS, HEAD_DIM))
        outs[1].append(v.reshape(bp, seq, N_HEADS, HEAD_DIM))
        outs[2].append(cnew)

        st = state_conv[l]
        zpad = jnp.zeros((bs, dseq - (CONV_W - 1), dc), F32)
        s2 = jnp.concatenate([st, zpad], axis=1).reshape(bs * dseq, dc)
        s1 = jnp.concatenate([st[:, 1:2], jnp.zeros((bs, dseq - 1, dc), F32)], axis=1).reshape(bs * dseq, dc)
        x1s = _ffn_call(xs, g1, wg1, wu1, wd1, tm=bs * dseq, name="ffn1_sample")
        qs, ks, vs, ycs, us = _inproj_sample(x1s, gm, win, conv_w[l], gc, bd, s1, s2, seq=dseq)
        hm = lambda a: a.reshape(bs, dseq, N_HEADS, HEAD_DIM).transpose(0, 2, 1, 3)
        yas = _attn_sample_t(page_table, hm(qs), hm(ks), hm(vs), ga.reshape(N_HEADS, 1, HEAD_DIM), ck, cv,
                             page_base=l * n_pool, past_len=past_len)
        yas = yas.transpose(0, 2, 1, 3)
        xs = _ffn_call(x1s, g2, wg2, wu2, wd2, tm=bs * dseq,
                       mix=(ycs, yas.reshape(bs * dseq, D_ATT), woc, woa), final=gfin, name="ffn2_sample")
        outs[3].append(ks.reshape(bs, dseq, N_HEADS, HEAD_DIM))
        outs[4].append(vs.reshape(bs, dseq, N_HEADS, HEAD_DIM))
        outs[5].append(us[:, dseq - (CONV_W - 1):, :])

    y_prompt = xp.reshape(bp, seq, d)
    y_sample = xs.reshape(bs, dseq, d)
    kp, vp, cp, ksn, vsn, csn = (jnp.stack(o) for o in outs)
    return (y_prompt, y_sample, kp, vp, cp, ksn, vsn, csn)
```

```python
import functools

import jax
import jax.numpy as jnp
from jax import lax
from jax.experimental import pallas as pl
from jax.experimental.pallas import tpu as pltpu

F32 = jnp.float32
BF16 = jnp.bfloat16

N_HEADS = 8
HEAD_DIM = 64
D_ATT = N_HEADS * HEAD_DIM
N_CONV_GROUPS = 8
CONV_W = 3
MOBA_BLOCK = 256
MOBA_TOPK = 3
EPS = 1e-5
NEG_INF = -1e30
SCALE = HEAD_DIM ** -0.5
LOG2E = 1.4426950408889634

LANES = 128
HEADS_PER_SLAB = LANES // HEAD_DIM
N_SLABS = D_ATT // LANES
VMEM_LIMIT = 56 * 1024 * 1024
K_TILE = 8
V_TILE = 8

NT_DIMS = (((1,), (1,)), ((), ()))


def _slope(h):
    return 2.0 ** (-(8.0 / N_HEADS) * (h + 1))


def _dot(a, b):
    return jnp.dot(a, b, preferred_element_type=F32)


def _dot_nt(a, b, precision=None):
    return lax.dot_general(a, b, NT_DIMS, precision=precision, preferred_element_type=F32)


def _rms(x, g):
    ms = jnp.mean(x * x, axis=-1, keepdims=True)
    return x * lax.rsqrt(ms + EPS) * g


def _group_sumsq(y, bd_ref):
    y2 = y * y
    hi = y2.astype(BF16)
    lo = (y2 - hi.astype(F32)).astype(BF16)
    bd = bd_ref[...]
    return _dot(hi, bd) + _dot(lo, bd)


def _group_rms(y, g, bd_ref):
    ms = _group_sumsq(y, bd_ref) * (1.0 / HEAD_DIM)
    return y * lax.rsqrt(ms + EPS) * g


def _const_spec(shape):
    nd = len(shape)
    return pl.BlockSpec(shape, lambda *_: (0,) * nd, pipeline_mode=pl.Buffered(1))


def _ffn_body(*refs, mix, final, n_chunks):
    it = iter(refs)
    x_ref = next(it)
    if mix:
        yc_ref, ya_ref, woc_ref, woa_ref = next(it), next(it), next(it), next(it)
    g_ref, wg_ref, wu_ref, wd_ref = next(it), next(it), next(it), next(it)
    gf_ref = next(it) if final else None
    o_ref = next(it)

    x = x_ref[...]
    if mix:
        x = x + _dot(yc_ref[...], woc_ref[...]) + _dot(ya_ref[...], woa_ref[...])
    h = _rms(x, g_ref[...]).astype(BF16)
    d_ff = wg_ref.shape[1]
    cw = d_ff // n_chunks
    acc = jnp.zeros(x.shape, F32)
    for c in range(n_chunks):
        gate = _dot(h, wg_ref[:, c * cw:(c + 1) * cw])
        up = _dot(h, wu_ref[:, c * cw:(c + 1) * cw])
        act = (gate * jax.nn.sigmoid(gate) * up).astype(BF16)
        acc = acc + _dot(act, wd_ref[c * cw:(c + 1) * cw, :])
    x = x + 0.5 * acc
    if final:
        x = _rms(x, gf_ref[...])
    o_ref[...] = x


def _ffn_call(x, g, wg, wu, wd, *, tm, mix=None, final=None, name):
    n, d = x.shape
    d_ff = wg.shape[1]
    row = lambda w: pl.BlockSpec((tm, w), lambda i: (i, 0))
    ins, specs = [x], [row(d)]
    if mix is not None:
        yc, ya, woc, woa = mix
        ins += [yc, ya, woc, woa]
        specs += [row(yc.shape[1]), row(ya.shape[1]), _const_spec(woc.shape), _const_spec(woa.shape)]
    ins += [g, wg, wu, wd]
    specs += [_const_spec(g.shape), _const_spec(wg.shape), _const_spec(wu.shape), _const_spec(wd.shape)]
    if final is not None:
        ins.append(final)
        specs.append(_const_spec(final.shape))
    n_chunks = 2
    assert d_ff % (n_chunks * LANES) == 0
    body = functools.partial(_ffn_body, mix=mix is not None, final=final is not None, n_chunks=n_chunks)
    return pl.pallas_call(
        body,
        grid=(n // tm,),
        in_specs=specs,
        out_specs=row(d),
        out_shape=jax.ShapeDtypeStruct((n, d), F32),
        compiler_params=pltpu.CompilerParams(dimension_semantics=("arbitrary",),
                                             vmem_limit_bytes=VMEM_LIMIT),
        name=name,
    )(*ins)


def _inproj_body(*refs, tm, tiles_per_seq, sample):
    it = iter(refs)
    x_ref, g_ref, win_ref, cw_ref, cn_ref, bd_ref = (next(it) for _ in range(6))
    if sample:
        s1_ref, s2_ref = next(it), next(it)
        q_ref, k_ref, v_ref, yc_ref, u_ref = (next(it) for _ in range(5))
    else:
        q_ref, kt_ref, vtf_ref, kb_ref, vt_ref, yc_ref, mean_ref, cnew_ref = (next(it) for _ in range(8))
    ubuf = next(it)

    dc = yc_ref.shape[1]
    h = _rms(x_ref[...], g_ref[...]).astype(BF16)
    piece = lambda c, w: _dot(h, win_ref[:, c:c + w])
    hc = piece(0, dc)
    bg = piece(dc, dc)
    cg = piece(2 * dc, dc)
    q_ref[...] = piece(3 * dc, D_ATT)
    k = piece(3 * dc + D_ATT, D_ATT)
    v = piece(3 * dc + 2 * D_ATT, D_ATT)
    if sample:
        k_ref[...] = k
        v_ref[...] = v

    u = cg * hc
    if sample:
        ubuf[0:8, :] = jnp.zeros((8, dc), F32)
    else:
        first = (pl.program_id(0) % tiles_per_seq) == 0

        @pl.when(first)
        def _():
            ubuf[0:8, :] = jnp.zeros((8, dc), F32)

        @pl.when(jnp.logical_not(first))
        def _():
            ubuf[0:8, :] = ubuf[tm:tm + 8, :]

    ubuf[8:tm + 8, :] = u
    um1 = ubuf[7:tm + 7, :]
    um2 = ubuf[6:tm + 6, :]
    if sample:
        t = lax.broadcasted_iota(jnp.int32, (tm, dc), 0) % u_ref.shape[1]
        um1 = jnp.where(t >= 1, um1, s1_ref[...])
        um2 = jnp.where(t >= 2, um2, s2_ref[...])
    cw = cw_ref[...]
    conv = um2 * cw[0:1, :] + um1 * cw[1:2, :] + u * cw[2:3, :]
    yc_ref[...] = _group_rms(bg * conv, cn_ref[...], bd_ref).astype(BF16)

    if sample:
        u_ref[...] = u.reshape(u_ref.shape)
    else:
        vt = v.T
        kt_ref[0] = k.T
        vtf_ref[0] = vt
        kb_ref[...] = k.astype(BF16)
        nblk = tm // MOBA_BLOCK
        for i in range(nblk):
            vt_ref[i] = vt[:, i * MOBA_BLOCK:(i + 1) * MOBA_BLOCK].astype(BF16)
        mean_ref[0] = jnp.sum(k.reshape(nblk, MOBA_BLOCK, D_ATT), axis=1) * (1.0 / MOBA_BLOCK)
        cnew_ref[0] = ubuf[tm + 6:tm + 8, :]


def _inproj_prompt(x, g, w_in, conv_w, conv_norm, bd, *, batch, tm):
    n, d = x.shape
    dc = conv_w.shape[1]
    seq = n // batch
    tps = seq // tm
    nblk = tm // MOBA_BLOCK
    row = lambda w: pl.BlockSpec((tm, w), lambda i: (i, 0))
    tok_minor = pl.BlockSpec((1, D_ATT, tm), lambda i: (i // tps, 0, i % tps))
    f = lambda w, dt: jax.ShapeDtypeStruct((n, w), dt)
    body = functools.partial(_inproj_body, tm=tm, tiles_per_seq=tps, sample=False)
    return pl.pallas_call(
        body,
        grid=(n // tm,),
        in_specs=[row(d), _const_spec(g.shape), _const_spec(w_in.shape), _const_spec(conv_w.shape),
                  _const_spec(conv_norm.shape), _const_spec(bd.shape)],
        out_specs=[row(D_ATT), tok_minor, tok_minor, row(D_ATT),
                   pl.BlockSpec((nblk, D_ATT, MOBA_BLOCK), lambda i: (i, 0, 0)), row(dc),
                   pl.BlockSpec((1, nblk, D_ATT), lambda i: (i, 0, 0)),
                   pl.BlockSpec((1, CONV_W - 1, dc), lambda i: (i // tps, 0, 0))],
        out_shape=[f(D_ATT, F32), jax.ShapeDtypeStruct((batch, D_ATT, seq), F32),
                   jax.ShapeDtypeStruct((batch, D_ATT, seq), F32), f(D_ATT, BF16),
                   jax.ShapeDtypeStruct((n // MOBA_BLOCK, D_ATT, MOBA_BLOCK), BF16), f(dc, BF16),
                   jax.ShapeDtypeStruct((n // tm, nblk, D_ATT), F32),
                   jax.ShapeDtypeStruct((batch, CONV_W - 1, dc), F32)],
        scratch_shapes=[pltpu.VMEM((tm + 8, dc), F32)],
        compiler_params=pltpu.CompilerParams(dimension_semantics=("arbitrary",),
                                             vmem_limit_bytes=VMEM_LIMIT),
        name="inproj_prompt",
    )(x, g, w_in, conv_w, conv_norm, bd)


def _inproj_sample(x, g, w_in, conv_w, conv_norm, bd, s1, s2, *, seq):
    n, d = x.shape
    dc = conv_w.shape[1]
    tm = n
    full = lambda shape: pl.BlockSpec(shape, lambda i: (0,) * len(shape))
    f = lambda w, dt: jax.ShapeDtypeStruct((n, w), dt)
    body = functools.partial(_inproj_body, tm=tm, tiles_per_seq=1, sample=True)
    return pl.pallas_call(
        body,
        grid=(1,),
        in_specs=[full((tm, d)), full(g.shape), full(w_in.shape), full(conv_w.shape),
                  full(conv_norm.shape), full(bd.shape), full((tm, dc)), full((tm, dc))],
        out_specs=[full((tm, D_ATT)), full((tm, D_ATT)), full((tm, D_ATT)), full((tm, dc)),
                   full((n // seq, seq, dc))],
        out_shape=[f(D_ATT, F32), f(D_ATT, F32), f(D_ATT, F32), f(dc, BF16),
                   jax.ShapeDtypeStruct((n // seq, seq, dc), F32)],
        scratch_shapes=[pltpu.VMEM((tm + 8, dc), F32)],
        compiler_params=pltpu.CompilerParams(dimension_semantics=("arbitrary",),
                                             vmem_limit_bytes=VMEM_LIMIT),
        name="inproj_sample",
    )(x, g, w_in, conv_w, conv_norm, bd, s1, s2)


def _attn_prompt_body(q_ref, kb_ref, vt_ref, mean_ref, gain_ref, o_ref,
                      bias_own, bias_past, qabt_ref, colb_ref, so_ref, sa_ref, sb_ref, m_ref, l_ref, acc_ref):
    blk = MOBA_BLOCK
    b = pl.program_id(0)
    j = pl.program_id(1)
    nb = mean_ref.shape[1]
    group = LANES // N_HEADS
    qcols = HEADS_PER_SLAB * blk
    lane_q = lax.broadcasted_iota(jnp.int32, (1, qcols), 1)

    def slope_row(p):
        return jnp.where(lane_q < blk, LOG2E * _slope(HEADS_PER_SLAB * p), LOG2E * _slope(HEADS_PER_SLAB * p + 1))

    @pl.when((b == 0) & (j == 0))
    def _init_bias():
        kk = lax.broadcasted_iota(jnp.int32, (blk, qcols), 0)
        qq = lax.broadcasted_iota(jnp.int32, (blk, qcols), 1)
        d = ((qq % blk) - kk).astype(F32)
        for p in range(N_SLABS):
            bias_past[p] = -slope_row(p) * d
            bias_own[p] = jnp.where(d >= 0, -slope_row(p) * d, NEG_INF)

    qt = q_ref[...].T
    means = mean_ref[0]
    if nb < group:
        means = jnp.concatenate([means, jnp.zeros((group - nb, D_ATT), F32)], axis=0)
    mt = jnp.concatenate([means] * N_HEADS, axis=0)
    rh = lax.broadcasted_iota(jnp.int32, mt.shape, 0) // group
    ch = lax.broadcasted_iota(jnp.int32, mt.shape, 1) // HEAD_DIM
    mbd = jnp.where(rh == ch, mt, 0.0)
    gate_t = jnp.dot(mbd, qt, precision=lax.Precision.HIGHEST, preferred_element_type=F32)
    gate = jnp.concatenate([gate_t[h * group:(h + 1) * group, :] for h in range(N_HEADS)], axis=1)

    n_idx = lax.broadcasted_iota(jnp.int32, gate.shape, 0)
    valid = n_idx < j
    gm = jnp.where(valid, gate, NEG_INF)
    rank = jnp.zeros(gate.shape, jnp.int32)
    for m in range(nb):
        other = gm[m:m + 1, :]
        beats = (other > gm) | ((other == gm) & (n_idx > m))
        rank = rank + beats.astype(jnp.int32)
    colb_ref[...] = jnp.where((rank < MOBA_TOPK) & valid, 0.0, NEG_INF)

    row_d = lax.broadcasted_iota(jnp.int32, (LANES, blk), 0)
    for p in range(N_SLABS):
        qs = qt[p * LANES:(p + 1) * LANES, :] * (SCALE * LOG2E)
        qa = jnp.where(row_d < HEAD_DIM, qs, 0.0)
        qb = jnp.where(row_d >= HEAD_DIM, qs, 0.0)
        qabt_ref[p] = jnp.concatenate([qa, qb], axis=1).astype(BF16)

    slabs = [slice(p * LANES, (p + 1) * LANES) for p in range(N_SLABS)]

    def park_scores(n, dst):
        off = pl.multiple_of(n * blk, blk)
        for p in range(N_SLABS):
            dst[p] = _dot(kb_ref[pl.ds(off, blk), slabs[p]], qabt_ref[p])

    def reduce_past(n, src):
        shift = ((j - n) * blk).astype(F32)
        for p in range(N_SLABS):
            crow = colb_ref[pl.ds(n, 1), p * qcols:(p + 1) * qcols] - slope_row(p) * shift
            sb = src[p] + bias_past[p]
            m_prev = m_ref[p]
            m_new = jnp.maximum(m_prev, jnp.max(sb, axis=0, keepdims=True) + crow)
            alpha = jnp.exp2(m_prev - m_new)
            e = jnp.exp2(sb - (m_new - crow))
            m_ref[p] = m_new
            l_ref[p] = alpha * l_ref[p] + jnp.sum(e, axis=0, keepdims=True)
            acc_ref[p] = alpha * acc_ref[p] + _dot(vt_ref[n, slabs[p], :], e.astype(BF16))

    park_scores(j, so_ref)
    park_scores(0, sa_ref)
    for p in range(N_SLABS):
        sb = so_ref[p] + bias_own[p]
        m = jnp.max(sb, axis=0, keepdims=True)
        e = jnp.exp2(sb - m)
        m_ref[p] = m
        l_ref[p] = jnp.sum(e, axis=0, keepdims=True)
        acc_ref[p] = _dot(vt_ref[j, slabs[p], :], e.astype(BF16))

    def two_blocks(i, carry):
        n0 = 2 * i
        park_scores(n0 + 1, sb_ref)
        reduce_past(n0, sa_ref)
        park_scores(jnp.minimum(n0 + 2, j), sa_ref)
        reduce_past(n0 + 1, sb_ref)
        return carry

    lax.fori_loop(0, j // 2, two_blocks, 0)

    @pl.when(j % 2 == 1)
    def _odd_tail():
        reduce_past(j - 1, sa_ref)

    for p in range(N_SLABS):
        o = acc_ref[p] / l_ref[p]
        o2 = jnp.where(row_d < HEAD_DIM, o[:, :blk], o[:, blk:])
        sq = o2 * o2
        ms_a = jnp.sum(sq[:HEAD_DIM], axis=0, keepdims=True) * (1.0 / HEAD_DIM)
        ms_b = jnp.sum(sq[HEAD_DIM:], axis=0, keepdims=True) * (1.0 / HEAD_DIM)
        inv = jnp.where(row_d < HEAD_DIM, lax.rsqrt(ms_a + EPS), lax.rsqrt(ms_b + EPS))
        ls = slice(p * LANES, (p + 1) * LANES)
        o_ref[:, ls] = ((o2 * inv).T * gain_ref[:, ls]).astype(BF16)


def _attn_prompt(q, kb, vt, means, gain, *, batch):
    n = q.shape[0]
    seq = n // batch
    nb = seq // MOBA_BLOCK
    group = LANES // N_HEADS
    assert nb <= group and seq % MOBA_BLOCK == 0
    means = means.reshape(batch, nb, D_ATT)
    qcols = HEADS_PER_SLAB * MOBA_BLOCK
    return pl.pallas_call(
        _attn_prompt_body,
        grid=(batch, nb),
        in_specs=[pl.BlockSpec((MOBA_BLOCK, D_ATT), lambda b, j: (b * nb + j, 0)),
                  pl.BlockSpec((seq, D_ATT), lambda b, j: (b, 0)),
                  pl.BlockSpec((nb, D_ATT, MOBA_BLOCK), lambda b, j: (b, 0, 0)),
                  pl.BlockSpec((1, nb, D_ATT), lambda b, j: (b, 0, 0)),
                  pl.BlockSpec(gain.shape, lambda b, j: (0, 0))],
        out_specs=pl.BlockSpec((MOBA_BLOCK, D_ATT), lambda b, j: (b * nb + j, 0)),
        out_shape=jax.ShapeDtypeStruct((n, D_ATT), BF16),
        scratch_shapes=[pltpu.VMEM((N_SLABS, MOBA_BLOCK, qcols), F32),
                        pltpu.VMEM((N_SLABS, MOBA_BLOCK, qcols), F32),
                        pltpu.VMEM((N_SLABS, LANES, qcols), BF16),
                        pltpu.VMEM((group, N_HEADS * MOBA_BLOCK), F32),
                        pltpu.VMEM((N_SLABS, MOBA_BLOCK, qcols), F32),
                        pltpu.VMEM((N_SLABS, MOBA_BLOCK, qcols), F32),
                        pltpu.VMEM((N_SLABS, MOBA_BLOCK, qcols), F32),
                        pltpu.VMEM((N_SLABS, 1, qcols), F32),
                        pltpu.VMEM((N_SLABS, 1, qcols), F32),
                        pltpu.VMEM((N_SLABS, LANES, qcols), F32)],
        compiler_params=pltpu.CompilerParams(dimension_semantics=("arbitrary", "arbitrary"),
                                             vmem_limit_bytes=VMEM_LIMIT),
        name="attn_prompt",
    )(q, kb, vt, means, gain)


def _attn_sample_t_body(pt_ref, q_ref, kn_ref, vn_ref, gain_ref, bd_ref, ckt_hbm, cvt_hbm, o_ref,
                        kbuf, vbuf, s_ref, acc_ref, ksem, vsem, *, past_len, page, page_base):
    b = pl.program_id(0)
    nbat = pl.num_programs(0)
    n_pages = past_len // page
    ppb = MOBA_BLOCK // page
    nb = past_len // MOBA_BLOCK
    t_new = q_ref.shape[1]
    rows = N_HEADS * t_new

    def page_copy(hbm, buf, sem, bb, pg):
        return pltpu.make_async_copy(hbm.at[page_base + pt_ref[bb, pg]], buf.at[pg], sem.at[pg])

    def start_all(hbm, buf, sem, bb):
        def body(pg, c):
            page_copy(hbm, buf, sem, bb, pg).start()
            return c
        lax.fori_loop(0, n_pages, body, 0)

    @pl.when(b == 0)
    def _():
        start_all(ckt_hbm, kbuf, ksem, b)
        start_all(cvt_hbm, vbuf, vsem, b)

    qt = jnp.concatenate([q_ref[0]] * N_HEADS, axis=0)
    rh = lax.broadcasted_iota(jnp.int32, qt.shape, 0) // t_new
    ch = lax.broadcasted_iota(jnp.int32, qt.shape, 1) // HEAD_DIM
    qs = jnp.where(rh == ch, qt, 0.0) * SCALE
    q_hi = qs.astype(BF16)
    q_lo = (qs - q_hi.astype(F32)).astype(BF16)
    qq = jnp.concatenate([q_hi, q_lo], axis=0)

    def k_tile(i, c):
        pgs = [i * K_TILE + k for k in range(K_TILE)]
        for pg in pgs:
            page_copy(ckt_hbm, kbuf, ksem, b, pg).wait()
        for pg in pgs:
            s2 = _dot(qq, kbuf[pg].reshape(D_ATT, page).astype(BF16))
            s_ref[pg] = s2[:rows] + s2[rows:]
        return c
    lax.fori_loop(0, n_pages // K_TILE, k_tile, 0)

    @pl.when(b + 1 < nbat)
    def _():
        start_all(ckt_hbm, kbuf, ksem, b + 1)

    lane = lax.broadcasted_iota(jnp.int32, (rows, LANES), 1)
    gate = jnp.zeros((rows, LANES), F32)
    gcols = []
    for n in range(nb):
        tot = s_ref[n * ppb]
        for i in range(1, ppb):
            tot = tot + s_ref[n * ppb + i]
        g = jnp.sum(tot, axis=1, keepdims=True)
        gcols.append(g)
        gate = jnp.where(lane == n, g, gate)
    rank = jnp.zeros(gate.shape, jnp.int32)
    for m in range(nb):
        beats = (gcols[m] > gate) | ((gcols[m] == gate) & (lane > m))
        rank = rank + beats.astype(jnp.int32)
    colb = jnp.where(rank < MOBA_TOPK, 0.0, NEG_INF)

    r1 = lax.broadcasted_iota(jnp.int32, (rows, 1), 0)
    tq = r1 % t_new
    slope = jnp.zeros((rows, 1), F32)
    for h in range(N_HEADS):
        slope = jnp.where(r1 // t_new == h, _slope(h), slope)
    in_page = slope * (tq - lane).astype(F32)

    zpad = jnp.zeros((LANES - t_new, D_ATT), F32)
    kn = jnp.concatenate([kn_ref[0], zpad], axis=0).astype(BF16)
    vn = jnp.concatenate([vn_ref[0], zpad], axis=0).astype(BF16)
    s2 = _dot_nt(qq, kn)
    s_own = jnp.where(lane <= tq, s2[:rows] + s2[rows:] - in_page, NEG_INF)

    mrun = s_own
    for n in range(nb):
        mask_n = jnp.sum(jnp.where(lane == n, colb, 0.0), axis=1, keepdims=True)
        for i in range(ppb):
            pg = n * ppb + i
            sn = s_ref[pg] - in_page + (mask_n - slope * float(past_len - pg * page))
            s_ref[pg] = sn
            mrun = jnp.maximum(mrun, sn)
    m = jnp.max(mrun, axis=1, keepdims=True)

    e_own = jnp.exp(s_own - m)
    lrun = e_own
    for pg in range(n_pages):
        e = jnp.exp(s_ref[pg] - m)
        s_ref[pg] = e
        lrun = lrun + e
    l = jnp.sum(lrun, axis=1, keepdims=True)

    acc_ref[...] = jnp.zeros(acc_ref.shape, F32)
    zrows = jnp.zeros((LANES - rows, V_TILE * page), BF16)

    def v_tile(i, c):
        pgs = [i * V_TILE + k for k in range(V_TILE)]
        for pg in pgs:
            page_copy(cvt_hbm, vbuf, vsem, b, pg).wait()
        vt = jnp.concatenate([vbuf[pg].reshape(D_ATT, page) for pg in pgs], axis=1).astype(BF16)
        p = jnp.concatenate([s_ref[pg] for pg in pgs], axis=1).astype(BF16)
        acc_ref[...] += _dot_nt(vt, jnp.concatenate([p, zrows], axis=0))
        return c
    lax.fori_loop(0, n_pages // V_TILE, v_tile, 0)

    @pl.when(b + 1 < nbat)
    def _():
        start_all(cvt_hbm, vbuf, vsem, b + 1)

    acc = acc_ref[...].T[:rows] + _dot(e_own.astype(BF16), vn)
    accn = acc / l
    ch8 = lax.broadcasted_iota(jnp.int32, (t_new, D_ATT), 1) // HEAD_DIM
    out = jnp.zeros((t_new, D_ATT), F32)
    for h in range(N_HEADS):
        out = jnp.where(ch8 == h, accn[h * t_new:(h + 1) * t_new, :], out)
    o_ref[0] = _group_rms(out, gain_ref[...], bd_ref).astype(BF16)


def _attn_sample_t(page_table, q, kn, vn, gain, bd, cache_kt, cache_vt, *, page_base, past_len):
    nbat, t_new, _ = q.shape
    page = cache_kt.shape[3]
    n_pages = past_len // page
    rows = N_HEADS * t_new
    assert past_len % MOBA_BLOCK == 0 and MOBA_BLOCK % page == 0 and page == LANES
    assert rows <= LANES and t_new % 8 == 0 and past_len // MOBA_BLOCK <= LANES
    assert n_pages % V_TILE == 0 and n_pages % K_TILE == 0
    body = functools.partial(_attn_sample_t_body, past_len=past_len, page=page, page_base=page_base)
    per_b = pl.BlockSpec((1, t_new, D_ATT), lambda b, pt: (b, 0, 0))
    grid_spec = pltpu.PrefetchScalarGridSpec(
        num_scalar_prefetch=1,
        grid=(nbat,),
        in_specs=[per_b, per_b, per_b,
                  pl.BlockSpec(gain.shape, lambda b, pt: (0, 0)),
                  pl.BlockSpec(bd.shape, lambda b, pt: (0, 0)),
                  pl.BlockSpec(memory_space=pl.ANY),
                  pl.BlockSpec(memory_space=pl.ANY)],
        out_specs=per_b,
        scratch_shapes=[pltpu.VMEM((n_pages, N_HEADS, HEAD_DIM, page), F32),
                        pltpu.VMEM((n_pages, N_HEADS, HEAD_DIM, page), F32),
                        pltpu.VMEM((n_pages, rows, page), F32),
                        pltpu.VMEM((D_ATT, LANES), F32),
                        pltpu.SemaphoreType.DMA((n_pages,)),
                        pltpu.SemaphoreType.DMA((n_pages,))],
    )
    return pl.pallas_call(
        body,
        grid_spec=grid_spec,
        out_shape=jax.ShapeDtypeStruct((nbat, t_new, D_ATT), BF16),
        compiler_params=pltpu.CompilerParams(dimension_semantics=("arbitrary",),
                                             vmem_limit_bytes=VMEM_LIMIT),
        name="attn_sample",
    )(page_table, q, kn, vn, gain, bd, cache_kt, cache_vt)


def kernel(x_prompt, x_sample, cache_k, cache_v, state_conv, page_table, ffn1_norm, ffn1_w_gu, ffn1_w_down,
           mix_norm, w_in, conv_w, conv_out_norm, attn_out_norm, w_out, ffn2_norm, ffn2_w_gu, ffn2_w_down,
           final_norm):
    bp, seq, d = x_prompt.shape
    bs, dseq, _ = x_sample.shape
    depth, n_pool, page = cache_k.shape[:3]
    d_ff = ffn1_w_down.shape[1]
    dc = conv_w.shape[2]
    past_len = page_table.shape[1] * page

    ck = jnp.transpose(cache_k, (0, 1, 3, 4, 2)).reshape(depth * n_pool, N_HEADS, HEAD_DIM, page)
    cv = jnp.transpose(cache_v, (0, 1, 3, 4, 2)).reshape(depth * n_pool, N_HEADS, HEAD_DIM, page)
    gi = lax.broadcasted_iota(jnp.int32, (D_ATT, D_ATT), 0) // HEAD_DIM
    gj = lax.broadcasted_iota(jnp.int32, (D_ATT, D_ATT), 1) // HEAD_DIM
    bd = (gi == gj).astype(BF16)

    xp = x_prompt.reshape(bp * seq, d)
    xs = x_sample.reshape(bs * dseq, d)
    tm_p = 512
    row = lambda a: a.reshape(1, -1)
    outs = [[] for _ in range(6)]
    for l in range(depth):
        wg1, wu1 = ffn1_w_gu[l][:, :d_ff].astype(BF16), ffn1_w_gu[l][:, d_ff:].astype(BF16)
        wd1 = ffn1_w_down[l].astype(BF16)
        wg2, wu2 = ffn2_w_gu[l][:, :d_ff].astype(BF16), ffn2_w_gu[l][:, d_ff:].astype(BF16)
        wd2 = ffn2_w_down[l].astype(BF16)
        win = w_in[l].astype(BF16)
        woc, woa = w_out[l][:dc].astype(BF16), w_out[l][dc:].astype(BF16)
        g1, gm, g2 = row(ffn1_norm[l]), row(mix_norm[l]), row(ffn2_norm[l])
        gc, ga = row(conv_out_norm[l]), row(attn_out_norm[l])
        last = l == depth - 1
        gfin = row(final_norm) if last else None

        x1 = _ffn_call(xp, g1, wg1, wu1, wd1, tm=tm_p, name="ffn1_prompt")
        q, kt, vtf, kb, vt, yc, means, cnew = _inproj_prompt(x1, gm, win, conv_w[l], gc, bd, batch=bp, tm=tm_p)
        ya = _attn_prompt(q, kb, vt, means, ga, batch=bp)
        xp = _ffn_call(x1, g2, wg2, wu2, wd2, tm=tm_p, mix=(yc, ya, woc, woa), final=gfin, name="ffn2_prompt")
        tok_major = lambda a: a.reshape(bp, N_HEADS, HEAD_DIM, seq).transpose(0, 3, 1, 2)
        outs[0].append(tok_major(kt))
        outs[1].append(tok_major(vtf))
        outs[2].append(cnew)

        st = state_conv[l]
        zpad = jnp.zeros((bs, dseq - (CONV_W - 1), dc), F32)
        s2 = jnp.concatenate([st, zpad], axis=1).reshape(bs * dseq, dc)
        s1 = jnp.concatenate([st[:, 1:2], jnp.zeros((bs, dseq - 1, dc), F32)], axis=1).reshape(bs * dseq, dc)
        x1s = _ffn_call(xs, g1, wg1, wu1, wd1, tm=bs * dseq, name="ffn1_sample")
        qs, ks, vs, ycs, us = _inproj_sample(x1s, gm, win, conv_w[l], gc, bd, s1, s2, seq=dseq)
        r3 = lambda a: a.reshape(bs, dseq, D_ATT)
        yas = _attn_sample_t(page_table, r3(qs), r3(ks), r3(vs), ga, bd, ck, cv,
                             page_base=l * n_pool, past_len=past_len)
        xs = _ffn_call(x1s, g2, wg2, wu2, wd2, tm=bs * dseq,
                       mix=(ycs, yas.reshape(bs * dseq, D_ATT), woc, woa), final=gfin, name="ffn2_sample")
        outs[3].append(ks.reshape(bs, dseq, N_HEADS, HEAD_DIM))
        outs[4].append(vs.reshape(bs, dseq, N_HEADS, HEAD_DIM))
        outs[5].append(us[:, dseq - (CONV_W - 1):, :])

    y_prompt = xp.reshape(bp, seq, d)
    y_sample = xs.reshape(bs, dseq, d)
    kp, vp, cp, ksn, vsn, csn = (jnp.stack(o) for o in outs)
    return (y_prompt, y_sample, kp, vp, cp, ksn, vsn, csn)
```

```python
import functools

import jax
import jax.numpy as jnp
from jax import lax
from jax.experimental import pallas as pl
from jax.experimental.pallas import tpu as pltpu

F32 = jnp.float32
BF16 = jnp.bfloat16

N_HEADS = 8
HEAD_DIM = 64
D_ATT = N_HEADS * HEAD_DIM
N_CONV_GROUPS = 8
CONV_W = 3
MOBA_BLOCK = 256
MOBA_TOPK = 3
EPS = 1e-5
NEG_INF = -1e30
SCALE = HEAD_DIM ** -0.5
LOG2E = 1.4426950408889634

LANES = 128
MXU_WIDTH = 256
HEADS_PER_SLAB = LANES // HEAD_DIM
N_SLABS = D_ATT // LANES
VMEM_LIMIT = 56 * 1024 * 1024
LEAD = 1
SUM_ROWS = 16
K_TILE = 8
V_TILE = 8

NT_DIMS = (((1,), (1,)), ((), ()))


def _slope(h):
    return 2.0 ** (-(8.0 / N_HEADS) * (h + 1))


def _dot(a, b):
    return jnp.dot(a, b, preferred_element_type=F32)


def _dot_nt(a, b, precision=None):
    return lax.dot_general(a, b, NT_DIMS, precision=precision, preferred_element_type=F32)


def _rms(x, g):
    ms = jnp.mean(x * x, axis=-1, keepdims=True)
    return x * lax.rsqrt(ms + EPS) * g


def _group_sumsq(y, bd_ref):
    y2 = y * y
    hi = y2.astype(BF16)
    lo = (y2 - hi.astype(F32)).astype(BF16)
    bd = bd_ref[...]
    return _dot(hi, bd) + _dot(lo, bd)


def _group_rms(y, g, bd_ref):
    ms = _group_sumsq(y, bd_ref) * (1.0 / HEAD_DIM)
    return y * lax.rsqrt(ms + EPS) * g


def _const_spec(shape):
    nd = len(shape)
    return pl.BlockSpec(shape, lambda *_: (0,) * nd, pipeline_mode=pl.Buffered(1))


def _ffn_body(*refs, mix, final, bounds):
    it = iter(refs)
    x_ref = next(it)
    if mix:
        yc_ref, ya_ref, woc_ref, woa_ref = next(it), next(it), next(it), next(it)
    g_ref, wg_ref, wu_ref, wd_ref = next(it), next(it), next(it), next(it)
    gf_ref = next(it) if final else None
    o_ref = next(it)

    x = x_ref[...]
    if mix:
        x = x + _dot(yc_ref[...], woc_ref[...]) + _dot(ya_ref[...], woa_ref[...])
    h = _rms(x, g_ref[...]).astype(BF16)
    acc = jnp.zeros(x.shape, F32)
    for lo, hi in zip(bounds[:-1], bounds[1:]):
        gate = _dot(h, wg_ref[:, lo:hi])
        up = _dot(h, wu_ref[:, lo:hi])
        act = (gate * jax.nn.sigmoid(gate) * up).astype(BF16)
        acc = acc + _dot(act, wd_ref[lo:hi, :])
    x = x + 0.5 * acc
    if final:
        x = _rms(x, gf_ref[...])
    o_ref[...] = x


def _ffn_call(x, g, wg, wu, wd, *, tm, mix=None, final=None, name):
    n, d = x.shape
    d_ff = wg.shape[1]
    row = lambda w: pl.BlockSpec((tm, w), lambda i: (i, 0))
    ins, specs = [x], [row(d)]
    if mix is not None:
        yc, ya, woc, woa = mix
        ins += [yc, ya, woc, woa]
        specs += [row(yc.shape[1]), row(ya.shape[1]), _const_spec(woc.shape), _const_spec(woa.shape)]
    ins += [g, wg, wu, wd]
    specs += [_const_spec(g.shape), _const_spec(wg.shape), _const_spec(wu.shape), _const_spec(wd.shape)]
    if final is not None:
        ins.append(final)
        specs.append(_const_spec(final.shape))
    assert d_ff % MXU_WIDTH == 0
    tiles = d_ff // MXU_WIDTH
    bounds = (0, (tiles + 1) // 2 * MXU_WIDTH, d_ff)
    body = functools.partial(_ffn_body, mix=mix is not None, final=final is not None, bounds=bounds)
    return pl.pallas_call(
        body,
        grid=(n // tm,),
        in_specs=specs,
        out_specs=row(d),
        out_shape=jax.ShapeDtypeStruct((n, d), F32),
        compiler_params=pltpu.CompilerParams(dimension_semantics=("arbitrary",),
                                             vmem_limit_bytes=VMEM_LIMIT),
        name=name,
    )(*ins)


def _inproj_body(*refs, tm, tiles_per_seq, sample):
    it = iter(refs)
    x_ref, g_ref, win_ref, cw_ref, cn_ref, bd_ref = (next(it) for _ in range(6))
    if sample:
        s1_ref, s2_ref = next(it), next(it)
        q_ref, k_ref, v_ref, yc_ref, u_ref = (next(it) for _ in range(5))
    else:
        q_ref, kt_ref, vtf_ref, kb_ref, vt_ref, yc_ref, mean_ref, cnew_ref = (next(it) for _ in range(8))
    ubuf = next(it)

    dc = yc_ref.shape[1]
    h = _rms(x_ref[...], g_ref[...]).astype(BF16)
    piece = lambda c, w: _dot(h, win_ref[:, c:c + w])
    hc = piece(0, dc)
    bg = piece(dc, dc)
    cg = piece(2 * dc, dc)
    q_ref[...] = piece(3 * dc, D_ATT)
    k = piece(3 * dc + D_ATT, D_ATT)
    v = piece(3 * dc + 2 * D_ATT, D_ATT)
    if sample:
        k_ref[...] = k
        v_ref[...] = v

    u = cg * hc
    if sample:
        ubuf[0:8, :] = jnp.zeros((8, dc), F32)
    else:
        first = (pl.program_id(0) % tiles_per_seq) == 0

        @pl.when(first)
        def _():
            ubuf[0:8, :] = jnp.zeros((8, dc), F32)

        @pl.when(jnp.logical_not(first))
        def _():
            ubuf[0:8, :] = ubuf[tm:tm + 8, :]

    ubuf[8:tm + 8, :] = u
    um1 = ubuf[7:tm + 7, :]
    um2 = ubuf[6:tm + 6, :]
    if sample:
        t = lax.broadcasted_iota(jnp.int32, (tm, dc), 0) % u_ref.shape[1]
        um1 = jnp.where(t >= 1, um1, s1_ref[...])
        um2 = jnp.where(t >= 2, um2, s2_ref[...])
    cw = cw_ref[...]
    conv = um2 * cw[0:1, :] + um1 * cw[1:2, :] + u * cw[2:3, :]
    yc_ref[...] = _group_rms(bg * conv, cn_ref[...], bd_ref).astype(BF16)

    if sample:
        u_ref[...] = u.reshape(u_ref.shape)
    else:
        vt = v.T
        kt_ref[0] = k.T
        vtf_ref[0] = vt
        kb_ref[...] = k.astype(BF16)
        nblk = tm // MOBA_BLOCK
        for i in range(nblk):
            vt_ref[i] = vt[:, i * MOBA_BLOCK:(i + 1) * MOBA_BLOCK].astype(BF16)
        mean_ref[0] = jnp.sum(k.reshape(nblk, MOBA_BLOCK, D_ATT), axis=1) * (1.0 / MOBA_BLOCK)
        cnew_ref[0] = ubuf[tm + 6:tm + 8, :]


def _inproj_prompt(x, g, w_in, conv_w, conv_norm, bd, *, batch, tm):
    n, d = x.shape
    dc = conv_w.shape[1]
    seq = n // batch
    tps = seq // tm
    nblk = tm // MOBA_BLOCK
    row = lambda w: pl.BlockSpec((tm, w), lambda i: (i, 0))
    tok_minor = pl.BlockSpec((1, D_ATT, tm), lambda i: (i // tps, 0, i % tps))
    f = lambda w, dt: jax.ShapeDtypeStruct((n, w), dt)
    body = functools.partial(_inproj_body, tm=tm, tiles_per_seq=tps, sample=False)
    return pl.pallas_call(
        body,
        grid=(n // tm,),
        in_specs=[row(d), _const_spec(g.shape), _const_spec(w_in.shape), _const_spec(conv_w.shape),
                  _const_spec(conv_norm.shape), _const_spec(bd.shape)],
        out_specs=[row(D_ATT), tok_minor, tok_minor, row(D_ATT),
                   pl.BlockSpec((nblk, D_ATT, MOBA_BLOCK), lambda i: (i, 0, 0)), row(dc),
                   pl.BlockSpec((1, nblk, D_ATT), lambda i: (i, 0, 0)),
                   pl.BlockSpec((1, CONV_W - 1, dc), lambda i: (i // tps, 0, 0))],
        out_shape=[f(D_ATT, F32), jax.ShapeDtypeStruct((batch, D_ATT, seq), F32),
                   jax.ShapeDtypeStruct((batch, D_ATT, seq), F32), f(D_ATT, BF16),
                   jax.ShapeDtypeStruct((n // MOBA_BLOCK, D_ATT, MOBA_BLOCK), BF16), f(dc, BF16),
                   jax.ShapeDtypeStruct((n // tm, nblk, D_ATT), F32),
                   jax.ShapeDtypeStruct((batch, CONV_W - 1, dc), F32)],
        scratch_shapes=[pltpu.VMEM((tm + 8, dc), F32)],
        compiler_params=pltpu.CompilerParams(dimension_semantics=("arbitrary",),
                                             vmem_limit_bytes=VMEM_LIMIT),
        name="inproj_prompt",
    )(x, g, w_in, conv_w, conv_norm, bd)


def _inproj_sample(x, g, w_in, conv_w, conv_norm, bd, s1, s2, *, seq):
    n, d = x.shape
    dc = conv_w.shape[1]
    tm = n
    full = lambda shape: pl.BlockSpec(shape, lambda i: (0,) * len(shape))
    f = lambda w, dt: jax.ShapeDtypeStruct((n, w), dt)
    body = functools.partial(_inproj_body, tm=tm, tiles_per_seq=1, sample=True)
    return pl.pallas_call(
        body,
        grid=(1,),
        in_specs=[full((tm, d)), full(g.shape), full(w_in.shape), full(conv_w.shape),
                  full(conv_norm.shape), full(bd.shape), full((tm, dc)), full((tm, dc))],
        out_specs=[full((tm, D_ATT)), full((tm, D_ATT)), full((tm, D_ATT)), full((tm, dc)),
                   full((n // seq, seq, dc))],
        out_shape=[f(D_ATT, F32), f(D_ATT, F32), f(D_ATT, F32), f(dc, BF16),
                   jax.ShapeDtypeStruct((n // seq, seq, dc), F32)],
        scratch_shapes=[pltpu.VMEM((tm + 8, dc), F32)],
        compiler_params=pltpu.CompilerParams(dimension_semantics=("arbitrary",),
                                             vmem_limit_bytes=VMEM_LIMIT),
        name="inproj_sample",
    )(x, g, w_in, conv_w, conv_norm, bd, s1, s2)


def _split3(x):
    hi = x.astype(BF16).astype(F32)
    mid = (x - hi).astype(BF16).astype(F32)
    lo = (x - hi - mid).astype(BF16).astype(F32)
    return hi, mid, lo


def _attn_prompt_body(q_ref, kb_ref, vt_ref, mean_ref, gain_ref, o_ref,
                      causal_ref, featk_ref, qabt_ref, colb_ref, so_ref, sa_ref, sb_ref, m_ref, l_ref, acc_ref):
    blk = MOBA_BLOCK
    b = pl.program_id(0)
    j = pl.program_id(1)
    nb = mean_ref.shape[1]
    group = LANES // N_HEADS
    qcols = HEADS_PER_SLAB * blk
    lane_q = lax.broadcasted_iota(jnp.int32, (1, qcols), 1)

    def slope_row(p):
        return jnp.where(lane_q < blk, LOG2E * _slope(HEADS_PER_SLAB * p), LOG2E * _slope(HEADS_PER_SLAB * p + 1))

    @pl.when((b == 0) & (j == 0))
    def _init_tables():
        kk = lax.broadcasted_iota(jnp.int32, (blk, qcols), 0)
        qq = lax.broadcasted_iota(jnp.int32, (blk, qcols), 1)
        causal_ref[...] = jnp.where((qq % blk) >= kk, 0.0, NEG_INF)
        ki = lax.broadcasted_iota(jnp.int32, (blk, LANES), 0).astype(F32)
        kl = lax.broadcasted_iota(jnp.int32, (blk, LANES), 1)
        featk_ref[...] = jnp.where(kl < 3, ki, jnp.where(kl < 6, 1.0, 0.0)).astype(BF16)
        fr = lax.broadcasted_iota(jnp.int32, (LANES, qcols), 0)
        for p in range(N_SLABS):
            a = slope_row(p)
            terms = _split3(a) + _split3(-a * (lane_q % blk).astype(F32))
            feat = jnp.zeros((LANES, qcols), F32)
            for r, t in enumerate(terms):
                feat = jnp.where(fr == r, t, feat)
            qabt_ref[p, LANES:, :] = feat.astype(BF16)

    qt = q_ref[...].T
    means = mean_ref[0]
    if nb < group:
        means = jnp.concatenate([means, jnp.zeros((group - nb, D_ATT), F32)], axis=0)
    mt = jnp.concatenate([means] * N_HEADS, axis=0)
    rh = lax.broadcasted_iota(jnp.int32, mt.shape, 0) // group
    ch = lax.broadcasted_iota(jnp.int32, mt.shape, 1) // HEAD_DIM
    mbd = jnp.where(rh == ch, mt, 0.0)
    gate_t = jnp.dot(mbd, qt, precision=lax.Precision.HIGHEST, preferred_element_type=F32)
    gate = jnp.concatenate([gate_t[h * group:(h + 1) * group, :] for h in range(N_HEADS)], axis=1)

    n_idx = lax.broadcasted_iota(jnp.int32, gate.shape, 0)
    valid = n_idx < j
    gm = jnp.where(valid, gate, NEG_INF)
    rank = jnp.zeros(gate.shape, jnp.int32)
    for m in range(nb):
        other = gm[m:m + 1, :]
        beats = (other > gm) | ((other == gm) & (n_idx > m))
        rank = rank + beats.astype(jnp.int32)
    colb_ref[...] = jnp.where((rank < MOBA_TOPK) & valid, 0.0, NEG_INF)

    row_d = lax.broadcasted_iota(jnp.int32, (LANES, blk), 0)
    for p in range(N_SLABS):
        qs = qt[p * LANES:(p + 1) * LANES, :] * (SCALE * LOG2E)
        qa = jnp.where(row_d < HEAD_DIM, qs, 0.0)
        qb = jnp.where(row_d >= HEAD_DIM, qs, 0.0)
        qabt_ref[p, :LANES, :] = jnp.concatenate([qa, qb], axis=1).astype(BF16)

    slabs = [slice(p * LANES, (p + 1) * LANES) for p in range(N_SLABS)]
    ones_rows = jnp.ones((SUM_ROWS, blk), BF16)

    def park_scores(n, dst, p):
        off = pl.multiple_of(n * blk, blk)
        keys = jnp.concatenate([kb_ref[pl.ds(off, blk), slabs[p]], featk_ref[...]], axis=1)
        dst[p] = _dot(keys, qabt_ref[p])

    def weighted_values(n, p, e):
        va = jnp.concatenate([vt_ref[n, slabs[p], :], ones_rows], axis=0)
        pv = _dot(va, e.astype(BF16))
        return pv[:LANES], pv[LANES:LANES + 1]

    def reduce_past(n, src, p):
        shift = ((j - n) * blk).astype(F32)
        crow = colb_ref[pl.ds(n, 1), p * qcols:(p + 1) * qcols] - slope_row(p) * shift
        sb = src[p]
        m_prev = m_ref[p]
        m_new = jnp.maximum(m_prev, jnp.max(sb, axis=0, keepdims=True) + crow)
        alpha = jnp.exp2(m_prev - m_new)
        pv, esum = weighted_values(n, p, jnp.exp2(sb - (m_new - crow)))
        m_ref[p] = m_new
        l_ref[p] = alpha * l_ref[p] + esum
        acc_ref[p] = alpha * acc_ref[p] + pv

    for p in range(N_SLABS):
        park_scores(j, so_ref, p)
    for p in range(N_SLABS):
        park_scores(0, sa_ref, p)
    for p in range(N_SLABS):
        sb = so_ref[p] + causal_ref[...]
        m = jnp.max(sb, axis=0, keepdims=True)
        pv, esum = weighted_values(j, p, jnp.exp2(sb - m))
        m_ref[p] = m
        l_ref[p] = esum
        acc_ref[p] = pv

    def two_blocks(i, carry):
        n0 = 2 * i
        n2 = jnp.minimum(n0 + 2, j)
        parks = [(n0 + 1, sb_ref, p) for p in range(N_SLABS)] + [(n2, sa_ref, p) for p in range(N_SLABS)]
        reduces = [(n0, sa_ref, p) for p in range(N_SLABS)] + [(n0 + 1, sb_ref, p) for p in range(N_SLABS)]
        for k in range(LEAD):
            park_scores(*parks[k])
        for k, red in enumerate(reduces):
            reduce_past(*red)
            if k + LEAD < len(parks):
                park_scores(*parks[k + LEAD])
        return carry

    lax.fori_loop(0, j // 2, two_blocks, 0)

    @pl.when(j % 2 == 1)
    def _odd_tail():
        for p in range(N_SLABS):
            reduce_past(j - 1, sa_ref, p)

    for p in range(N_SLABS):
        o = acc_ref[p] / l_ref[p]
        o2 = jnp.where(row_d < HEAD_DIM, o[:, :blk], o[:, blk:])
        sq = o2 * o2
        ms_a = jnp.sum(sq[:HEAD_DIM], axis=0, keepdims=True) * (1.0 / HEAD_DIM)
        ms_b = jnp.sum(sq[HEAD_DIM:], axis=0, keepdims=True) * (1.0 / HEAD_DIM)
        inv = jnp.where(row_d < HEAD_DIM, lax.rsqrt(ms_a + EPS), lax.rsqrt(ms_b + EPS))
        ls = slice(p * LANES, (p + 1) * LANES)
        o_ref[:, ls] = ((o2 * inv).T * gain_ref[:, ls]).astype(BF16)


def _attn_prompt(q, kb, vt, means, gain, *, batch):
    n = q.shape[0]
    seq = n // batch
    nb = seq // MOBA_BLOCK
    group = LANES // N_HEADS
    assert nb <= group and seq % MOBA_BLOCK == 0
    means = means.reshape(batch, nb, D_ATT)
    qcols = HEADS_PER_SLAB * MOBA_BLOCK
    return pl.pallas_call(
        _attn_prompt_body,
        grid=(batch, nb),
        in_specs=[pl.BlockSpec((MOBA_BLOCK, D_ATT), lambda b, j: (b * nb + j, 0)),
                  pl.BlockSpec((seq, D_ATT), lambda b, j: (b, 0)),
                  pl.BlockSpec((nb, D_ATT, MOBA_BLOCK), lambda b, j: (b, 0, 0)),
                  pl.BlockSpec((1, nb, D_ATT), lambda b, j: (b, 0, 0)),
                  pl.BlockSpec(gain.shape, lambda b, j: (0, 0))],
        out_specs=pl.BlockSpec((MOBA_BLOCK, D_ATT), lambda b, j: (b * nb + j, 0)),
        out_shape=jax.ShapeDtypeStruct((n, D_ATT), BF16),
        scratch_shapes=[pltpu.VMEM((MOBA_BLOCK, qcols), F32),
                        pltpu.VMEM((MOBA_BLOCK, LANES), BF16),
                        pltpu.VMEM((N_SLABS, 2 * LANES, qcols), BF16),
                        pltpu.VMEM((group, N_HEADS * MOBA_BLOCK), F32),
                        pltpu.VMEM((N_SLABS, MOBA_BLOCK, qcols), F32),
                        pltpu.VMEM((N_SLABS, MOBA_BLOCK, qcols), F32),
                        pltpu.VMEM((N_SLABS, MOBA_BLOCK, qcols), F32),
                        pltpu.VMEM((N_SLABS, 1, qcols), F32),
                        pltpu.VMEM((N_SLABS, 1, qcols), F32),
                        pltpu.VMEM((N_SLABS, LANES, qcols), F32)],
        compiler_params=pltpu.CompilerParams(dimension_semantics=("arbitrary", "arbitrary"),
                                             vmem_limit_bytes=VMEM_LIMIT),
        name="attn_prompt",
    )(q, kb, vt, means, gain)


def _attn_sample_t_body(pt_ref, q_ref, kn_ref, vn_ref, gain_ref, bd_ref, ckt_hbm, cvt_hbm, o_ref,
                        kbuf, vbuf, s_ref, acc_ref, ksem, vsem, *, past_len, page, page_base):
    b = pl.program_id(0)
    nbat = pl.num_programs(0)
    n_pages = past_len // page
    ppb = MOBA_BLOCK // page
    nb = past_len // MOBA_BLOCK
    t_new = q_ref.shape[1]
    rows = N_HEADS * t_new

    def page_copy(hbm, buf, sem, bb, pg):
        return pltpu.make_async_copy(hbm.at[page_base + pt_ref[bb, pg]], buf.at[pg], sem.at[pg])

    def start_all(hbm, buf, sem, bb):
        def body(pg, c):
            page_copy(hbm, buf, sem, bb, pg).start()
            return c
        lax.fori_loop(0, n_pages, body, 0)

    @pl.when(b == 0)
    def _():
        start_all(ckt_hbm, kbuf, ksem, b)
        start_all(cvt_hbm, vbuf, vsem, b)

    qt = jnp.concatenate([q_ref[0]] * N_HEADS, axis=0)
    rh = lax.broadcasted_iota(jnp.int32, qt.shape, 0) // t_new
    ch = lax.broadcasted_iota(jnp.int32, qt.shape, 1) // HEAD_DIM
    qs = jnp.where(rh == ch, qt, 0.0) * SCALE
    q_hi = qs.astype(BF16)
    q_lo = (qs - q_hi.astype(F32)).astype(BF16)
    qq = jnp.concatenate([q_hi, q_lo], axis=0)

    def k_tile(i, c):
        pgs = [i * K_TILE + k for k in range(K_TILE)]
        for pg in pgs:
            page_copy(ckt_hbm, kbuf, ksem, b, pg).wait()
        for pg in pgs:
            s2 = _dot(qq, kbuf[pg].reshape(D_ATT, page).astype(BF16))
            s_ref[pg] = s2[:rows] + s2[rows:]
        return c
    lax.fori_loop(0, n_pages // K_TILE, k_tile, 0)

    @pl.when(b + 1 < nbat)
    def _():
        start_all(ckt_hbm, kbuf, ksem, b + 1)

    lane = lax.broadcasted_iota(jnp.int32, (rows, LANES), 1)
    gate = jnp.zeros((rows, LANES), F32)
    gcols = []
    for n in range(nb):
        tot = s_ref[n * ppb]
        for i in range(1, ppb):
            tot = tot + s_ref[n * ppb + i]
        g = jnp.sum(tot, axis=1, keepdims=True)
        gcols.append(g)
        gate = jnp.where(lane == n, g, gate)
    rank = jnp.zeros(gate.shape, jnp.int32)
    for m in range(nb):
        beats = (gcols[m] > gate) | ((gcols[m] == gate) & (lane > m))
        rank = rank + beats.astype(jnp.int32)
    colb = jnp.where(rank < MOBA_TOPK, 0.0, NEG_INF)

    r1 = lax.broadcasted_iota(jnp.int32, (rows, 1), 0)
    tq = r1 % t_new
    slope = jnp.zeros((rows, 1), F32)
    for h in range(N_HEADS):
        slope = jnp.where(r1 // t_new == h, _slope(h), slope)
    in_page = slope * (tq - lane).astype(F32)

    zpad = jnp.zeros((LANES - t_new, D_ATT), F32)
    kn = jnp.concatenate([kn_ref[0], zpad], axis=0).astype(BF16)
    vn = jnp.concatenate([vn_ref[0], zpad], axis=0).astype(BF16)
    s2 = _dot_nt(qq, kn)
    s_own = jnp.where(lane <= tq, s2[:rows] + s2[rows:] - in_page, NEG_INF)

    mrun = s_own
    for n in range(nb):
        mask_n = jnp.sum(jnp.where(lane == n, colb, 0.0), axis=1, keepdims=True)
        for i in range(ppb):
            pg = n * ppb + i
            sn = s_ref[pg] - in_page + (mask_n - slope * float(past_len - pg * page))
            s_ref[pg] = sn
            mrun = jnp.maximum(mrun, sn)
    m = jnp.max(mrun, axis=1, keepdims=True)

    e_own = jnp.exp(s_own - m)
    lrun = e_own
    for pg in range(n_pages):
        e = jnp.exp(s_ref[pg] - m)
        s_ref[pg] = e
        lrun = lrun + e
    l = jnp.sum(lrun, axis=1, keepdims=True)

    acc_ref[...] = jnp.zeros(acc_ref.shape, F32)
    zrows = jnp.zeros((LANES - rows, V_TILE * page), BF16)

    def v_tile(i, c):
        pgs = [i * V_TILE + k for k in range(V_TILE)]
        for pg in pgs:
            page_copy(cvt_hbm, vbuf, vsem, b, pg).wait()
        vt = jnp.concatenate([vbuf[pg].reshape(D_ATT, page) for pg in pgs], axis=1).astype(BF16)
        p = jnp.concatenate([s_ref[pg] for pg in pgs], axis=1).astype(BF16)
        acc_ref[...] += _dot_nt(vt, jnp.concatenate([p, zrows], axis=0))
        return c
    lax.fori_loop(0, n_pages // V_TILE, v_tile, 0)

    @pl.when(b + 1 < nbat)
    def _():
        start_all(cvt_hbm, vbuf, vsem, b + 1)

    acc = acc_ref[...].T[:rows] + _dot(e_own.astype(BF16), vn)
    accn = acc / l
    ch8 = lax.broadcasted_iota(jnp.int32, (t_new, D_ATT), 1) // HEAD_DIM
    out = jnp.zeros((t_new, D_ATT), F32)
    for h in range(N_HEADS):
        out = jnp.where(ch8 == h, accn[h * t_new:(h + 1) * t_new, :], out)
    o_ref[0] = _group_rms(out, gain_ref[...], bd_ref).astype(BF16)


def _attn_sample_t(page_table, q, kn, vn, gain, bd, cache_kt, cache_vt, *, page_base, past_len):
    nbat, t_new, _ = q.shape
    page = cache_kt.shape[3]
    n_pages = past_len // page
    rows = N_HEADS * t_new
    assert past_len % MOBA_BLOCK == 0 and MOBA_BLOCK % page == 0 and page == LANES
    assert rows <= LANES and t_new % 8 == 0 and past_len // MOBA_BLOCK <= LANES
    assert n_pages % V_TILE == 0 and n_pages % K_TILE == 0
    body = functools.partial(_attn_sample_t_body, past_len=past_len, page=page, page_base=page_base)
    per_b = pl.BlockSpec((1, t_new, D_ATT), lambda b, pt: (b, 0, 0))
    grid_spec = pltpu.PrefetchScalarGridSpec(
        num_scalar_prefetch=1,
        grid=(nbat,),
        in_specs=[per_b, per_b, per_b,
                  pl.BlockSpec(gain.shape, lambda b, pt: (0, 0)),
                  pl.BlockSpec(bd.shape, lambda b, pt: (0, 0)),
                  pl.BlockSpec(memory_space=pl.ANY),
                  pl.BlockSpec(memory_space=pl.ANY)],
        out_specs=per_b,
        scratch_shapes=[pltpu.VMEM((n_pages, N_HEADS, HEAD_DIM, page), F32),
                        pltpu.VMEM((n_pages, N_HEADS, HEAD_DIM, page), F32),
                        pltpu.VMEM((n_pages, rows, page), F32),
                        pltpu.VMEM((D_ATT, LANES), F32),
                        pltpu.SemaphoreType.DMA((n_pages,)),
                        pltpu.SemaphoreType.DMA((n_pages,))],
    )
    return pl.pallas_call(
        body,
        grid_spec=grid_spec,
        out_shape=jax.ShapeDtypeStruct((nbat, t_new, D_ATT), BF16),
        compiler_params=pltpu.CompilerParams(dimension_semantics=("arbitrary",),
                                             vmem_limit_bytes=VMEM_LIMIT),
        name="attn_sample",
    )(page_table, q, kn, vn, gain, bd, cache_kt, cache_vt)


def kernel(x_prompt, x_sample, cache_k, cache_v, state_conv, page_table, ffn1_norm, ffn1_w_gu, ffn1_w_down,
           mix_norm, w_in, conv_w, conv_out_norm, attn_out_norm, w_out, ffn2_norm, ffn2_w_gu, ffn2_w_down,
           final_norm):
    bp, seq, d = x_prompt.shape
    bs, dseq, _ = x_sample.shape
    depth, n_pool, page = cache_k.shape[:3]
    d_ff = ffn1_w_down.shape[1]
    dc = conv_w.shape[2]
    past_len = page_table.shape[1] * page

    ck = jnp.transpose(cache_k, (0, 1, 3, 4, 2)).reshape(depth * n_pool, N_HEADS, HEAD_DIM, page)
    cv = jnp.transpose(cache_v, (0, 1, 3, 4, 2)).reshape(depth * n_pool, N_HEADS, HEAD_DIM, page)
    gi = lax.broadcasted_iota(jnp.int32, (D_ATT, D_ATT), 0) // HEAD_DIM
    gj = lax.broadcasted_iota(jnp.int32, (D_ATT, D_ATT), 1) // HEAD_DIM
    bd = (gi == gj).astype(BF16)

    xp = x_prompt.reshape(bp * seq, d)
    xs = x_sample.reshape(bs * dseq, d)
    tm_p = 512
    row = lambda a: a.reshape(1, -1)
    outs = [[] for _ in range(6)]
    for l in range(depth):
        wg1, wu1 = ffn1_w_gu[l][:, :d_ff].astype(BF16), ffn1_w_gu[l][:, d_ff:].astype(BF16)
        wd1 = ffn1_w_down[l].astype(BF16)
        wg2, wu2 = ffn2_w_gu[l][:, :d_ff].astype(BF16), ffn2_w_gu[l][:, d_ff:].astype(BF16)
        wd2 = ffn2_w_down[l].astype(BF16)
        win = w_in[l].astype(BF16)
        woc, woa = w_out[l][:dc].astype(BF16), w_out[l][dc:].astype(BF16)
        g1, gm, g2 = row(ffn1_norm[l]), row(mix_norm[l]), row(ffn2_norm[l])
        gc, ga = row(conv_out_norm[l]), row(attn_out_norm[l])
        last = l == depth - 1
        gfin = row(final_norm) if last else None

        x1 = _ffn_call(xp, g1, wg1, wu1, wd1, tm=tm_p, name="ffn1_prompt")
        q, kt, vtf, kb, vt, yc, means, cnew = _inproj_prompt(x1, gm, win, conv_w[l], gc, bd, batch=bp, tm=tm_p)
        ya = _attn_prompt(q, kb, vt, means, ga, batch=bp)
        xp = _ffn_call(x1, g2, wg2, wu2, wd2, tm=tm_p, mix=(yc, ya, woc, woa), final=gfin, name="ffn2_prompt")
        tok_major = lambda a: a.reshape(bp, N_HEADS, HEAD_DIM, seq).transpose(0, 3, 1, 2)
        outs[0].append(tok_major(kt))
        outs[1].append(tok_major(vtf))
        outs[2].append(cnew)

        st = state_conv[l]
        zpad = jnp.zeros((bs, dseq - (CONV_W - 1), dc), F32)
        s2 = jnp.concatenate([st, zpad], axis=1).reshape(bs * dseq, dc)
        s1 = jnp.concatenate([st[:, 1:2], jnp.zeros((bs, dseq - 1, dc), F32)], axis=1).reshape(bs * dseq, dc)
        x1s = _ffn_call(xs, g1, wg1, wu1, wd1, tm=bs * dseq, name="ffn1_sample")
        qs, ks, vs, ycs, us = _inproj_sample(x1s, gm, win, conv_w[l], gc, bd, s1, s2, seq=dseq)
        r3 = lambda a: a.reshape(bs, dseq, D_ATT)
        yas = _attn_sample_t(page_table, r3(qs), r3(ks), r3(vs), ga, bd, ck, cv,
                             page_base=l * n_pool, past_len=past_len)
        xs = _ffn_call(x1s, g2, wg2, wu2, wd2, tm=bs * dseq,
                       mix=(ycs, yas.reshape(bs * dseq, D_ATT), woc, woa), final=gfin, name="ffn2_sample")
        outs[3].append(ks.reshape(bs, dseq, N_HEADS, HEAD_DIM))
        outs[4].append(vs.reshape(bs, dseq, N_HEADS, HEAD_DIM))
        outs[5].append(us[:, dseq - (CONV_W - 1):, :])

    y_prompt = xp.reshape(bp, seq, d)
    y_sample = xs.reshape(bs, dseq, d)
    kp, vp, cp, ksn, vsn, csn = (jnp.stack(o) for o in outs)
    return (y_prompt, y_sample, kp, vp, cp, ksn, vsn, csn)
```

```python
import functools

import jax
import jax.numpy as jnp
from jax import lax
from jax.experimental import pallas as pl
from jax.experimental.pallas import tpu as pltpu

F32 = jnp.float32
BF16 = jnp.bfloat16

N_HEADS = 8
HEAD_DIM = 64
D_ATT = N_HEADS * HEAD_DIM
N_CONV_GROUPS = 8
CONV_W = 3
MOBA_BLOCK = 256
MOBA_TOPK = 3
EPS = 1e-5
NEG_INF = -1e30
REMOVED = -3e38
SCALE = HEAD_DIM ** -0.5
LOG2E = 1.4426950408889634

LANES = 128
MXU_WIDTH = 256
HEADS_PER_SLAB = LANES // HEAD_DIM
N_SLABS = D_ATT // LANES
VMEM_LIMIT = 56 * 1024 * 1024
LEAD = 1
SUM_ROWS = 16
K_TILE = 8
V_TILE = 8

NT_DIMS = (((1,), (1,)), ((), ()))


def _slope(h):
    return 2.0 ** (-(8.0 / N_HEADS) * (h + 1))


def _dot(a, b):
    return jnp.dot(a, b, preferred_element_type=F32)


def _dot_nt(a, b, precision=None):
    return lax.dot_general(a, b, NT_DIMS, precision=precision, preferred_element_type=F32)


def _rms(x, g):
    ms = jnp.mean(x * x, axis=-1, keepdims=True)
    return x * lax.rsqrt(ms + EPS) * g


def _group_sumsq(y, bd_ref):
    y2 = y * y
    hi = y2.astype(BF16)
    lo = (y2 - hi.astype(F32)).astype(BF16)
    bd = bd_ref[...]
    return _dot(hi, bd) + _dot(lo, bd)


def _group_rms(y, g, bd_ref):
    ms = _group_sumsq(y, bd_ref) * (1.0 / HEAD_DIM)
    return y * lax.rsqrt(ms + EPS) * g


def _const_spec(shape):
    nd = len(shape)
    return pl.BlockSpec(shape, lambda *_: (0,) * nd, pipeline_mode=pl.Buffered(1))


def _ffn_body(*refs, mix, final, bounds):
    it = iter(refs)
    x_ref = next(it)
    if mix:
        yc_ref, ya_ref, woc_ref, woa_ref = next(it), next(it), next(it), next(it)
    g_ref, wgu_ref, wd_ref = next(it), next(it), next(it)
    d_ff = wd_ref.shape[0]
    gf_ref = next(it) if final else None
    o_ref = next(it)

    x = x_ref[...]
    if mix:
        x = x + _dot(yc_ref[...], woc_ref[...]) + _dot(ya_ref[...], woa_ref[...])
    h = _rms(x, g_ref[...]).astype(BF16)
    acc = jnp.zeros(x.shape, F32)
    for lo, hi in zip(bounds[:-1], bounds[1:]):
        gate = _dot(h, wgu_ref[:, lo:hi])
        up = _dot(h, wgu_ref[:, d_ff + lo:d_ff + hi])
        act = (gate * jax.nn.sigmoid(gate) * up).astype(BF16)
        acc = acc + _dot(act, wd_ref[lo:hi, :])
    x = x + 0.5 * acc
    if final:
        x = _rms(x, gf_ref[...])
    o_ref[...] = x


def _ffn_call(x, g, wgu, wd, *, tm, mix=None, final=None, name):
    n, d = x.shape
    d_ff = wd.shape[0]
    assert wgu.shape[1] == 2 * d_ff and d_ff % LANES == 0
    row = lambda w: pl.BlockSpec((tm, w), lambda i: (i, 0))
    ins, specs = [x], [row(d)]
    if mix is not None:
        yc, ya, woc, woa = mix
        ins += [yc, ya, woc, woa]
        specs += [row(yc.shape[1]), row(ya.shape[1]), _const_spec(woc.shape), _const_spec(woa.shape)]
    ins += [g, wgu, wd]
    specs += [_const_spec(g.shape), _const_spec(wgu.shape), _const_spec(wd.shape)]
    if final is not None:
        ins.append(final)
        specs.append(_const_spec(final.shape))
    assert d_ff % MXU_WIDTH == 0
    tiles = d_ff // MXU_WIDTH
    bounds = (0, (tiles + 1) // 2 * MXU_WIDTH, d_ff)
    body = functools.partial(_ffn_body, mix=mix is not None, final=final is not None, bounds=bounds)
    return pl.pallas_call(
        body,
        grid=(n // tm,),
        in_specs=specs,
        out_specs=row(d),
        out_shape=jax.ShapeDtypeStruct((n, d), F32),
        compiler_params=pltpu.CompilerParams(dimension_semantics=("arbitrary",),
                                             vmem_limit_bytes=VMEM_LIMIT),
        name=name,
    )(*ins)


def _inproj_body(*refs, tm, tiles_per_seq, sample):
    it = iter(refs)
    x_ref, g_ref, win_ref, cw_ref, cn_ref, bd_ref = (next(it) for _ in range(6))
    if sample:
        s1_ref, s2_ref = next(it), next(it)
        q_ref, k_ref, v_ref, yc_ref, u_ref = (next(it) for _ in range(5))
    else:
        q_ref, kt_ref, vtf_ref, kb_ref, vt_ref, yc_ref, mean_ref, cnew_ref = (next(it) for _ in range(8))
    ubuf = next(it)

    dc = yc_ref.shape[1]
    if sample:
        ubuf[0:8, :] = jnp.zeros((8, dc), F32)
    else:
        first = (pl.program_id(0) % tiles_per_seq) == 0

        @pl.when(first)
        def _():
            ubuf[0:8, :] = jnp.zeros((8, dc), F32)

        @pl.when(jnp.logical_not(first))
        def _():
            ubuf[0:8, :] = ubuf[tm:tm + 8, :]

    h = _rms(x_ref[...], g_ref[...]).astype(BF16)
    piece = lambda c, w: _dot(h, win_ref[:, c:c + w])
    hc = piece(0, dc)
    cg = piece(2 * dc, dc)
    bg = piece(dc, dc)
    k = piece(3 * dc + D_ATT, D_ATT)
    v = piece(3 * dc + 2 * D_ATT, D_ATT)
    if sample:
        k_ref[...] = k
        v_ref[...] = v

    u = cg * hc
    ubuf[8:tm + 8, :] = u
    um1 = ubuf[7:tm + 7, :]
    um2 = ubuf[6:tm + 6, :]
    if sample:
        t = lax.broadcasted_iota(jnp.int32, (tm, dc), 0) % u_ref.shape[1]
        um1 = jnp.where(t >= 1, um1, s1_ref[...])
        um2 = jnp.where(t >= 2, um2, s2_ref[...])
    cw = cw_ref[...]
    conv = um2 * cw[0:1, :] + um1 * cw[1:2, :] + u * cw[2:3, :]
    yc_ref[...] = _group_rms(bg * conv, cn_ref[...], bd_ref).astype(BF16)
    q_ref[...] = piece(3 * dc, D_ATT)

    if sample:
        u_ref[...] = u.reshape(u_ref.shape)
    else:
        vt = v.T
        kt_ref[0] = k.T
        vtf_ref[0] = vt
        kb_ref[...] = k.astype(BF16)
        nblk = tm // MOBA_BLOCK
        for i in range(nblk):
            vt_ref[i] = vt[:, i * MOBA_BLOCK:(i + 1) * MOBA_BLOCK].astype(BF16)
        mean_ref[0] = jnp.sum(k.reshape(nblk, MOBA_BLOCK, D_ATT), axis=1) * (1.0 / MOBA_BLOCK)
        cnew_ref[0] = ubuf[tm + 6:tm + 8, :]


def _inproj_prompt(x, g, w_in, conv_w, conv_norm, bd, *, batch, tm):
    n, d = x.shape
    dc = conv_w.shape[1]
    seq = n // batch
    tps = seq // tm
    nblk = tm // MOBA_BLOCK
    row = lambda w: pl.BlockSpec((tm, w), lambda i: (i, 0))
    tok_minor = pl.BlockSpec((1, D_ATT, tm), lambda i: (i // tps, 0, i % tps))
    f = lambda w, dt: jax.ShapeDtypeStruct((n, w), dt)
    body = functools.partial(_inproj_body, tm=tm, tiles_per_seq=tps, sample=False)
    return pl.pallas_call(
        body,
        grid=(n // tm,),
        in_specs=[row(d), _const_spec(g.shape), _const_spec(w_in.shape), _const_spec(conv_w.shape),
                  _const_spec(conv_norm.shape), _const_spec(bd.shape)],
        out_specs=[row(D_ATT), tok_minor, tok_minor, row(D_ATT),
                   pl.BlockSpec((nblk, D_ATT, MOBA_BLOCK), lambda i: (i, 0, 0)), row(dc),
                   pl.BlockSpec((1, nblk, D_ATT), lambda i: (i, 0, 0)),
                   pl.BlockSpec((1, CONV_W - 1, dc), lambda i: (i // tps, 0, 0))],
        out_shape=[f(D_ATT, F32), jax.ShapeDtypeStruct((batch, D_ATT, seq), F32),
                   jax.ShapeDtypeStruct((batch, D_ATT, seq), F32), f(D_ATT, BF16),
                   jax.ShapeDtypeStruct((n // MOBA_BLOCK, D_ATT, MOBA_BLOCK), BF16), f(dc, BF16),
                   jax.ShapeDtypeStruct((n // tm, nblk, D_ATT), F32),
                   jax.ShapeDtypeStruct((batch, CONV_W - 1, dc), F32)],
        scratch_shapes=[pltpu.VMEM((tm + 8, dc), F32)],
        compiler_params=pltpu.CompilerParams(dimension_semantics=("arbitrary",),
                                             vmem_limit_bytes=VMEM_LIMIT),
        name="inproj_prompt",
    )(x, g, w_in, conv_w, conv_norm, bd)


def _inproj_sample(x, g, w_in, conv_w, conv_norm, bd, s1, s2, *, seq):
    n, d = x.shape
    dc = conv_w.shape[1]
    tm = n
    full = lambda shape: pl.BlockSpec(shape, lambda i: (0,) * len(shape))
    f = lambda w, dt: jax.ShapeDtypeStruct((n, w), dt)
    body = functools.partial(_inproj_body, tm=tm, tiles_per_seq=1, sample=True)
    return pl.pallas_call(
        body,
        grid=(1,),
        in_specs=[full((tm, d)), full(g.shape), full(w_in.shape), full(conv_w.shape),
                  full(conv_norm.shape), full(bd.shape), full((tm, dc)), full((tm, dc))],
        out_specs=[full((tm, D_ATT)), full((tm, D_ATT)), full((tm, D_ATT)), full((tm, dc)),
                   full((n // seq, seq, dc))],
        out_shape=[f(D_ATT, F32), f(D_ATT, F32), f(D_ATT, F32), f(dc, BF16),
                   jax.ShapeDtypeStruct((n // seq, seq, dc), F32)],
        scratch_shapes=[pltpu.VMEM((tm + 8, dc), F32)],
        compiler_params=pltpu.CompilerParams(dimension_semantics=("arbitrary",),
                                             vmem_limit_bytes=VMEM_LIMIT),
        name="inproj_sample",
    )(x, g, w_in, conv_w, conv_norm, bd, s1, s2)


def _split3(x):
    hi = x.astype(BF16).astype(F32)
    mid = (x - hi).astype(BF16).astype(F32)
    lo = (x - hi - mid).astype(BF16).astype(F32)
    return hi, mid, lo


def _attn_prompt_body(q_ref, kb_ref, vt_ref, mean_ref, gain_ref, o_ref,
                      causal_ref, featk_ref, qabt_ref, colb_ref, so_ref, sa_ref, sb_ref, m_ref, l_ref, acc_ref):
    blk = MOBA_BLOCK
    b = pl.program_id(0)
    j = pl.program_id(1)
    nb = mean_ref.shape[1]
    group = LANES // N_HEADS
    qcols = HEADS_PER_SLAB * blk
    lane_q = lax.broadcasted_iota(jnp.int32, (1, qcols), 1)

    def slope_row(p):
        return jnp.where(lane_q < blk, LOG2E * _slope(HEADS_PER_SLAB * p), LOG2E * _slope(HEADS_PER_SLAB * p + 1))

    @pl.when((b == 0) & (j == 0))
    def _init_tables():
        kk = lax.broadcasted_iota(jnp.int32, (blk, qcols), 0)
        qq = lax.broadcasted_iota(jnp.int32, (blk, qcols), 1)
        causal_ref[...] = jnp.where((qq % blk) >= kk, 0.0, NEG_INF)
        ki = lax.broadcasted_iota(jnp.int32, (blk, LANES), 0).astype(F32)
        kl = lax.broadcasted_iota(jnp.int32, (blk, LANES), 1)
        featk_ref[...] = jnp.where(kl < 3, ki, jnp.where(kl < 6, 1.0, 0.0)).astype(BF16)
        fr = lax.broadcasted_iota(jnp.int32, (LANES, qcols), 0)
        for p in range(N_SLABS):
            a = slope_row(p)
            terms = _split3(a) + _split3(-a * (lane_q % blk).astype(F32))
            feat = jnp.zeros((LANES, qcols), F32)
            for r, t in enumerate(terms):
                feat = jnp.where(fr == r, t, feat)
            qabt_ref[p, LANES:, :] = feat.astype(BF16)

    qt = q_ref[...].T
    means = mean_ref[0]
    if nb < group:
        means = jnp.concatenate([means, jnp.zeros((group - nb, D_ATT), F32)], axis=0)
    mt = jnp.concatenate([means] * N_HEADS, axis=0)
    rh = lax.broadcasted_iota(jnp.int32, mt.shape, 0) // group
    ch = lax.broadcasted_iota(jnp.int32, mt.shape, 1) // HEAD_DIM
    mbd = jnp.where(rh == ch, mt, 0.0)
    gate_t = jnp.dot(mbd, qt, precision=lax.Precision.HIGHEST, preferred_element_type=F32)
    gate = jnp.concatenate([gate_t[h * group:(h + 1) * group, :] for h in range(N_HEADS)], axis=1)

    n_idx = lax.broadcasted_iota(jnp.int32, gate.shape, 0)
    n_f = n_idx.astype(F32)
    valid = n_idx < j
    work = jnp.where(valid, gate, NEG_INF)
    picked = jnp.zeros(gate.shape, F32)
    for _ in range(MOBA_TOPK):
        top = jnp.max(work, axis=0, keepdims=True)
        first = jnp.min(jnp.where(work == top, n_f, float(group)), axis=0, keepdims=True)
        pick = n_f == first
        picked = jnp.where(pick, 1.0, picked)
        work = jnp.where(pick, REMOVED, work)
    colb_ref[...] = jnp.where((picked > 0.0) & valid, 0.0, NEG_INF)

    row_d = lax.broadcasted_iota(jnp.int32, (LANES, blk), 0)
    for p in range(N_SLABS):
        qs = qt[p * LANES:(p + 1) * LANES, :] * (SCALE * LOG2E)
        qa = jnp.where(row_d < HEAD_DIM, qs, 0.0)
        qb = jnp.where(row_d >= HEAD_DIM, qs, 0.0)
        qabt_ref[p, :LANES, :] = jnp.concatenate([qa, qb], axis=1).astype(BF16)

    slabs = [slice(p * LANES, (p + 1) * LANES) for p in range(N_SLABS)]
    ones_rows = jnp.ones((SUM_ROWS, blk), BF16)

    def park_scores(n, dst, p):
        off = pl.multiple_of(n * blk, blk)
        keys = jnp.concatenate([kb_ref[pl.ds(off, blk), slabs[p]], featk_ref[...]], axis=1)
        dst[p] = _dot(keys, qabt_ref[p])

    def weighted_values(n, p, e):
        va = jnp.concatenate([vt_ref[n, slabs[p], :], ones_rows], axis=0)
        pv = _dot(va, e.astype(BF16))
        return pv[:LANES], pv[LANES:LANES + 1]

    def reduce_past(n, src, p):
        shift = ((j - n) * blk).astype(F32)
        crow = colb_ref[pl.ds(n, 1), p * qcols:(p + 1) * qcols] - slope_row(p) * shift
        sb = src[p]
        m_prev = m_ref[p]
        m_new = jnp.maximum(m_prev, jnp.max(sb, axis=0, keepdims=True) + crow)
        alpha = jnp.exp2(m_prev - m_new)
        pv, esum = weighted_values(n, p, jnp.exp2(sb - (m_new - crow)))
        m_ref[p] = m_new
        l_ref[p] = alpha * l_ref[p] + esum
        acc_ref[p] = alpha * acc_ref[p] + pv

    for p in range(N_SLABS):
        park_scores(j, so_ref, p)
    for p in range(N_SLABS):
        park_scores(0, sa_ref, p)
    for p in range(N_SLABS):
        sb = so_ref[p] + causal_ref[...]
        m = jnp.max(sb, axis=0, keepdims=True)
        pv, esum = weighted_values(j, p, jnp.exp2(sb - m))
        m_ref[p] = m
        l_ref[p] = esum
        acc_ref[p] = pv

    def two_blocks(i, carry):
        n0 = 2 * i
        n2 = jnp.minimum(n0 + 2, j)
        parks = [(n0 + 1, sb_ref, p) for p in range(N_SLABS)] + [(n2, sa_ref, p) for p in range(N_SLABS)]
        reduces = [(n0, sa_ref, p) for p in range(N_SLABS)] + [(n0 + 1, sb_ref, p) for p in range(N_SLABS)]
        for k in range(LEAD):
            park_scores(*parks[k])
        for k, red in enumerate(reduces):
            reduce_past(*red)
            if k + LEAD < len(parks):
                park_scores(*parks[k + LEAD])
        return carry

    lax.fori_loop(0, j // 2, two_blocks, 0)

    @pl.when(j % 2 == 1)
    def _odd_tail():
        for p in range(N_SLABS):
            reduce_past(j - 1, sa_ref, p)

    for p in range(N_SLABS):
        o = acc_ref[p] / l_ref[p]
        o2 = jnp.where(row_d < HEAD_DIM, o[:, :blk], o[:, blk:])
        sq = o2 * o2
        ms_a = jnp.sum(sq[:HEAD_DIM], axis=0, keepdims=True) * (1.0 / HEAD_DIM)
        ms_b = jnp.sum(sq[HEAD_DIM:], axis=0, keepdims=True) * (1.0 / HEAD_DIM)
        inv = jnp.where(row_d < HEAD_DIM, lax.rsqrt(ms_a + EPS), lax.rsqrt(ms_b + EPS))
        ls = slice(p * LANES, (p + 1) * LANES)
        o_ref[:, ls] = ((o2 * inv).T * gain_ref[:, ls]).astype(BF16)


def _attn_prompt(q, kb, vt, means, gain, *, batch):
    n = q.shape[0]
    seq = n // batch
    nb = seq // MOBA_BLOCK
    group = LANES // N_HEADS
    assert nb <= group and seq % MOBA_BLOCK == 0
    means = means.reshape(batch, nb, D_ATT)
    qcols = HEADS_PER_SLAB * MOBA_BLOCK
    return pl.pallas_call(
        _attn_prompt_body,
        grid=(batch, nb),
        in_specs=[pl.BlockSpec((MOBA_BLOCK, D_ATT), lambda b, j: (b * nb + j, 0)),
                  pl.BlockSpec((seq, D_ATT), lambda b, j: (b, 0)),
                  pl.BlockSpec((nb, D_ATT, MOBA_BLOCK), lambda b, j: (b, 0, 0)),
                  pl.BlockSpec((1, nb, D_ATT), lambda b, j: (b, 0, 0)),
                  pl.BlockSpec(gain.shape, lambda b, j: (0, 0))],
        out_specs=pl.BlockSpec((MOBA_BLOCK, D_ATT), lambda b, j: (b * nb + j, 0)),
        out_shape=jax.ShapeDtypeStruct((n, D_ATT), BF16),
        scratch_shapes=[pltpu.VMEM((MOBA_BLOCK, qcols), F32),
                        pltpu.VMEM((MOBA_BLOCK, LANES), BF16),
                        pltpu.VMEM((N_SLABS, 2 * LANES, qcols), BF16),
                        pltpu.VMEM((group, N_HEADS * MOBA_BLOCK), F32),
                        pltpu.VMEM((N_SLABS, MOBA_BLOCK, qcols), F32),
                        pltpu.VMEM((N_SLABS, MOBA_BLOCK, qcols), F32),
                        pltpu.VMEM((N_SLABS, MOBA_BLOCK, qcols), F32),
                        pltpu.VMEM((N_SLABS, 1, qcols), F32),
                        pltpu.VMEM((N_SLABS, 1, qcols), F32),
                        pltpu.VMEM((N_SLABS, LANES, qcols), F32)],
        compiler_params=pltpu.CompilerParams(dimension_semantics=("arbitrary", "arbitrary"),
                                             vmem_limit_bytes=VMEM_LIMIT),
        name="attn_prompt",
    )(q, kb, vt, means, gain)


def _attn_sample_t_body(pt_ref, q_ref, kn_ref, vn_ref, gain_ref, bd_ref, ckt_hbm, cvt_hbm, o_ref,
                        kbuf, vbuf, s_ref, acc_ref, ksem, vsem, *, past_len, page, page_base):
    b = pl.program_id(0)
    nbat = pl.num_programs(0)
    n_pages = past_len // page
    ppb = MOBA_BLOCK // page
    nb = past_len // MOBA_BLOCK
    t_new = q_ref.shape[1]
    rows = N_HEADS * t_new

    def page_copy(hbm, buf, sem, bb, pg):
        return pltpu.make_async_copy(hbm.at[page_base + pt_ref[bb, pg]], buf.at[pg], sem.at[pg])

    def start_all(hbm, buf, sem, bb):
        def body(pg, c):
            page_copy(hbm, buf, sem, bb, pg).start()
            return c
        lax.fori_loop(0, n_pages, body, 0)

    @pl.when(b == 0)
    def _():
        start_all(ckt_hbm, kbuf, ksem, b)
        start_all(cvt_hbm, vbuf, vsem, b)

    qt = jnp.concatenate([q_ref[0]] * N_HEADS, axis=0)
    rh = lax.broadcasted_iota(jnp.int32, qt.shape, 0) // t_new
    ch = lax.broadcasted_iota(jnp.int32, qt.shape, 1) // HEAD_DIM
    qs = jnp.where(rh == ch, qt, 0.0) * SCALE
    q_hi = qs.astype(BF16)
    q_lo = (qs - q_hi.astype(F32)).astype(BF16)
    qq = jnp.concatenate([q_hi, q_lo], axis=0)

    def k_tile(i, c):
        pgs = [i * K_TILE + k for k in range(K_TILE)]
        for pg in pgs:
            page_copy(ckt_hbm, kbuf, ksem, b, pg).wait()
        for pg in pgs:
            s2 = _dot(qq, kbuf[pg].reshape(D_ATT, page).astype(BF16))
            s_ref[pg] = s2[:rows] + s2[rows:]
        return c
    lax.fori_loop(0, n_pages // K_TILE, k_tile, 0)

    @pl.when(b + 1 < nbat)
    def _():
        start_all(ckt_hbm, kbuf, ksem, b + 1)

    lane = lax.broadcasted_iota(jnp.int32, (rows, LANES), 1)
    gate = jnp.zeros((rows, LANES), F32)
    gcols = []
    for n in range(nb):
        tot = s_ref[n * ppb]
        for i in range(1, ppb):
            tot = tot + s_ref[n * ppb + i]
        g = jnp.sum(tot, axis=1, keepdims=True)
        gcols.append(g)
        gate = jnp.where(lane == n, g, gate)
    rank = jnp.zeros(gate.shape, jnp.int32)
    for m in range(nb):
        beats = (gcols[m] > gate) | ((gcols[m] == gate) & (lane > m))
        rank = rank + beats.astype(jnp.int32)
    colb = jnp.where(rank < MOBA_TOPK, 0.0, NEG_INF)

    r1 = lax.broadcasted_iota(jnp.int32, (rows, 1), 0)
    tq = r1 % t_new
    slope = jnp.zeros((rows, 1), F32)
    for h in range(N_HEADS):
        slope = jnp.where(r1 // t_new == h, _slope(h), slope)
    in_page = slope * (tq - lane).astype(F32)

    zpad = jnp.zeros((LANES - t_new, D_ATT), F32)
    kn = jnp.concatenate([kn_ref[0], zpad], axis=0).astype(BF16)
    vn = jnp.concatenate([vn_ref[0], zpad], axis=0).astype(BF16)
    s2 = _dot_nt(qq, kn)
    s_own = jnp.where(lane <= tq, s2[:rows] + s2[rows:] - in_page, NEG_INF)

    mrun = s_own
    for n in range(nb):
        mask_n = jnp.sum(jnp.where(lane == n, colb, 0.0), axis=1, keepdims=True)
        for i in range(ppb):
            pg = n * ppb + i
            sn = s_ref[pg] - in_page + (mask_n - slope * float(past_len - pg * page))
            s_ref[pg] = sn
            mrun = jnp.maximum(mrun, sn)
    m = jnp.max(mrun, axis=1, keepdims=True)

    e_own = jnp.exp(s_own - m)
    lrun = e_own
    for pg in range(n_pages):
        e = jnp.exp(s_ref[pg] - m)
        s_ref[pg] = e
        lrun = lrun + e
    l = jnp.sum(lrun, axis=1, keepdims=True)

    acc_ref[...] = jnp.zeros(acc_ref.shape, F32)
    zrows = jnp.zeros((LANES - rows, V_TILE * page), BF16)

    def v_tile(i, c):
        pgs = [i * V_TILE + k for k in range(V_TILE)]
        for pg in pgs:
            page_copy(cvt_hbm, vbuf, vsem, b, pg).wait()
        vt = jnp.concatenate([vbuf[pg].reshape(D_ATT, page) for pg in pgs], axis=1).astype(BF16)
        p = jnp.concatenate([s_ref[pg] for pg in pgs], axis=1).astype(BF16)
        acc_ref[...] += _dot_nt(vt, jnp.concatenate([p, zrows], axis=0))
        return c
    lax.fori_loop(0, n_pages // V_TILE, v_tile, 0)

    @pl.when(b + 1 < nbat)
    def _():
        start_all(cvt_hbm, vbuf, vsem, b + 1)

    acc = acc_ref[...].T[:rows] + _dot(e_own.astype(BF16), vn)
    accn = acc / l
    ch8 = lax.broadcasted_iota(jnp.int32, (t_new, D_ATT), 1) // HEAD_DIM
    out = jnp.zeros((t_new, D_ATT), F32)
    for h in range(N_HEADS):
        out = jnp.where(ch8 == h, accn[h * t_new:(h + 1) * t_new, :], out)
    o_ref[0] = _group_rms(out, gain_ref[...], bd_ref).astype(BF16)


def _attn_sample_t(page_table, q, kn, vn, gain, bd, cache_kt, cache_vt, *, page_base, past_len):
    nbat, t_new, _ = q.shape
    page = cache_kt.shape[3]
    n_pages = past_len // page
    rows = N_HEADS * t_new
    assert past_len % MOBA_BLOCK == 0 and MOBA_BLOCK % page == 0 and page == LANES
    assert rows <= LANES and t_new % 8 == 0 and past_len // MOBA_BLOCK <= LANES
    assert n_pages % V_TILE == 0 and n_pages % K_TILE == 0
    body = functools.partial(_attn_sample_t_body, past_len=past_len, page=page, page_base=page_base)
    per_b = pl.BlockSpec((1, t_new, D_ATT), lambda b, pt: (b, 0, 0))
    grid_spec = pltpu.PrefetchScalarGridSpec(
        num_scalar_prefetch=1,
        grid=(nbat,),
        in_specs=[per_b, per_b, per_b,
                  pl.BlockSpec(gain.shape, lambda b, pt: (0, 0)),
                  pl.BlockSpec(bd.shape, lambda b, pt: (0, 0)),
                  pl.BlockSpec(memory_space=pl.ANY),
                  pl.BlockSpec(memory_space=pl.ANY)],
        out_specs=per_b,
        scratch_shapes=[pltpu.VMEM((n_pages, N_HEADS, HEAD_DIM, page), F32),
                        pltpu.VMEM((n_pages, N_HEADS, HEAD_DIM, page), F32),
                        pltpu.VMEM((n_pages, rows, page), F32),
                        pltpu.VMEM((D_ATT, LANES), F32),
                        pltpu.SemaphoreType.DMA((n_pages,)),
                        pltpu.SemaphoreType.DMA((n_pages,))],
    )
    return pl.pallas_call(
        body,
        grid_spec=grid_spec,
        out_shape=jax.ShapeDtypeStruct((nbat, t_new, D_ATT), BF16),
        compiler_params=pltpu.CompilerParams(dimension_semantics=("arbitrary",),
                                             vmem_limit_bytes=VMEM_LIMIT),
        name="attn_sample",
    )(page_table, q, kn, vn, gain, bd, cache_kt, cache_vt)


def kernel(x_prompt, x_sample, cache_k, cache_v, state_conv, page_table, ffn1_norm, ffn1_w_gu, ffn1_w_down,
           mix_norm, w_in, conv_w, conv_out_norm, attn_out_norm, w_out, ffn2_norm, ffn2_w_gu, ffn2_w_down,
           final_norm):
    bp, seq, d = x_prompt.shape
    bs, dseq, _ = x_sample.shape
    depth, n_pool, page = cache_k.shape[:3]
    d_ff = ffn1_w_down.shape[1]
    dc = conv_w.shape[2]
    past_len = page_table.shape[1] * page

    ck = jnp.transpose(cache_k, (0, 1, 3, 4, 2)).reshape(depth * n_pool, N_HEADS, HEAD_DIM, page)
    cv = jnp.transpose(cache_v, (0, 1, 3, 4, 2)).reshape(depth * n_pool, N_HEADS, HEAD_DIM, page)
    gi = lax.broadcasted_iota(jnp.int32, (D_ATT, D_ATT), 0) // HEAD_DIM
    gj = lax.broadcasted_iota(jnp.int32, (D_ATT, D_ATT), 1) // HEAD_DIM
    bd = (gi == gj).astype(BF16)

    xp = x_prompt.reshape(bp * seq, d)
    xs = x_sample.reshape(bs * dseq, d)
    tm_p = 512
    row = lambda a: a.reshape(1, -1)
    outs = [[] for _ in range(6)]
    for l in range(depth):
        wgu1, wd1 = ffn1_w_gu[l].astype(BF16), ffn1_w_down[l].astype(BF16)
        wgu2, wd2 = ffn2_w_gu[l].astype(BF16), ffn2_w_down[l].astype(BF16)
        win = w_in[l].astype(BF16)
        woc, woa = w_out[l][:dc].astype(BF16), w_out[l][dc:].astype(BF16)
        g1, gm, g2 = row(ffn1_norm[l]), row(mix_norm[l]), row(ffn2_norm[l])
        gc, ga = row(conv_out_norm[l]), row(attn_out_norm[l])
        last = l == depth - 1
        gfin = row(final_norm) if last else None

        x1 = _ffn_call(xp, g1, wgu1, wd1, tm=tm_p, name="ffn1_prompt")
        q, kt, vtf, kb, vt, yc, means, cnew = _inproj_prompt(x1, gm, win, conv_w[l], gc, bd, batch=bp, tm=tm_p)
        ya = _attn_prompt(q, kb, vt, means, ga, batch=bp)
        xp = _ffn_call(x1, g2, wgu2, wd2, tm=tm_p, mix=(yc, ya, woc, woa), final=gfin, name="ffn2_prompt")
        tok_major = lambda a: a.reshape(bp, N_HEADS, HEAD_DIM, seq).transpose(0, 3, 1, 2)
        outs[0].append(tok_major(kt))
        outs[1].append(tok_major(vtf))
        outs[2].append(cnew)

        st = state_conv[l]
        zpad = jnp.zeros((bs, dseq - (CONV_W - 1), dc), F32)
        s2 = jnp.concatenate([st, zpad], axis=1).reshape(bs * dseq, dc)
        s1 = jnp.concatenate([st[:, 1:2], jnp.zeros((bs, dseq - 1, dc), F32)], axis=1).reshape(bs * dseq, dc)
        x1s = _ffn_call(xs, g1, wgu1, wd1, tm=bs * dseq, name="ffn1_sample")
        qs, ks, vs, ycs, us = _inproj_sample(x1s, gm, win, conv_w[l], gc, bd, s1, s2, seq=dseq)
        r3 = lambda a: a.reshape(bs, dseq, D_ATT)
        yas = _attn_sample_t(page_table, r3(qs), r3(ks), r3(vs), ga, bd, ck, cv,
                             page_base=l * n_pool, past_len=past_len)
        xs = _ffn_call(x1s, g2, wgu2, wd2, tm=bs * dseq,
                       mix=(ycs, yas.reshape(bs * dseq, D_ATT), woc, woa), final=gfin, name="ffn2_sample")
        outs[3].append(ks.reshape(bs, dseq, N_HEADS, HEAD_DIM))
        outs[4].append(vs.reshape(bs, dseq, N_HEADS, HEAD_DIM))
        outs[5].append(us[:, dseq - (CONV_W - 1):, :])

    y_prompt = xp.reshape(bp, seq, d)
    y_sample = xs.reshape(bs, dseq, d)
    kp, vp, cp, ksn, vsn, csn = (jnp.stack(o) for o in outs)
    return (y_prompt, y_sample, kp, vp, cp, ksn, vsn, csn)
```

```python
import functools

import jax
import jax.numpy as jnp
from jax import lax
from jax.experimental import pallas as pl
from jax.experimental.pallas import tpu as pltpu

F32 = jnp.float32
BF16 = jnp.bfloat16

N_HEADS = 8
HEAD_DIM = 64
D_ATT = N_HEADS * HEAD_DIM
N_CONV_GROUPS = 8
CONV_W = 3
MOBA_BLOCK = 256
MOBA_TOPK = 3
EPS = 1e-5
NEG_INF = -1e30
REMOVED = -3e38
SCALE = HEAD_DIM ** -0.5
LOG2E = 1.4426950408889634

LANES = 128
MXU_WIDTH = 256
HEADS_PER_SLAB = LANES // HEAD_DIM
N_SLABS = D_ATT // LANES
VMEM_LIMIT = 56 * 1024 * 1024
LEAD = 1
SUM_ROWS = 16
DMA_THREADS = 2
K_TILE = 8
V_TILE = 8

NT_DIMS = (((1,), (1,)), ((), ()))


def _slope(h):
    return 2.0 ** (-(8.0 / N_HEADS) * (h + 1))


def _dot(a, b):
    return jnp.dot(a, b, preferred_element_type=F32)


def _dot_nt(a, b, precision=None):
    return lax.dot_general(a, b, NT_DIMS, precision=precision, preferred_element_type=F32)


def _rms(x, g):
    ms = jnp.mean(x * x, axis=-1, keepdims=True)
    return x * lax.rsqrt(ms + EPS) * g


def _group_sumsq(y, bd_ref):
    y2 = y * y
    hi = y2.astype(BF16)
    lo = (y2 - hi.astype(F32)).astype(BF16)
    bd = bd_ref[...]
    return _dot(hi, bd) + _dot(lo, bd)


def _group_rms(y, g, bd_ref):
    ms = _group_sumsq(y, bd_ref) * (1.0 / HEAD_DIM)
    return y * lax.rsqrt(ms + EPS) * g


def _const_spec(shape):
    nd = len(shape)
    return pl.BlockSpec(shape, lambda *_: (0,) * nd, pipeline_mode=pl.Buffered(1))


def _ffn_body(*refs, mix, final, bounds):
    it = iter(refs)
    x_ref = next(it)
    if mix:
        yc_ref, ya_ref, woc_ref, woa_ref = next(it), next(it), next(it), next(it)
    g_ref, wg_ref, wu_ref, wd_ref = next(it), next(it), next(it), next(it)
    gf_ref = next(it) if final else None
    o_ref = next(it)

    x = x_ref[...]
    if mix:
        x = x + _dot(yc_ref[...], woc_ref[...]) + _dot(ya_ref[...], woa_ref[...])
    h = _rms(x, g_ref[...]).astype(BF16)
    acc = jnp.zeros(x.shape, F32)
    for lo, hi in zip(bounds[:-1], bounds[1:]):
        gate = _dot(h, wg_ref[:, lo:hi])
        up = _dot(h, wu_ref[:, lo:hi])
        act = (gate * jax.nn.sigmoid(gate) * up).astype(BF16)
        acc = acc + _dot(act, wd_ref[lo:hi, :])
    x = x + 0.5 * acc
    if final:
        x = _rms(x, gf_ref[...])
    o_ref[...] = x


def _ffn_call(x, g, wg, wu, wd, *, tm, mix=None, final=None, name):
    n, d = x.shape
    d_ff = wd.shape[0]
    row = lambda w: pl.BlockSpec((tm, w), lambda i: (i, 0))
    ins, specs = [x], [row(d)]
    if mix is not None:
        yc, ya, woc, woa = mix
        ins += [yc, ya, woc, woa]
        specs += [row(yc.shape[1]), row(ya.shape[1]), _const_spec(woc.shape), _const_spec(woa.shape)]
    ins += [g, wg, wu, wd]
    specs += [_const_spec(g.shape), _const_spec(wg.shape), _const_spec(wu.shape), _const_spec(wd.shape)]
    if final is not None:
        ins.append(final)
        specs.append(_const_spec(final.shape))
    assert d_ff % MXU_WIDTH == 0
    tiles = d_ff // MXU_WIDTH
    bounds = (0, (tiles + 1) // 2 * MXU_WIDTH, d_ff)
    body = functools.partial(_ffn_body, mix=mix is not None, final=final is not None, bounds=bounds)
    return pl.pallas_call(
        body,
        grid=(n // tm,),
        in_specs=specs,
        out_specs=row(d),
        out_shape=jax.ShapeDtypeStruct((n, d), F32),
        compiler_params=pltpu.CompilerParams(dimension_semantics=("arbitrary",),
                                             vmem_limit_bytes=VMEM_LIMIT),
        name=name,
    )(*ins)


def _ffn_stream_body(*refs, mix, final):
    it = iter(refs)
    x_ref = next(it)
    if mix:
        yc_ref, ya_ref, woc_ref, woa_ref = next(it), next(it), next(it), next(it)
    g_ref, wg_ref, wu_ref, wd_ref = next(it), next(it), next(it), next(it)
    gf_ref = next(it) if final else None
    o_ref, wgb_ref, wub_ref, wdb_ref = next(it), next(it), next(it), next(it)
    if mix:
        wocb_ref, woab_ref = next(it), next(it)
    x_scr, h_scr, acc_scr = next(it), next(it), next(it)
    c = pl.program_id(0)

    @pl.when(c == 0)
    def _():
        x = x_ref[...]
        if mix:
            woc, woa = woc_ref[...].astype(BF16), woa_ref[...].astype(BF16)
            wocb_ref[...] = woc
            woab_ref[...] = woa
            x = x + _dot(yc_ref[...], woc) + _dot(ya_ref[...], woa)
        x_scr[...] = x
        h_scr[...] = _rms(x, g_ref[...]).astype(BF16)
        acc_scr[...] = jnp.zeros(acc_scr.shape, F32)

    wg, wu, wd = wg_ref[...].astype(BF16), wu_ref[...].astype(BF16), wd_ref[...].astype(BF16)
    wgb_ref[...] = wg
    wub_ref[...] = wu
    wdb_ref[...] = wd
    h = h_scr[...]
    gate = _dot(h, wg)
    act = (gate * jax.nn.sigmoid(gate) * _dot(h, wu)).astype(BF16)
    acc_scr[...] += _dot(act, wd)

    @pl.when(c == pl.num_programs(0) - 1)
    def _():
        x = x_scr[...] + 0.5 * acc_scr[...]
        if final:
            x = _rms(x, gf_ref[...])
        o_ref[...] = x


def _ffn_stream_call(x, g, w_gu, w_down, *, mix=None, final=None, name):
    n, d = x.shape
    d_ff = w_down.shape[0]
    tw = MXU_WIDTH
    assert d_ff % tw == 0 and w_gu.shape == (d, 2 * d_ff)
    nt = d_ff // tw
    full = lambda shape: pl.BlockSpec(shape, lambda c: (0,) * len(shape))
    ins, specs = [x], [full((n, d))]
    outs = [jax.ShapeDtypeStruct((n, d), F32), jax.ShapeDtypeStruct((d, d_ff), BF16),
            jax.ShapeDtypeStruct((d, d_ff), BF16), jax.ShapeDtypeStruct((d_ff, d), BF16)]
    out_specs = [full((n, d)), pl.BlockSpec((d, tw), lambda c: (0, c)), pl.BlockSpec((d, tw), lambda c: (0, c)),
                 pl.BlockSpec((tw, d), lambda c: (c, 0))]
    if mix is not None:
        yc, ya, w_out = mix
        dm = yc.shape[1]
        assert w_out.shape == (dm + ya.shape[1], d) and ya.shape[1] == dm
        ins += [yc, ya, w_out, w_out]
        specs += [full(yc.shape), full(ya.shape), pl.BlockSpec((dm, d), lambda c: (0, 0)),
                  pl.BlockSpec((dm, d), lambda c: (1, 0))]
        outs += [jax.ShapeDtypeStruct((dm, d), BF16)] * 2
        out_specs += [full((dm, d))] * 2
    ins += [g, w_gu, w_gu, w_down]
    specs += [full(g.shape), pl.BlockSpec((d, tw), lambda c: (0, c)), pl.BlockSpec((d, tw), lambda c: (0, nt + c)),
              pl.BlockSpec((tw, d), lambda c: (c, 0))]
    if final is not None:
        ins.append(final)
        specs.append(full(final.shape))
    body = functools.partial(_ffn_stream_body, mix=mix is not None, final=final is not None)
    return pl.pallas_call(
        body,
        grid=(nt,),
        in_specs=specs,
        out_specs=out_specs,
        out_shape=outs,
        scratch_shapes=[pltpu.VMEM((n, d), F32), pltpu.VMEM((n, d), BF16), pltpu.VMEM((n, d), F32)],
        compiler_params=pltpu.CompilerParams(dimension_semantics=("arbitrary",),
                                             vmem_limit_bytes=VMEM_LIMIT),
        name=name,
    )(*ins)


def _inproj_body(*refs, tm, tiles_per_seq, sample):
    it = iter(refs)
    x_ref, g_ref, win_ref, cw_ref, cn_ref, bd_ref = (next(it) for _ in range(6))
    if sample:
        s1_ref, s2_ref = next(it), next(it)
        q_ref, k_ref, v_ref, yc_ref, u_ref, winb_ref = (next(it) for _ in range(6))
    else:
        q_ref, kt_ref, vtf_ref, kb_ref, vt_ref, yc_ref, mean_ref, cnew_ref = (next(it) for _ in range(8))
    ubuf = next(it)

    dc = yc_ref.shape[1]
    if sample:
        ubuf[0:8, :] = jnp.zeros((8, dc), F32)
    else:
        first = (pl.program_id(0) % tiles_per_seq) == 0

        @pl.when(first)
        def _():
            ubuf[0:8, :] = jnp.zeros((8, dc), F32)

        @pl.when(jnp.logical_not(first))
        def _():
            ubuf[0:8, :] = ubuf[tm:tm + 8, :]

    h = _rms(x_ref[...], g_ref[...]).astype(BF16)
    if sample:
        def piece(c, w):
            wp = win_ref[:, c:c + w].astype(BF16)
            winb_ref[:, c:c + w] = wp
            return _dot(h, wp)
    else:
        piece = lambda c, w: _dot(h, win_ref[:, c:c + w])
    hc = piece(0, dc)
    cg = piece(2 * dc, dc)
    bg = piece(dc, dc)
    k = piece(3 * dc + D_ATT, D_ATT)
    v = piece(3 * dc + 2 * D_ATT, D_ATT)
    if sample:
        k_ref[...] = k
        v_ref[...] = v

    u = cg * hc
    ubuf[8:tm + 8, :] = u
    um1 = ubuf[7:tm + 7, :]
    um2 = ubuf[6:tm + 6, :]
    if sample:
        t = lax.broadcasted_iota(jnp.int32, (tm, dc), 0) % u_ref.shape[1]
        um1 = jnp.where(t >= 1, um1, s1_ref[...])
        um2 = jnp.where(t >= 2, um2, s2_ref[...])
    cw = cw_ref[...]
    conv = um2 * cw[0:1, :] + um1 * cw[1:2, :] + u * cw[2:3, :]
    yc_ref[...] = _group_rms(bg * conv, cn_ref[...], bd_ref).astype(BF16)
    q_ref[...] = piece(3 * dc, D_ATT)

    if sample:
        u_ref[...] = u.reshape(u_ref.shape)
    else:
        vt = v.T
        kt_ref[0] = k.T
        vtf_ref[0] = vt
        kb_ref[...] = k.astype(BF16)
        nblk = tm // MOBA_BLOCK
        for i in range(nblk):
            vt_ref[i] = vt[:, i * MOBA_BLOCK:(i + 1) * MOBA_BLOCK].astype(BF16)
        mean_ref[0] = jnp.sum(k.reshape(nblk, MOBA_BLOCK, D_ATT), axis=1) * (1.0 / MOBA_BLOCK)
        cnew_ref[0] = ubuf[tm + 6:tm + 8, :]


def _inproj_prompt(x, g, w_in, conv_w, conv_norm, bd, *, batch, tm):
    n, d = x.shape
    dc = conv_w.shape[1]
    seq = n // batch
    tps = seq // tm
    nblk = tm // MOBA_BLOCK
    row = lambda w: pl.BlockSpec((tm, w), lambda i: (i, 0))
    tok_minor = pl.BlockSpec((1, D_ATT, tm), lambda i: (i // tps, 0, i % tps))
    f = lambda w, dt: jax.ShapeDtypeStruct((n, w), dt)
    body = functools.partial(_inproj_body, tm=tm, tiles_per_seq=tps, sample=False)
    return pl.pallas_call(
        body,
        grid=(n // tm,),
        in_specs=[row(d), _const_spec(g.shape), _const_spec(w_in.shape), _const_spec(conv_w.shape),
                  _const_spec(conv_norm.shape), _const_spec(bd.shape)],
        out_specs=[row(D_ATT), tok_minor, tok_minor, row(D_ATT),
                   pl.BlockSpec((nblk, D_ATT, MOBA_BLOCK), lambda i: (i, 0, 0)), row(dc),
                   pl.BlockSpec((1, nblk, D_ATT), lambda i: (i, 0, 0)),
                   pl.BlockSpec((1, CONV_W - 1, dc), lambda i: (i // tps, 0, 0))],
        out_shape=[f(D_ATT, F32), jax.ShapeDtypeStruct((batch, D_ATT, seq), F32),
                   jax.ShapeDtypeStruct((batch, D_ATT, seq), F32), f(D_ATT, BF16),
                   jax.ShapeDtypeStruct((n // MOBA_BLOCK, D_ATT, MOBA_BLOCK), BF16), f(dc, BF16),
                   jax.ShapeDtypeStruct((n // tm, nblk, D_ATT), F32),
                   jax.ShapeDtypeStruct((batch, CONV_W - 1, dc), F32)],
        scratch_shapes=[pltpu.VMEM((tm + 8, dc), F32)],
        compiler_params=pltpu.CompilerParams(dimension_semantics=("arbitrary",),
                                             vmem_limit_bytes=VMEM_LIMIT),
        name="inproj_prompt",
    )(x, g, w_in, conv_w, conv_norm, bd)


def _inproj_sample(x, g, w_in, conv_w, conv_norm, bd, s1, s2, *, seq):
    n, d = x.shape
    dc = conv_w.shape[1]
    tm = n
    full = lambda shape: pl.BlockSpec(shape, lambda i: (0,) * len(shape))
    f = lambda w, dt: jax.ShapeDtypeStruct((n, w), dt)
    body = functools.partial(_inproj_body, tm=tm, tiles_per_seq=1, sample=True)
    return pl.pallas_call(
        body,
        grid=(1,),
        in_specs=[full((tm, d)), full(g.shape), full(w_in.shape), full(conv_w.shape),
                  full(conv_norm.shape), full(bd.shape), full((tm, dc)), full((tm, dc))],
        out_specs=[full((tm, D_ATT)), full((tm, D_ATT)), full((tm, D_ATT)), full((tm, dc)),
                   full((n // seq, seq, dc)), full(w_in.shape)],
        out_shape=[f(D_ATT, F32), f(D_ATT, F32), f(D_ATT, F32), f(dc, BF16),
                   jax.ShapeDtypeStruct((n // seq, seq, dc), F32), jax.ShapeDtypeStruct(w_in.shape, BF16)],
        scratch_shapes=[pltpu.VMEM((tm + 8, dc), F32)],
        compiler_params=pltpu.CompilerParams(dimension_semantics=("arbitrary",),
                                             vmem_limit_bytes=VMEM_LIMIT),
        name="inproj_sample",
    )(x, g, w_in, conv_w, conv_norm, bd, s1, s2)


def _split3(x):
    hi = x.astype(BF16).astype(F32)
    mid = (x - hi).astype(BF16).astype(F32)
    lo = (x - hi - mid).astype(BF16).astype(F32)
    return hi, mid, lo


def _attn_prompt_body(q_ref, kb_ref, vt_ref, mean_ref, gain_ref, o_ref,
                      causal_ref, featk_ref, qabt_ref, colb_ref, so_ref, sa_ref, sb_ref, m_ref, l_ref, acc_ref):
    blk = MOBA_BLOCK
    b = pl.program_id(0)
    j = pl.program_id(1)
    nb = mean_ref.shape[1]
    group = LANES // N_HEADS
    qcols = HEADS_PER_SLAB * blk
    lane_q = lax.broadcasted_iota(jnp.int32, (1, qcols), 1)

    def slope_row(p):
        return jnp.where(lane_q < blk, LOG2E * _slope(HEADS_PER_SLAB * p), LOG2E * _slope(HEADS_PER_SLAB * p + 1))

    @pl.when((b == 0) & (j == 0))
    def _init_tables():
        kk = lax.broadcasted_iota(jnp.int32, (blk, qcols), 0)
        qq = lax.broadcasted_iota(jnp.int32, (blk, qcols), 1)
        causal_ref[...] = jnp.where((qq % blk) >= kk, 0.0, NEG_INF)
        ki = lax.broadcasted_iota(jnp.int32, (blk, LANES), 0).astype(F32)
        kl = lax.broadcasted_iota(jnp.int32, (blk, LANES), 1)
        featk_ref[...] = jnp.where(kl < 3, ki, jnp.where(kl < 6, 1.0, 0.0)).astype(BF16)
        fr = lax.broadcasted_iota(jnp.int32, (LANES, qcols), 0)
        for p in range(N_SLABS):
            a = slope_row(p)
            terms = _split3(a) + _split3(-a * (lane_q % blk).astype(F32))
            feat = jnp.zeros((LANES, qcols), F32)
            for r, t in enumerate(terms):
                feat = jnp.where(fr == r, t, feat)
            qabt_ref[p, LANES:, :] = feat.astype(BF16)

    qt = q_ref[...].T
    means = mean_ref[0]
    if nb < group:
        means = jnp.concatenate([means, jnp.zeros((group - nb, D_ATT), F32)], axis=0)
    mt = jnp.concatenate([means] * N_HEADS, axis=0)
    rh = lax.broadcasted_iota(jnp.int32, mt.shape, 0) // group
    ch = lax.broadcasted_iota(jnp.int32, mt.shape, 1) // HEAD_DIM
    mbd = jnp.where(rh == ch, mt, 0.0)
    gate_t = jnp.dot(mbd, qt, precision=lax.Precision.HIGHEST, preferred_element_type=F32)
    gate = jnp.concatenate([gate_t[h * group:(h + 1) * group, :] for h in range(N_HEADS)], axis=1)

    n_idx = lax.broadcasted_iota(jnp.int32, gate.shape, 0)
    n_f = n_idx.astype(F32)
    valid = n_idx < j
    work = jnp.where(valid, gate, NEG_INF)
    picked = jnp.zeros(gate.shape, F32)
    for _ in range(MOBA_TOPK):
        top = jnp.max(work, axis=0, keepdims=True)
        first = jnp.min(jnp.where(work == top, n_f, float(group)), axis=0, keepdims=True)
        pick = n_f == first
        picked = jnp.where(pick, 1.0, picked)
        work = jnp.where(pick, REMOVED, work)
    colb_ref[...] = jnp.where((picked > 0.0) & valid, 0.0, NEG_INF)

    row_d = lax.broadcasted_iota(jnp.int32, (LANES, blk), 0)
    for p in range(N_SLABS):
        qs = qt[p * LANES:(p + 1) * LANES, :] * (SCALE * LOG2E)
        qa = jnp.where(row_d < HEAD_DIM, qs, 0.0)
        qb = jnp.where(row_d >= HEAD_DIM, qs, 0.0)
        qabt_ref[p, :LANES, :] = jnp.concatenate([qa, qb], axis=1).astype(BF16)

    slabs = [slice(p * LANES, (p + 1) * LANES) for p in range(N_SLABS)]
    ones_rows = jnp.ones((SUM_ROWS, blk), BF16)

    def park_scores(n, dst, p):
        off = pl.multiple_of(n * blk, blk)
        keys = jnp.concatenate([kb_ref[pl.ds(off, blk), slabs[p]], featk_ref[...]], axis=1)
        dst[p] = _dot(keys, qabt_ref[p])

    def weighted_values(n, p, e):
        va = jnp.concatenate([vt_ref[n, slabs[p], :], ones_rows], axis=0)
        pv = _dot(va, e.astype(BF16))
        return pv[:LANES], pv[LANES:LANES + 1]

    def reduce_past(n, src, p):
        shift = ((j - n) * blk).astype(F32)
        crow = colb_ref[pl.ds(n, 1), p * qcols:(p + 1) * qcols] - slope_row(p) * shift
        sb = src[p]
        m_prev = m_ref[p]
        m_new = jnp.maximum(m_prev, jnp.max(sb, axis=0, keepdims=True) + crow)
        alpha = jnp.exp2(m_prev - m_new)
        pv, esum = weighted_values(n, p, jnp.exp2(sb - (m_new - crow)))
        m_ref[p] = m_new
        l_ref[p] = alpha * l_ref[p] + esum
        acc_ref[p] = alpha * acc_ref[p] + pv

    for p in range(N_SLABS):
        park_scores(j, so_ref, p)
    for p in range(N_SLABS):
        park_scores(0, sa_ref, p)
    for p in range(N_SLABS):
        sb = so_ref[p] + causal_ref[...]
        m = jnp.max(sb, axis=0, keepdims=True)
        pv, esum = weighted_values(j, p, jnp.exp2(sb - m))
        m_ref[p] = m
        l_ref[p] = esum
        acc_ref[p] = pv

    def two_blocks(i, carry):
        n0 = 2 * i
        n2 = jnp.minimum(n0 + 2, j)
        parks = [(n0 + 1, sb_ref, p) for p in range(N_SLABS)] + [(n2, sa_ref, p) for p in range(N_SLABS)]
        reduces = [(n0, sa_ref, p) for p in range(N_SLABS)] + [(n0 + 1, sb_ref, p) for p in range(N_SLABS)]
        for k in range(LEAD):
            park_scores(*parks[k])
        for k, red in enumerate(reduces):
            reduce_past(*red)
            if k + LEAD < len(parks):
                park_scores(*parks[k + LEAD])
        return carry

    lax.fori_loop(0, j // 2, two_blocks, 0)

    @pl.when(j % 2 == 1)
    def _odd_tail():
        for p in range(N_SLABS):
            reduce_past(j - 1, sa_ref, p)

    for p in range(N_SLABS):
        o = acc_ref[p] / l_ref[p]
        o2 = jnp.where(row_d < HEAD_DIM, o[:, :blk], o[:, blk:])
        sq = o2 * o2
        ms_a = jnp.sum(sq[:HEAD_DIM], axis=0, keepdims=True) * (1.0 / HEAD_DIM)
        ms_b = jnp.sum(sq[HEAD_DIM:], axis=0, keepdims=True) * (1.0 / HEAD_DIM)
        inv = jnp.where(row_d < HEAD_DIM, lax.rsqrt(ms_a + EPS), lax.rsqrt(ms_b + EPS))
        ls = slice(p * LANES, (p + 1) * LANES)
        o_ref[:, ls] = ((o2 * inv).T * gain_ref[:, ls]).astype(BF16)


def _attn_prompt(q, kb, vt, means, gain, *, batch):
    n = q.shape[0]
    seq = n // batch
    nb = seq // MOBA_BLOCK
    group = LANES // N_HEADS
    assert nb <= group and seq % MOBA_BLOCK == 0
    means = means.reshape(batch, nb, D_ATT)
    qcols = HEADS_PER_SLAB * MOBA_BLOCK
    return pl.pallas_call(
        _attn_prompt_body,
        grid=(batch, nb),
        in_specs=[pl.BlockSpec((MOBA_BLOCK, D_ATT), lambda b, j: (b * nb + j, 0)),
                  pl.BlockSpec((seq, D_ATT), lambda b, j: (b, 0)),
                  pl.BlockSpec((nb, D_ATT, MOBA_BLOCK), lambda b, j: (b, 0, 0)),
                  pl.BlockSpec((1, nb, D_ATT), lambda b, j: (b, 0, 0)),
                  pl.BlockSpec(gain.shape, lambda b, j: (0, 0))],
        out_specs=pl.BlockSpec((MOBA_BLOCK, D_ATT), lambda b, j: (b * nb + j, 0)),
        out_shape=jax.ShapeDtypeStruct((n, D_ATT), BF16),
        scratch_shapes=[pltpu.VMEM((MOBA_BLOCK, qcols), F32),
                        pltpu.VMEM((MOBA_BLOCK, LANES), BF16),
                        pltpu.VMEM((N_SLABS, 2 * LANES, qcols), BF16),
                        pltpu.VMEM((group, N_HEADS * MOBA_BLOCK), F32),
                        pltpu.VMEM((N_SLABS, MOBA_BLOCK, qcols), F32),
                        pltpu.VMEM((N_SLABS, MOBA_BLOCK, qcols), F32),
                        pltpu.VMEM((N_SLABS, MOBA_BLOCK, qcols), F32),
                        pltpu.VMEM((N_SLABS, 1, qcols), F32),
                        pltpu.VMEM((N_SLABS, 1, qcols), F32),
                        pltpu.VMEM((N_SLABS, LANES, qcols), F32)],
        compiler_params=pltpu.CompilerParams(dimension_semantics=("arbitrary", "arbitrary"),
                                             vmem_limit_bytes=VMEM_LIMIT),
        name="attn_prompt",
    )(q, kb, vt, means, gain)


def _attn_sample_t_body(pt_ref, q_ref, kn_ref, vn_ref, gain_ref, bd_ref, ckt_hbm, cvt_hbm, o_ref,
                        kbuf, vbuf, s_ref, acc_ref, ksem, vsem, *, past_len, page, page_base):
    b = pl.program_id(0)
    nbat = pl.num_programs(0)
    n_pages = past_len // page
    ppb = MOBA_BLOCK // page
    nb = past_len // MOBA_BLOCK
    t_new = q_ref.shape[1]
    rows = N_HEADS * t_new

    def page_copy(hbm, buf, sem, bb, pg):
        return pltpu.make_async_copy(hbm.at[page_base + pt_ref[bb, pg]], buf.at[pg], sem.at[pg])

    def start_all(hbm, buf, sem, bb):
        def body(i, c):
            for k in range(DMA_THREADS):
                page_copy(hbm, buf, sem, bb, i * DMA_THREADS + k).start(priority=k)
            return c
        lax.fori_loop(0, n_pages // DMA_THREADS, body, 0)

    @pl.when(b == 0)
    def _():
        start_all(ckt_hbm, kbuf, ksem, b)
        start_all(cvt_hbm, vbuf, vsem, b)

    qt = jnp.concatenate([q_ref[0]] * N_HEADS, axis=0)
    rh = lax.broadcasted_iota(jnp.int32, qt.shape, 0) // t_new
    ch = lax.broadcasted_iota(jnp.int32, qt.shape, 1) // HEAD_DIM
    qs = jnp.where(rh == ch, qt, 0.0) * SCALE
    q_hi = qs.astype(BF16)
    q_lo = (qs - q_hi.astype(F32)).astype(BF16)
    qq = jnp.concatenate([q_hi, q_lo], axis=0)

    def k_tile(i, c):
        pgs = [i * K_TILE + k for k in range(K_TILE)]
        for pg in pgs:
            page_copy(ckt_hbm, kbuf, ksem, b, pg).wait()
        for pg in pgs:
            s2 = _dot(qq, kbuf[pg].reshape(D_ATT, page).astype(BF16))
            s_ref[pg] = s2[:rows] + s2[rows:]
        return c
    lax.fori_loop(0, n_pages // K_TILE, k_tile, 0)

    @pl.when(b + 1 < nbat)
    def _():
        start_all(ckt_hbm, kbuf, ksem, b + 1)

    lane = lax.broadcasted_iota(jnp.int32, (rows, LANES), 1)
    gate = jnp.zeros((rows, LANES), F32)
    gcols = []
    for n in range(nb):
        tot = s_ref[n * ppb]
        for i in range(1, ppb):
            tot = tot + s_ref[n * ppb + i]
        g = jnp.sum(tot, axis=1, keepdims=True)
        gcols.append(g)
        gate = jnp.where(lane == n, g, gate)
    rank = jnp.zeros(gate.shape, jnp.int32)
    for m in range(nb):
        beats = (gcols[m] > gate) | ((gcols[m] == gate) & (lane > m))
        rank = rank + beats.astype(jnp.int32)
    colb = jnp.where(rank < MOBA_TOPK, 0.0, NEG_INF)

    r1 = lax.broadcasted_iota(jnp.int32, (rows, 1), 0)
    tq = r1 % t_new
    slope = jnp.zeros((rows, 1), F32)
    for h in range(N_HEADS):
        slope = jnp.where(r1 // t_new == h, _slope(h), slope)
    in_page = slope * (tq - lane).astype(F32)

    zpad = jnp.zeros((LANES - t_new, D_ATT), F32)
    kn = jnp.concatenate([kn_ref[0], zpad], axis=0).astype(BF16)
    vn = jnp.concatenate([vn_ref[0], zpad], axis=0).astype(BF16)
    s2 = _dot_nt(qq, kn)
    s_own = jnp.where(lane <= tq, s2[:rows] + s2[rows:] - in_page, NEG_INF)

    mrun = s_own
    for n in range(nb):
        mask_n = jnp.sum(jnp.where(lane == n, colb, 0.0), axis=1, keepdims=True)
        for i in range(ppb):
            pg = n * ppb + i
            sn = s_ref[pg] - in_page + (mask_n - slope * float(past_len - pg * page))
            s_ref[pg] = sn
            mrun = jnp.maximum(mrun, sn)
    m = jnp.max(mrun, axis=1, keepdims=True)

    e_own = jnp.exp(s_own - m)
    lrun = e_own
    for pg in range(n_pages):
        e = jnp.exp(s_ref[pg] - m)
        s_ref[pg] = e
        lrun = lrun + e
    l = jnp.sum(lrun, axis=1, keepdims=True)

    acc_ref[...] = jnp.zeros(acc_ref.shape, F32)
    zrows = jnp.zeros((LANES - rows, V_TILE * page), BF16)

    def v_tile(i, c):
        pgs = [i * V_TILE + k for k in range(V_TILE)]
        for pg in pgs:
            page_copy(cvt_hbm, vbuf, vsem, b, pg).wait()
        vt = jnp.concatenate([vbuf[pg].reshape(D_ATT, page) for pg in pgs], axis=1).astype(BF16)
        p = jnp.concatenate([s_ref[pg] for pg in pgs], axis=1).astype(BF16)
        acc_ref[...] += _dot_nt(vt, jnp.concatenate([p, zrows], axis=0))
        return c
    lax.fori_loop(0, n_pages // V_TILE, v_tile, 0)

    @pl.when(b + 1 < nbat)
    def _():
        start_all(cvt_hbm, vbuf, vsem, b + 1)

    acc = acc_ref[...].T[:rows] + _dot(e_own.astype(BF16), vn)
    accn = acc / l
    ch8 = lax.broadcasted_iota(jnp.int32, (t_new, D_ATT), 1) // HEAD_DIM
    out = jnp.zeros((t_new, D_ATT), F32)
    for h in range(N_HEADS):
        out = jnp.where(ch8 == h, accn[h * t_new:(h + 1) * t_new, :], out)
    o_ref[0] = _group_rms(out, gain_ref[...], bd_ref).astype(BF16)


def _attn_sample_t(page_table, q, kn, vn, gain, bd, cache_kt, cache_vt, *, page_base, past_len):
    nbat, t_new, _ = q.shape
    page = cache_kt.shape[3]
    n_pages = past_len // page
    rows = N_HEADS * t_new
    assert past_len % MOBA_BLOCK == 0 and MOBA_BLOCK % page == 0 and page == LANES
    assert rows <= LANES and t_new % 8 == 0 and past_len // MOBA_BLOCK <= LANES
    assert n_pages % V_TILE == 0 and n_pages % K_TILE == 0
    body = functools.partial(_attn_sample_t_body, past_len=past_len, page=page, page_base=page_base)
    per_b = pl.BlockSpec((1, t_new, D_ATT), lambda b, pt: (b, 0, 0))
    grid_spec = pltpu.PrefetchScalarGridSpec(
        num_scalar_prefetch=1,
        grid=(nbat,),
        in_specs=[per_b, per_b, per_b,
                  pl.BlockSpec(gain.shape, lambda b, pt: (0, 0)),
                  pl.BlockSpec(bd.shape, lambda b, pt: (0, 0)),
                  pl.BlockSpec(memory_space=pl.ANY),
                  pl.BlockSpec(memory_space=pl.ANY)],
        out_specs=per_b,
        scratch_shapes=[pltpu.VMEM((n_pages, N_HEADS, HEAD_DIM, page), F32),
                        pltpu.VMEM((n_pages, N_HEADS, HEAD_DIM, page), F32),
                        pltpu.VMEM((n_pages, rows, page), F32),
                        pltpu.VMEM((D_ATT, LANES), F32),
                        pltpu.SemaphoreType.DMA((n_pages,)),
                        pltpu.SemaphoreType.DMA((n_pages,))],
    )
    return pl.pallas_call(
        body,
        grid_spec=grid_spec,
        out_shape=jax.ShapeDtypeStruct((nbat, t_new, D_ATT), BF16),
        compiler_params=pltpu.CompilerParams(dimension_semantics=("arbitrary",),
                                             vmem_limit_bytes=VMEM_LIMIT),
        name="attn_sample",
    )(page_table, q, kn, vn, gain, bd, cache_kt, cache_vt)


def kernel(x_prompt, x_sample, cache_k, cache_v, state_conv, page_table, ffn1_norm, ffn1_w_gu, ffn1_w_down,
           mix_norm, w_in, conv_w, conv_out_norm, attn_out_norm, w_out, ffn2_norm, ffn2_w_gu, ffn2_w_down,
           final_norm):
    bp, seq, d = x_prompt.shape
    bs, dseq, _ = x_sample.shape
    depth, n_pool, page = cache_k.shape[:3]
    d_ff = ffn1_w_down.shape[1]
    dc = conv_w.shape[2]
    past_len = page_table.shape[1] * page

    ck = jnp.transpose(cache_k, (0, 1, 3, 4, 2)).reshape(depth * n_pool, N_HEADS, HEAD_DIM, page)
    cv = jnp.transpose(cache_v, (0, 1, 3, 4, 2)).reshape(depth * n_pool, N_HEADS, HEAD_DIM, page)
    gi = lax.broadcasted_iota(jnp.int32, (D_ATT, D_ATT), 0) // HEAD_DIM
    gj = lax.broadcasted_iota(jnp.int32, (D_ATT, D_ATT), 1) // HEAD_DIM
    bd = (gi == gj).astype(BF16)

    xp = x_prompt.reshape(bp * seq, d)
    xs = x_sample.reshape(bs * dseq, d)
    tm_p = 512
    row = lambda a: a.reshape(1, -1)
    outs = [[] for _ in range(6)]
    for l in range(depth):
        g1, gm, g2 = row(ffn1_norm[l]), row(mix_norm[l]), row(ffn2_norm[l])
        gc, ga = row(conv_out_norm[l]), row(attn_out_norm[l])
        last = l == depth - 1
        gfin = row(final_norm) if last else None

        st = state_conv[l]
        zpad = jnp.zeros((bs, dseq - (CONV_W - 1), dc), F32)
        s2 = jnp.concatenate([st, zpad], axis=1).reshape(bs * dseq, dc)
        s1 = jnp.concatenate([st[:, 1:2], jnp.zeros((bs, dseq - 1, dc), F32)], axis=1).reshape(bs * dseq, dc)
        x1s, wg1, wu1, wd1 = _ffn_stream_call(xs, g1, ffn1_w_gu[l], ffn1_w_down[l], name="ffn1_sample")
        qs, ks, vs, ycs, us, win = _inproj_sample(x1s, gm, w_in[l], conv_w[l], gc, bd, s1, s2, seq=dseq)
        r3 = lambda a: a.reshape(bs, dseq, D_ATT)
        yas = _attn_sample_t(page_table, r3(qs), r3(ks), r3(vs), ga, bd, ck, cv,
                             page_base=l * n_pool, past_len=past_len)
        xs, wg2, wu2, wd2, woc, woa = _ffn_stream_call(
            x1s, g2, ffn2_w_gu[l], ffn2_w_down[l], mix=(ycs, yas.reshape(bs * dseq, D_ATT), w_out[l]),
            final=gfin, name="ffn2_sample")
        outs[3].append(ks.reshape(bs, dseq, N_HEADS, HEAD_DIM))
        outs[4].append(vs.reshape(bs, dseq, N_HEADS, HEAD_DIM))
        outs[5].append(us[:, dseq - (CONV_W - 1):, :])

        x1 = _ffn_call(xp, g1, wg1, wu1, wd1, tm=tm_p, name="ffn1_prompt")
        q, kt, vtf, kb, vt, yc, means, cnew = _inproj_prompt(x1, gm, win, conv_w[l], gc, bd, batch=bp, tm=tm_p)
        ya = _attn_prompt(q, kb, vt, means, ga, batch=bp)
        xp = _ffn_call(x1, g2, wg2, wu2, wd2, tm=tm_p, mix=(yc, ya, woc, woa), final=gfin, name="ffn2_prompt")
        tok_major = lambda a: a.reshape(bp, N_HEADS, HEAD_DIM, seq).transpose(0, 3, 1, 2)
        outs[0].append(tok_major(kt))
        outs[1].append(tok_major(vtf))
        outs[2].append(cnew)

    y_prompt = xp.reshape(bp, seq, d)
    y_sample = xs.reshape(bs, dseq, d)
    kp, vp, cp, ksn, vsn, csn = (jnp.stack(o) for o in outs)
    return (y_prompt, y_sample, kp, vp, cp, ksn, vsn, csn)
```

```python
import functools

import jax
import jax.numpy as jnp
from jax import lax
from jax.experimental import pallas as pl
from jax.experimental.pallas import tpu as pltpu

F32 = jnp.float32
BF16 = jnp.bfloat16

N_HEADS = 8
HEAD_DIM = 64
D_ATT = N_HEADS * HEAD_DIM
N_CONV_GROUPS = 8
CONV_W = 3
MOBA_BLOCK = 256
MOBA_TOPK = 3
EPS = 1e-5
NEG_INF = -1e30
REMOVED = -3e38
SCALE = HEAD_DIM ** -0.5
LOG2E = 1.4426950408889634

LANES = 128
MXU_WIDTH = 256
HEADS_PER_SLAB = LANES // HEAD_DIM
N_SLABS = D_ATT // LANES
VMEM_LIMIT = 56 * 1024 * 1024
SUM_ROWS = 16
DMA_THREADS = 2
K_TILE = 8
V_TILE = 8

NT_DIMS = (((1,), (1,)), ((), ()))


def _slope(h):
    return 2.0 ** (-(8.0 / N_HEADS) * (h + 1))


def _dot(a, b):
    return jnp.dot(a, b, preferred_element_type=F32)


def _dot_nt(a, b, precision=None):
    return lax.dot_general(a, b, NT_DIMS, precision=precision, preferred_element_type=F32)


def _rms(x, g):
    ms = jnp.mean(x * x, axis=-1, keepdims=True)
    return x * lax.rsqrt(ms + EPS) * g


def _group_sumsq(y, bd_ref):
    y2 = y * y
    hi = y2.astype(BF16)
    lo = (y2 - hi.astype(F32)).astype(BF16)
    bd = bd_ref[...]
    return _dot(hi, bd) + _dot(lo, bd)


def _group_rms(y, g, bd_ref):
    ms = _group_sumsq(y, bd_ref) * (1.0 / HEAD_DIM)
    return y * lax.rsqrt(ms + EPS) * g


def _const_spec(shape):
    nd = len(shape)
    return pl.BlockSpec(shape, lambda *_: (0,) * nd, pipeline_mode=pl.Buffered(1))


def _ffn_body(*refs, mix, final, bounds):
    it = iter(refs)
    x_ref = next(it)
    if mix:
        yc_ref, ya_ref, woc_ref, woa_ref = next(it), next(it), next(it), next(it)
    g_ref, wg_ref, wu_ref, wd_ref = next(it), next(it), next(it), next(it)
    gf_ref = next(it) if final else None
    o_ref = next(it)

    x = x_ref[...]
    if mix:
        x = x + _dot(yc_ref[...], woc_ref[...]) + _dot(ya_ref[...], woa_ref[...])
    h = _rms(x, g_ref[...]).astype(BF16)
    acc = jnp.zeros(x.shape, F32)
    for lo, hi in zip(bounds[:-1], bounds[1:]):
        gate = _dot(h, wg_ref[:, lo:hi])
        up = _dot(h, wu_ref[:, lo:hi])
        act = (gate * jax.nn.sigmoid(gate) * up).astype(BF16)
        acc = acc + _dot(act, wd_ref[lo:hi, :])
    x = x + 0.5 * acc
    if final:
        x = _rms(x, gf_ref[...])
    o_ref[...] = x


def _ffn_call(x, g, wg, wu, wd, *, tm, mix=None, final=None, name):
    n, d = x.shape
    d_ff = wd.shape[0]
    row = lambda w: pl.BlockSpec((tm, w), lambda i: (i, 0))
    ins, specs = [x], [row(d)]
    if mix is not None:
        yc, ya, woc, woa = mix
        ins += [yc, ya, woc, woa]
        specs += [row(yc.shape[1]), row(ya.shape[1]), _const_spec(woc.shape), _const_spec(woa.shape)]
    ins += [g, wg, wu, wd]
    specs += [_const_spec(g.shape), _const_spec(wg.shape), _const_spec(wu.shape), _const_spec(wd.shape)]
    if final is not None:
        ins.append(final)
        specs.append(_const_spec(final.shape))
    assert d_ff % MXU_WIDTH == 0
    tiles = d_ff // MXU_WIDTH
    bounds = (0, (tiles + 1) // 2 * MXU_WIDTH, d_ff)
    body = functools.partial(_ffn_body, mix=mix is not None, final=final is not None, bounds=bounds)
    return pl.pallas_call(
        body,
        grid=(n // tm,),
        in_specs=specs,
        out_specs=row(d),
        out_shape=jax.ShapeDtypeStruct((n, d), F32),
        compiler_params=pltpu.CompilerParams(dimension_semantics=("arbitrary",),
                                             vmem_limit_bytes=VMEM_LIMIT),
        name=name,
    )(*ins)


def _ffn_stream_body(*refs, mix, final):
    it = iter(refs)
    x_ref = next(it)
    if mix:
        yc_ref, ya_ref, woc_ref, woa_ref = next(it), next(it), next(it), next(it)
    g_ref, wg_ref, wu_ref, wd_ref = next(it), next(it), next(it), next(it)
    gf_ref = next(it) if final else None
    o_ref, wgb_ref, wub_ref, wdb_ref = next(it), next(it), next(it), next(it)
    if mix:
        wocb_ref, woab_ref = next(it), next(it)
    x_scr, h_scr, acc_scr = next(it), next(it), next(it)
    c = pl.program_id(0)

    @pl.when(c == 0)
    def _():
        x = x_ref[...]
        if mix:
            woc, woa = woc_ref[...].astype(BF16), woa_ref[...].astype(BF16)
            wocb_ref[...] = woc
            woab_ref[...] = woa
            x = x + _dot(yc_ref[...], woc) + _dot(ya_ref[...], woa)
        x_scr[...] = x
        h_scr[...] = _rms(x, g_ref[...]).astype(BF16)
        acc_scr[...] = jnp.zeros(acc_scr.shape, F32)

    wg, wu, wd = wg_ref[...].astype(BF16), wu_ref[...].astype(BF16), wd_ref[...].astype(BF16)
    wgb_ref[...] = wg
    wub_ref[...] = wu
    wdb_ref[...] = wd
    h = h_scr[...]
    gate = _dot(h, wg)
    act = (gate * jax.nn.sigmoid(gate) * _dot(h, wu)).astype(BF16)
    acc_scr[...] += _dot(act, wd)

    @pl.when(c == pl.num_programs(0) - 1)
    def _():
        x = x_scr[...] + 0.5 * acc_scr[...]
        if final:
            x = _rms(x, gf_ref[...])
        o_ref[...] = x


def _ffn_stream_call(x, g, w_gu, w_down, *, mix=None, final=None, name):
    n, d = x.shape
    d_ff = w_down.shape[0]
    tw = MXU_WIDTH
    assert d_ff % tw == 0 and w_gu.shape == (d, 2 * d_ff)
    nt = d_ff // tw
    full = lambda shape: pl.BlockSpec(shape, lambda c: (0,) * len(shape))
    ins, specs = [x], [full((n, d))]
    outs = [jax.ShapeDtypeStruct((n, d), F32), jax.ShapeDtypeStruct((d, d_ff), BF16),
            jax.ShapeDtypeStruct((d, d_ff), BF16), jax.ShapeDtypeStruct((d_ff, d), BF16)]
    out_specs = [full((n, d)), pl.BlockSpec((d, tw), lambda c: (0, c)), pl.BlockSpec((d, tw), lambda c: (0, c)),
                 pl.BlockSpec((tw, d), lambda c: (c, 0))]
    if mix is not None:
        yc, ya, w_out = mix
        dm = yc.shape[1]
        assert w_out.shape == (dm + ya.shape[1], d) and ya.shape[1] == dm
        ins += [yc, ya, w_out, w_out]
        specs += [full(yc.shape), full(ya.shape), pl.BlockSpec((dm, d), lambda c: (0, 0)),
                  pl.BlockSpec((dm, d), lambda c: (1, 0))]
        outs += [jax.ShapeDtypeStruct((dm, d), BF16)] * 2
        out_specs += [full((dm, d))] * 2
    ins += [g, w_gu, w_gu, w_down]
    specs += [full(g.shape), pl.BlockSpec((d, tw), lambda c: (0, c)), pl.BlockSpec((d, tw), lambda c: (0, nt + c)),
              pl.BlockSpec((tw, d), lambda c: (c, 0))]
    if final is not None:
        ins.append(final)
        specs.append(full(final.shape))
    body = functools.partial(_ffn_stream_body, mix=mix is not None, final=final is not None)
    return pl.pallas_call(
        body,
        grid=(nt,),
        in_specs=specs,
        out_specs=out_specs,
        out_shape=outs,
        scratch_shapes=[pltpu.VMEM((n, d), F32), pltpu.VMEM((n, d), BF16), pltpu.VMEM((n, d), F32)],
        compiler_params=pltpu.CompilerParams(dimension_semantics=("arbitrary",),
                                             vmem_limit_bytes=VMEM_LIMIT),
        name=name,
    )(*ins)


def _inproj_body(*refs, tm, tiles_per_seq, sample):
    it = iter(refs)
    x_ref, g_ref, win_ref, cw_ref, cn_ref, bd_ref = (next(it) for _ in range(6))
    if sample:
        s1_ref, s2_ref = next(it), next(it)
        q_ref, k_ref, v_ref, yc_ref, u_ref, winb_ref = (next(it) for _ in range(6))
    else:
        q_ref, kt_ref, vtf_ref, kb_ref, vt_ref, yc_ref, mean_ref, cnew_ref = (next(it) for _ in range(8))
    ubuf = next(it)

    dc = yc_ref.shape[1]
    if sample:
        ubuf[0:8, :] = jnp.zeros((8, dc), F32)
    else:
        first = (pl.program_id(0) % tiles_per_seq) == 0

        @pl.when(first)
        def _():
            ubuf[0:8, :] = jnp.zeros((8, dc), F32)

        @pl.when(jnp.logical_not(first))
        def _():
            ubuf[0:8, :] = ubuf[tm:tm + 8, :]

    h = _rms(x_ref[...], g_ref[...]).astype(BF16)
    if sample:
        def piece(c, w):
            wp = win_ref[:, c:c + w].astype(BF16)
            winb_ref[:, c:c + w] = wp
            return _dot(h, wp)
    else:
        piece = lambda c, w: _dot(h, win_ref[:, c:c + w])
    hc = piece(0, dc)
    cg = piece(2 * dc, dc)
    bg = piece(dc, dc)
    k = piece(3 * dc + D_ATT, D_ATT)
    v = piece(3 * dc + 2 * D_ATT, D_ATT)
    if sample:
        k_ref[...] = k
        v_ref[...] = v

    u = cg * hc
    ubuf[8:tm + 8, :] = u
    um1 = ubuf[7:tm + 7, :]
    um2 = ubuf[6:tm + 6, :]
    if sample:
        t = lax.broadcasted_iota(jnp.int32, (tm, dc), 0) % u_ref.shape[1]
        um1 = jnp.where(t >= 1, um1, s1_ref[...])
        um2 = jnp.where(t >= 2, um2, s2_ref[...])
    cw = cw_ref[...]
    conv = um2 * cw[0:1, :] + um1 * cw[1:2, :] + u * cw[2:3, :]
    yc_ref[...] = _group_rms(bg * conv, cn_ref[...], bd_ref).astype(BF16)
    q_ref[...] = piece(3 * dc, D_ATT)

    if sample:
        u_ref[...] = u.reshape(u_ref.shape)
    else:
        vt = v.T
        kt_ref[0] = k.T
        vtf_ref[0] = vt
        kb_ref[...] = k.astype(BF16)
        nblk = tm // MOBA_BLOCK
        for i in range(nblk):
            vt_ref[i] = vt[:, i * MOBA_BLOCK:(i + 1) * MOBA_BLOCK].astype(BF16)
        mean_ref[0] = jnp.sum(k.reshape(nblk, MOBA_BLOCK, D_ATT), axis=1) * (1.0 / MOBA_BLOCK)
        cnew_ref[0] = ubuf[tm + 6:tm + 8, :]


def _inproj_prompt(x, g, w_in, conv_w, conv_norm, bd, *, batch, tm):
    n, d = x.shape
    dc = conv_w.shape[1]
    seq = n // batch
    tps = seq // tm
    nblk = tm // MOBA_BLOCK
    row = lambda w: pl.BlockSpec((tm, w), lambda i: (i, 0))
    tok_minor = pl.BlockSpec((1, D_ATT, tm), lambda i: (i // tps, 0, i % tps))
    f = lambda w, dt: jax.ShapeDtypeStruct((n, w), dt)
    body = functools.partial(_inproj_body, tm=tm, tiles_per_seq=tps, sample=False)
    return pl.pallas_call(
        body,
        grid=(n // tm,),
        in_specs=[row(d), _const_spec(g.shape), _const_spec(w_in.shape), _const_spec(conv_w.shape),
                  _const_spec(conv_norm.shape), _const_spec(bd.shape)],
        out_specs=[row(D_ATT), tok_minor, tok_minor, row(D_ATT),
                   pl.BlockSpec((nblk, D_ATT, MOBA_BLOCK), lambda i: (i, 0, 0)), row(dc),
                   pl.BlockSpec((1, nblk, D_ATT), lambda i: (i, 0, 0)),
                   pl.BlockSpec((1, CONV_W - 1, dc), lambda i: (i // tps, 0, 0))],
        out_shape=[f(D_ATT, F32), jax.ShapeDtypeStruct((batch, D_ATT, seq), F32),
                   jax.ShapeDtypeStruct((batch, D_ATT, seq), F32), f(D_ATT, BF16),
                   jax.ShapeDtypeStruct((n // MOBA_BLOCK, D_ATT, MOBA_BLOCK), BF16), f(dc, BF16),
                   jax.ShapeDtypeStruct((n // tm, nblk, D_ATT), F32),
                   jax.ShapeDtypeStruct((batch, CONV_W - 1, dc), F32)],
        scratch_shapes=[pltpu.VMEM((tm + 8, dc), F32)],
        compiler_params=pltpu.CompilerParams(dimension_semantics=("arbitrary",),
                                             vmem_limit_bytes=VMEM_LIMIT),
        name="inproj_prompt",
    )(x, g, w_in, conv_w, conv_norm, bd)


def _inproj_sample(x, g, w_in, conv_w, conv_norm, bd, s1, s2, *, seq):
    n, d = x.shape
    dc = conv_w.shape[1]
    tm = n
    full = lambda shape: pl.BlockSpec(shape, lambda i: (0,) * len(shape))
    f = lambda w, dt: jax.ShapeDtypeStruct((n, w), dt)
    body = functools.partial(_inproj_body, tm=tm, tiles_per_seq=1, sample=True)
    return pl.pallas_call(
        body,
        grid=(1,),
        in_specs=[full((tm, d)), full(g.shape), full(w_in.shape), full(conv_w.shape),
                  full(conv_norm.shape), full(bd.shape), full((tm, dc)), full((tm, dc))],
        out_specs=[full((tm, D_ATT)), full((tm, D_ATT)), full((tm, D_ATT)), full((tm, dc)),
                   full((n // seq, seq, dc)), full(w_in.shape)],
        out_shape=[f(D_ATT, F32), f(D_ATT, F32), f(D_ATT, F32), f(dc, BF16),
                   jax.ShapeDtypeStruct((n // seq, seq, dc), F32), jax.ShapeDtypeStruct(w_in.shape, BF16)],
        scratch_shapes=[pltpu.VMEM((tm + 8, dc), F32)],
        compiler_params=pltpu.CompilerParams(dimension_semantics=("arbitrary",),
                                             vmem_limit_bytes=VMEM_LIMIT),
        name="inproj_sample",
    )(x, g, w_in, conv_w, conv_norm, bd, s1, s2)


def _split3(x):
    hi = x.astype(BF16).astype(F32)
    mid = (x - hi).astype(BF16).astype(F32)
    lo = (x - hi - mid).astype(BF16).astype(F32)
    return hi, mid, lo


def _attn_prompt_body(q_ref, kb_ref, vt_ref, mean_ref, gain_ref, o_ref,
                      causal_ref, featk_ref, qabt_ref, colb_ref, so_ref, sa0_ref, sb0_ref, sa1_ref, sb1_ref,
                      m_ref, l_ref, acc_ref):
    blk = MOBA_BLOCK
    b = pl.program_id(0)
    j = pl.program_id(1)
    nb = mean_ref.shape[1]
    group = LANES // N_HEADS
    qcols = HEADS_PER_SLAB * blk
    lane_q = lax.broadcasted_iota(jnp.int32, (1, qcols), 1)

    def slope_row(p):
        return jnp.where(lane_q < blk, LOG2E * _slope(HEADS_PER_SLAB * p), LOG2E * _slope(HEADS_PER_SLAB * p + 1))

    @pl.when((b == 0) & (j == 0))
    def _init_tables():
        kk = lax.broadcasted_iota(jnp.int32, (blk, qcols), 0)
        qq = lax.broadcasted_iota(jnp.int32, (blk, qcols), 1)
        causal_ref[...] = jnp.where((qq % blk) >= kk, 0.0, NEG_INF)
        ki = lax.broadcasted_iota(jnp.int32, (blk, LANES), 0).astype(F32)
        kl = lax.broadcasted_iota(jnp.int32, (blk, LANES), 1)
        featk_ref[...] = jnp.where(kl < 3, ki, jnp.where(kl < 6, 1.0, 0.0)).astype(BF16)
        fr = lax.broadcasted_iota(jnp.int32, (LANES, qcols), 0)
        for p in range(N_SLABS):
            a = slope_row(p)
            terms = _split3(a) + _split3(-a * (lane_q % blk).astype(F32))
            feat = jnp.zeros((LANES, qcols), F32)
            for r, t in enumerate(terms):
                feat = jnp.where(fr == r, t, feat)
            qabt_ref[p, LANES:, :] = feat.astype(BF16)

    qt = q_ref[...].T
    means = mean_ref[0]
    if nb < group:
        means = jnp.concatenate([means, jnp.zeros((group - nb, D_ATT), F32)], axis=0)
    mt = jnp.concatenate([means] * N_HEADS, axis=0)
    rh = lax.broadcasted_iota(jnp.int32, mt.shape, 0) // group
    ch = lax.broadcasted_iota(jnp.int32, mt.shape, 1) // HEAD_DIM
    mbd = jnp.where(rh == ch, mt, 0.0)
    gate_t = jnp.dot(mbd, qt, precision=lax.Precision.HIGHEST, preferred_element_type=F32)
    gate = jnp.concatenate([gate_t[h * group:(h + 1) * group, :] for h in range(N_HEADS)], axis=1)

    n_idx = lax.broadcasted_iota(jnp.int32, gate.shape, 0)
    n_f = n_idx.astype(F32)
    valid = n_idx < j
    work = jnp.where(valid, gate, NEG_INF)
    picked = jnp.zeros(gate.shape, F32)
    for _ in range(MOBA_TOPK):
        top = jnp.max(work, axis=0, keepdims=True)
        first = jnp.min(jnp.where(work == top, n_f, float(group)), axis=0, keepdims=True)
        pick = n_f == first
        picked = jnp.where(pick, 1.0, picked)
        work = jnp.where(pick, REMOVED, work)
    colb_ref[...] = jnp.where((picked > 0.0) & valid, 0.0, NEG_INF)

    row_d = lax.broadcasted_iota(jnp.int32, (LANES, blk), 0)
    for p in range(N_SLABS):
        qs = qt[p * LANES:(p + 1) * LANES, :] * (SCALE * LOG2E)
        qa = jnp.where(row_d < HEAD_DIM, qs, 0.0)
        qb = jnp.where(row_d >= HEAD_DIM, qs, 0.0)
        qabt_ref[p, :LANES, :] = jnp.concatenate([qa, qb], axis=1).astype(BF16)

    slabs = [slice(p * LANES, (p + 1) * LANES) for p in range(N_SLABS)]

    def park_scores(n, dst, p):
        off = pl.multiple_of(n * blk, blk)
        keys = jnp.concatenate([kb_ref[pl.ds(off, blk), slabs[p]], featk_ref[...]], axis=1)
        dst[p] = _dot(keys, qabt_ref[p])

    def weighted_values(blocks, p, e):
        vt = jnp.concatenate([vt_ref[n, slabs[p], :] for n in blocks], axis=1)
        va = jnp.concatenate([vt, jnp.ones((SUM_ROWS, vt.shape[1]), BF16)], axis=0)
        pv = _dot(va, e)
        return pv[:LANES], pv[LANES:LANES + 1]

    def reduce_past(blocks, srcs, p):
        cs = slice(p * qcols, (p + 1) * qcols)
        crows = [colb_ref[pl.ds(n, 1), cs] - slope_row(p) * ((j - n) * blk).astype(F32) for n in blocks]
        m_prev = m_ref[p]
        m_new = m_prev
        for src, crow in zip(srcs, crows):
            m_new = jnp.maximum(m_new, jnp.max(src[p], axis=0, keepdims=True) + crow)
        alpha = jnp.exp2(m_prev - m_new)
        e = jnp.concatenate([jnp.exp2(src[p] - (m_new - crow)).astype(BF16) for src, crow in zip(srcs, crows)],
                            axis=0)
        pv, esum = weighted_values(blocks, p, e)
        m_ref[p] = m_new
        l_ref[p] = alpha * l_ref[p] + esum
        acc_ref[p] = alpha * acc_ref[p] + pv

    pair0, pair1 = (sa0_ref, sb0_ref), (sa1_ref, sb1_ref)
    for p in range(N_SLABS):
        park_scores(j, so_ref, p)
    for p in range(N_SLABS):
        park_scores(0, pair0[0], p)
        park_scores(jnp.minimum(1, j), pair0[1], p)
    for p in range(N_SLABS):
        sb = so_ref[p] + causal_ref[...]
        m = jnp.max(sb, axis=0, keepdims=True)
        pv, esum = weighted_values([j], p, jnp.exp2(sb - m).astype(BF16))
        m_ref[p] = m
        l_ref[p] = esum
        acc_ref[p] = pv

    def park_pair(n, pair, p):
        park_scores(jnp.minimum(n, j), pair[0], p)
        park_scores(jnp.minimum(n + 1, j), pair[1], p)

    def four_blocks(i, carry):
        n0 = 4 * i
        park_pair(n0 + 2, pair1, 0)
        for p in range(N_SLABS):
            reduce_past([n0, n0 + 1], pair0, p)
            if p + 1 < N_SLABS:
                park_pair(n0 + 2, pair1, p + 1)
        park_pair(n0 + 4, pair0, 0)
        for p in range(N_SLABS):
            reduce_past([n0 + 2, n0 + 3], pair1, p)
            if p + 1 < N_SLABS:
                park_pair(n0 + 4, pair0, p + 1)
        return carry

    lax.fori_loop(0, j // 4, four_blocks, 0)

    nr = (j // 4) * 4
    rem = j - nr

    @pl.when(rem == 1)
    def _():
        for p in range(N_SLABS):
            reduce_past([nr], pair0[:1], p)

    @pl.when(rem >= 2)
    def _():
        for p in range(N_SLABS):
            reduce_past([nr, nr + 1], pair0, p)

    @pl.when(rem == 3)
    def _():
        for p in range(N_SLABS):
            park_scores(nr + 2, pair1[0], p)
        for p in range(N_SLABS):
            reduce_past([nr + 2], pair1[:1], p)

    for p in range(N_SLABS):
        o = acc_ref[p] / l_ref[p]
        o2 = jnp.where(row_d < HEAD_DIM, o[:, :blk], o[:, blk:])
        sq = o2 * o2
        ms_a = jnp.sum(sq[:HEAD_DIM], axis=0, keepdims=True) * (1.0 / HEAD_DIM)
        ms_b = jnp.sum(sq[HEAD_DIM:], axis=0, keepdims=True) * (1.0 / HEAD_DIM)
        inv = jnp.where(row_d < HEAD_DIM, lax.rsqrt(ms_a + EPS), lax.rsqrt(ms_b + EPS))
        ls = slice(p * LANES, (p + 1) * LANES)
        o_ref[:, ls] = ((o2 * inv).T * gain_ref[:, ls]).astype(BF16)


def _attn_prompt(q, kb, vt, means, gain, *, batch):
    n = q.shape[0]
    seq = n // batch
    nb = seq // MOBA_BLOCK
    group = LANES // N_HEADS
    assert nb <= group and seq % MOBA_BLOCK == 0
    means = means.reshape(batch, nb, D_ATT)
    qcols = HEADS_PER_SLAB * MOBA_BLOCK
    return pl.pallas_call(
        _attn_prompt_body,
        grid=(batch, nb),
        in_specs=[pl.BlockSpec((MOBA_BLOCK, D_ATT), lambda b, j: (b * nb + j, 0)),
                  pl.BlockSpec((seq, D_ATT), lambda b, j: (b, 0)),
                  pl.BlockSpec((nb, D_ATT, MOBA_BLOCK), lambda b, j: (b, 0, 0)),
                  pl.BlockSpec((1, nb, D_ATT), lambda b, j: (b, 0, 0)),
                  pl.BlockSpec(gain.shape, lambda b, j: (0, 0))],
        out_specs=pl.BlockSpec((MOBA_BLOCK, D_ATT), lambda b, j: (b * nb + j, 0)),
        out_shape=jax.ShapeDtypeStruct((n, D_ATT), BF16),
        scratch_shapes=[pltpu.VMEM((MOBA_BLOCK, qcols), F32),
                        pltpu.VMEM((MOBA_BLOCK, LANES), BF16),
                        pltpu.VMEM((N_SLABS, 2 * LANES, qcols), BF16),
                        pltpu.VMEM((group, N_HEADS * MOBA_BLOCK), F32),
                        pltpu.VMEM((N_SLABS, MOBA_BLOCK, qcols), F32),
                        pltpu.VMEM((N_SLABS, MOBA_BLOCK, qcols), F32),
                        pltpu.VMEM((N_SLABS, MOBA_BLOCK, qcols), F32),
                        pltpu.VMEM((N_SLABS, MOBA_BLOCK, qcols), F32),
                        pltpu.VMEM((N_SLABS, MOBA_BLOCK, qcols), F32),
                        pltpu.VMEM((N_SLABS, 1, qcols), F32),
                        pltpu.VMEM((N_SLABS, 1, qcols), F32),
                        pltpu.VMEM((N_SLABS, LANES, qcols), F32)],
        compiler_params=pltpu.CompilerParams(dimension_semantics=("arbitrary", "arbitrary"),
                                             vmem_limit_bytes=VMEM_LIMIT),
        name="attn_prompt",
    )(q, kb, vt, means, gain)


def _attn_sample_t_body(pt_ref, q_ref, kn_ref, vn_ref, gain_ref, bd_ref, ckt_hbm, cvt_hbm, o_ref,
                        kbuf, vbuf, s_ref, acc_ref, ksem, vsem, *, past_len, page, page_base):
    b = pl.program_id(0)
    nbat = pl.num_programs(0)
    n_pages = past_len // page
    ppb = MOBA_BLOCK // page
    nb = past_len // MOBA_BLOCK
    t_new = q_ref.shape[1]
    rows = N_HEADS * t_new

    def page_copy(hbm, buf, sem, bb, pg):
        return pltpu.make_async_copy(hbm.at[page_base + pt_ref[bb, pg]], buf.at[pg], sem.at[pg])

    def start_all(hbm, buf, sem, bb):
        def body(i, c):
            for k in range(DMA_THREADS):
                page_copy(hbm, buf, sem, bb, i * DMA_THREADS + k).start(priority=k)
            return c
        lax.fori_loop(0, n_pages // DMA_THREADS, body, 0)

    @pl.when(b == 0)
    def _():
        start_all(ckt_hbm, kbuf, ksem, b)
        start_all(cvt_hbm, vbuf, vsem, b)

    qt = jnp.concatenate([q_ref[0]] * N_HEADS, axis=0)
    rh = lax.broadcasted_iota(jnp.int32, qt.shape, 0) // t_new
    ch = lax.broadcasted_iota(jnp.int32, qt.shape, 1) // HEAD_DIM
    qs = jnp.where(rh == ch, qt, 0.0) * SCALE
    q_hi = qs.astype(BF16)
    q_lo = (qs - q_hi.astype(F32)).astype(BF16)
    qq = jnp.concatenate([q_hi, q_lo], axis=0)

    def k_tile(i, c):
        pgs = [i * K_TILE + k for k in range(K_TILE)]
        for pg in pgs:
            page_copy(ckt_hbm, kbuf, ksem, b, pg).wait()
        for pg in pgs:
            s2 = _dot(qq, kbuf[pg].reshape(D_ATT, page).astype(BF16))
            s_ref[pg] = s2[:rows] + s2[rows:]
        return c
    lax.fori_loop(0, n_pages // K_TILE, k_tile, 0)

    @pl.when(b + 1 < nbat)
    def _():
        start_all(ckt_hbm, kbuf, ksem, b + 1)

    lane = lax.broadcasted_iota(jnp.int32, (rows, LANES), 1)
    gate = jnp.zeros((rows, LANES), F32)
    gcols = []
    for n in range(nb):
        tot = s_ref[n * ppb]
        for i in range(1, ppb):
            tot = tot + s_ref[n * ppb + i]
        g = jnp.sum(tot, axis=1, keepdims=True)
        gcols.append(g)
        gate = jnp.where(lane == n, g, gate)
    rank = jnp.zeros(gate.shape, jnp.int32)
    for m in range(nb):
        beats = (gcols[m] > gate) | ((gcols[m] == gate) & (lane > m))
        rank = rank + beats.astype(jnp.int32)
    colb = jnp.where(rank < MOBA_TOPK, 0.0, NEG_INF)

    r1 = lax.broadcasted_iota(jnp.int32, (rows, 1), 0)
    tq = r1 % t_new
    slope = jnp.zeros((rows, 1), F32)
    for h in range(N_HEADS):
        slope = jnp.where(r1 // t_new == h, _slope(h), slope)
    in_page = slope * (tq - lane).astype(F32)

    zpad = jnp.zeros((LANES - t_new, D_ATT), F32)
    kn = jnp.concatenate([kn_ref[0], zpad], axis=0).astype(BF16)
    vn = jnp.concatenate([vn_ref[0], zpad], axis=0).astype(BF16)
    s2 = _dot_nt(qq, kn)
    s_own = jnp.where(lane <= tq, s2[:rows] + s2[rows:] - in_page, NEG_INF)

    mrun = s_own
    for n in range(nb):
        mask_n = jnp.sum(jnp.where(lane == n, colb, 0.0), axis=1, keepdims=True)
        for i in range(ppb):
            pg = n * ppb + i
            sn = s_ref[pg] - in_page + (mask_n - slope * float(past_len - pg * page))
            s_ref[pg] = sn
            mrun = jnp.maximum(mrun, sn)
    m = jnp.max(mrun, axis=1, keepdims=True)

    e_own = jnp.exp(s_own - m)
    lrun = e_own
    for pg in range(n_pages):
        e = jnp.exp(s_ref[pg] - m)
        s_ref[pg] = e
        lrun = lrun + e
    l = jnp.sum(lrun, axis=1, keepdims=True)

    acc_ref[...] = jnp.zeros(acc_ref.shape, F32)
    zrows = jnp.zeros((LANES - rows, V_TILE * page), BF16)

    def v_tile(i, c):
        pgs = [i * V_TILE + k for k in range(V_TILE)]
        for pg in pgs:
            page_copy(cvt_hbm, vbuf, vsem, b, pg).wait()
        vt = jnp.concatenate([vbuf[pg].reshape(D_ATT, page) for pg in pgs], axis=1).astype(BF16)
        p = jnp.concatenate([s_ref[pg] for pg in pgs], axis=1).astype(BF16)
        acc_ref[...] += _dot_nt(vt, jnp.concatenate([p, zrows], axis=0))
        return c
    lax.fori_loop(0, n_pages // V_TILE, v_tile, 0)

    @pl.when(b + 1 < nbat)
    def _():
        start_all(cvt_hbm, vbuf, vsem, b + 1)

    acc = acc_ref[...].T[:rows] + _dot(e_own.astype(BF16), vn)
    accn = acc / l
    ch8 = lax.broadcasted_iota(jnp.int32, (t_new, D_ATT), 1) // HEAD_DIM
    out = jnp.zeros((t_new, D_ATT), F32)
    for h in range(N_HEADS):
        out = jnp.where(ch8 == h, accn[h * t_new:(h + 1) * t_new, :], out)
    o_ref[0] = _group_rms(out, gain_ref[...], bd_ref).astype(BF16)


def _attn_sample_t(page_table, q, kn, vn, gain, bd, cache_kt, cache_vt, *, page_base, past_len):
    nbat, t_new, _ = q.shape
    page = cache_kt.shape[3]
    n_pages = past_len // page
    rows = N_HEADS * t_new
    assert past_len % MOBA_BLOCK == 0 and MOBA_BLOCK % page == 0 and page == LANES
    assert rows <= LANES and t_new % 8 == 0 and past_len // MOBA_BLOCK <= LANES
    assert n_pages % V_TILE == 0 and n_pages % K_TILE == 0
    body = functools.partial(_attn_sample_t_body, past_len=past_len, page=page, page_base=page_base)
    per_b = pl.BlockSpec((1, t_new, D_ATT), lambda b, pt: (b, 0, 0))
    grid_spec = pltpu.PrefetchScalarGridSpec(
        num_scalar_prefetch=1,
        grid=(nbat,),
        in_specs=[per_b, per_b, per_b,
                  pl.BlockSpec(gain.shape, lambda b, pt: (0, 0)),
                  pl.BlockSpec(bd.shape, lambda b, pt: (0, 0)),
                  pl.BlockSpec(memory_space=pl.ANY),
                  pl.BlockSpec(memory_space=pl.ANY)],
        out_specs=per_b,
        scratch_shapes=[pltpu.VMEM((n_pages, N_HEADS, HEAD_DIM, page), F32),
                        pltpu.VMEM((n_pages, N_HEADS, HEAD_DIM, page), F32),
                        pltpu.VMEM((n_pages, rows, page), F32),
                        pltpu.VMEM((D_ATT, LANES), F32),
                        pltpu.SemaphoreType.DMA((n_pages,)),
                        pltpu.SemaphoreType.DMA((n_pages,))],
    )
    return pl.pallas_call(
        body,
        grid_spec=grid_spec,
        out_shape=jax.ShapeDtypeStruct((nbat, t_new, D_ATT), BF16),
        compiler_params=pltpu.CompilerParams(dimension_semantics=("arbitrary",),
                                             vmem_limit_bytes=VMEM_LIMIT),
        name="attn_sample",
    )(page_table, q, kn, vn, gain, bd, cache_kt, cache_vt)


def kernel(x_prompt, x_sample, cache_k, cache_v, state_conv, page_table, ffn1_norm, ffn1_w_gu, ffn1_w_down,
           mix_norm, w_in, conv_w, conv_out_norm, attn_out_norm, w_out, ffn2_norm, ffn2_w_gu, ffn2_w_down,
           final_norm):
    bp, seq, d = x_prompt.shape
    bs, dseq, _ = x_sample.shape
    depth, n_pool, page = cache_k.shape[:3]
    d_ff = ffn1_w_down.shape[1]
    dc = conv_w.shape[2]
    past_len = page_table.shape[1] * page

    ck = jnp.transpose(cache_k, (0, 1, 3, 4, 2)).reshape(depth * n_pool, N_HEADS, HEAD_DIM, page)
    cv = jnp.transpose(cache_v, (0, 1, 3, 4, 2)).reshape(depth * n_pool, N_HEADS, HEAD_DIM, page)
    gi = lax.broadcasted_iota(jnp.int32, (D_ATT, D_ATT), 0) // HEAD_DIM
    gj = lax.broadcasted_iota(jnp.int32, (D_ATT, D_ATT), 1) // HEAD_DIM
    bd = (gi == gj).astype(BF16)

    xp = x_prompt.reshape(bp * seq, d)
    xs = x_sample.reshape(bs * dseq, d)
    tm_p = 512
    row = lambda a: a.reshape(1, -1)
    outs = [[] for _ in range(6)]
    for l in range(depth):
        g1, gm, g2 = row(ffn1_norm[l]), row(mix_norm[l]), row(ffn2_norm[l])
        gc, ga = row(conv_out_norm[l]), row(attn_out_norm[l])
        last = l == depth - 1
        gfin = row(final_norm) if last else None

        st = state_conv[l]
        zpad = jnp.zeros((bs, dseq - (CONV_W - 1), dc), F32)
        s2 = jnp.concatenate([st, zpad], axis=1).reshape(bs * dseq, dc)
        s1 = jnp.concatenate([st[:, 1:2], jnp.zeros((bs, dseq - 1, dc), F32)], axis=1).reshape(bs * dseq, dc)
        x1s, wg1, wu1, wd1 = _ffn_stream_call(xs, g1, ffn1_w_gu[l], ffn1_w_down[l], name="ffn1_sample")
        qs, ks, vs, ycs, us, win = _inproj_sample(x1s, gm, w_in[l], conv_w[l], gc, bd, s1, s2, seq=dseq)
        r3 = lambda a: a.reshape(bs, dseq, D_ATT)
        yas = _attn_sample_t(page_table, r3(qs), r3(ks), r3(vs), ga, bd, ck, cv,
                             page_base=l * n_pool, past_len=past_len)
        xs, wg2, wu2, wd2, woc, woa = _ffn_stream_call(
            x1s, g2, ffn2_w_gu[l], ffn2_w_down[l], mix=(ycs, yas.reshape(bs * dseq, D_ATT), w_out[l]),
            final=gfin, name="ffn2_sample")
        outs[3].append(ks.reshape(bs, dseq, N_HEADS, HEAD_DIM))
        outs[4].append(vs.reshape(bs, dseq, N_HEADS, HEAD_DIM))
        outs[5].append(us[:, dseq - (CONV_W - 1):, :])

        x1 = _ffn_call(xp, g1, wg1, wu1, wd1, tm=tm_p, name="ffn1_prompt")
        q, kt, vtf, kb, vt, yc, means, cnew = _inproj_prompt(x1, gm, win, conv_w[l], gc, bd, batch=bp, tm=tm_p)
        ya = _attn_prompt(q, kb, vt, means, ga, batch=bp)
        xp = _ffn_call(x1, g2, wg2, wu2, wd2, tm=tm_p, mix=(yc, ya, woc, woa), final=gfin, name="ffn2_prompt")
        tok_major = lambda a: a.reshape(bp, N_HEADS, HEAD_DIM, seq).transpose(0, 3, 1, 2)
        outs[0].append(tok_major(kt))
        outs[1].append(tok_major(vtf))
        outs[2].append(cnew)

    y_prompt = xp.reshape(bp, seq, d)
    y_sample = xs.reshape(bs, dseq, d)
    kp, vp, cp, ksn, vsn, csn = (jnp.stack(o) for o in outs)
    return (y_prompt, y_sample, kp, vp, cp, ksn, vsn, csn)
```

```python
import functools

import jax
import jax.numpy as jnp
from jax import lax
from jax.experimental import pallas as pl
from jax.experimental.pallas import tpu as pltpu

F32 = jnp.float32
BF16 = jnp.bfloat16

N_HEADS = 8
HEAD_DIM = 64
D_ATT = N_HEADS * HEAD_DIM
N_CONV_GROUPS = 8
CONV_W = 3
MOBA_BLOCK = 256
MOBA_TOPK = 3
EPS = 1e-5
NEG_INF = -1e30
REMOVED = -3e38
SCALE = HEAD_DIM ** -0.5
LOG2E = 1.4426950408889634

LANES = 128
MXU_WIDTH = 256
HEADS_PER_SLAB = LANES // HEAD_DIM
N_SLABS = D_ATT // LANES
VMEM_LIMIT = 56 * 1024 * 1024
PROMPT_ROW_TILE = 512
UPDATE_BLOCKS = 2
SUM_ROWS = 16
DMA_THREADS = 2
K_TILE = 8
V_TILE = 8

NT_DIMS = (((1,), (1,)), ((), ()))


def _slope(h):
    return 2.0 ** (-(8.0 / N_HEADS) * (h + 1))


def _dot(a, b):
    return jnp.dot(a, b, preferred_element_type=F32)


def _dot_nt(a, b, precision=None):
    return lax.dot_general(a, b, NT_DIMS, precision=precision, preferred_element_type=F32)


def _rms(x, g):
    ms = jnp.mean(x * x, axis=-1, keepdims=True)
    return x * lax.rsqrt(ms + EPS) * g


def _group_sumsq(y, bd_ref):
    y2 = y * y
    hi = y2.astype(BF16)
    lo = (y2 - hi.astype(F32)).astype(BF16)
    bd = bd_ref[...]
    return _dot(hi, bd) + _dot(lo, bd)


def _group_rms(y, g, bd_ref):
    ms = _group_sumsq(y, bd_ref) * (1.0 / HEAD_DIM)
    return y * lax.rsqrt(ms + EPS) * g


def _const_spec(shape):
    nd = len(shape)
    return pl.BlockSpec(shape, lambda *_: (0,) * nd, pipeline_mode=pl.Buffered(1))


def _ffn_body(*refs, mix, final, bounds):
    it = iter(refs)
    x_ref = next(it)
    if mix:
        yc_ref, ya_ref, woc_ref, woa_ref = next(it), next(it), next(it), next(it)
    g_ref, wg_ref, wu_ref, wd_ref = next(it), next(it), next(it), next(it)
    gf_ref = next(it) if final else None
    o_ref = next(it)

    x = x_ref[...]
    if mix:
        x = x + _dot(yc_ref[...], woc_ref[...]) + _dot(ya_ref[...], woa_ref[...])
    h = _rms(x, g_ref[...]).astype(BF16)
    acc = jnp.zeros(x.shape, F32)
    for lo, hi in zip(bounds[:-1], bounds[1:]):
        gate = _dot(h, wg_ref[:, lo:hi])
        up = _dot(h, wu_ref[:, lo:hi])
        act = (gate * jax.nn.sigmoid(gate) * up).astype(BF16)
        acc = acc + _dot(act, wd_ref[lo:hi, :])
    x = x + 0.5 * acc
    if final:
        x = _rms(x, gf_ref[...])
    o_ref[...] = x


def _ffn_call(x, g, wg, wu, wd, *, tm, mix=None, final=None, name):
    n, d = x.shape
    d_ff = wd.shape[0]
    row = lambda w: pl.BlockSpec((tm, w), lambda i: (i, 0))
    ins, specs = [x], [row(d)]
    if mix is not None:
        yc, ya, woc, woa = mix
        ins += [yc, ya, woc, woa]
        specs += [row(yc.shape[1]), row(ya.shape[1]), _const_spec(woc.shape), _const_spec(woa.shape)]
    ins += [g, wg, wu, wd]
    specs += [_const_spec(g.shape), _const_spec(wg.shape), _const_spec(wu.shape), _const_spec(wd.shape)]
    if final is not None:
        ins.append(final)
        specs.append(_const_spec(final.shape))
    assert d_ff % MXU_WIDTH == 0
    tiles = d_ff // MXU_WIDTH
    bounds = (0, (tiles + 1) // 2 * MXU_WIDTH, d_ff)
    body = functools.partial(_ffn_body, mix=mix is not None, final=final is not None, bounds=bounds)
    return pl.pallas_call(
        body,
        grid=(n // tm,),
        in_specs=specs,
        out_specs=row(d),
        out_shape=jax.ShapeDtypeStruct((n, d), F32),
        compiler_params=pltpu.CompilerParams(dimension_semantics=("arbitrary",),
                                             vmem_limit_bytes=VMEM_LIMIT),
        name=name,
    )(*ins)


def _ffn_stream_body(*refs, mix, final):
    it = iter(refs)
    x_ref = next(it)
    if mix:
        yc_ref, ya_ref, woc_ref, woa_ref = next(it), next(it), next(it), next(it)
    g_ref, wg_ref, wu_ref, wd_ref = next(it), next(it), next(it), next(it)
    gf_ref = next(it) if final else None
    o_ref, wgb_ref, wub_ref, wdb_ref = next(it), next(it), next(it), next(it)
    if mix:
        wocb_ref, woab_ref = next(it), next(it)
    x_scr, h_scr, acc_scr = next(it), next(it), next(it)
    c = pl.program_id(0)

    @pl.when(c == 0)
    def _():
        x = x_ref[...]
        if mix:
            woc, woa = woc_ref[...].astype(BF16), woa_ref[...].astype(BF16)
            wocb_ref[...] = woc
            woab_ref[...] = woa
            x = x + _dot(yc_ref[...], woc) + _dot(ya_ref[...], woa)
        x_scr[...] = x
        h_scr[...] = _rms(x, g_ref[...]).astype(BF16)
        acc_scr[...] = jnp.zeros(acc_scr.shape, F32)

    wg, wu, wd = wg_ref[...].astype(BF16), wu_ref[...].astype(BF16), wd_ref[...].astype(BF16)
    wgb_ref[...] = wg
    wub_ref[...] = wu
    wdb_ref[...] = wd
    h = h_scr[...]
    gate = _dot(h, wg)
    act = (gate * jax.nn.sigmoid(gate) * _dot(h, wu)).astype(BF16)
    acc_scr[...] += _dot(act, wd)

    @pl.when(c == pl.num_programs(0) - 1)
    def _():
        x = x_scr[...] + 0.5 * acc_scr[...]
        if final:
            x = _rms(x, gf_ref[...])
        o_ref[...] = x


def _ffn_stream_call(x, g, w_gu, w_down, *, mix=None, final=None, name):
    n, d = x.shape
    d_ff = w_down.shape[0]
    tw = MXU_WIDTH
    assert d_ff % tw == 0 and w_gu.shape == (d, 2 * d_ff)
    nt = d_ff // tw
    full = lambda shape: pl.BlockSpec(shape, lambda c: (0,) * len(shape))
    ins, specs = [x], [full((n, d))]
    outs = [jax.ShapeDtypeStruct((n, d), F32), jax.ShapeDtypeStruct((d, d_ff), BF16),
            jax.ShapeDtypeStruct((d, d_ff), BF16), jax.ShapeDtypeStruct((d_ff, d), BF16)]
    out_specs = [full((n, d)), pl.BlockSpec((d, tw), lambda c: (0, c)), pl.BlockSpec((d, tw), lambda c: (0, c)),
                 pl.BlockSpec((tw, d), lambda c: (c, 0))]
    if mix is not None:
        yc, ya, w_out = mix
        dm = yc.shape[1]
        assert w_out.shape == (dm + ya.shape[1], d) and ya.shape[1] == dm
        ins += [yc, ya, w_out, w_out]
        specs += [full(yc.shape), full(ya.shape), pl.BlockSpec((dm, d), lambda c: (0, 0)),
                  pl.BlockSpec((dm, d), lambda c: (1, 0))]
        outs += [jax.ShapeDtypeStruct((dm, d), BF16)] * 2
        out_specs += [full((dm, d))] * 2
    ins += [g, w_gu, w_gu, w_down]
    specs += [full(g.shape), pl.BlockSpec((d, tw), lambda c: (0, c)), pl.BlockSpec((d, tw), lambda c: (0, nt + c)),
              pl.BlockSpec((tw, d), lambda c: (c, 0))]
    if final is not None:
        ins.append(final)
        specs.append(full(final.shape))
    body = functools.partial(_ffn_stream_body, mix=mix is not None, final=final is not None)
    return pl.pallas_call(
        body,
        grid=(nt,),
        in_specs=specs,
        out_specs=out_specs,
        out_shape=outs,
        scratch_shapes=[pltpu.VMEM((n, d), F32), pltpu.VMEM((n, d), BF16), pltpu.VMEM((n, d), F32)],
        compiler_params=pltpu.CompilerParams(dimension_semantics=("arbitrary",),
                                             vmem_limit_bytes=VMEM_LIMIT),
        name=name,
    )(*ins)


def _inproj_body(*refs, tm, tiles_per_seq, sample):
    it = iter(refs)
    x_ref, g_ref, win_ref, cw_ref, cn_ref, bd_ref = (next(it) for _ in range(6))
    if sample:
        s1_ref, s2_ref = next(it), next(it)
        q_ref, k_ref, v_ref, yc_ref, u_ref, winb_ref = (next(it) for _ in range(6))
    else:
        q_ref, kt_ref, vtf_ref, kb_ref, vt_ref, yc_ref, mean_ref, cnew_ref = (next(it) for _ in range(8))
    ubuf = next(it)

    dc = yc_ref.shape[1]
    if sample:
        ubuf[0:8, :] = jnp.zeros((8, dc), F32)
    else:
        first = (pl.program_id(0) % tiles_per_seq) == 0

        @pl.when(first)
        def _():
            ubuf[0:8, :] = jnp.zeros((8, dc), F32)

        @pl.when(jnp.logical_not(first))
        def _():
            ubuf[0:8, :] = ubuf[tm:tm + 8, :]

    h = _rms(x_ref[...], g_ref[...]).astype(BF16)
    if sample:
        def piece(c, w):
            wp = win_ref[:, c:c + w].astype(BF16)
            winb_ref[:, c:c + w] = wp
            return _dot(h, wp)
    else:
        piece = lambda c, w: _dot(h, win_ref[:, c:c + w])
    hc = piece(0, dc)
    cg = piece(2 * dc, dc)
    bg = piece(dc, dc)
    k = piece(3 * dc + D_ATT, D_ATT)
    v = piece(3 * dc + 2 * D_ATT, D_ATT)
    if sample:
        k_ref[...] = k
        v_ref[...] = v

    u = cg * hc
    ubuf[8:tm + 8, :] = u
    um1 = ubuf[7:tm + 7, :]
    um2 = ubuf[6:tm + 6, :]
    if sample:
        t = lax.broadcasted_iota(jnp.int32, (tm, dc), 0) % u_ref.shape[1]
        um1 = jnp.where(t >= 1, um1, s1_ref[...])
        um2 = jnp.where(t >= 2, um2, s2_ref[...])
    cw = cw_ref[...]
    conv = um2 * cw[0:1, :] + um1 * cw[1:2, :] + u * cw[2:3, :]
    yc_ref[...] = _group_rms(bg * conv, cn_ref[...], bd_ref).astype(BF16)
    q_ref[...] = piece(3 * dc, D_ATT)

    if sample:
        u_ref[...] = u.reshape(u_ref.shape)
    else:
        vt = v.T
        kt_ref[0] = k.T
        vtf_ref[0] = vt
        kb_ref[...] = k.astype(BF16)
        nblk = tm // MOBA_BLOCK
        for i in range(nblk):
            vt_ref[i] = vt[:, i * MOBA_BLOCK:(i + 1) * MOBA_BLOCK].astype(BF16)
        mean_ref[0] = jnp.sum(k.reshape(nblk, MOBA_BLOCK, D_ATT), axis=1) * (1.0 / MOBA_BLOCK)
        cnew_ref[0] = ubuf[tm + 6:tm + 8, :]


def _inproj_prompt(x, g, w_in, conv_w, conv_norm, bd, *, batch, tm):
    n, d = x.shape
    dc = conv_w.shape[1]
    seq = n // batch
    tps = seq // tm
    nblk = tm // MOBA_BLOCK
    row = lambda w: pl.BlockSpec((tm, w), lambda i: (i, 0))
    tok_minor = pl.BlockSpec((1, D_ATT, tm), lambda i: (i // tps, 0, i % tps))
    f = lambda w, dt: jax.ShapeDtypeStruct((n, w), dt)
    body = functools.partial(_inproj_body, tm=tm, tiles_per_seq=tps, sample=False)
    return pl.pallas_call(
        body,
        grid=(n // tm,),
        in_specs=[row(d), _const_spec(g.shape), _const_spec(w_in.shape), _const_spec(conv_w.shape),
                  _const_spec(conv_norm.shape), _const_spec(bd.shape)],
        out_specs=[row(D_ATT), tok_minor, tok_minor, row(D_ATT),
                   pl.BlockSpec((nblk, D_ATT, MOBA_BLOCK), lambda i: (i, 0, 0)), row(dc),
                   pl.BlockSpec((1, nblk, D_ATT), lambda i: (i, 0, 0)),
                   pl.BlockSpec((1, CONV_W - 1, dc), lambda i: (i // tps, 0, 0))],
        out_shape=[f(D_ATT, F32), jax.ShapeDtypeStruct((batch, D_ATT, seq), F32),
                   jax.ShapeDtypeStruct((batch, D_ATT, seq), F32), f(D_ATT, BF16),
                   jax.ShapeDtypeStruct((n // MOBA_BLOCK, D_ATT, MOBA_BLOCK), BF16), f(dc, BF16),
                   jax.ShapeDtypeStruct((n // tm, nblk, D_ATT), F32),
                   jax.ShapeDtypeStruct((batch, CONV_W - 1, dc), F32)],
        scratch_shapes=[pltpu.VMEM((tm + 8, dc), F32)],
        compiler_params=pltpu.CompilerParams(dimension_semantics=("arbitrary",),
                                             vmem_limit_bytes=VMEM_LIMIT),
        name="inproj_prompt",
    )(x, g, w_in, conv_w, conv_norm, bd)


def _inproj_sample(x, g, w_in, conv_w, conv_norm, bd, s1, s2, *, seq):
    n, d = x.shape
    dc = conv_w.shape[1]
    tm = n
    full = lambda shape: pl.BlockSpec(shape, lambda i: (0,) * len(shape))
    f = lambda w, dt: jax.ShapeDtypeStruct((n, w), dt)
    body = functools.partial(_inproj_body, tm=tm, tiles_per_seq=1, sample=True)
    return pl.pallas_call(
        body,
        grid=(1,),
        in_specs=[full((tm, d)), full(g.shape), full(w_in.shape), full(conv_w.shape),
                  full(conv_norm.shape), full(bd.shape), full((tm, dc)), full((tm, dc))],
        out_specs=[full((tm, D_ATT)), full((tm, D_ATT)), full((tm, D_ATT)), full((tm, dc)),
                   full((n // seq, seq, dc)), full(w_in.shape)],
        out_shape=[f(D_ATT, F32), f(D_ATT, F32), f(D_ATT, F32), f(dc, BF16),
                   jax.ShapeDtypeStruct((n // seq, seq, dc), F32), jax.ShapeDtypeStruct(w_in.shape, BF16)],
        scratch_shapes=[pltpu.VMEM((tm + 8, dc), F32)],
        compiler_params=pltpu.CompilerParams(dimension_semantics=("arbitrary",),
                                             vmem_limit_bytes=VMEM_LIMIT),
        name="inproj_sample",
    )(x, g, w_in, conv_w, conv_norm, bd, s1, s2)


def _split3(x):
    hi = x.astype(BF16).astype(F32)
    mid = (x - hi).astype(BF16).astype(F32)
    lo = (x - hi - mid).astype(BF16).astype(F32)
    return hi, mid, lo


def _attn_prompt_body(q_ref, kb_ref, vt_ref, mean_ref, gain_ref, o_ref,
                      causal_ref, featk_ref, qabt_ref, colb_ref, so_ref, seta_ref, setb_ref,
                      m_ref, l_ref, acc_ref):
    blk = MOBA_BLOCK
    b = pl.program_id(0)
    j = pl.program_id(1)
    nb = mean_ref.shape[1]
    group = LANES // N_HEADS
    qcols = HEADS_PER_SLAB * blk
    lane_q = lax.broadcasted_iota(jnp.int32, (1, qcols), 1)

    def slope_row(p):
        return jnp.where(lane_q < blk, LOG2E * _slope(HEADS_PER_SLAB * p), LOG2E * _slope(HEADS_PER_SLAB * p + 1))

    @pl.when((b == 0) & (j == 0))
    def _init_tables():
        kk = lax.broadcasted_iota(jnp.int32, (blk, qcols), 0)
        qq = lax.broadcasted_iota(jnp.int32, (blk, qcols), 1)
        causal_ref[...] = jnp.where((qq % blk) >= kk, 0.0, NEG_INF)
        ki = lax.broadcasted_iota(jnp.int32, (blk, LANES), 0).astype(F32)
        kl = lax.broadcasted_iota(jnp.int32, (blk, LANES), 1)
        featk_ref[...] = jnp.where(kl < 3, ki, jnp.where(kl < 6, 1.0, 0.0)).astype(BF16)
        fr = lax.broadcasted_iota(jnp.int32, (LANES, qcols), 0)
        for p in range(N_SLABS):
            a = slope_row(p)
            terms = _split3(a) + _split3(-a * (lane_q % blk).astype(F32))
            feat = jnp.zeros((LANES, qcols), F32)
            for r, t in enumerate(terms):
                feat = jnp.where(fr == r, t, feat)
            qabt_ref[p, LANES:, :] = feat.astype(BF16)

    qt = q_ref[...].T
    means = mean_ref[0]
    if nb < group:
        means = jnp.concatenate([means, jnp.zeros((group - nb, D_ATT), F32)], axis=0)
    mt = jnp.concatenate([means] * N_HEADS, axis=0)
    rh = lax.broadcasted_iota(jnp.int32, mt.shape, 0) // group
    ch = lax.broadcasted_iota(jnp.int32, mt.shape, 1) // HEAD_DIM
    mbd = jnp.where(rh == ch, mt, 0.0)
    gate_t = jnp.dot(mbd, qt, precision=lax.Precision.HIGHEST, preferred_element_type=F32)
    gate = jnp.concatenate([gate_t[h * group:(h + 1) * group, :] for h in range(N_HEADS)], axis=1)

    n_idx = lax.broadcasted_iota(jnp.int32, gate.shape, 0)
    n_f = n_idx.astype(F32)
    valid = n_idx < j
    work = jnp.where(valid, gate, NEG_INF)
    picked = jnp.zeros(gate.shape, F32)
    for _ in range(MOBA_TOPK):
        top = jnp.max(work, axis=0, keepdims=True)
        first = jnp.min(jnp.where(work == top, n_f, float(group)), axis=0, keepdims=True)
        pick = n_f == first
        picked = jnp.where(pick, 1.0, picked)
        work = jnp.where(pick, REMOVED, work)
    colb_ref[...] = jnp.where((picked > 0.0) & valid, 0.0, NEG_INF)

    row_d = lax.broadcasted_iota(jnp.int32, (LANES, blk), 0)
    for p in range(N_SLABS):
        qs = qt[p * LANES:(p + 1) * LANES, :] * (SCALE * LOG2E)
        qa = jnp.where(row_d < HEAD_DIM, qs, 0.0)
        qb = jnp.where(row_d >= HEAD_DIM, qs, 0.0)
        qabt_ref[p, :LANES, :] = jnp.concatenate([qa, qb], axis=1).astype(BF16)

    slabs = [slice(p * LANES, (p + 1) * LANES) for p in range(N_SLABS)]

    def scores(n, p):
        off = pl.multiple_of(n * blk, blk)
        keys = jnp.concatenate([kb_ref[pl.ds(off, blk), slabs[p]], featk_ref[...]], axis=1)
        return _dot(keys, qabt_ref[p])

    sets = (seta_ref, setb_ref)

    def park_unit(n_first, count, p):
        for g in range(count):
            sets[p % 2][g] = scores(jnp.minimum(n_first + g, j), p)

    def weighted_values(blocks, p, e):
        vt = jnp.concatenate([vt_ref[n, slabs[p], :] for n in blocks], axis=1)
        ones = jnp.ones((SUM_ROWS, vt.shape[1]), BF16)
        pvs = [_dot(jnp.concatenate([vt[h * HEAD_DIM:(h + 1) * HEAD_DIM], ones], axis=0),
                    e[:, h * blk:(h + 1) * blk]) for h in range(HEADS_PER_SLAB)]
        return (jnp.concatenate([pv[:HEAD_DIM] for pv in pvs], axis=0),
                jnp.concatenate([pv[HEAD_DIM:HEAD_DIM + 1] for pv in pvs], axis=1))

    def by_head(row):
        return jnp.where(row_d < HEAD_DIM, row[:, :blk], row[:, blk:])

    def reduce_unit(n_first, count, p):
        cs = slice(p * qcols, (p + 1) * qcols)
        src = sets[p % 2]
        blocks = [n_first + g for g in range(count)]
        crows = [colb_ref[pl.ds(n, 1), cs] - slope_row(p) * ((j - n) * blk).astype(F32) for n in blocks]
        m_prev = m_ref[p]
        m_new = m_prev
        for g, crow in enumerate(crows):
            m_new = jnp.maximum(m_new, jnp.max(src[g], axis=0, keepdims=True) + crow)
        alpha = jnp.exp2(m_prev - m_new)
        e = jnp.concatenate([jnp.exp2(src[g] - (m_new - crow)).astype(BF16) for g, crow in enumerate(crows)],
                            axis=0)
        pv, esum = weighted_values(blocks, p, e)
        m_ref[p] = m_new
        l_ref[p] = alpha * l_ref[p] + esum
        acc_ref[p] = by_head(alpha) * acc_ref[p] + pv

    for p in range(N_SLABS):
        so_ref[p] = scores(j, p)
    park_unit(0, UPDATE_BLOCKS, 0)
    for p in range(N_SLABS):
        sb = so_ref[p] + causal_ref[...]
        m = jnp.max(sb, axis=0, keepdims=True)
        pv, esum = weighted_values([j], p, jnp.exp2(sb - m).astype(BF16))
        m_ref[p] = m
        l_ref[p] = esum
        acc_ref[p] = pv

    def sweep(n_first, count, n_after):
        for p in range(N_SLABS):
            if p + 1 < N_SLABS:
                park_unit(n_first, count, p + 1)
            elif n_after is not None:
                park_unit(n_after, UPDATE_BLOCKS, 0)
            reduce_unit(n_first, count, p)

    def trip(t, carry):
        sweep(UPDATE_BLOCKS * t, UPDATE_BLOCKS, UPDATE_BLOCKS * (t + 1))
        return carry

    lax.fori_loop(0, j // UPDATE_BLOCKS, trip, 0)

    nr = (j // UPDATE_BLOCKS) * UPDATE_BLOCKS
    for r in range(1, UPDATE_BLOCKS):
        @pl.when(j - nr == r)
        def _(r=r):
            sweep(nr, r, None)

    for p in range(N_SLABS):
        o2 = acc_ref[p] / by_head(l_ref[p])
        sq = o2 * o2
        ms_a = jnp.sum(sq[:HEAD_DIM], axis=0, keepdims=True) * (1.0 / HEAD_DIM)
        ms_b = jnp.sum(sq[HEAD_DIM:], axis=0, keepdims=True) * (1.0 / HEAD_DIM)
        inv = jnp.where(row_d < HEAD_DIM, lax.rsqrt(ms_a + EPS), lax.rsqrt(ms_b + EPS))
        ls = slice(p * LANES, (p + 1) * LANES)
        o_ref[:, ls] = ((o2 * inv).T * gain_ref[:, ls]).astype(BF16)


def _attn_prompt(q, kb, vt, means, gain, *, batch):
    n = q.shape[0]
    seq = n // batch
    nb = seq // MOBA_BLOCK
    group = LANES // N_HEADS
    assert nb <= group and seq % MOBA_BLOCK == 0
    means = means.reshape(batch, nb, D_ATT)
    qcols = HEADS_PER_SLAB * MOBA_BLOCK
    return pl.pallas_call(
        _attn_prompt_body,
        grid=(batch, nb),
        in_specs=[pl.BlockSpec((MOBA_BLOCK, D_ATT), lambda b, j: (b * nb + j, 0)),
                  pl.BlockSpec((seq, D_ATT), lambda b, j: (b, 0)),
                  pl.BlockSpec((nb, D_ATT, MOBA_BLOCK), lambda b, j: (b, 0, 0)),
                  pl.BlockSpec((1, nb, D_ATT), lambda b, j: (b, 0, 0)),
                  pl.BlockSpec(gain.shape, lambda b, j: (0, 0))],
        out_specs=pl.BlockSpec((MOBA_BLOCK, D_ATT), lambda b, j: (b * nb + j, 0)),
        out_shape=jax.ShapeDtypeStruct((n, D_ATT), BF16),
        scratch_shapes=[pltpu.VMEM((MOBA_BLOCK, qcols), F32),
                        pltpu.VMEM((MOBA_BLOCK, LANES), BF16),
                        pltpu.VMEM((N_SLABS, 2 * LANES, qcols), BF16),
                        pltpu.VMEM((group, N_HEADS * MOBA_BLOCK), F32),
                        pltpu.VMEM((N_SLABS, MOBA_BLOCK, qcols), F32),
                        pltpu.VMEM((UPDATE_BLOCKS, MOBA_BLOCK, qcols), F32),
                        pltpu.VMEM((UPDATE_BLOCKS, MOBA_BLOCK, qcols), F32),
                        pltpu.VMEM((N_SLABS, 1, qcols), F32),
                        pltpu.VMEM((N_SLABS, 1, qcols), F32),
                        pltpu.VMEM((N_SLABS, LANES, MOBA_BLOCK), F32)],
        compiler_params=pltpu.CompilerParams(dimension_semantics=("arbitrary", "arbitrary"),
                                             vmem_limit_bytes=VMEM_LIMIT),
        name="attn_prompt",
    )(q, kb, vt, means, gain)


def _attn_sample_t_body(pt_ref, q_ref, kn_ref, vn_ref, gain_ref, bd_ref, ckt_hbm, cvt_hbm, o_ref,
                        kbuf, vbuf, s_ref, acc_ref, ksem, vsem, *, past_len, page, page_base):
    b = pl.program_id(0)
    nbat = pl.num_programs(0)
    n_pages = past_len // page
    ppb = MOBA_BLOCK // page
    nb = past_len // MOBA_BLOCK
    t_new = q_ref.shape[1]
    rows = N_HEADS * t_new

    def page_copy(hbm, buf, sem, bb, pg):
        return pltpu.make_async_copy(hbm.at[page_base + pt_ref[bb, pg]], buf.at[pg], sem.at[pg])

    def start_all(hbm, buf, sem, bb):
        def body(i, c):
            for k in range(DMA_THREADS):
                page_copy(hbm, buf, sem, bb, i * DMA_THREADS + k).start(priority=k)
            return c
        lax.fori_loop(0, n_pages // DMA_THREADS, body, 0)

    @pl.when(b == 0)
    def _():
        start_all(ckt_hbm, kbuf, ksem, b)
        start_all(cvt_hbm, vbuf, vsem, b)

    qt = jnp.concatenate([q_ref[0]] * N_HEADS, axis=0)
    rh = lax.broadcasted_iota(jnp.int32, qt.shape, 0) // t_new
    ch = lax.broadcasted_iota(jnp.int32, qt.shape, 1) // HEAD_DIM
    qs = jnp.where(rh == ch, qt, 0.0) * SCALE
    q_hi = qs.astype(BF16)
    q_lo = (qs - q_hi.astype(F32)).astype(BF16)
    qq = jnp.concatenate([q_hi, q_lo], axis=0)

    def k_tile(i, c):
        pgs = [i * K_TILE + k for k in range(K_TILE)]
        for pg in pgs:
            page_copy(ckt_hbm, kbuf, ksem, b, pg).wait()
        for pg in pgs:
            s2 = _dot(qq, kbuf[pg].reshape(D_ATT, page).astype(BF16))
            s_ref[pg] = s2[:rows] + s2[rows:]
        return c
    lax.fori_loop(0, n_pages // K_TILE, k_tile, 0)

    @pl.when(b + 1 < nbat)
    def _():
        start_all(ckt_hbm, kbuf, ksem, b + 1)

    lane = lax.broadcasted_iota(jnp.int32, (rows, LANES), 1)
    gate = jnp.zeros((rows, LANES), F32)
    gcols = []
    for n in range(nb):
        tot = s_ref[n * ppb]
        for i in range(1, ppb):
            tot = tot + s_ref[n * ppb + i]
        g = jnp.sum(tot, axis=1, keepdims=True)
        gcols.append(g)
        gate = jnp.where(lane == n, g, gate)
    rank = jnp.zeros(gate.shape, jnp.int32)
    for m in range(nb):
        beats = (gcols[m] > gate) | ((gcols[m] == gate) & (lane > m))
        rank = rank + beats.astype(jnp.int32)
    colb = jnp.where(rank < MOBA_TOPK, 0.0, NEG_INF)

    r1 = lax.broadcasted_iota(jnp.int32, (rows, 1), 0)
    tq = r1 % t_new
    slope = jnp.zeros((rows, 1), F32)
    for h in range(N_HEADS):
        slope = jnp.where(r1 // t_new == h, _slope(h), slope)
    in_page = slope * (tq - lane).astype(F32)

    zpad = jnp.zeros((LANES - t_new, D_ATT), F32)
    kn = jnp.concatenate([kn_ref[0], zpad], axis=0).astype(BF16)
    vn = jnp.concatenate([vn_ref[0], zpad], axis=0).astype(BF16)
    s2 = _dot_nt(qq, kn)
    s_own = jnp.where(lane <= tq, s2[:rows] + s2[rows:] - in_page, NEG_INF)

    mrun = s_own
    for n in range(nb):
        mask_n = jnp.sum(jnp.where(lane == n, colb, 0.0), axis=1, keepdims=True)
        for i in range(ppb):
            pg = n * ppb + i
            sn = s_ref[pg] - in_page + (mask_n - slope * float(past_len - pg * page))
            s_ref[pg] = sn
            mrun = jnp.maximum(mrun, sn)
    m = jnp.max(mrun, axis=1, keepdims=True)

    e_own = jnp.exp(s_own - m)
    lrun = e_own
    for pg in range(n_pages):
        e = jnp.exp(s_ref[pg] - m)
        s_ref[pg] = e
        lrun = lrun + e
    l = jnp.sum(lrun, axis=1, keepdims=True)

    acc_ref[...] = jnp.zeros(acc_ref.shape, F32)
    zrows = jnp.zeros((LANES - rows, V_TILE * page), BF16)

    def v_tile(i, c):
        pgs = [i * V_TILE + k for k in range(V_TILE)]
        for pg in pgs:
            page_copy(cvt_hbm, vbuf, vsem, b, pg).wait()
        vt = jnp.concatenate([vbuf[pg].reshape(D_ATT, page) for pg in pgs], axis=1).astype(BF16)
        p = jnp.concatenate([s_ref[pg] for pg in pgs], axis=1).astype(BF16)
        acc_ref[...] += _dot_nt(vt, jnp.concatenate([p, zrows], axis=0))
        return c
    lax.fori_loop(0, n_pages // V_TILE, v_tile, 0)

    @pl.when(b + 1 < nbat)
    def _():
        start_all(cvt_hbm, vbuf, vsem, b + 1)

    acc = acc_ref[...].T[:rows] + _dot(e_own.astype(BF16), vn)
    accn = acc / l
    ch8 = lax.broadcasted_iota(jnp.int32, (t_new, D_ATT), 1) // HEAD_DIM
    out = jnp.zeros((t_new, D_ATT), F32)
    for h in range(N_HEADS):
        out = jnp.where(ch8 == h, accn[h * t_new:(h + 1) * t_new, :], out)
    o_ref[0] = _group_rms(out, gain_ref[...], bd_ref).astype(BF16)


def _attn_sample_t(page_table, q, kn, vn, gain, bd, cache_kt, cache_vt, *, page_base, past_len):
    nbat, t_new, _ = q.shape
    page = cache_kt.shape[3]
    n_pages = past_len // page
    rows = N_HEADS * t_new
    assert past_len % MOBA_BLOCK == 0 and MOBA_BLOCK % page == 0 and page == LANES
    assert rows <= LANES and t_new % 8 == 0 and past_len // MOBA_BLOCK <= LANES
    assert n_pages % V_TILE == 0 and n_pages % K_TILE == 0
    body = functools.partial(_attn_sample_t_body, past_len=past_len, page=page, page_base=page_base)
    per_b = pl.BlockSpec((1, t_new, D_ATT), lambda b, pt: (b, 0, 0))
    grid_spec = pltpu.PrefetchScalarGridSpec(
        num_scalar_prefetch=1,
        grid=(nbat,),
        in_specs=[per_b, per_b, per_b,
                  pl.BlockSpec(gain.shape, lambda b, pt: (0, 0)),
                  pl.BlockSpec(bd.shape, lambda b, pt: (0, 0)),
                  pl.BlockSpec(memory_space=pl.ANY),
                  pl.BlockSpec(memory_space=pl.ANY)],
        out_specs=per_b,
        scratch_shapes=[pltpu.VMEM((n_pages, N_HEADS, HEAD_DIM, page), F32),
                        pltpu.VMEM((n_pages, N_HEADS, HEAD_DIM, page), F32),
                        pltpu.VMEM((n_pages, rows, page), F32),
                        pltpu.VMEM((D_ATT, LANES), F32),
                        pltpu.SemaphoreType.DMA((n_pages,)),
                        pltpu.SemaphoreType.DMA((n_pages,))],
    )
    return pl.pallas_call(
        body,
        grid_spec=grid_spec,
        out_shape=jax.ShapeDtypeStruct((nbat, t_new, D_ATT), BF16),
        compiler_params=pltpu.CompilerParams(dimension_semantics=("arbitrary",),
                                             vmem_limit_bytes=VMEM_LIMIT),
        name="attn_sample",
    )(page_table, q, kn, vn, gain, bd, cache_kt, cache_vt)


def kernel(x_prompt, x_sample, cache_k, cache_v, state_conv, page_table, ffn1_norm, ffn1_w_gu, ffn1_w_down,
           mix_norm, w_in, conv_w, conv_out_norm, attn_out_norm, w_out, ffn2_norm, ffn2_w_gu, ffn2_w_down,
           final_norm):
    bp, seq, d = x_prompt.shape
    bs, dseq, _ = x_sample.shape
    depth, n_pool, page = cache_k.shape[:3]
    dc = conv_w.shape[2]
    past_len = page_table.shape[1] * page

    ck = jnp.transpose(cache_k, (0, 1, 3, 4, 2)).reshape(depth * n_pool, N_HEADS, HEAD_DIM, page)
    cv = jnp.transpose(cache_v, (0, 1, 3, 4, 2)).reshape(depth * n_pool, N_HEADS, HEAD_DIM, page)
    gi = lax.broadcasted_iota(jnp.int32, (D_ATT, D_ATT), 0) // HEAD_DIM
    gj = lax.broadcasted_iota(jnp.int32, (D_ATT, D_ATT), 1) // HEAD_DIM
    bd = (gi == gj).astype(BF16)

    xp = x_prompt.reshape(bp * seq, d)
    xs = x_sample.reshape(bs * dseq, d)
    tm_p = PROMPT_ROW_TILE
    assert seq % tm_p == 0 and tm_p % MOBA_BLOCK == 0
    row = lambda a: a.reshape(1, -1)
    outs = [[] for _ in range(6)]
    for l in range(depth):
        g1, gm, g2 = row(ffn1_norm[l]), row(mix_norm[l]), row(ffn2_norm[l])
        gc, ga = row(conv_out_norm[l]), row(attn_out_norm[l])
        last = l == depth - 1
        gfin = row(final_norm) if last else None

        st = state_conv[l]
        zpad = jnp.zeros((bs, dseq - (CONV_W - 1), dc), F32)
        s2 = jnp.concatenate([st, zpad], axis=1).reshape(bs * dseq, dc)
        s1 = jnp.concatenate([st[:, 1:2], jnp.zeros((bs, dseq - 1, dc), F32)], axis=1).reshape(bs * dseq, dc)
        x1s, wg1, wu1, wd1 = _ffn_stream_call(xs, g1, ffn1_w_gu[l], ffn1_w_down[l], name="ffn1_sample")
        qs, ks, vs, ycs, us, win = _inproj_sample(x1s, gm, w_in[l], conv_w[l], gc, bd, s1, s2, seq=dseq)
        r3 = lambda a: a.reshape(bs, dseq, D_ATT)
        yas = _attn_sample_t(page_table, r3(qs), r3(ks), r3(vs), ga, bd, ck, cv,
                             page_base=l * n_pool, past_len=past_len)
        xs, wg2, wu2, wd2, woc, woa = _ffn_stream_call(
            x1s, g2, ffn2_w_gu[l], ffn2_w_down[l], mix=(ycs, yas.reshape(bs * dseq, D_ATT), w_out[l]),
            final=gfin, name="ffn2_sample")
        outs[3].append(ks.reshape(bs, dseq, N_HEADS, HEAD_DIM))
        outs[4].append(vs.reshape(bs, dseq, N_HEADS, HEAD_DIM))
        outs[5].append(us[:, dseq - (CONV_W - 1):, :])

        x1 = _ffn_call(xp, g1, wg1, wu1, wd1, tm=tm_p, name="ffn1_prompt")
        q, kt, vtf, kb, vt, yc, means, cnew = _inproj_prompt(x1, gm, win, conv_w[l], gc, bd, batch=bp, tm=tm_p)
        ya = _attn_prompt(q, kb, vt, means, ga, batch=bp)
        xp = _ffn_call(x1, g2, wg2, wu2, wd2, tm=tm_p, mix=(yc, ya, woc, woa), final=gfin, name="ffn2_prompt")
        tok_major = lambda a: a.reshape(bp, N_HEADS, HEAD_DIM, seq).transpose(0, 3, 1, 2)
        outs[0].append(tok_major(kt))
        outs[1].append(tok_major(vtf))
        outs[2].append(cnew)

    y_prompt = xp.reshape(bp, seq, d)
    y_sample = xs.reshape(bs, dseq, d)
    kp, vp, cp, ksn, vsn, csn = (jnp.stack(o) for o in outs)
    return (y_prompt, y_sample, kp, vp, cp, ksn, vsn, csn)
```

```python
import functools

import jax
import jax.numpy as jnp
from jax import lax
from jax.experimental import pallas as pl
from jax.experimental.pallas import tpu as pltpu

F32 = jnp.float32
BF16 = jnp.bfloat16

N_HEADS = 8
HEAD_DIM = 64
D_ATT = N_HEADS * HEAD_DIM
N_CONV_GROUPS = 8
CONV_W = 3
MOBA_BLOCK = 256
MOBA_TOPK = 3
EPS = 1e-5
NEG_INF = -1e30
REMOVED = -3e38
SCALE = HEAD_DIM ** -0.5
LOG2E = 1.4426950408889634

LANES = 128
MXU_WIDTH = 256
HEADS_PER_SLAB = LANES // HEAD_DIM
N_SLABS = D_ATT // LANES
VMEM_LIMIT = 56 * 1024 * 1024
PROMPT_ROW_TILE = 512
UPDATE_BLOCKS = 2
SUM_ROWS = 16
DMA_THREADS = 2
K_TILE = 8
V_TILE = 8

NT_DIMS = (((1,), (1,)), ((), ()))


def _slope(h):
    return 2.0 ** (-(8.0 / N_HEADS) * (h + 1))


def _dot(a, b):
    return jnp.dot(a, b, preferred_element_type=F32)


def _dot_nt(a, b, precision=None):
    return lax.dot_general(a, b, NT_DIMS, precision=precision, preferred_element_type=F32)


def _rms(x, g):
    ms = jnp.mean(x * x, axis=-1, keepdims=True)
    return x * lax.rsqrt(ms + EPS) * g


def _group_sumsq(y, bd_ref):
    y2 = y * y
    hi = y2.astype(BF16)
    lo = (y2 - hi.astype(F32)).astype(BF16)
    bd = bd_ref[...]
    return _dot(hi, bd) + _dot(lo, bd)


def _group_rms(y, g, bd_ref):
    ms = _group_sumsq(y, bd_ref) * (1.0 / HEAD_DIM)
    return y * lax.rsqrt(ms + EPS) * g


def _const_spec(shape):
    nd = len(shape)
    return pl.BlockSpec(shape, lambda *_: (0,) * nd, pipeline_mode=pl.Buffered(1))


def _ffn_body(*refs, mix, final, bounds):
    it = iter(refs)
    x_ref = next(it)
    if mix:
        yc_ref, ya_ref, woc_ref, woa_ref = next(it), next(it), next(it), next(it)
    g_ref, wg_ref, wu_ref, wd_ref = next(it), next(it), next(it), next(it)
    gf_ref = next(it) if final else None
    o_ref = next(it)

    x = x_ref[...]
    if mix:
        x = x + _dot(yc_ref[...], woc_ref[...]) + _dot(ya_ref[...], woa_ref[...])
    h = _rms(x, g_ref[...]).astype(BF16)
    acc = jnp.zeros(x.shape, F32)
    for lo, hi in zip(bounds[:-1], bounds[1:]):
        gate = _dot(h, wg_ref[:, lo:hi])
        up = _dot(h, wu_ref[:, lo:hi])
        act = (gate * jax.nn.sigmoid(gate) * up).astype(BF16)
        acc = acc + _dot(act, wd_ref[lo:hi, :])
    x = x + 0.5 * acc
    if final:
        x = _rms(x, gf_ref[...])
    o_ref[...] = x


def _ffn_call(x, g, wg, wu, wd, *, tm, mix=None, final=None, name):
    n, d = x.shape
    d_ff = wd.shape[0]
    row = lambda w: pl.BlockSpec((tm, w), lambda i: (i, 0))
    ins, specs = [x], [row(d)]
    if mix is not None:
        yc, ya, woc, woa = mix
        ins += [yc, ya, woc, woa]
        specs += [row(yc.shape[1]), row(ya.shape[1]), _const_spec(woc.shape), _const_spec(woa.shape)]
    ins += [g, wg, wu, wd]
    specs += [_const_spec(g.shape), _const_spec(wg.shape), _const_spec(wu.shape), _const_spec(wd.shape)]
    if final is not None:
        ins.append(final)
        specs.append(_const_spec(final.shape))
    assert d_ff % MXU_WIDTH == 0
    tiles = d_ff // MXU_WIDTH
    bounds = (0, (tiles + 1) // 2 * MXU_WIDTH, d_ff)
    body = functools.partial(_ffn_body, mix=mix is not None, final=final is not None, bounds=bounds)
    return pl.pallas_call(
        body,
        grid=(n // tm,),
        in_specs=specs,
        out_specs=row(d),
        out_shape=jax.ShapeDtypeStruct((n, d), F32),
        compiler_params=pltpu.CompilerParams(dimension_semantics=("arbitrary",),
                                             vmem_limit_bytes=VMEM_LIMIT),
        name=name,
    )(*ins)


def _ffn_stream_body(*refs, mix, final):
    it = iter(refs)
    x_ref = next(it)
    if mix:
        yc_ref, ya_ref, woc_ref, woa_ref = next(it), next(it), next(it), next(it)
    g_ref, wg_ref, wu_ref, wd_ref = next(it), next(it), next(it), next(it)
    gf_ref = next(it) if final else None
    o_ref, wgb_ref, wub_ref, wdb_ref = next(it), next(it), next(it), next(it)
    if mix:
        wocb_ref, woab_ref = next(it), next(it)
    x_scr, h_scr, acc_scr = next(it), next(it), next(it)
    c = pl.program_id(0)

    @pl.when(c == 0)
    def _():
        x = x_ref[...]
        if mix:
            woc, woa = woc_ref[...].astype(BF16), woa_ref[...].astype(BF16)
            wocb_ref[...] = woc
            woab_ref[...] = woa
            x = x + _dot(yc_ref[...], woc) + _dot(ya_ref[...], woa)
        x_scr[...] = x
        h_scr[...] = _rms(x, g_ref[...]).astype(BF16)
        acc_scr[...] = jnp.zeros(acc_scr.shape, F32)

    wg, wu, wd = wg_ref[...].astype(BF16), wu_ref[...].astype(BF16), wd_ref[...].astype(BF16)
    wgb_ref[...] = wg
    wub_ref[...] = wu
    wdb_ref[...] = wd
    h = h_scr[...]
    gate = _dot(h, wg)
    act = (gate * jax.nn.sigmoid(gate) * _dot(h, wu)).astype(BF16)
    acc_scr[...] += _dot(act, wd)

    @pl.when(c == pl.num_programs(0) - 1)
    def _():
        x = x_scr[...] + 0.5 * acc_scr[...]
        if final:
            x = _rms(x, gf_ref[...])
        o_ref[...] = x


def _ffn_stream_call(x, g, w_gu, w_down, *, mix=None, final=None, name):
    n, d = x.shape
    d_ff = w_down.shape[0]
    tw = MXU_WIDTH
    assert d_ff % tw == 0 and w_gu.shape == (d, 2 * d_ff)
    nt = d_ff // tw
    full = lambda shape: pl.BlockSpec(shape, lambda c: (0,) * len(shape))
    ins, specs = [x], [full((n, d))]
    outs = [jax.ShapeDtypeStruct((n, d), F32), jax.ShapeDtypeStruct((d, d_ff), BF16),
            jax.ShapeDtypeStruct((d, d_ff), BF16), jax.ShapeDtypeStruct((d_ff, d), BF16)]
    out_specs = [full((n, d)), pl.BlockSpec((d, tw), lambda c: (0, c)), pl.BlockSpec((d, tw), lambda c: (0, c)),
                 pl.BlockSpec((tw, d), lambda c: (c, 0))]
    if mix is not None:
        yc, ya, w_out = mix
        dm = yc.shape[1]
        assert w_out.shape == (dm + ya.shape[1], d) and ya.shape[1] == dm
        ins += [yc, ya, w_out, w_out]
        specs += [full(yc.shape), full(ya.shape), pl.BlockSpec((dm, d), lambda c: (0, 0)),
                  pl.BlockSpec((dm, d), lambda c: (1, 0))]
        outs += [jax.ShapeDtypeStruct((dm, d), BF16)] * 2
        out_specs += [full((dm, d))] * 2
    ins += [g, w_gu, w_gu, w_down]
    specs += [full(g.shape), pl.BlockSpec((d, tw), lambda c: (0, c)), pl.BlockSpec((d, tw), lambda c: (0, nt + c)),
              pl.BlockSpec((tw, d), lambda c: (c, 0))]
    if final is not None:
        ins.append(final)
        specs.append(full(final.shape))
    body = functools.partial(_ffn_stream_body, mix=mix is not None, final=final is not None)
    return pl.pallas_call(
        body,
        grid=(nt,),
        in_specs=specs,
        out_specs=out_specs,
        out_shape=outs,
        scratch_shapes=[pltpu.VMEM((n, d), F32), pltpu.VMEM((n, d), BF16), pltpu.VMEM((n, d), F32)],
        compiler_params=pltpu.CompilerParams(dimension_semantics=("arbitrary",),
                                             vmem_limit_bytes=VMEM_LIMIT),
        name=name,
    )(*ins)


def _inproj_body(*refs, tm, tiles_per_seq, sample):
    it = iter(refs)
    x_ref, g_ref, win_ref, cw_ref, cn_ref, bd_ref = (next(it) for _ in range(6))
    if sample:
        s1_ref, s2_ref = next(it), next(it)
        q_ref, k_ref, v_ref, yc_ref, u_ref, winb_ref = (next(it) for _ in range(6))
    else:
        q_ref, kt_ref, vtf_ref, kb_ref, vt_ref, yc_ref, mean_ref, cnew_ref = (next(it) for _ in range(8))
    ubuf = next(it)

    dc = yc_ref.shape[1]
    if sample:
        ubuf[0:8, :] = jnp.zeros((8, dc), F32)
    else:
        first = (pl.program_id(0) % tiles_per_seq) == 0

        @pl.when(first)
        def _():
            ubuf[0:8, :] = jnp.zeros((8, dc), F32)

        @pl.when(jnp.logical_not(first))
        def _():
            ubuf[0:8, :] = ubuf[tm:tm + 8, :]

    h = _rms(x_ref[...], g_ref[...]).astype(BF16)
    if sample:
        def piece(c, w):
            wp = win_ref[:, c:c + w].astype(BF16)
            winb_ref[:, c:c + w] = wp
            return _dot(h, wp)
    else:
        piece = lambda c, w: _dot(h, win_ref[:, c:c + w])
    hc = piece(0, dc)
    cg = piece(2 * dc, dc)
    bg = piece(dc, dc)
    k = piece(3 * dc + D_ATT, D_ATT)
    v = piece(3 * dc + 2 * D_ATT, D_ATT)
    if sample:
        k_ref[...] = k
        v_ref[...] = v

    u = cg * hc
    ubuf[8:tm + 8, :] = u
    um1 = ubuf[7:tm + 7, :]
    um2 = ubuf[6:tm + 6, :]
    if sample:
        t = lax.broadcasted_iota(jnp.int32, (tm, dc), 0) % u_ref.shape[1]
        um1 = jnp.where(t >= 1, um1, s1_ref[...])
        um2 = jnp.where(t >= 2, um2, s2_ref[...])
    cw = cw_ref[...]
    conv = um2 * cw[0:1, :] + um1 * cw[1:2, :] + u * cw[2:3, :]
    yc_ref[...] = _group_rms(bg * conv, cn_ref[...], bd_ref).astype(BF16)
    q_ref[...] = piece(3 * dc, D_ATT)

    if sample:
        u_ref[...] = u.reshape(u_ref.shape)
    else:
        vt = v.T
        kt_ref[0] = k.T
        vtf_ref[0] = vt
        kb_ref[...] = k.astype(BF16)
        nblk = tm // MOBA_BLOCK
        for i in range(nblk):
            vt_ref[i] = vt[:, i * MOBA_BLOCK:(i + 1) * MOBA_BLOCK].astype(BF16)
        mean_ref[0] = jnp.sum(k.reshape(nblk, MOBA_BLOCK, D_ATT), axis=1) * (1.0 / MOBA_BLOCK)
        cnew_ref[0] = ubuf[tm + 6:tm + 8, :]


def _inproj_prompt(x, g, w_in, conv_w, conv_norm, bd, *, batch, tm):
    n, d = x.shape
    dc = conv_w.shape[1]
    seq = n // batch
    tps = seq // tm
    nblk = tm // MOBA_BLOCK
    row = lambda w: pl.BlockSpec((tm, w), lambda i: (i, 0))
    tok_minor = pl.BlockSpec((1, D_ATT, tm), lambda i: (i // tps, 0, i % tps))
    f = lambda w, dt: jax.ShapeDtypeStruct((n, w), dt)
    body = functools.partial(_inproj_body, tm=tm, tiles_per_seq=tps, sample=False)
    return pl.pallas_call(
        body,
        grid=(n // tm,),
        in_specs=[row(d), _const_spec(g.shape), _const_spec(w_in.shape), _const_spec(conv_w.shape),
                  _const_spec(conv_norm.shape), _const_spec(bd.shape)],
        out_specs=[row(D_ATT), tok_minor, tok_minor, row(D_ATT),
                   pl.BlockSpec((nblk, D_ATT, MOBA_BLOCK), lambda i: (i, 0, 0)), row(dc),
                   pl.BlockSpec((1, nblk, D_ATT), lambda i: (i, 0, 0)),
                   pl.BlockSpec((1, CONV_W - 1, dc), lambda i: (i // tps, 0, 0))],
        out_shape=[f(D_ATT, F32), jax.ShapeDtypeStruct((batch, D_ATT, seq), F32),
                   jax.ShapeDtypeStruct((batch, D_ATT, seq), F32), f(D_ATT, BF16),
                   jax.ShapeDtypeStruct((n // MOBA_BLOCK, D_ATT, MOBA_BLOCK), BF16), f(dc, BF16),
                   jax.ShapeDtypeStruct((n // tm, nblk, D_ATT), F32),
                   jax.ShapeDtypeStruct((batch, CONV_W - 1, dc), F32)],
        scratch_shapes=[pltpu.VMEM((tm + 8, dc), F32)],
        compiler_params=pltpu.CompilerParams(dimension_semantics=("arbitrary",),
                                             vmem_limit_bytes=VMEM_LIMIT),
        name="inproj_prompt",
    )(x, g, w_in, conv_w, conv_norm, bd)


def _inproj_sample(x, g, w_in, conv_w, conv_norm, bd, s1, s2, *, seq):
    n, d = x.shape
    dc = conv_w.shape[1]
    tm = n
    full = lambda shape: pl.BlockSpec(shape, lambda i: (0,) * len(shape))
    f = lambda w, dt: jax.ShapeDtypeStruct((n, w), dt)
    body = functools.partial(_inproj_body, tm=tm, tiles_per_seq=1, sample=True)
    return pl.pallas_call(
        body,
        grid=(1,),
        in_specs=[full((tm, d)), full(g.shape), full(w_in.shape), full(conv_w.shape),
                  full(conv_norm.shape), full(bd.shape), full((tm, dc)), full((tm, dc))],
        out_specs=[full((tm, D_ATT)), full((tm, D_ATT)), full((tm, D_ATT)), full((tm, dc)),
                   full((n // seq, seq, dc)), full(w_in.shape)],
        out_shape=[f(D_ATT, F32), f(D_ATT, F32), f(D_ATT, F32), f(dc, BF16),
                   jax.ShapeDtypeStruct((n // seq, seq, dc), F32), jax.ShapeDtypeStruct(w_in.shape, BF16)],
        scratch_shapes=[pltpu.VMEM((tm + 8, dc), F32)],
        compiler_params=pltpu.CompilerParams(dimension_semantics=("arbitrary",),
                                             vmem_limit_bytes=VMEM_LIMIT),
        name="inproj_sample",
    )(x, g, w_in, conv_w, conv_norm, bd, s1, s2)


def _split3(x):
    hi = x.astype(BF16).astype(F32)
    mid = (x - hi).astype(BF16).astype(F32)
    lo = (x - hi - mid).astype(BF16).astype(F32)
    return hi, mid, lo


def _attn_prompt_body(q_ref, kb_ref, vt_ref, mean_ref, gain_ref, o_ref,
                      causal_ref, featk_ref, qabt_ref, colb_ref, so_ref, seta_ref, setb_ref,
                      m_ref, l_ref, acc_ref):
    blk = MOBA_BLOCK
    b = pl.program_id(0)
    j = pl.program_id(1)
    nb = mean_ref.shape[1]
    group = LANES // N_HEADS
    qcols = HEADS_PER_SLAB * blk
    lane_q = lax.broadcasted_iota(jnp.int32, (1, qcols), 1)

    def slope_row(p):
        return jnp.where(lane_q < blk, LOG2E * _slope(HEADS_PER_SLAB * p), LOG2E * _slope(HEADS_PER_SLAB * p + 1))

    @pl.when((b == 0) & (j == 0))
    def _init_tables():
        kk = lax.broadcasted_iota(jnp.int32, (blk, qcols), 0)
        qq = lax.broadcasted_iota(jnp.int32, (blk, qcols), 1)
        causal_ref[...] = jnp.where((qq % blk) >= kk, 0.0, NEG_INF)
        ki = lax.broadcasted_iota(jnp.int32, (blk, LANES), 0).astype(F32)
        kl = lax.broadcasted_iota(jnp.int32, (blk, LANES), 1)
        featk_ref[...] = jnp.where(kl < 3, ki, jnp.where(kl < 6, 1.0, 0.0)).astype(BF16)
        fr = lax.broadcasted_iota(jnp.int32, (LANES, qcols), 0)
        for p in range(N_SLABS):
            a = slope_row(p)
            terms = _split3(a) + _split3(-a * (lane_q % blk).astype(F32))
            feat = jnp.zeros((LANES, qcols), F32)
            for r, t in enumerate(terms):
                feat = jnp.where(fr == r, t, feat)
            qabt_ref[p, LANES:, :] = feat.astype(BF16)

    qt = q_ref[...].T
    means = mean_ref[0]
    if nb < group:
        means = jnp.concatenate([means, jnp.zeros((group - nb, D_ATT), F32)], axis=0)
    mt = jnp.concatenate([means] * N_HEADS, axis=0)
    rh = lax.broadcasted_iota(jnp.int32, mt.shape, 0) // group
    ch = lax.broadcasted_iota(jnp.int32, mt.shape, 1) // HEAD_DIM
    mbd = jnp.where(rh == ch, mt, 0.0)
    gate_t = jnp.dot(mbd, qt, precision=lax.Precision.HIGHEST, preferred_element_type=F32)
    gate = jnp.concatenate([gate_t[h * group:(h + 1) * group, :] for h in range(N_HEADS)], axis=1)

    n_idx = lax.broadcasted_iota(jnp.int32, gate.shape, 0)
    n_f = n_idx.astype(F32)
    valid = n_idx < j
    work = jnp.where(valid, gate, NEG_INF)
    picked = jnp.zeros(gate.shape, F32)
    for _ in range(MOBA_TOPK):
        top = jnp.max(work, axis=0, keepdims=True)
        first = jnp.min(jnp.where(work == top, n_f, float(group)), axis=0, keepdims=True)
        pick = n_f == first
        picked = jnp.where(pick, 1.0, picked)
        work = jnp.where(pick, REMOVED, work)
    colb_ref[...] = jnp.where((picked > 0.0) & valid, 0.0, NEG_INF)

    row_d = lax.broadcasted_iota(jnp.int32, (LANES, blk), 0)
    for p in range(N_SLABS):
        qs = qt[p * LANES:(p + 1) * LANES, :] * (SCALE * LOG2E)
        qa = jnp.where(row_d < HEAD_DIM, qs, 0.0)
        qb = jnp.where(row_d >= HEAD_DIM, qs, 0.0)
        qabt_ref[p, :LANES, :] = jnp.concatenate([qa, qb], axis=1).astype(BF16)

    slabs = [slice(p * LANES, (p + 1) * LANES) for p in range(N_SLABS)]

    def scores(n, p):
        off = pl.multiple_of(n * blk, blk)
        keys = jnp.concatenate([kb_ref[pl.ds(off, blk), slabs[p]], featk_ref[...]], axis=1)
        return _dot(keys, qabt_ref[p])

    sets = (seta_ref, setb_ref)

    def park_unit(n_first, count, p):
        for g in range(count):
            sets[p % 2][g] = scores(jnp.minimum(n_first + g, j), p)

    def weighted_values(blocks, p, e):
        vt = jnp.concatenate([vt_ref[n, slabs[p], :] for n in blocks], axis=1)
        ones = jnp.ones((SUM_ROWS, vt.shape[1]), BF16)
        pvs = [_dot(jnp.concatenate([vt[h * HEAD_DIM:(h + 1) * HEAD_DIM], ones], axis=0),
                    e[:, h * blk:(h + 1) * blk]) for h in range(HEADS_PER_SLAB)]
        return (jnp.concatenate([pv[:HEAD_DIM] for pv in pvs], axis=0),
                jnp.concatenate([pv[HEAD_DIM:HEAD_DIM + 1] for pv in pvs], axis=1))

    def by_head(row):
        return jnp.where(row_d < HEAD_DIM, row[:, :blk], row[:, blk:])

    def reduce_unit(n_first, count, p):
        cs = slice(p * qcols, (p + 1) * qcols)
        src = sets[p % 2]
        blocks = [n_first + g for g in range(count)]
        crows = [colb_ref[pl.ds(n, 1), cs] - slope_row(p) * ((j - n) * blk).astype(F32) for n in blocks]
        m_prev = m_ref[p]
        m_new = m_prev
        for g, crow in enumerate(crows):
            m_new = jnp.maximum(m_new, jnp.max(src[g], axis=0, keepdims=True) + crow)
        alpha = jnp.exp2(m_prev - m_new)
        e = jnp.concatenate([jnp.exp2(src[g] - (m_new - crow)).astype(BF16) for g, crow in enumerate(crows)],
                            axis=0)
        pv, esum = weighted_values(blocks, p, e)
        m_ref[p] = m_new
        l_ref[p] = alpha * l_ref[p] + esum
        acc_ref[p] = by_head(alpha) * acc_ref[p] + pv

    for p in range(N_SLABS):
        so_ref[p] = scores(j, p)
    park_unit(0, UPDATE_BLOCKS, 0)
    for p in range(N_SLABS):
        sb = so_ref[p] + causal_ref[...]
        m = jnp.max(sb, axis=0, keepdims=True)
        pv, esum = weighted_values([j], p, jnp.exp2(sb - m).astype(BF16))
        m_ref[p] = m
        l_ref[p] = esum
        acc_ref[p] = pv

    def sweep(n_first, count, n_after):
        for p in range(N_SLABS):
            if p + 1 < N_SLABS:
                park_unit(n_first, count, p + 1)
            elif n_after is not None:
                park_unit(n_after, UPDATE_BLOCKS, 0)
            reduce_unit(n_first, count, p)

    def trip(t, carry):
        sweep(UPDATE_BLOCKS * t, UPDATE_BLOCKS, UPDATE_BLOCKS * (t + 1))
        return carry

    lax.fori_loop(0, j // UPDATE_BLOCKS, trip, 0)

    nr = (j // UPDATE_BLOCKS) * UPDATE_BLOCKS
    for r in range(1, UPDATE_BLOCKS):
        @pl.when(j - nr == r)
        def _(r=r):
            sweep(nr, r, None)

    for p in range(N_SLABS):
        o2 = acc_ref[p] / by_head(l_ref[p])
        sq = o2 * o2
        ms_a = jnp.sum(sq[:HEAD_DIM], axis=0, keepdims=True) * (1.0 / HEAD_DIM)
        ms_b = jnp.sum(sq[HEAD_DIM:], axis=0, keepdims=True) * (1.0 / HEAD_DIM)
        inv = jnp.where(row_d < HEAD_DIM, lax.rsqrt(ms_a + EPS), lax.rsqrt(ms_b + EPS))
        ls = slice(p * LANES, (p + 1) * LANES)
        o_ref[:, ls] = ((o2 * inv).T * gain_ref[:, ls]).astype(BF16)


def _attn_prompt(q, kb, vt, means, gain, *, batch):
    n = q.shape[0]
    seq = n // batch
    nb = seq // MOBA_BLOCK
    group = LANES // N_HEADS
    assert nb <= group and seq % MOBA_BLOCK == 0
    means = means.reshape(batch, nb, D_ATT)
    qcols = HEADS_PER_SLAB * MOBA_BLOCK
    return pl.pallas_call(
        _attn_prompt_body,
        grid=(batch, nb),
        in_specs=[pl.BlockSpec((MOBA_BLOCK, D_ATT), lambda b, j: (b * nb + j, 0)),
                  pl.BlockSpec((seq, D_ATT), lambda b, j: (b, 0)),
                  pl.BlockSpec((nb, D_ATT, MOBA_BLOCK), lambda b, j: (b, 0, 0)),
                  pl.BlockSpec((1, nb, D_ATT), lambda b, j: (b, 0, 0)),
                  pl.BlockSpec(gain.shape, lambda b, j: (0, 0))],
        out_specs=pl.BlockSpec((MOBA_BLOCK, D_ATT), lambda b, j: (b * nb + j, 0)),
        out_shape=jax.ShapeDtypeStruct((n, D_ATT), BF16),
        scratch_shapes=[pltpu.VMEM((MOBA_BLOCK, qcols), F32),
                        pltpu.VMEM((MOBA_BLOCK, LANES), BF16),
                        pltpu.VMEM((N_SLABS, 2 * LANES, qcols), BF16),
                        pltpu.VMEM((group, N_HEADS * MOBA_BLOCK), F32),
                        pltpu.VMEM((N_SLABS, MOBA_BLOCK, qcols), F32),
                        pltpu.VMEM((UPDATE_BLOCKS, MOBA_BLOCK, qcols), F32),
                        pltpu.VMEM((UPDATE_BLOCKS, MOBA_BLOCK, qcols), F32),
                        pltpu.VMEM((N_SLABS, 1, qcols), F32),
                        pltpu.VMEM((N_SLABS, 1, qcols), F32),
                        pltpu.VMEM((N_SLABS, LANES, MOBA_BLOCK), F32)],
        compiler_params=pltpu.CompilerParams(dimension_semantics=("arbitrary", "arbitrary"),
                                             vmem_limit_bytes=VMEM_LIMIT),
        name="attn_prompt",
    )(q, kb, vt, means, gain)


def _attn_sample_t_body(pt_ref, q_ref, kn_ref, vn_ref, gain_ref, bd_ref, ckt_hbm, cvt_hbm, o_ref,
                        kbuf, vbuf, s_ref, acc_ref, ksem, vsem, *, past_len, page, page_base):
    b = pl.program_id(0)
    nbat = pl.num_programs(0)
    n_pages = past_len // page
    ppb = MOBA_BLOCK // page
    nb = past_len // MOBA_BLOCK
    t_new = q_ref.shape[1]
    rows = N_HEADS * t_new

    def page_copy(hbm, buf, sem, bb, pg):
        return pltpu.make_async_copy(hbm.at[page_base + pt_ref[bb, pg]], buf.at[pg], sem.at[pg])

    def start_all(hbm, buf, sem, bb):
        def body(i, c):
            for k in range(DMA_THREADS):
                page_copy(hbm, buf, sem, bb, i * DMA_THREADS + k).start(priority=k)
            return c
        lax.fori_loop(0, n_pages // DMA_THREADS, body, 0)

    def refill(hbm, buf, sem, pgs):
        @pl.when(b + 1 < nbat)
        def _():
            for k, pg in enumerate(pgs):
                page_copy(hbm, buf, sem, b + 1, pg).start(priority=k % DMA_THREADS)

    @pl.when(b == 0)
    def _():
        start_all(ckt_hbm, kbuf, ksem, b)
        start_all(cvt_hbm, vbuf, vsem, b)

    qt = jnp.concatenate([q_ref[0]] * N_HEADS, axis=0)
    rh = lax.broadcasted_iota(jnp.int32, qt.shape, 0) // t_new
    ch = lax.broadcasted_iota(jnp.int32, qt.shape, 1) // HEAD_DIM
    qs = jnp.where(rh == ch, qt, 0.0) * SCALE
    q_hi = qs.astype(BF16)
    q_lo = (qs - q_hi.astype(F32)).astype(BF16)
    qq = jnp.concatenate([q_hi, q_lo], axis=0)

    def k_tile(i, c):
        pgs = [i * K_TILE + k for k in range(K_TILE)]
        for pg in pgs:
            page_copy(ckt_hbm, kbuf, ksem, b, pg).wait()
        for pg in pgs:
            s2 = _dot(qq, kbuf[pg].reshape(D_ATT, page).astype(BF16))
            s_ref[pg] = s2[:rows] + s2[rows:]
        refill(ckt_hbm, kbuf, ksem, pgs)
        return c
    lax.fori_loop(0, n_pages // K_TILE, k_tile, 0)

    lane = lax.broadcasted_iota(jnp.int32, (rows, LANES), 1)
    gate = jnp.zeros((rows, LANES), F32)
    gcols = []
    for n in range(nb):
        tot = s_ref[n * ppb]
        for i in range(1, ppb):
            tot = tot + s_ref[n * ppb + i]
        g = jnp.sum(tot, axis=1, keepdims=True)
        gcols.append(g)
        gate = jnp.where(lane == n, g, gate)
    rank = jnp.zeros(gate.shape, jnp.int32)
    for m in range(nb):
        beats = (gcols[m] > gate) | ((gcols[m] == gate) & (lane > m))
        rank = rank + beats.astype(jnp.int32)
    colb = jnp.where(rank < MOBA_TOPK, 0.0, NEG_INF)

    r1 = lax.broadcasted_iota(jnp.int32, (rows, 1), 0)
    tq = r1 % t_new
    slope = jnp.zeros((rows, 1), F32)
    for h in range(N_HEADS):
        slope = jnp.where(r1 // t_new == h, _slope(h), slope)
    in_page = slope * (tq - lane).astype(F32)

    zpad = jnp.zeros((LANES - t_new, D_ATT), F32)
    kn = jnp.concatenate([kn_ref[0], zpad], axis=0).astype(BF16)
    vn = jnp.concatenate([vn_ref[0], zpad], axis=0).astype(BF16)
    s2 = _dot_nt(qq, kn)
    s_own = jnp.where(lane <= tq, s2[:rows] + s2[rows:] - in_page, NEG_INF)

    mrun = s_own
    for n in range(nb):
        mask_n = jnp.sum(jnp.where(lane == n, colb, 0.0), axis=1, keepdims=True)
        for i in range(ppb):
            pg = n * ppb + i
            sn = s_ref[pg] - in_page + (mask_n - slope * float(past_len - pg * page))
            s_ref[pg] = sn
            mrun = jnp.maximum(mrun, sn)
    m = jnp.max(mrun, axis=1, keepdims=True)

    e_own = jnp.exp(s_own - m)
    lrun = e_own
    for pg in range(n_pages):
        e = jnp.exp(s_ref[pg] - m)
        s_ref[pg] = e
        lrun = lrun + e
    l = jnp.sum(lrun, axis=1, keepdims=True)

    acc_ref[...] = jnp.zeros(acc_ref.shape, F32)
    zrows = jnp.zeros((LANES - rows, V_TILE * page), BF16)

    def v_tile(i, c):
        pgs = [i * V_TILE + k for k in range(V_TILE)]
        for pg in pgs:
            page_copy(cvt_hbm, vbuf, vsem, b, pg).wait()
        vt = jnp.concatenate([vbuf[pg].reshape(D_ATT, page) for pg in pgs], axis=1).astype(BF16)
        p = jnp.concatenate([s_ref[pg] for pg in pgs], axis=1).astype(BF16)
        acc_ref[...] += _dot_nt(vt, jnp.concatenate([p, zrows], axis=0))
        refill(cvt_hbm, vbuf, vsem, pgs)
        return c
    lax.fori_loop(0, n_pages // V_TILE, v_tile, 0)

    acc = acc_ref[...].T[:rows] + _dot(e_own.astype(BF16), vn)
    accn = acc / l
    ch8 = lax.broadcasted_iota(jnp.int32, (t_new, D_ATT), 1) // HEAD_DIM
    out = jnp.zeros((t_new, D_ATT), F32)
    for h in range(N_HEADS):
        out = jnp.where(ch8 == h, accn[h * t_new:(h + 1) * t_new, :], out)
    o_ref[0] = _group_rms(out, gain_ref[...], bd_ref).astype(BF16)


def _attn_sample_t(page_table, q, kn, vn, gain, bd, cache_kt, cache_vt, *, page_base, past_len):
    nbat, t_new, _ = q.shape
    page = cache_kt.shape[3]
    n_pages = past_len // page
    rows = N_HEADS * t_new
    assert past_len % MOBA_BLOCK == 0 and MOBA_BLOCK % page == 0 and page == LANES
    assert rows <= LANES and t_new % 8 == 0 and past_len // MOBA_BLOCK <= LANES
    assert n_pages % V_TILE == 0 and n_pages % K_TILE == 0
    body = functools.partial(_attn_sample_t_body, past_len=past_len, page=page, page_base=page_base)
    per_b = pl.BlockSpec((1, t_new, D_ATT), lambda b, pt: (b, 0, 0))
    grid_spec = pltpu.PrefetchScalarGridSpec(
        num_scalar_prefetch=1,
        grid=(nbat,),
        in_specs=[per_b, per_b, per_b,
                  pl.BlockSpec(gain.shape, lambda b, pt: (0, 0)),
                  pl.BlockSpec(bd.shape, lambda b, pt: (0, 0)),
                  pl.BlockSpec(memory_space=pl.ANY),
                  pl.BlockSpec(memory_space=pl.ANY)],
        out_specs=per_b,
        scratch_shapes=[pltpu.VMEM((n_pages, N_HEADS, HEAD_DIM, page), F32),
                        pltpu.VMEM((n_pages, N_HEADS, HEAD_DIM, page), F32),
                        pltpu.VMEM((n_pages, rows, page), F32),
                        pltpu.VMEM((D_ATT, LANES), F32),
                        pltpu.SemaphoreType.DMA((n_pages,)),
                        pltpu.SemaphoreType.DMA((n_pages,))],
    )
    return pl.pallas_call(
        body,
        grid_spec=grid_spec,
        out_shape=jax.ShapeDtypeStruct((nbat, t_new, D_ATT), BF16),
        compiler_params=pltpu.CompilerParams(dimension_semantics=("arbitrary",),
                                             vmem_limit_bytes=VMEM_LIMIT),
        name="attn_sample",
    )(page_table, q, kn, vn, gain, bd, cache_kt, cache_vt)


def kernel(x_prompt, x_sample, cache_k, cache_v, state_conv, page_table, ffn1_norm, ffn1_w_gu, ffn1_w_down,
           mix_norm, w_in, conv_w, conv_out_norm, attn_out_norm, w_out, ffn2_norm, ffn2_w_gu, ffn2_w_down,
           final_norm):
    bp, seq, d = x_prompt.shape
    bs, dseq, _ = x_sample.shape
    depth, n_pool, page = cache_k.shape[:3]
    dc = conv_w.shape[2]
    past_len = page_table.shape[1] * page

    ck = jnp.transpose(cache_k, (0, 1, 3, 4, 2)).reshape(depth * n_pool, N_HEADS, HEAD_DIM, page)
    cv = jnp.transpose(cache_v, (0, 1, 3, 4, 2)).reshape(depth * n_pool, N_HEADS, HEAD_DIM, page)
    gi = lax.broadcasted_iota(jnp.int32, (D_ATT, D_ATT), 0) // HEAD_DIM
    gj = lax.broadcasted_iota(jnp.int32, (D_ATT, D_ATT), 1) // HEAD_DIM
    bd = (gi == gj).astype(BF16)

    xp = x_prompt.reshape(bp * seq, d)
    xs = x_sample.reshape(bs * dseq, d)
    tm_p = PROMPT_ROW_TILE
    assert seq % tm_p == 0 and tm_p % MOBA_BLOCK == 0
    row = lambda a: a.reshape(1, -1)
    outs = [[] for _ in range(6)]
    for l in range(depth):
        g1, gm, g2 = row(ffn1_norm[l]), row(mix_norm[l]), row(ffn2_norm[l])
        gc, ga = row(conv_out_norm[l]), row(attn_out_norm[l])
        last = l == depth - 1
        gfin = row(final_norm) if last else None

        st = state_conv[l]
        zpad = jnp.zeros((bs, dseq - (CONV_W - 1), dc), F32)
        s2 = jnp.concatenate([st, zpad], axis=1).reshape(bs * dseq, dc)
        s1 = jnp.concatenate([st[:, 1:2], jnp.zeros((bs, dseq - 1, dc), F32)], axis=1).reshape(bs * dseq, dc)
        x1s, wg1, wu1, wd1 = _ffn_stream_call(xs, g1, ffn1_w_gu[l], ffn1_w_down[l], name="ffn1_sample")
        qs, ks, vs, ycs, us, win = _inproj_sample(x1s, gm, w_in[l], conv_w[l], gc, bd, s1, s2, seq=dseq)
        r3 = lambda a: a.reshape(bs, dseq, D_ATT)
        yas = _attn_sample_t(page_table, r3(qs), r3(ks), r3(vs), ga, bd, ck, cv,
                             page_base=l * n_pool, past_len=past_len)
        xs, wg2, wu2, wd2, woc, woa = _ffn_stream_call(
            x1s, g2, ffn2_w_gu[l], ffn2_w_down[l], mix=(ycs, yas.reshape(bs * dseq, D_ATT), w_out[l]),
            final=gfin, name="ffn2_sample")
        outs[3].append(ks.reshape(bs, dseq, N_HEADS, HEAD_DIM))
        outs[4].append(vs.reshape(bs, dseq, N_HEADS, HEAD_DIM))
        outs[5].append(us[:, dseq - (CONV_W - 1):, :])

        x1 = _ffn_call(xp, g1, wg1, wu1, wd1, tm=tm_p, name="ffn1_prompt")
        q, kt, vtf, kb, vt, yc, means, cnew = _inproj_prompt(x1, gm, win, conv_w[l], gc, bd, batch=bp, tm=tm_p)
        ya = _attn_prompt(q, kb, vt, means, ga, batch=bp)
        xp = _ffn_call(x1, g2, wg2, wu2, wd2, tm=tm_p, mix=(yc, ya, woc, woa), final=gfin, name="ffn2_prompt")
        tok_major = lambda a: a.reshape(bp, N_HEADS, HEAD_DIM, seq).transpose(0, 3, 1, 2)
        outs[0].append(tok_major(kt))
        outs[1].append(tok_major(vtf))
        outs[2].append(cnew)

    y_prompt = xp.reshape(bp, seq, d)
    y_sample = xs.reshape(bs, dseq, d)
    kp, vp, cp, ksn, vsn, csn = (jnp.stack(o) for o in outs)
    return (y_prompt, y_sample, kp, vp, cp, ksn, vsn, csn)
```

```python
import functools

import jax
import jax.numpy as jnp
from jax import lax
from jax.experimental import pallas as pl
from jax.experimental.pallas import tpu as pltpu

F32 = jnp.float32
BF16 = jnp.bfloat16

N_HEADS = 8
HEAD_DIM = 64
D_ATT = N_HEADS * HEAD_DIM
N_CONV_GROUPS = 8
CONV_W = 3
MOBA_BLOCK = 256
MOBA_TOPK = 3
EPS = 1e-5
NEG_INF = -1e30
REMOVED = -3e38
SCALE = HEAD_DIM ** -0.5
LOG2E = 1.4426950408889634

LANES = 128
MXU_WIDTH = 256
HEADS_PER_SLAB = LANES // HEAD_DIM
N_SLABS = D_ATT // LANES
VMEM_LIMIT = 56 * 1024 * 1024
PROMPT_ROW_TILE = 512
UPDATE_BLOCKS = 2
SUM_ROWS = 16
DMA_THREADS = 2
K_TILE = 16
V_TILE = 16

NT_DIMS = (((1,), (1,)), ((), ()))


def _slope(h):
    return 2.0 ** (-(8.0 / N_HEADS) * (h + 1))


def _dot(a, b):
    return jnp.dot(a, b, preferred_element_type=F32)


def _dot_nt(a, b):
    return lax.dot_general(a, b, NT_DIMS, preferred_element_type=F32)


def _rms(x, g):
    ms = jnp.mean(x * x, axis=-1, keepdims=True)
    return x * lax.rsqrt(ms + EPS) * g


def _group_sumsq(y, bd_ref):
    y2 = y * y
    hi = y2.astype(BF16)
    lo = (y2 - hi.astype(F32)).astype(BF16)
    bd = bd_ref[...]
    return _dot(hi, bd) + _dot(lo, bd)


def _group_rms(y, g, bd_ref):
    ms = _group_sumsq(y, bd_ref) * (1.0 / HEAD_DIM)
    return y * lax.rsqrt(ms + EPS) * g


def _const_spec(shape):
    nd = len(shape)
    return pl.BlockSpec(shape, lambda *_: (0,) * nd, pipeline_mode=pl.Buffered(1))


def _ffn_body(*refs, mix, final, bounds):
    it = iter(refs)
    x_ref = next(it)
    if mix:
        yc_ref, ya_ref, woc_ref, woa_ref = next(it), next(it), next(it), next(it)
    g_ref, wg_ref, wu_ref, wd_ref = next(it), next(it), next(it), next(it)
    gf_ref = next(it) if final else None
    o_ref = next(it)

    x = x_ref[...]
    if mix:
        x = x + _dot(yc_ref[...], woc_ref[...]) + _dot(ya_ref[...], woa_ref[...])
    h = _rms(x, g_ref[...]).astype(BF16)
    acc = jnp.zeros(x.shape, F32)
    for lo, hi in zip(bounds[:-1], bounds[1:]):
        gate = _dot(h, wg_ref[:, lo:hi])
        up = _dot(h, wu_ref[:, lo:hi])
        act = (gate * jax.nn.sigmoid(gate) * up).astype(BF16)
        acc = acc + _dot(act, wd_ref[lo:hi, :])
    x = x + 0.5 * acc
    if final:
        x = _rms(x, gf_ref[...])
    o_ref[...] = x


def _ffn_call(x, g, wg, wu, wd, *, tm, mix=None, final=None, name):
    n, d = x.shape
    d_ff = wd.shape[0]
    row = lambda w: pl.BlockSpec((tm, w), lambda i: (i, 0))
    ins, specs = [x], [row(d)]
    if mix is not None:
        yc, ya, woc, woa = mix
        ins += [yc, ya, woc, woa]
        specs += [row(yc.shape[1]), row(ya.shape[1]), _const_spec(woc.shape), _const_spec(woa.shape)]
    ins += [g, wg, wu, wd]
    specs += [_const_spec(g.shape), _const_spec(wg.shape), _const_spec(wu.shape), _const_spec(wd.shape)]
    if final is not None:
        ins.append(final)
        specs.append(_const_spec(final.shape))
    assert d_ff % MXU_WIDTH == 0
    tiles = d_ff // MXU_WIDTH
    bounds = (0, (tiles + 1) // 2 * MXU_WIDTH, d_ff)
    body = functools.partial(_ffn_body, mix=mix is not None, final=final is not None, bounds=bounds)
    return pl.pallas_call(
        body,
        grid=(n // tm,),
        in_specs=specs,
        out_specs=row(d),
        out_shape=jax.ShapeDtypeStruct((n, d), F32),
        compiler_params=pltpu.CompilerParams(dimension_semantics=("arbitrary",),
                                             vmem_limit_bytes=VMEM_LIMIT),
        name=name,
    )(*ins)


def _ffn_stream_body(*refs, mix, final):
    it = iter(refs)
    x_ref = next(it)
    if mix:
        yc_ref, ya_ref, woc_ref, woa_ref = next(it), next(it), next(it), next(it)
    g_ref, wg_ref, wu_ref, wd_ref = next(it), next(it), next(it), next(it)
    gf_ref = next(it) if final else None
    o_ref, wgb_ref, wub_ref, wdb_ref = next(it), next(it), next(it), next(it)
    if mix:
        wocb_ref, woab_ref = next(it), next(it)
    x_scr, h_scr, acc_scr = next(it), next(it), next(it)
    c = pl.program_id(0)

    @pl.when(c == 0)
    def _():
        x = x_ref[...]
        if mix:
            woc, woa = woc_ref[...].astype(BF16), woa_ref[...].astype(BF16)
            wocb_ref[...] = woc
            woab_ref[...] = woa
            x = x + _dot(yc_ref[...], woc) + _dot(ya_ref[...], woa)
        x_scr[...] = x
        h_scr[...] = _rms(x, g_ref[...]).astype(BF16)
        acc_scr[...] = jnp.zeros(acc_scr.shape, F32)

    wg, wu, wd = wg_ref[...].astype(BF16), wu_ref[...].astype(BF16), wd_ref[...].astype(BF16)
    wgb_ref[...] = wg
    wub_ref[...] = wu
    wdb_ref[...] = wd
    h = h_scr[...]
    gate = _dot(h, wg)
    act = (gate * jax.nn.sigmoid(gate) * _dot(h, wu)).astype(BF16)
    acc_scr[...] += _dot(act, wd)

    @pl.when(c == pl.num_programs(0) - 1)
    def _():
        x = x_scr[...] + 0.5 * acc_scr[...]
        if final:
            x = _rms(x, gf_ref[...])
        o_ref[...] = x


def _ffn_stream_call(x, g, w_gu, w_down, *, mix=None, final=None, name):
    n, d = x.shape
    d_ff = w_down.shape[0]
    tw = MXU_WIDTH
    assert d_ff % tw == 0 and w_gu.shape == (d, 2 * d_ff)
    nt = d_ff // tw
    full = lambda shape: pl.BlockSpec(shape, lambda c: (0,) * len(shape))
    ins, specs = [x], [full((n, d))]
    outs = [jax.ShapeDtypeStruct((n, d), F32), jax.ShapeDtypeStruct((d, d_ff), BF16),
            jax.ShapeDtypeStruct((d, d_ff), BF16), jax.ShapeDtypeStruct((d_ff, d), BF16)]
    out_specs = [full((n, d)), pl.BlockSpec((d, tw), lambda c: (0, c)), pl.BlockSpec((d, tw), lambda c: (0, c)),
                 pl.BlockSpec((tw, d), lambda c: (c, 0))]
    if mix is not None:
        yc, ya, w_out = mix
        dm = yc.shape[1]
        assert w_out.shape == (dm + ya.shape[1], d) and ya.shape[1] == dm
        ins += [yc, ya, w_out, w_out]
        specs += [full(yc.shape), full(ya.shape), pl.BlockSpec((dm, d), lambda c: (0, 0)),
                  pl.BlockSpec((dm, d), lambda c: (1, 0))]
        outs += [jax.ShapeDtypeStruct((dm, d), BF16)] * 2
        out_specs += [full((dm, d))] * 2
    ins += [g, w_gu, w_gu, w_down]
    specs += [full(g.shape), pl.BlockSpec((d, tw), lambda c: (0, c)), pl.BlockSpec((d, tw), lambda c: (0, nt + c)),
              pl.BlockSpec((tw, d), lambda c: (c, 0))]
    if final is not None:
        ins.append(final)
        specs.append(full(final.shape))
    body = functools.partial(_ffn_stream_body, mix=mix is not None, final=final is not None)
    return pl.pallas_call(
        body,
        grid=(nt,),
        in_specs=specs,
        out_specs=out_specs,
        out_shape=outs,
        scratch_shapes=[pltpu.VMEM((n, d), F32), pltpu.VMEM((n, d), BF16), pltpu.VMEM((n, d), F32)],
        compiler_params=pltpu.CompilerParams(dimension_semantics=("arbitrary",),
                                             vmem_limit_bytes=VMEM_LIMIT),
        name=name,
    )(*ins)


def _inproj_body(*refs, tm, tiles_per_seq, sample):
    it = iter(refs)
    x_ref, g_ref, win_ref, cw_ref, cn_ref, bd_ref = (next(it) for _ in range(6))
    if sample:
        s1_ref, s2_ref = next(it), next(it)
        q_ref, k_ref, v_ref, yc_ref, u_ref, winb_ref = (next(it) for _ in range(6))
    else:
        q_ref, kt_ref, vtf_ref, kb_ref, vt_ref, yc_ref, mean_ref, cnew_ref = (next(it) for _ in range(8))
    ubuf = next(it)

    dc = yc_ref.shape[1]
    if sample:
        ubuf[0:8, :] = jnp.zeros((8, dc), F32)
    else:
        first = (pl.program_id(0) % tiles_per_seq) == 0

        @pl.when(first)
        def _():
            ubuf[0:8, :] = jnp.zeros((8, dc), F32)

        @pl.when(jnp.logical_not(first))
        def _():
            ubuf[0:8, :] = ubuf[tm:tm + 8, :]

    h = _rms(x_ref[...], g_ref[...]).astype(BF16)
    if sample:
        def piece(c, w):
            wp = win_ref[:, c:c + w].astype(BF16)
            winb_ref[:, c:c + w] = wp
            return _dot(h, wp)
    else:
        piece = lambda c, w: _dot(h, win_ref[:, c:c + w])
    hc = piece(0, dc)
    cg = piece(2 * dc, dc)
    bg = piece(dc, dc)
    k = piece(3 * dc + D_ATT, D_ATT)
    v = piece(3 * dc + 2 * D_ATT, D_ATT)
    if sample:
        k_ref[...] = k
        v_ref[...] = v

    u = cg * hc
    ubuf[8:tm + 8, :] = u
    um1 = ubuf[7:tm + 7, :]
    um2 = ubuf[6:tm + 6, :]
    if sample:
        t = lax.broadcasted_iota(jnp.int32, (tm, dc), 0) % u_ref.shape[1]
        um1 = jnp.where(t >= 1, um1, s1_ref[...])
        um2 = jnp.where(t >= 2, um2, s2_ref[...])
    cw = cw_ref[...]
    conv = um2 * cw[0:1, :] + um1 * cw[1:2, :] + u * cw[2:3, :]
    yc_ref[...] = _group_rms(bg * conv, cn_ref[...], bd_ref).astype(BF16)
    q_ref[...] = piece(3 * dc, D_ATT)

    if sample:
        u_ref[...] = u.reshape(u_ref.shape)
    else:
        vt = v.T
        kt_ref[0] = k.T
        vtf_ref[0] = vt
        kb_ref[...] = k.astype(BF16)
        nblk = tm // MOBA_BLOCK
        for i in range(nblk):
            vt_ref[i] = vt[:, i * MOBA_BLOCK:(i + 1) * MOBA_BLOCK].astype(BF16)
        mean_ref[0] = jnp.sum(k.reshape(nblk, MOBA_BLOCK, D_ATT), axis=1) * (1.0 / MOBA_BLOCK)
        cnew_ref[0] = ubuf[tm + 6:tm + 8, :]


def _inproj_prompt(x, g, w_in, conv_w, conv_norm, bd, *, batch, tm):
    n, d = x.shape
    dc = conv_w.shape[1]
    seq = n // batch
    tps = seq // tm
    nblk = tm // MOBA_BLOCK
    row = lambda w: pl.BlockSpec((tm, w), lambda i: (i, 0))
    tok_minor = pl.BlockSpec((1, D_ATT, tm), lambda i: (i // tps, 0, i % tps))
    f = lambda w, dt: jax.ShapeDtypeStruct((n, w), dt)
    body = functools.partial(_inproj_body, tm=tm, tiles_per_seq=tps, sample=False)
    return pl.pallas_call(
        body,
        grid=(n // tm,),
        in_specs=[row(d), _const_spec(g.shape), _const_spec(w_in.shape), _const_spec(conv_w.shape),
                  _const_spec(conv_norm.shape), _const_spec(bd.shape)],
        out_specs=[row(D_ATT), tok_minor, tok_minor, row(D_ATT),
                   pl.BlockSpec((nblk, D_ATT, MOBA_BLOCK), lambda i: (i, 0, 0)), row(dc),
                   pl.BlockSpec((1, nblk, D_ATT), lambda i: (i, 0, 0)),
                   pl.BlockSpec((1, CONV_W - 1, dc), lambda i: (i // tps, 0, 0))],
        out_shape=[f(D_ATT, F32), jax.ShapeDtypeStruct((batch, D_ATT, seq), F32),
                   jax.ShapeDtypeStruct((batch, D_ATT, seq), F32), f(D_ATT, BF16),
                   jax.ShapeDtypeStruct((n // MOBA_BLOCK, D_ATT, MOBA_BLOCK), BF16), f(dc, BF16),
                   jax.ShapeDtypeStruct((n // tm, nblk, D_ATT), F32),
                   jax.ShapeDtypeStruct((batch, CONV_W - 1, dc), F32)],
        scratch_shapes=[pltpu.VMEM((tm + 8, dc), F32)],
        compiler_params=pltpu.CompilerParams(dimension_semantics=("arbitrary",),
                                             vmem_limit_bytes=VMEM_LIMIT),
        name="inproj_prompt",
    )(x, g, w_in, conv_w, conv_norm, bd)


def _inproj_sample(x, g, w_in, conv_w, conv_norm, bd, s1, s2, *, seq):
    n, d = x.shape
    dc = conv_w.shape[1]
    tm = n
    full = lambda shape: pl.BlockSpec(shape, lambda i: (0,) * len(shape))
    f = lambda w, dt: jax.ShapeDtypeStruct((n, w), dt)
    body = functools.partial(_inproj_body, tm=tm, tiles_per_seq=1, sample=True)
    return pl.pallas_call(
        body,
        grid=(1,),
        in_specs=[full((tm, d)), full(g.shape), full(w_in.shape), full(conv_w.shape),
                  full(conv_norm.shape), full(bd.shape), full((tm, dc)), full((tm, dc))],
        out_specs=[full((tm, D_ATT)), full((tm, D_ATT)), full((tm, D_ATT)), full((tm, dc)),
                   full((n // seq, seq, dc)), full(w_in.shape)],
        out_shape=[f(D_ATT, F32), f(D_ATT, F32), f(D_ATT, F32), f(dc, BF16),
                   jax.ShapeDtypeStruct((n // seq, seq, dc), F32), jax.ShapeDtypeStruct(w_in.shape, BF16)],
        scratch_shapes=[pltpu.VMEM((tm + 8, dc), F32)],
        compiler_params=pltpu.CompilerParams(dimension_semantics=("arbitrary",),
                                             vmem_limit_bytes=VMEM_LIMIT),
        name="inproj_sample",
    )(x, g, w_in, conv_w, conv_norm, bd, s1, s2)


def _split3(x):
    hi = x.astype(BF16).astype(F32)
    mid = (x - hi).astype(BF16).astype(F32)
    lo = (x - hi - mid).astype(BF16).astype(F32)
    return hi, mid, lo


def _attn_prompt_body(q_ref, kb_ref, vt_ref, mean_ref, gain_ref, o_ref,
                      causal_ref, featk_ref, qabt_ref, colb_ref, so_ref, seta_ref, setb_ref,
                      m_ref, l_ref, acc_ref):
    blk = MOBA_BLOCK
    b = pl.program_id(0)
    j = pl.program_id(1)
    nb = mean_ref.shape[1]
    group = LANES // N_HEADS
    qcols = HEADS_PER_SLAB * blk
    lane_q = lax.broadcasted_iota(jnp.int32, (1, qcols), 1)

    def slope_row(p):
        return jnp.where(lane_q < blk, LOG2E * _slope(HEADS_PER_SLAB * p), LOG2E * _slope(HEADS_PER_SLAB * p + 1))

    @pl.when((b == 0) & (j == 0))
    def _init_tables():
        kk = lax.broadcasted_iota(jnp.int32, (blk, qcols), 0)
        qq = lax.broadcasted_iota(jnp.int32, (blk, qcols), 1)
        causal_ref[...] = jnp.where((qq % blk) >= kk, 0.0, NEG_INF)
        ki = lax.broadcasted_iota(jnp.int32, (blk, LANES), 0).astype(F32)
        kl = lax.broadcasted_iota(jnp.int32, (blk, LANES), 1)
        featk_ref[...] = jnp.where(kl < 3, ki, jnp.where(kl < 6, 1.0, 0.0)).astype(BF16)
        fr = lax.broadcasted_iota(jnp.int32, (LANES, qcols), 0)
        for p in range(N_SLABS):
            a = slope_row(p)
            terms = _split3(a) + _split3(-a * (lane_q % blk).astype(F32))
            feat = jnp.zeros((LANES, qcols), F32)
            for r, t in enumerate(terms):
                feat = jnp.where(fr == r, t, feat)
            qabt_ref[p, LANES:, :] = feat.astype(BF16)

    qt = q_ref[...].T
    means = mean_ref[0]
    if nb < group:
        means = jnp.concatenate([means, jnp.zeros((group - nb, D_ATT), F32)], axis=0)
    mt = jnp.concatenate([means] * N_HEADS, axis=0)
    rh = lax.broadcasted_iota(jnp.int32, mt.shape, 0) // group
    ch = lax.broadcasted_iota(jnp.int32, mt.shape, 1) // HEAD_DIM
    mbd = jnp.where(rh == ch, mt, 0.0)
    gate_t = jnp.dot(mbd, qt, precision=lax.Precision.HIGHEST, preferred_element_type=F32)
    gate = jnp.concatenate([gate_t[h * group:(h + 1) * group, :] for h in range(N_HEADS)], axis=1)

    n_idx = lax.broadcasted_iota(jnp.int32, gate.shape, 0)
    n_f = n_idx.astype(F32)
    valid = n_idx < j
    work = jnp.where(valid, gate, NEG_INF)
    picked = jnp.zeros(gate.shape, F32)
    for _ in range(MOBA_TOPK):
        top = jnp.max(work, axis=0, keepdims=True)
        first = jnp.min(jnp.where(work == top, n_f, float(group)), axis=0, keepdims=True)
        pick = n_f == first
        picked = jnp.where(pick, 1.0, picked)
        work = jnp.where(pick, REMOVED, work)
    colb_ref[...] = jnp.where((picked > 0.0) & valid, 0.0, NEG_INF)

    row_d = lax.broadcasted_iota(jnp.int32, (LANES, blk), 0)
    for p in range(N_SLABS):
        qs = qt[p * LANES:(p + 1) * LANES, :] * (SCALE * LOG2E)
        qa = jnp.where(row_d < HEAD_DIM, qs, 0.0)
        qb = jnp.where(row_d >= HEAD_DIM, qs, 0.0)
        qabt_ref[p, :LANES, :] = jnp.concatenate([qa, qb], axis=1).astype(BF16)

    slabs = [slice(p * LANES, (p + 1) * LANES) for p in range(N_SLABS)]

    def scores(n, p):
        off = pl.multiple_of(n * blk, blk)
        keys = jnp.concatenate([kb_ref[pl.ds(off, blk), slabs[p]], featk_ref[...]], axis=1)
        return _dot(keys, qabt_ref[p])

    sets = (seta_ref, setb_ref)

    def park_unit(n_first, count, p):
        for g in range(count):
            sets[p % 2][g] = scores(jnp.minimum(n_first + g, j), p)

    def weighted_values(blocks, p, e):
        vt = jnp.concatenate([vt_ref[n, slabs[p], :] for n in blocks], axis=1)
        ones = jnp.ones((SUM_ROWS, vt.shape[1]), BF16)
        pvs = [_dot(jnp.concatenate([vt[h * HEAD_DIM:(h + 1) * HEAD_DIM], ones], axis=0),
                    e[:, h * blk:(h + 1) * blk]) for h in range(HEADS_PER_SLAB)]
        return (jnp.concatenate([pv[:HEAD_DIM] for pv in pvs], axis=0),
                jnp.concatenate([pv[HEAD_DIM:HEAD_DIM + 1] for pv in pvs], axis=1))

    def by_head(row):
        return jnp.where(row_d < HEAD_DIM, row[:, :blk], row[:, blk:])

    def reduce_unit(n_first, count, p):
        cs = slice(p * qcols, (p + 1) * qcols)
        src = sets[p % 2]
        blocks = [n_first + g for g in range(count)]
        crows = [colb_ref[pl.ds(n, 1), cs] - slope_row(p) * ((j - n) * blk).astype(F32) for n in blocks]
        m_prev = m_ref[p]
        m_new = m_prev
        for g, crow in enumerate(crows):
            m_new = jnp.maximum(m_new, jnp.max(src[g], axis=0, keepdims=True) + crow)
        alpha = jnp.exp2(m_prev - m_new)
        e = jnp.concatenate([jnp.exp2(src[g] - (m_new - crow)).astype(BF16) for g, crow in enumerate(crows)],
                            axis=0)
        pv, esum = weighted_values(blocks, p, e)
        m_ref[p] = m_new
        l_ref[p] = alpha * l_ref[p] + esum
        acc_ref[p] = by_head(alpha) * acc_ref[p] + pv

    for p in range(N_SLABS):
        so_ref[p] = scores(j, p)
    park_unit(0, UPDATE_BLOCKS, 0)
    for p in range(N_SLABS):
        sb = so_ref[p] + causal_ref[...]
        m = jnp.max(sb, axis=0, keepdims=True)
        pv, esum = weighted_values([j], p, jnp.exp2(sb - m).astype(BF16))
        m_ref[p] = m
        l_ref[p] = esum
        acc_ref[p] = pv

    def sweep(n_first, count, n_after):
        for p in range(N_SLABS):
            if p + 1 < N_SLABS:
                park_unit(n_first, count, p + 1)
            elif n_after is not None:
                park_unit(n_after, UPDATE_BLOCKS, 0)
            reduce_unit(n_first, count, p)

    def trip(t, carry):
        sweep(UPDATE_BLOCKS * t, UPDATE_BLOCKS, UPDATE_BLOCKS * (t + 1))
        return carry

    lax.fori_loop(0, j // UPDATE_BLOCKS, trip, 0)

    nr = (j // UPDATE_BLOCKS) * UPDATE_BLOCKS
    for r in range(1, UPDATE_BLOCKS):
        @pl.when(j - nr == r)
        def _(r=r):
            sweep(nr, r, None)

    for p in range(N_SLABS):
        o2 = acc_ref[p] / by_head(l_ref[p])
        sq = o2 * o2
        ms_a = jnp.sum(sq[:HEAD_DIM], axis=0, keepdims=True) * (1.0 / HEAD_DIM)
        ms_b = jnp.sum(sq[HEAD_DIM:], axis=0, keepdims=True) * (1.0 / HEAD_DIM)
        inv = jnp.where(row_d < HEAD_DIM, lax.rsqrt(ms_a + EPS), lax.rsqrt(ms_b + EPS))
        ls = slice(p * LANES, (p + 1) * LANES)
        o_ref[:, ls] = ((o2 * inv).T * gain_ref[:, ls]).astype(BF16)


def _attn_prompt(q, kb, vt, means, gain, *, batch):
    n = q.shape[0]
    seq = n // batch
    nb = seq // MOBA_BLOCK
    group = LANES // N_HEADS
    assert nb <= group and seq % MOBA_BLOCK == 0
    means = means.reshape(batch, nb, D_ATT)
    qcols = HEADS_PER_SLAB * MOBA_BLOCK
    return pl.pallas_call(
        _attn_prompt_body,
        grid=(batch, nb),
        in_specs=[pl.BlockSpec((MOBA_BLOCK, D_ATT), lambda b, j: (b * nb + j, 0)),
                  pl.BlockSpec((seq, D_ATT), lambda b, j: (b, 0)),
                  pl.BlockSpec((nb, D_ATT, MOBA_BLOCK), lambda b, j: (b, 0, 0)),
                  pl.BlockSpec((1, nb, D_ATT), lambda b, j: (b, 0, 0)),
                  pl.BlockSpec(gain.shape, lambda b, j: (0, 0))],
        out_specs=pl.BlockSpec((MOBA_BLOCK, D_ATT), lambda b, j: (b * nb + j, 0)),
        out_shape=jax.ShapeDtypeStruct((n, D_ATT), BF16),
        scratch_shapes=[pltpu.VMEM((MOBA_BLOCK, qcols), F32),
                        pltpu.VMEM((MOBA_BLOCK, LANES), BF16),
                        pltpu.VMEM((N_SLABS, 2 * LANES, qcols), BF16),
                        pltpu.VMEM((group, N_HEADS * MOBA_BLOCK), F32),
                        pltpu.VMEM((N_SLABS, MOBA_BLOCK, qcols), F32),
                        pltpu.VMEM((UPDATE_BLOCKS, MOBA_BLOCK, qcols), F32),
                        pltpu.VMEM((UPDATE_BLOCKS, MOBA_BLOCK, qcols), F32),
                        pltpu.VMEM((N_SLABS, 1, qcols), F32),
                        pltpu.VMEM((N_SLABS, 1, qcols), F32),
                        pltpu.VMEM((N_SLABS, LANES, MOBA_BLOCK), F32)],
        compiler_params=pltpu.CompilerParams(dimension_semantics=("arbitrary", "arbitrary"),
                                             vmem_limit_bytes=VMEM_LIMIT),
        name="attn_prompt",
    )(q, kb, vt, means, gain)


def _attn_sample_t_body(pt_ref, q_ref, kn_ref, vn_ref, gain_ref, bd_ref, ckt_hbm, cvt_hbm, o_ref,
                        kbuf, vbuf, s_ref, acc_ref, ksem, vsem, *, past_len, page, page_base):
    b = pl.program_id(0)
    nbat = pl.num_programs(0)
    n_pages = past_len // page
    ppb = MOBA_BLOCK // page
    nb = past_len // MOBA_BLOCK
    t_new = q_ref.shape[1]
    rows = N_HEADS * t_new

    def page_copy(hbm, buf, sem, bb, pg):
        return pltpu.make_async_copy(hbm.at[page_base + pt_ref[bb, pg]], buf.at[pg], sem.at[pg])

    def start_all(hbm, buf, sem, bb):
        def body(i, c):
            for k in range(DMA_THREADS):
                page_copy(hbm, buf, sem, bb, i * DMA_THREADS + k).start(priority=k)
            return c
        lax.fori_loop(0, n_pages // DMA_THREADS, body, 0)

    def refill(hbm, buf, sem, pgs):
        @pl.when(b + 1 < nbat)
        def _():
            for k, pg in enumerate(pgs):
                page_copy(hbm, buf, sem, b + 1, pg).start(priority=k % DMA_THREADS)

    @pl.when(b == 0)
    def _():
        start_all(ckt_hbm, kbuf, ksem, b)
        start_all(cvt_hbm, vbuf, vsem, b)

    qt = jnp.concatenate([q_ref[0]] * N_HEADS, axis=0)
    rh = lax.broadcasted_iota(jnp.int32, qt.shape, 0) // t_new
    ch = lax.broadcasted_iota(jnp.int32, qt.shape, 1) // HEAD_DIM
    qs = jnp.where(rh == ch, qt, 0.0) * SCALE
    q_hi = qs.astype(BF16)
    q_lo = (qs - q_hi.astype(F32)).astype(BF16)
    qq = jnp.concatenate([q_hi, q_lo], axis=0)

    def k_tile(i, c):
        pgs = [i * K_TILE + k for k in range(K_TILE)]
        for pg in pgs:
            page_copy(ckt_hbm, kbuf, ksem, b, pg).wait()
        for pg in pgs:
            s2 = _dot(qq, kbuf[pg].reshape(D_ATT, page).astype(BF16))
            s_ref[pg] = s2[:rows] + s2[rows:]
        refill(ckt_hbm, kbuf, ksem, pgs)
        return c
    lax.fori_loop(0, n_pages // K_TILE, k_tile, 0)

    lane = lax.broadcasted_iota(jnp.int32, (rows, LANES), 1)
    gate = jnp.zeros((rows, LANES), F32)
    gcols = []
    for n in range(nb):
        tot = s_ref[n * ppb]
        for i in range(1, ppb):
            tot = tot + s_ref[n * ppb + i]
        g = jnp.sum(tot, axis=1, keepdims=True)
        gcols.append(g)
        gate = jnp.where(lane == n, g, gate)
    rank = jnp.zeros(gate.shape, jnp.int32)
    for m in range(nb):
        beats = (gcols[m] > gate) | ((gcols[m] == gate) & (lane > m))
        rank = rank + beats.astype(jnp.int32)
    colb = jnp.where(rank < MOBA_TOPK, 0.0, NEG_INF)

    r1 = lax.broadcasted_iota(jnp.int32, (rows, 1), 0)
    tq = r1 % t_new
    slope = jnp.zeros((rows, 1), F32)
    for h in range(N_HEADS):
        slope = jnp.where(r1 // t_new == h, _slope(h), slope)
    in_page = slope * (tq - lane).astype(F32)

    zpad = jnp.zeros((LANES - t_new, D_ATT), F32)
    kn = jnp.concatenate([kn_ref[0], zpad], axis=0).astype(BF16)
    vn = jnp.concatenate([vn_ref[0], zpad], axis=0).astype(BF16)
    s2 = _dot_nt(qq, kn)
    s_own = jnp.where(lane <= tq, s2[:rows] + s2[rows:] - in_page, NEG_INF)

    mrun = s_own
    for n in range(nb):
        mask_n = jnp.sum(jnp.where(lane == n, colb, 0.0), axis=1, keepdims=True)
        for i in range(ppb):
            pg = n * ppb + i
            sn = s_ref[pg] - in_page + (mask_n - slope * float(past_len - pg * page))
            s_ref[pg] = sn
            mrun = jnp.maximum(mrun, sn)
    m = jnp.max(mrun, axis=1, keepdims=True)

    e_own = jnp.exp(s_own - m)
    lrun = e_own
    for pg in range(n_pages):
        e = jnp.exp(s_ref[pg] - m)
        s_ref[pg] = e
        lrun = lrun + e
    l = jnp.sum(lrun, axis=1, keepdims=True)

    acc_ref[...] = jnp.zeros(acc_ref.shape, F32)
    zrows = jnp.zeros((LANES - rows, V_TILE * page), BF16)

    def v_tile(i, c):
        pgs = [i * V_TILE + k for k in range(V_TILE)]
        for pg in pgs:
            page_copy(cvt_hbm, vbuf, vsem, b, pg).wait()
        vt = jnp.concatenate([vbuf[pg].reshape(D_ATT, page) for pg in pgs], axis=1).astype(BF16)
        p = jnp.concatenate([s_ref[pg] for pg in pgs], axis=1).astype(BF16)
        acc_ref[...] += _dot_nt(vt, jnp.concatenate([p, zrows], axis=0))
        refill(cvt_hbm, vbuf, vsem, pgs)
        return c
    lax.fori_loop(0, n_pages // V_TILE, v_tile, 0)

    acc = acc_ref[...].T[:rows] + _dot(e_own.astype(BF16), vn)
    accn = acc / l
    ch8 = lax.broadcasted_iota(jnp.int32, (t_new, D_ATT), 1) // HEAD_DIM
    out = jnp.zeros((t_new, D_ATT), F32)
    for h in range(N_HEADS):
        out = jnp.where(ch8 == h, accn[h * t_new:(h + 1) * t_new, :], out)
    o_ref[0] = _group_rms(out, gain_ref[...], bd_ref).astype(BF16)


def _attn_sample_t(page_table, q, kn, vn, gain, bd, cache_kt, cache_vt, *, page_base, past_len):
    nbat, t_new, _ = q.shape
    page = cache_kt.shape[3]
    n_pages = past_len // page
    rows = N_HEADS * t_new
    assert past_len % MOBA_BLOCK == 0 and MOBA_BLOCK % page == 0 and page == LANES
    assert rows <= LANES and t_new % 8 == 0 and past_len // MOBA_BLOCK <= LANES
    assert n_pages % V_TILE == 0 and n_pages % K_TILE == 0
    body = functools.partial(_attn_sample_t_body, past_len=past_len, page=page, page_base=page_base)
    per_b = pl.BlockSpec((1, t_new, D_ATT), lambda b, pt: (b, 0, 0))
    grid_spec = pltpu.PrefetchScalarGridSpec(
        num_scalar_prefetch=1,
        grid=(nbat,),
        in_specs=[per_b, per_b, per_b,
                  pl.BlockSpec(gain.shape, lambda b, pt: (0, 0)),
                  pl.BlockSpec(bd.shape, lambda b, pt: (0, 0)),
                  pl.BlockSpec(memory_space=pl.ANY),
                  pl.BlockSpec(memory_space=pl.ANY)],
        out_specs=per_b,
        scratch_shapes=[pltpu.VMEM((n_pages, N_HEADS, HEAD_DIM, page), F32),
                        pltpu.VMEM((n_pages, N_HEADS, HEAD_DIM, page), F32),
                        pltpu.VMEM((n_pages, rows, page), F32),
                        pltpu.VMEM((D_ATT, LANES), F32),
                        pltpu.SemaphoreType.DMA((n_pages,)),
                        pltpu.SemaphoreType.DMA((n_pages,))],
    )
    return pl.pallas_call(
        body,
        grid_spec=grid_spec,
        out_shape=jax.ShapeDtypeStruct((nbat, t_new, D_ATT), BF16),
        compiler_params=pltpu.CompilerParams(dimension_semantics=("arbitrary",),
                                             vmem_limit_bytes=VMEM_LIMIT),
        name="attn_sample",
    )(page_table, q, kn, vn, gain, bd, cache_kt, cache_vt)


def kernel(x_prompt, x_sample, cache_k, cache_v, state_conv, page_table, ffn1_norm, ffn1_w_gu, ffn1_w_down,
           mix_norm, w_in, conv_w, conv_out_norm, attn_out_norm, w_out, ffn2_norm, ffn2_w_gu, ffn2_w_down,
           final_norm):
    bp, seq, d = x_prompt.shape
    bs, dseq, _ = x_sample.shape
    depth, n_pool, page = cache_k.shape[:3]
    dc = conv_w.shape[2]
    past_len = page_table.shape[1] * page

    ck = jnp.transpose(cache_k, (0, 1, 3, 4, 2)).reshape(depth * n_pool, N_HEADS, HEAD_DIM, page)
    cv = jnp.transpose(cache_v, (0, 1, 3, 4, 2)).reshape(depth * n_pool, N_HEADS, HEAD_DIM, page)
    gi = lax.broadcasted_iota(jnp.int32, (D_ATT, D_ATT), 0) // HEAD_DIM
    gj = lax.broadcasted_iota(jnp.int32, (D_ATT, D_ATT), 1) // HEAD_DIM
    bd = (gi == gj).astype(BF16)

    xp = x_prompt.reshape(bp * seq, d)
    xs = x_sample.reshape(bs * dseq, d)
    tm_p = PROMPT_ROW_TILE
    assert seq % tm_p == 0 and tm_p % MOBA_BLOCK == 0
    assert dc // N_CONV_GROUPS == HEAD_DIM and dc == D_ATT
    row = lambda a: a.reshape(1, -1)
    outs = [[] for _ in range(6)]
    for l in range(depth):
        g1, gm, g2 = row(ffn1_norm[l]), row(mix_norm[l]), row(ffn2_norm[l])
        gc, ga = row(conv_out_norm[l]), row(attn_out_norm[l])
        last = l == depth - 1
        gfin = row(final_norm) if last else None

        st = state_conv[l]
        zpad = jnp.zeros((bs, dseq - (CONV_W - 1), dc), F32)
        s2 = jnp.concatenate([st, zpad], axis=1).reshape(bs * dseq, dc)
        s1 = jnp.concatenate([st[:, 1:2], jnp.zeros((bs, dseq - 1, dc), F32)], axis=1).reshape(bs * dseq, dc)
        x1s, wg1, wu1, wd1 = _ffn_stream_call(xs, g1, ffn1_w_gu[l], ffn1_w_down[l], name="ffn1_sample")
        qs, ks, vs, ycs, us, win = _inproj_sample(x1s, gm, w_in[l], conv_w[l], gc, bd, s1, s2, seq=dseq)
        r3 = lambda a: a.reshape(bs, dseq, D_ATT)
        yas = _attn_sample_t(page_table, r3(qs), r3(ks), r3(vs), ga, bd, ck, cv,
                             page_base=l * n_pool, past_len=past_len)
        xs, wg2, wu2, wd2, woc, woa = _ffn_stream_call(
            x1s, g2, ffn2_w_gu[l], ffn2_w_down[l], mix=(ycs, yas.reshape(bs * dseq, D_ATT), w_out[l]),
            final=gfin, name="ffn2_sample")
        outs[3].append(ks.reshape(bs, dseq, N_HEADS, HEAD_DIM))
        outs[4].append(vs.reshape(bs, dseq, N_HEADS, HEAD_DIM))
        outs[5].append(us[:, dseq - (CONV_W - 1):, :])

        x1 = _ffn_call(xp, g1, wg1, wu1, wd1, tm=tm_p, name="ffn1_prompt")
        q, kt, vtf, kb, vt, yc, means, cnew = _inproj_prompt(x1, gm, win, conv_w[l], gc, bd, batch=bp, tm=tm_p)
        ya = _attn_prompt(q, kb, vt, means, ga, batch=bp)
        xp = _ffn_call(x1, g2, wg2, wu2, wd2, tm=tm_p, mix=(yc, ya, woc, woa), final=gfin, name="ffn2_prompt")
        tok_major = lambda a: a.reshape(bp, N_HEADS, HEAD_DIM, seq).transpose(0, 3, 1, 2)
        outs[0].append(tok_major(kt))
        outs[1].append(tok_major(vtf))
        outs[2].append(cnew)

    y_prompt = xp.reshape(bp, seq, d)
    y_sample = xs.reshape(bs, dseq, d)
    kp, vp, cp, ksn, vsn, csn = (jnp.stack(o) for o in outs)
    return (y_prompt, y_sample, kp, vp, cp, ksn, vsn, csn)
```

```python
import functools

import jax
import jax.numpy as jnp
from jax import lax
from jax.experimental import pallas as pl
from jax.experimental.pallas import tpu as pltpu

F32 = jnp.float32
BF16 = jnp.bfloat16

N_HEADS = 8
HEAD_DIM = 64
D_ATT = N_HEADS * HEAD_DIM
N_CONV_GROUPS = 8
CONV_W = 3
MOBA_BLOCK = 256
MOBA_TOPK = 3
EPS = 1e-5
NEG_INF = -1e30
FINITE_MAX = 3.0e38
REMOVED = -3e38
SCALE = HEAD_DIM ** -0.5
LOG2E = 1.4426950408889634

LANES = 128
MXU_WIDTH = 256
HEADS_PER_SLAB = LANES // HEAD_DIM
N_SLABS = D_ATT // LANES
VMEM_LIMIT = 56 * 1024 * 1024
PROMPT_ROW_TILE = 512
UPDATE_BLOCKS = 2
SUM_ROWS = 16
DMA_THREADS = 2
K_TILE = 16
V_TILE = 16

NT_DIMS = (((1,), (1,)), ((), ()))


def _slope(h):
    return 2.0 ** (-(8.0 / N_HEADS) * (h + 1))


def _dot(a, b):
    return jnp.dot(a, b, preferred_element_type=F32)


def _dot_nt(a, b):
    return lax.dot_general(a, b, NT_DIMS, preferred_element_type=F32)


def _rms(x, g):
    ms = jnp.mean(x * x, axis=-1, keepdims=True)
    return x * lax.rsqrt(ms + EPS) * g


def _group_sumsq(y, bd_ref):
    y2 = y * y
    hi = y2.astype(BF16)
    lo = (y2 - hi.astype(F32)).astype(BF16)
    bd = bd_ref[...]
    return _dot(hi, bd) + _dot(lo, bd)


def _group_rms(y, g, bd_ref):
    ms = _group_sumsq(y, bd_ref) * (1.0 / HEAD_DIM)
    return y * lax.rsqrt(ms + EPS) * g


def _const_spec(shape):
    nd = len(shape)
    return pl.BlockSpec(shape, lambda *_: (0,) * nd, pipeline_mode=pl.Buffered(1))


def _ffn_body(*refs, mix, final, bounds):
    it = iter(refs)
    x_ref = next(it)
    if mix:
        yc_ref, ya_ref, woc_ref, woa_ref = next(it), next(it), next(it), next(it)
    g_ref, wg_ref, wu_ref, wd_ref = next(it), next(it), next(it), next(it)
    gf_ref = next(it) if final else None
    o_ref = next(it)

    x = x_ref[...]
    if mix:
        x = x + _dot(yc_ref[...], woc_ref[...]) + _dot(ya_ref[...], woa_ref[...])
    h = _rms(x, g_ref[...]).astype(BF16)
    acc = jnp.zeros(x.shape, F32)
    for lo, hi in zip(bounds[:-1], bounds[1:]):
        gate = _dot(h, wg_ref[:, lo:hi])
        up = _dot(h, wu_ref[:, lo:hi])
        act = (gate * jax.nn.sigmoid(gate) * up).astype(BF16)
        acc = acc + _dot(act, wd_ref[lo:hi, :])
    x = x + 0.5 * acc
    if final:
        x = _rms(x, gf_ref[...])
    o_ref[...] = x


def _ffn_call(x, g, wg, wu, wd, *, tm, mix=None, final=None, name):
    n, d = x.shape
    d_ff = wd.shape[0]
    row = lambda w: pl.BlockSpec((tm, w), lambda i: (i, 0))
    ins, specs = [x], [row(d)]
    if mix is not None:
        yc, ya, woc, woa = mix
        ins += [yc, ya, woc, woa]
        specs += [row(yc.shape[1]), row(ya.shape[1]), _const_spec(woc.shape), _const_spec(woa.shape)]
    ins += [g, wg, wu, wd]
    specs += [_const_spec(g.shape), _const_spec(wg.shape), _const_spec(wu.shape), _const_spec(wd.shape)]
    if final is not None:
        ins.append(final)
        specs.append(_const_spec(final.shape))
    assert d_ff % MXU_WIDTH == 0
    tiles = d_ff // MXU_WIDTH
    bounds = (0, (tiles + 1) // 2 * MXU_WIDTH, d_ff)
    body = functools.partial(_ffn_body, mix=mix is not None, final=final is not None, bounds=bounds)
    return pl.pallas_call(
        body,
        grid=(n // tm,),
        in_specs=specs,
        out_specs=row(d),
        out_shape=jax.ShapeDtypeStruct((n, d), F32),
        compiler_params=pltpu.CompilerParams(dimension_semantics=("arbitrary",),
                                             vmem_limit_bytes=VMEM_LIMIT),
        name=name,
    )(*ins)


def _ffn_stream_body(*refs, mix, final):
    it = iter(refs)
    x_ref = next(it)
    if mix:
        yc_ref, ya_ref, woc_ref, woa_ref = next(it), next(it), next(it), next(it)
    g_ref, wg_ref, wu_ref, wd_ref = next(it), next(it), next(it), next(it)
    gf_ref = next(it) if final else None
    o_ref, wgb_ref, wub_ref, wdb_ref = next(it), next(it), next(it), next(it)
    if mix:
        wocb_ref, woab_ref = next(it), next(it)
    x_scr, h_scr, acc_scr = next(it), next(it), next(it)
    c = pl.program_id(0)

    @pl.when(c == 0)
    def _():
        x = x_ref[...]
        if mix:
            woc, woa = woc_ref[...].astype(BF16), woa_ref[...].astype(BF16)
            wocb_ref[...] = woc
            woab_ref[...] = woa
            x = x + _dot(yc_ref[...], woc) + _dot(ya_ref[...], woa)
        x_scr[...] = x
        h_scr[...] = _rms(x, g_ref[...]).astype(BF16)
        acc_scr[...] = jnp.zeros(acc_scr.shape, F32)

    wg, wu, wd = wg_ref[...].astype(BF16), wu_ref[...].astype(BF16), wd_ref[...].astype(BF16)
    wgb_ref[...] = wg
    wub_ref[...] = wu
    wdb_ref[...] = wd
    h = h_scr[...]
    gate = _dot(h, wg)
    act = (gate * jax.nn.sigmoid(gate) * _dot(h, wu)).astype(BF16)
    acc_scr[...] += _dot(act, wd)

    @pl.when(c == pl.num_programs(0) - 1)
    def _():
        x = x_scr[...] + 0.5 * acc_scr[...]
        if final:
            x = _rms(x, gf_ref[...])
        o_ref[...] = x


def _ffn_stream_call(x, g, w_gu, w_down, *, mix=None, final=None, name):
    n, d = x.shape
    d_ff = w_down.shape[0]
    tw = MXU_WIDTH
    assert d_ff % tw == 0 and w_gu.shape == (d, 2 * d_ff)
    nt = d_ff // tw
    full = lambda shape: pl.BlockSpec(shape, lambda c: (0,) * len(shape))
    ins, specs = [x], [full((n, d))]
    outs = [jax.ShapeDtypeStruct((n, d), F32), jax.ShapeDtypeStruct((d, d_ff), BF16),
            jax.ShapeDtypeStruct((d, d_ff), BF16), jax.ShapeDtypeStruct((d_ff, d), BF16)]
    out_specs = [full((n, d)), pl.BlockSpec((d, tw), lambda c: (0, c)), pl.BlockSpec((d, tw), lambda c: (0, c)),
                 pl.BlockSpec((tw, d), lambda c: (c, 0))]
    if mix is not None:
        yc, ya, w_out = mix
        dm = yc.shape[1]
        assert w_out.shape == (dm + ya.shape[1], d) and ya.shape[1] == dm
        ins += [yc, ya, w_out, w_out]
        specs += [full(yc.shape), full(ya.shape), pl.BlockSpec((dm, d), lambda c: (0, 0)),
                  pl.BlockSpec((dm, d), lambda c: (1, 0))]
        outs += [jax.ShapeDtypeStruct((dm, d), BF16)] * 2
        out_specs += [full((dm, d))] * 2
    ins += [g, w_gu, w_gu, w_down]
    specs += [full(g.shape), pl.BlockSpec((d, tw), lambda c: (0, c)), pl.BlockSpec((d, tw), lambda c: (0, nt + c)),
              pl.BlockSpec((tw, d), lambda c: (c, 0))]
    if final is not None:
        ins.append(final)
        specs.append(full(final.shape))
    body = functools.partial(_ffn_stream_body, mix=mix is not None, final=final is not None)
    return pl.pallas_call(
        body,
        grid=(nt,),
        in_specs=specs,
        out_specs=out_specs,
        out_shape=outs,
        scratch_shapes=[pltpu.VMEM((n, d), F32), pltpu.VMEM((n, d), BF16), pltpu.VMEM((n, d), F32)],
        compiler_params=pltpu.CompilerParams(dimension_semantics=("arbitrary",),
                                             vmem_limit_bytes=VMEM_LIMIT),
        name=name,
    )(*ins)


def _inproj_body(*refs, tm, tiles_per_seq, sample):
    it = iter(refs)
    x_ref, g_ref, win_ref, cw_ref, cn_ref, bd_ref = (next(it) for _ in range(6))
    if sample:
        s1_ref, s2_ref = next(it), next(it)
        q_ref, k_ref, v_ref, yc_ref, u_ref, winb_ref = (next(it) for _ in range(6))
    else:
        q_ref, kt_ref, vtf_ref, kb_ref, vt_ref, yc_ref, mean_ref, cnew_ref = (next(it) for _ in range(8))
    ubuf = next(it)

    dc = yc_ref.shape[1]
    if sample:
        ubuf[0:8, :] = jnp.zeros((8, dc), F32)
    else:
        first = (pl.program_id(0) % tiles_per_seq) == 0

        @pl.when(first)
        def _():
            ubuf[0:8, :] = jnp.zeros((8, dc), F32)

        @pl.when(jnp.logical_not(first))
        def _():
            ubuf[0:8, :] = ubuf[tm:tm + 8, :]

    h = _rms(x_ref[...], g_ref[...]).astype(BF16)
    if sample:
        def piece(c, w):
            wp = win_ref[:, c:c + w].astype(BF16)
            winb_ref[:, c:c + w] = wp
            return _dot(h, wp)
    else:
        piece = lambda c, w: _dot(h, win_ref[:, c:c + w])
    hc = piece(0, dc)
    cg = piece(2 * dc, dc)
    bg = piece(dc, dc)
    k = piece(3 * dc + D_ATT, D_ATT)
    v = piece(3 * dc + 2 * D_ATT, D_ATT)
    if sample:
        k_ref[...] = k
        v_ref[...] = v

    u = cg * hc
    ubuf[8:tm + 8, :] = u
    um1 = ubuf[7:tm + 7, :]
    um2 = ubuf[6:tm + 6, :]
    if sample:
        t = lax.broadcasted_iota(jnp.int32, (tm, dc), 0) % u_ref.shape[1]
        um1 = jnp.where(t >= 1, um1, s1_ref[...])
        um2 = jnp.where(t >= 2, um2, s2_ref[...])
    cw = cw_ref[...]
    conv = um2 * cw[0:1, :] + um1 * cw[1:2, :] + u * cw[2:3, :]
    yc_ref[...] = _group_rms(bg * conv, cn_ref[...], bd_ref).astype(BF16)
    q_ref[...] = piece(3 * dc, D_ATT)

    if sample:
        u_ref[...] = u.reshape(u_ref.shape)
    else:
        vt = v.T
        kt_ref[0] = k.T
        vtf_ref[0] = vt
        kb_ref[...] = k.astype(BF16)
        nblk = tm // MOBA_BLOCK
        for i in range(nblk):
            vt_ref[i] = vt[:, i * MOBA_BLOCK:(i + 1) * MOBA_BLOCK].astype(BF16)
        mean_ref[0] = jnp.sum(k.reshape(nblk, MOBA_BLOCK, D_ATT), axis=1) * (1.0 / MOBA_BLOCK)
        cnew_ref[0] = ubuf[tm + 6:tm + 8, :]


def _inproj_prompt(x, g, w_in, conv_w, conv_norm, bd, *, batch, tm):
    n, d = x.shape
    dc = conv_w.shape[1]
    seq = n // batch
    tps = seq // tm
    nblk = tm // MOBA_BLOCK
    row = lambda w: pl.BlockSpec((tm, w), lambda i: (i, 0))
    tok_minor = pl.BlockSpec((1, D_ATT, tm), lambda i: (i // tps, 0, i % tps))
    f = lambda w, dt: jax.ShapeDtypeStruct((n, w), dt)
    body = functools.partial(_inproj_body, tm=tm, tiles_per_seq=tps, sample=False)
    return pl.pallas_call(
        body,
        grid=(n // tm,),
        in_specs=[row(d), _const_spec(g.shape), _const_spec(w_in.shape), _const_spec(conv_w.shape),
                  _const_spec(conv_norm.shape), _const_spec(bd.shape)],
        out_specs=[row(D_ATT), tok_minor, tok_minor, row(D_ATT),
                   pl.BlockSpec((nblk, D_ATT, MOBA_BLOCK), lambda i: (i, 0, 0)), row(dc),
                   pl.BlockSpec((1, nblk, D_ATT), lambda i: (i, 0, 0)),
                   pl.BlockSpec((1, CONV_W - 1, dc), lambda i: (i // tps, 0, 0))],
        out_shape=[f(D_ATT, F32), jax.ShapeDtypeStruct((batch, D_ATT, seq), F32),
                   jax.ShapeDtypeStruct((batch, D_ATT, seq), F32), f(D_ATT, BF16),
                   jax.ShapeDtypeStruct((n // MOBA_BLOCK, D_ATT, MOBA_BLOCK), BF16), f(dc, BF16),
                   jax.ShapeDtypeStruct((n // tm, nblk, D_ATT), F32),
                   jax.ShapeDtypeStruct((batch, CONV_W - 1, dc), F32)],
        scratch_shapes=[pltpu.VMEM((tm + 8, dc), F32)],
        compiler_params=pltpu.CompilerParams(dimension_semantics=("arbitrary",),
                                             vmem_limit_bytes=VMEM_LIMIT),
        name="inproj_prompt",
    )(x, g, w_in, conv_w, conv_norm, bd)


def _inproj_sample(x, g, w_in, conv_w, conv_norm, bd, s1, s2, *, seq):
    n, d = x.shape
    dc = conv_w.shape[1]
    tm = n
    full = lambda shape: pl.BlockSpec(shape, lambda i: (0,) * len(shape))
    f = lambda w, dt: jax.ShapeDtypeStruct((n, w), dt)
    body = functools.partial(_inproj_body, tm=tm, tiles_per_seq=1, sample=True)
    return pl.pallas_call(
        body,
        grid=(1,),
        in_specs=[full((tm, d)), full(g.shape), full(w_in.shape), full(conv_w.shape),
                  full(conv_norm.shape), full(bd.shape), full((tm, dc)), full((tm, dc))],
        out_specs=[full((tm, D_ATT)), full((tm, D_ATT)), full((tm, D_ATT)), full((tm, dc)),
                   full((n // seq, seq, dc)), full(w_in.shape)],
        out_shape=[f(D_ATT, F32), f(D_ATT, F32), f(D_ATT, F32), f(dc, BF16),
                   jax.ShapeDtypeStruct((n // seq, seq, dc), F32), jax.ShapeDtypeStruct(w_in.shape, BF16)],
        scratch_shapes=[pltpu.VMEM((tm + 8, dc), F32)],
        compiler_params=pltpu.CompilerParams(dimension_semantics=("arbitrary",),
                                             vmem_limit_bytes=VMEM_LIMIT),
        name="inproj_sample",
    )(x, g, w_in, conv_w, conv_norm, bd, s1, s2)


def _split3(x):
    hi = x.astype(BF16).astype(F32)
    mid = (x - hi).astype(BF16).astype(F32)
    lo = (x - hi - mid).astype(BF16).astype(F32)
    return hi, mid, lo


def _attn_prompt_body(q_ref, kb_ref, vt_ref, mean_ref, gain_ref, o_ref,
                      causal_ref, featk_ref, qabt_ref, colb_ref, so_ref, seta_ref, setb_ref,
                      m_ref, l_ref, acc_ref):
    blk = MOBA_BLOCK
    b = pl.program_id(0)
    j = pl.program_id(1)
    nb = mean_ref.shape[1]
    group = LANES // N_HEADS
    qcols = HEADS_PER_SLAB * blk
    lane_q = lax.broadcasted_iota(jnp.int32, (1, qcols), 1)

    def slope_row(p):
        return jnp.where(lane_q < blk, LOG2E * _slope(HEADS_PER_SLAB * p), LOG2E * _slope(HEADS_PER_SLAB * p + 1))

    @pl.when((b == 0) & (j == 0))
    def _init_tables():
        kk = lax.broadcasted_iota(jnp.int32, (blk, qcols), 0)
        qq = lax.broadcasted_iota(jnp.int32, (blk, qcols), 1)
        causal_ref[...] = jnp.where((qq % blk) >= kk, 0.0, NEG_INF)
        ki = lax.broadcasted_iota(jnp.int32, (blk, LANES), 0).astype(F32)
        kl = lax.broadcasted_iota(jnp.int32, (blk, LANES), 1)
        featk_ref[...] = jnp.where(kl < 3, ki, jnp.where(kl < 6, 1.0, 0.0)).astype(BF16)
        fr = lax.broadcasted_iota(jnp.int32, (LANES, qcols), 0)
        for p in range(N_SLABS):
            a = slope_row(p)
            terms = _split3(a) + _split3(-a * (lane_q % blk).astype(F32))
            feat = jnp.zeros((LANES, qcols), F32)
            for r, t in enumerate(terms):
                feat = jnp.where(fr == r, t, feat)
            qabt_ref[p, LANES:, :] = feat.astype(BF16)

    qt = q_ref[...].T
    means = mean_ref[0]
    if nb < group:
        means = jnp.concatenate([means, jnp.zeros((group - nb, D_ATT), F32)], axis=0)
    mt = jnp.concatenate([means] * N_HEADS, axis=0)
    rh = lax.broadcasted_iota(jnp.int32, mt.shape, 0) // group
    ch = lax.broadcasted_iota(jnp.int32, mt.shape, 1) // HEAD_DIM
    mbd = jnp.where(rh == ch, mt, 0.0)
    gate_t = jnp.dot(mbd, qt, precision=lax.Precision.HIGHEST, preferred_element_type=F32)
    gate = jnp.concatenate([gate_t[h * group:(h + 1) * group, :] for h in range(N_HEADS)], axis=1)

    n_idx = lax.broadcasted_iota(jnp.int32, gate.shape, 0)
    n_f = n_idx.astype(F32)
    valid = n_idx < j
    work = jnp.where(valid, gate, NEG_INF)
    picked = jnp.zeros(gate.shape, F32)
    for _ in range(MOBA_TOPK):
        top = jnp.max(work, axis=0, keepdims=True)
        first = jnp.min(jnp.where(work == top, n_f, float(group)), axis=0, keepdims=True)
        pick = n_f == first
        picked = jnp.where(pick, 1.0, picked)
        work = jnp.where(pick, REMOVED, work)
    colb_ref[...] = jnp.where((picked > 0.0) & valid, 0.0, NEG_INF)

    row_d = lax.broadcasted_iota(jnp.int32, (LANES, blk), 0)
    for p in range(N_SLABS):
        qs = qt[p * LANES:(p + 1) * LANES, :] * (SCALE * LOG2E)
        qa = jnp.where(row_d < HEAD_DIM, qs, 0.0)
        qb = jnp.where(row_d >= HEAD_DIM, qs, 0.0)
        qabt_ref[p, :LANES, :] = jnp.concatenate([qa, qb], axis=1).astype(BF16)

    slabs = [slice(p * LANES, (p + 1) * LANES) for p in range(N_SLABS)]

    def scores(n, p):
        off = pl.multiple_of(n * blk, blk)
        keys = jnp.concatenate([kb_ref[pl.ds(off, blk), slabs[p]], featk_ref[...]], axis=1)
        return _dot(keys, qabt_ref[p])

    sets = (seta_ref, setb_ref)

    def park_unit(n_first, count, p):
        for g in range(count):
            sets[p % 2][g] = scores(jnp.minimum(n_first + g, j), p)

    def weighted_values(blocks, p, e):
        vt = jnp.concatenate([vt_ref[n, slabs[p], :] for n in blocks], axis=1)
        ones = jnp.ones((SUM_ROWS, vt.shape[1]), BF16)
        pvs = [_dot(jnp.concatenate([vt[h * HEAD_DIM:(h + 1) * HEAD_DIM], ones], axis=0),
                    e[:, h * blk:(h + 1) * blk]) for h in range(HEADS_PER_SLAB)]
        return (jnp.concatenate([pv[:HEAD_DIM] for pv in pvs], axis=0),
                jnp.concatenate([pv[HEAD_DIM:HEAD_DIM + 1] for pv in pvs], axis=1))

    def by_head(row):
        return jnp.where(row_d < HEAD_DIM, row[:, :blk], row[:, blk:])

    def reduce_unit(n_first, count, p, track_max):
        cs = slice(p * qcols, (p + 1) * qcols)
        src = sets[p % 2]
        blocks = [n_first + g for g in range(count)]
        crows = [colb_ref[pl.ds(n, 1), cs] - slope_row(p) * ((j - n) * blk).astype(F32) for n in blocks]
        m_prev = m_ref[p]
        if track_max:
            m_new = m_prev
            for g, crow in enumerate(crows):
                m_new = jnp.maximum(m_new, jnp.max(src[g], axis=0, keepdims=True) + crow)
            alpha = jnp.exp2(m_prev - m_new)
            m_ref[p] = m_new
        else:
            m_new = m_prev
        e = jnp.concatenate([jnp.exp2(src[g] - (m_new - crow)).astype(BF16) for g, crow in enumerate(crows)],
                            axis=0)
        pv, esum = weighted_values(blocks, p, e)
        if track_max:
            l_ref[p] = alpha * l_ref[p] + esum
            acc_ref[p] = by_head(alpha) * acc_ref[p] + pv
        else:
            l_ref[p] = l_ref[p] + esum
            acc_ref[p] = acc_ref[p] + pv

    def sweep(n_first, count, n_after, track_max):
        for p in range(N_SLABS):
            if p + 1 < N_SLABS:
                park_unit(n_first, count, p + 1)
            elif n_after is not None:
                park_unit(n_after, UPDATE_BLOCKS, 0)
            reduce_unit(n_first, count, p, track_max)

    def attend(track_max):
        for p in range(N_SLABS):
            so_ref[p] = scores(j, p)
        park_unit(0, UPDATE_BLOCKS, 0)
        for p in range(N_SLABS):
            sb = so_ref[p] + causal_ref[...]
            m = jnp.max(sb, axis=0, keepdims=True)
            pv, esum = weighted_values([j], p, jnp.exp2(sb - m).astype(BF16))
            m_ref[p] = m
            l_ref[p] = esum
            acc_ref[p] = pv

        def trip(t, carry):
            sweep(UPDATE_BLOCKS * t, UPDATE_BLOCKS, UPDATE_BLOCKS * (t + 1), track_max)
            return carry

        lax.fori_loop(0, j // UPDATE_BLOCKS, trip, 0)

        nr = (j // UPDATE_BLOCKS) * UPDATE_BLOCKS
        for r in range(1, UPDATE_BLOCKS):
            @pl.when(j - nr == r)
            def _(r=r):
                sweep(nr, r, None, track_max)

    attend(track_max=False)
    l_top = l_ref[0]
    a_top = jnp.abs(acc_ref[0])
    for p in range(1, N_SLABS):
        l_top = jnp.maximum(l_top, l_ref[p])
        a_top = jnp.maximum(a_top, jnp.abs(acc_ref[p]))
    finite = (jnp.max(l_top) < FINITE_MAX) & (jnp.max(a_top) < FINITE_MAX)

    @pl.when(jnp.logical_not(finite))
    def _():
        attend(track_max=True)

    for p in range(N_SLABS):
        o2 = acc_ref[p] / by_head(l_ref[p])
        sq = o2 * o2
        ms_a = jnp.sum(sq[:HEAD_DIM], axis=0, keepdims=True) * (1.0 / HEAD_DIM)
        ms_b = jnp.sum(sq[HEAD_DIM:], axis=0, keepdims=True) * (1.0 / HEAD_DIM)
        inv = jnp.where(row_d < HEAD_DIM, lax.rsqrt(ms_a + EPS), lax.rsqrt(ms_b + EPS))
        ls = slice(p * LANES, (p + 1) * LANES)
        o_ref[:, ls] = ((o2 * inv).T * gain_ref[:, ls]).astype(BF16)


def _attn_prompt(q, kb, vt, means, gain, *, batch):
    n = q.shape[0]
    seq = n // batch
    nb = seq // MOBA_BLOCK
    group = LANES // N_HEADS
    assert nb <= group and seq % MOBA_BLOCK == 0
    means = means.reshape(batch, nb, D_ATT)
    qcols = HEADS_PER_SLAB * MOBA_BLOCK
    return pl.pallas_call(
        _attn_prompt_body,
        grid=(batch, nb),
        in_specs=[pl.BlockSpec((MOBA_BLOCK, D_ATT), lambda b, j: (b * nb + j, 0)),
                  pl.BlockSpec((seq, D_ATT), lambda b, j: (b, 0)),
                  pl.BlockSpec((nb, D_ATT, MOBA_BLOCK), lambda b, j: (b, 0, 0)),
                  pl.BlockSpec((1, nb, D_ATT), lambda b, j: (b, 0, 0)),
                  pl.BlockSpec(gain.shape, lambda b, j: (0, 0))],
        out_specs=pl.BlockSpec((MOBA_BLOCK, D_ATT), lambda b, j: (b * nb + j, 0)),
        out_shape=jax.ShapeDtypeStruct((n, D_ATT), BF16),
        scratch_shapes=[pltpu.VMEM((MOBA_BLOCK, qcols), F32),
                        pltpu.VMEM((MOBA_BLOCK, LANES), BF16),
                        pltpu.VMEM((N_SLABS, 2 * LANES, qcols), BF16),
                        pltpu.VMEM((group, N_HEADS * MOBA_BLOCK), F32),
                        pltpu.VMEM((N_SLABS, MOBA_BLOCK, qcols), F32),
                        pltpu.VMEM((UPDATE_BLOCKS, MOBA_BLOCK, qcols), F32),
                        pltpu.VMEM((UPDATE_BLOCKS, MOBA_BLOCK, qcols), F32),
                        pltpu.VMEM((N_SLABS, 1, qcols), F32),
                        pltpu.VMEM((N_SLABS, 1, qcols), F32),
                        pltpu.VMEM((N_SLABS, LANES, MOBA_BLOCK), F32)],
        compiler_params=pltpu.CompilerParams(dimension_semantics=("arbitrary", "arbitrary"),
                                             vmem_limit_bytes=VMEM_LIMIT),
        name="attn_prompt",
    )(q, kb, vt, means, gain)


def _attn_sample_t_body(pt_ref, q_ref, kn_ref, vn_ref, gain_ref, bd_ref, ckt_hbm, cvt_hbm, o_ref,
                        kbuf, vbuf, s_ref, acc_ref, ksem, vsem, *, past_len, page, page_base):
    b = pl.program_id(0)
    nbat = pl.num_programs(0)
    n_pages = past_len // page
    ppb = MOBA_BLOCK // page
    nb = past_len // MOBA_BLOCK
    t_new = q_ref.shape[1]
    rows = N_HEADS * t_new

    def page_copy(hbm, buf, sem, bb, pg):
        return pltpu.make_async_copy(hbm.at[page_base + pt_ref[bb, pg]], buf.at[pg], sem.at[pg])

    def start_all(hbm, buf, sem, bb):
        def body(i, c):
            for k in range(DMA_THREADS):
                page_copy(hbm, buf, sem, bb, i * DMA_THREADS + k).start(priority=k)
            return c
        lax.fori_loop(0, n_pages // DMA_THREADS, body, 0)

    def refill(hbm, buf, sem, pgs):
        @pl.when(b + 1 < nbat)
        def _():
            for k, pg in enumerate(pgs):
                page_copy(hbm, buf, sem, b + 1, pg).start(priority=k % DMA_THREADS)

    @pl.when(b == 0)
    def _():
        start_all(ckt_hbm, kbuf, ksem, b)
        start_all(cvt_hbm, vbuf, vsem, b)

    qt = jnp.concatenate([q_ref[0]] * N_HEADS, axis=0)
    rh = lax.broadcasted_iota(jnp.int32, qt.shape, 0) // t_new
    ch = lax.broadcasted_iota(jnp.int32, qt.shape, 1) // HEAD_DIM
    qs = jnp.where(rh == ch, qt, 0.0) * SCALE
    q_hi = qs.astype(BF16)
    q_lo = (qs - q_hi.astype(F32)).astype(BF16)
    qq = jnp.concatenate([q_hi, q_lo], axis=0)

    def k_tile(i, c):
        pgs = [i * K_TILE + k for k in range(K_TILE)]
        for pg in pgs:
            page_copy(ckt_hbm, kbuf, ksem, b, pg).wait()
        for pg in pgs:
            s2 = _dot(qq, kbuf[pg].reshape(D_ATT, page).astype(BF16))
            s_ref[pg] = s2[:rows] + s2[rows:]
        refill(ckt_hbm, kbuf, ksem, pgs)
        return c
    lax.fori_loop(0, n_pages // K_TILE, k_tile, 0)

    lane = lax.broadcasted_iota(jnp.int32, (rows, LANES), 1)
    gate = jnp.zeros((rows, LANES), F32)
    gcols = []
    for n in range(nb):
        tot = s_ref[n * ppb]
        for i in range(1, ppb):
            tot = tot + s_ref[n * ppb + i]
        g = jnp.sum(tot, axis=1, keepdims=True)
        gcols.append(g)
        gate = jnp.where(lane == n, g, gate)
    rank = jnp.zeros(gate.shape, jnp.int32)
    for m in range(nb):
        beats = (gcols[m] > gate) | ((gcols[m] == gate) & (lane > m))
        rank = rank + beats.astype(jnp.int32)
    colb = jnp.where(rank < MOBA_TOPK, 0.0, NEG_INF)

    r1 = lax.broadcasted_iota(jnp.int32, (rows, 1), 0)
    tq = r1 % t_new
    slope = jnp.zeros((rows, 1), F32)
    for h in range(N_HEADS):
        slope = jnp.where(r1 // t_new == h, _slope(h), slope)
    in_page = slope * (tq - lane).astype(F32)

    zpad = jnp.zeros((LANES - t_new, D_ATT), F32)
    kn = jnp.concatenate([kn_ref[0], zpad], axis=0).astype(BF16)
    vn = jnp.concatenate([vn_ref[0], zpad], axis=0).astype(BF16)
    s2 = _dot_nt(qq, kn)
    s_own = jnp.where(lane <= tq, s2[:rows] + s2[rows:] - in_page, NEG_INF)

    mrun = s_own
    for n in range(nb):
        mask_n = jnp.sum(jnp.where(lane == n, colb, 0.0), axis=1, keepdims=True)
        for i in range(ppb):
            pg = n * ppb + i
            sn = s_ref[pg] - in_page + (mask_n - slope * float(past_len - pg * page))
            s_ref[pg] = sn
            mrun = jnp.maximum(mrun, sn)
    m = jnp.max(mrun, axis=1, keepdims=True)

    e_own = jnp.exp(s_own - m)
    lrun = e_own
    for pg in range(n_pages):
        e = jnp.exp(s_ref[pg] - m)
        s_ref[pg] = e
        lrun = lrun + e
    l = jnp.sum(lrun, axis=1, keepdims=True)

    acc_ref[...] = jnp.zeros(acc_ref.shape, F32)
    zrows = jnp.zeros((LANES - rows, V_TILE * page), BF16)

    def v_tile(i, c):
        pgs = [i * V_TILE + k for k in range(V_TILE)]
        for pg in pgs:
            page_copy(cvt_hbm, vbuf, vsem, b, pg).wait()
        vt = jnp.concatenate([vbuf[pg].reshape(D_ATT, page) for pg in pgs], axis=1).astype(BF16)
        p = jnp.concatenate([s_ref[pg] for pg in pgs], axis=1).astype(BF16)
        acc_ref[...] += _dot_nt(vt, jnp.concatenate([p, zrows], axis=0))
        refill(cvt_hbm, vbuf, vsem, pgs)
        return c
    lax.fori_loop(0, n_pages // V_TILE, v_tile, 0)

    acc = acc_ref[...].T[:rows] + _dot(e_own.astype(BF16), vn)
    accn = acc / l
    ch8 = lax.broadcasted_iota(jnp.int32, (t_new, D_ATT), 1) // HEAD_DIM
    out = jnp.zeros((t_new, D_ATT), F32)
    for h in range(N_HEADS):
        out = jnp.where(ch8 == h, accn[h * t_new:(h + 1) * t_new, :], out)
    o_ref[0] = _group_rms(out, gain_ref[...], bd_ref).astype(BF16)


def _attn_sample_t(page_table, q, kn, vn, gain, bd, cache_kt, cache_vt, *, page_base, past_len):
    nbat, t_new, _ = q.shape
    page = cache_kt.shape[3]
    n_pages = past_len // page
    rows = N_HEADS * t_new
    assert past_len % MOBA_BLOCK == 0 and MOBA_BLOCK % page == 0 and page == LANES
    assert rows <= LANES and t_new % 8 == 0 and past_len // MOBA_BLOCK <= LANES
    assert n_pages % V_TILE == 0 and n_pages % K_TILE == 0
    body = functools.partial(_attn_sample_t_body, past_len=past_len, page=page, page_base=page_base)
    per_b = pl.BlockSpec((1, t_new, D_ATT), lambda b, pt: (b, 0, 0))
    grid_spec = pltpu.PrefetchScalarGridSpec(
        num_scalar_prefetch=1,
        grid=(nbat,),
        in_specs=[per_b, per_b, per_b,
                  pl.BlockSpec(gain.shape, lambda b, pt: (0, 0)),
                  pl.BlockSpec(bd.shape, lambda b, pt: (0, 0)),
                  pl.BlockSpec(memory_space=pl.ANY),
                  pl.BlockSpec(memory_space=pl.ANY)],
        out_specs=per_b,
        scratch_shapes=[pltpu.VMEM((n_pages, N_HEADS, HEAD_DIM, page), F32),
                        pltpu.VMEM((n_pages, N_HEADS, HEAD_DIM, page), F32),
                        pltpu.VMEM((n_pages, rows, page), F32),
                        pltpu.VMEM((D_ATT, LANES), F32),
                        pltpu.SemaphoreType.DMA((n_pages,)),
                        pltpu.SemaphoreType.DMA((n_pages,))],
    )
    return pl.pallas_call(
        body,
        grid_spec=grid_spec,
        out_shape=jax.ShapeDtypeStruct((nbat, t_new, D_ATT), BF16),
        compiler_params=pltpu.CompilerParams(dimension_semantics=("arbitrary",),
                                             vmem_limit_bytes=VMEM_LIMIT),
        name="attn_sample",
    )(page_table, q, kn, vn, gain, bd, cache_kt, cache_vt)


def kernel(x_prompt, x_sample, cache_k, cache_v, state_conv, page_table, ffn1_norm, ffn1_w_gu, ffn1_w_down,
           mix_norm, w_in, conv_w, conv_out_norm, attn_out_norm, w_out, ffn2_norm, ffn2_w_gu, ffn2_w_down,
           final_norm):
    bp, seq, d = x_prompt.shape
    bs, dseq, _ = x_sample.shape
    depth, n_pool, page = cache_k.shape[:3]
    dc = conv_w.shape[2]
    past_len = page_table.shape[1] * page

    ck = jnp.transpose(cache_k, (0, 1, 3, 4, 2)).reshape(depth * n_pool, N_HEADS, HEAD_DIM, page)
    cv = jnp.transpose(cache_v, (0, 1, 3, 4, 2)).reshape(depth * n_pool, N_HEADS, HEAD_DIM, page)
    gi = lax.broadcasted_iota(jnp.int32, (D_ATT, D_ATT), 0) // HEAD_DIM
    gj = lax.broadcasted_iota(jnp.int32, (D_ATT, D_ATT), 1) // HEAD_DIM
    bd = (gi == gj).astype(BF16)

    xp = x_prompt.reshape(bp * seq, d)
    xs = x_sample.reshape(bs * dseq, d)
    tm_p = PROMPT_ROW_TILE
    assert seq % tm_p == 0 and tm_p % MOBA_BLOCK == 0
    assert dc // N_CONV_GROUPS == HEAD_DIM and dc == D_ATT
    row = lambda a: a.reshape(1, -1)
    outs = [[] for _ in range(6)]
    for l in range(depth):
        g1, gm, g2 = row(ffn1_norm[l]), row(mix_norm[l]), row(ffn2_norm[l])
        gc, ga = row(conv_out_norm[l]), row(attn_out_norm[l])
        last = l == depth - 1
        gfin = row(final_norm) if last else None

        st = state_conv[l]
        zpad = jnp.zeros((bs, dseq - (CONV_W - 1), dc), F32)
        s2 = jnp.concatenate([st, zpad], axis=1).reshape(bs * dseq, dc)
        s1 = jnp.concatenate([st[:, 1:2], jnp.zeros((bs, dseq - 1, dc), F32)], axis=1).reshape(bs * dseq, dc)
        x1s, wg1, wu1, wd1 = _ffn_stream_call(xs, g1, ffn1_w_gu[l], ffn1_w_down[l], name="ffn1_sample")
        qs, ks, vs, ycs, us, win = _inproj_sample(x1s, gm, w_in[l], conv_w[l], gc, bd, s1, s2, seq=dseq)
        r3 = lambda a: a.reshape(bs, dseq, D_ATT)
        yas = _attn_sample_t(page_table, r3(qs), r3(ks), r3(vs), ga, bd, ck, cv,
                             page_base=l * n_pool, past_len=past_len)
        xs, wg2, wu2, wd2, woc, woa = _ffn_stream_call(
            x1s, g2, ffn2_w_gu[l], ffn2_w_down[l], mix=(ycs, yas.reshape(bs * dseq, D_ATT), w_out[l]),
            final=gfin, name="ffn2_sample")
        outs[3].append(ks.reshape(bs, dseq, N_HEADS, HEAD_DIM))
        outs[4].append(vs.reshape(bs, dseq, N_HEADS, HEAD_DIM))
        outs[5].append(us[:, dseq - (CONV_W - 1):, :])

        x1 = _ffn_call(xp, g1, wg1, wu1, wd1, tm=tm_p, name="ffn1_prompt")
        q, kt, vtf, kb, vt, yc, means, cnew = _inproj_prompt(x1, gm, win, conv_w[l], gc, bd, batch=bp, tm=tm_p)
        ya = _attn_prompt(q, kb, vt, means, ga, batch=bp)
        xp = _ffn_call(x1, g2, wg2, wu2, wd2, tm=tm_p, mix=(yc, ya, woc, woa), final=gfin, name="ffn2_prompt")
        tok_major = lambda a: a.reshape(bp, N_HEADS, HEAD_DIM, seq).transpose(0, 3, 1, 2)
        outs[0].append(tok_major(kt))
        outs[1].append(tok_major(vtf))
        outs[2].append(cnew)

    y_prompt = xp.reshape(bp, seq, d)
    y_sample = xs.reshape(bs, dseq, d)
    kp, vp, cp, ksn, vsn, csn = (jnp.stack(o) for o in outs)
    return (y_prompt, y_sample, kp, vp, cp, ksn, vsn, csn)
```

```python
import functools

import jax
import jax.numpy as jnp
from jax import lax
from jax.experimental import pallas as pl
from jax.experimental.pallas import tpu as pltpu

F32 = jnp.float32
BF16 = jnp.bfloat16

N_HEADS = 8
HEAD_DIM = 64
D_ATT = N_HEADS * HEAD_DIM
N_CONV_GROUPS = 8
CONV_W = 3
MOBA_BLOCK = 256
MOBA_TOPK = 3
EPS = 1e-5
NEG_INF = -1e30
FINITE_MAX = 3.0e38
REMOVED = -3e38
SCALE = HEAD_DIM ** -0.5
LOG2E = 1.4426950408889634

LANES = 128
MXU_WIDTH = 256
HEADS_PER_SLAB = LANES // HEAD_DIM
N_SLABS = D_ATT // LANES
VMEM_LIMIT = 56 * 1024 * 1024
PROMPT_ROW_TILE = 512
INPROJ_ROW_TILE = 1024
UPDATE_BLOCKS = 2
SUM_ROWS = 16
DMA_THREADS = 2
K_TILE = 16
V_TILE = 16

NT_DIMS = (((1,), (1,)), ((), ()))


def _slope(h):
    return 2.0 ** (-(8.0 / N_HEADS) * (h + 1))


def _dot(a, b):
    return jnp.dot(a, b, preferred_element_type=F32)


def _dot_nt(a, b):
    return lax.dot_general(a, b, NT_DIMS, preferred_element_type=F32)


def _rms(x, g):
    ms = jnp.mean(x * x, axis=-1, keepdims=True)
    return x * lax.rsqrt(ms + EPS) * g


def _group_sumsq(y, bd_ref):
    y2 = y * y
    hi = y2.astype(BF16)
    lo = (y2 - hi.astype(F32)).astype(BF16)
    bd = bd_ref[...]
    return _dot(hi, bd) + _dot(lo, bd)


def _group_rms(y, g, bd_ref):
    ms = _group_sumsq(y, bd_ref) * (1.0 / HEAD_DIM)
    return y * lax.rsqrt(ms + EPS) * g


def _const_spec(shape):
    nd = len(shape)
    return pl.BlockSpec(shape, lambda *_: (0,) * nd, pipeline_mode=pl.Buffered(1))


def _ffn_body(*refs, mix, final, bounds):
    it = iter(refs)
    x_ref = next(it)
    if mix:
        yc_ref, ya_ref, woc_ref, woa_ref = next(it), next(it), next(it), next(it)
    g_ref, wg_ref, wu_ref, wd_ref = next(it), next(it), next(it), next(it)
    gf_ref = next(it) if final else None
    o_ref = next(it)

    x = x_ref[...]
    if mix:
        x = x + _dot(yc_ref[...], woc_ref[...]) + _dot(ya_ref[...], woa_ref[...])
    h = _rms(x, g_ref[...]).astype(BF16)
    acc = jnp.zeros(x.shape, F32)
    for lo, hi in zip(bounds[:-1], bounds[1:]):
        gate = _dot(h, wg_ref[:, lo:hi])
        up = _dot(h, wu_ref[:, lo:hi])
        act = (gate * jax.nn.sigmoid(gate) * up).astype(BF16)
        acc = acc + _dot(act, wd_ref[lo:hi, :])
    x = x + 0.5 * acc
    if final:
        x = _rms(x, gf_ref[...])
    o_ref[...] = x


def _ffn_call(x, g, wg, wu, wd, *, tm, mix=None, final=None, name):
    n, d = x.shape
    d_ff = wd.shape[0]
    row = lambda w: pl.BlockSpec((tm, w), lambda i: (i, 0))
    ins, specs = [x], [row(d)]
    if mix is not None:
        yc, ya, woc, woa = mix
        ins += [yc, ya, woc, woa]
        specs += [row(yc.shape[1]), row(ya.shape[1]), _const_spec(woc.shape), _const_spec(woa.shape)]
    ins += [g, wg, wu, wd]
    specs += [_const_spec(g.shape), _const_spec(wg.shape), _const_spec(wu.shape), _const_spec(wd.shape)]
    if final is not None:
        ins.append(final)
        specs.append(_const_spec(final.shape))
    assert d_ff % MXU_WIDTH == 0
    tiles = d_ff // MXU_WIDTH
    bounds = (0, (tiles + 1) // 2 * MXU_WIDTH, d_ff)
    body = functools.partial(_ffn_body, mix=mix is not None, final=final is not None, bounds=bounds)
    return pl.pallas_call(
        body,
        grid=(n // tm,),
        in_specs=specs,
        out_specs=row(d),
        out_shape=jax.ShapeDtypeStruct((n, d), F32),
        compiler_params=pltpu.CompilerParams(dimension_semantics=("arbitrary",),
                                             vmem_limit_bytes=VMEM_LIMIT),
        name=name,
    )(*ins)


def _ffn_stream_body(*refs, mix, final):
    it = iter(refs)
    x_ref = next(it)
    if mix:
        yc_ref, ya_ref, woc_ref, woa_ref = next(it), next(it), next(it), next(it)
    g_ref, wg_ref, wu_ref, wd_ref = next(it), next(it), next(it), next(it)
    gf_ref = next(it) if final else None
    o_ref, wgb_ref, wub_ref, wdb_ref = next(it), next(it), next(it), next(it)
    if mix:
        wocb_ref, woab_ref = next(it), next(it)
    x_scr, h_scr, acc_scr = next(it), next(it), next(it)
    c = pl.program_id(0)

    @pl.when(c == 0)
    def _():
        x = x_ref[...]
        if mix:
            woc, woa = woc_ref[...].astype(BF16), woa_ref[...].astype(BF16)
            wocb_ref[...] = woc
            woab_ref[...] = woa
            x = x + _dot(yc_ref[...], woc) + _dot(ya_ref[...], woa)
        x_scr[...] = x
        h_scr[...] = _rms(x, g_ref[...]).astype(BF16)
        acc_scr[...] = jnp.zeros(acc_scr.shape, F32)

    wg, wu, wd = wg_ref[...].astype(BF16), wu_ref[...].astype(BF16), wd_ref[...].astype(BF16)
    wgb_ref[...] = wg
    wub_ref[...] = wu
    wdb_ref[...] = wd
    h = h_scr[...]
    gate = _dot(h, wg)
    act = (gate * jax.nn.sigmoid(gate) * _dot(h, wu)).astype(BF16)
    acc_scr[...] += _dot(act, wd)

    @pl.when(c == pl.num_programs(0) - 1)
    def _():
        x = x_scr[...] + 0.5 * acc_scr[...]
        if final:
            x = _rms(x, gf_ref[...])
        o_ref[...] = x


def _ffn_stream_call(x, g, w_gu, w_down, *, mix=None, final=None, name):
    n, d = x.shape
    d_ff = w_down.shape[0]
    tw = MXU_WIDTH
    assert d_ff % tw == 0 and w_gu.shape == (d, 2 * d_ff)
    nt = d_ff // tw
    full = lambda shape: pl.BlockSpec(shape, lambda c: (0,) * len(shape))
    ins, specs = [x], [full((n, d))]
    outs = [jax.ShapeDtypeStruct((n, d), F32), jax.ShapeDtypeStruct((d, d_ff), BF16),
            jax.ShapeDtypeStruct((d, d_ff), BF16), jax.ShapeDtypeStruct((d_ff, d), BF16)]
    out_specs = [full((n, d)), pl.BlockSpec((d, tw), lambda c: (0, c)), pl.BlockSpec((d, tw), lambda c: (0, c)),
                 pl.BlockSpec((tw, d), lambda c: (c, 0))]
    if mix is not None:
        yc, ya, w_out = mix
        dm = yc.shape[1]
        assert w_out.shape == (dm + ya.shape[1], d) and ya.shape[1] == dm
        ins += [yc, ya, w_out, w_out]
        specs += [full(yc.shape), full(ya.shape), pl.BlockSpec((dm, d), lambda c: (0, 0)),
                  pl.BlockSpec((dm, d), lambda c: (1, 0))]
        outs += [jax.ShapeDtypeStruct((dm, d), BF16)] * 2
        out_specs += [full((dm, d))] * 2
    ins += [g, w_gu, w_gu, w_down]
    specs += [full(g.shape), pl.BlockSpec((d, tw), lambda c: (0, c)), pl.BlockSpec((d, tw), lambda c: (0, nt + c)),
              pl.BlockSpec((tw, d), lambda c: (c, 0))]
    if final is not None:
        ins.append(final)
        specs.append(full(final.shape))
    body = functools.partial(_ffn_stream_body, mix=mix is not None, final=final is not None)
    return pl.pallas_call(
        body,
        grid=(nt,),
        in_specs=specs,
        out_specs=out_specs,
        out_shape=outs,
        scratch_shapes=[pltpu.VMEM((n, d), F32), pltpu.VMEM((n, d), BF16), pltpu.VMEM((n, d), F32)],
        compiler_params=pltpu.CompilerParams(dimension_semantics=("arbitrary",),
                                             vmem_limit_bytes=VMEM_LIMIT),
        name=name,
    )(*ins)


def _inproj_body(*refs, tm, tiles_per_seq, sample):
    it = iter(refs)
    x_ref, g_ref, win_ref, cw_ref, cn_ref, bd_ref = (next(it) for _ in range(6))
    if sample:
        s1_ref, s2_ref = next(it), next(it)
        q_ref, k_ref, v_ref, yc_ref, u_ref, winb_ref = (next(it) for _ in range(6))
    else:
        q_ref, kt_ref, vtf_ref, kb_ref, vt_ref, yc_ref, mean_ref, cnew_ref = (next(it) for _ in range(8))
    ubuf = next(it)

    dc = yc_ref.shape[1]
    if sample:
        ubuf[0:8, :] = jnp.zeros((8, dc), F32)
    else:
        first = (pl.program_id(0) % tiles_per_seq) == 0

        @pl.when(first)
        def _():
            ubuf[0:8, :] = jnp.zeros((8, dc), F32)

        @pl.when(jnp.logical_not(first))
        def _():
            ubuf[0:8, :] = ubuf[tm:tm + 8, :]

    h = _rms(x_ref[...], g_ref[...]).astype(BF16)
    if sample:
        def piece(c, w):
            wp = win_ref[:, c:c + w].astype(BF16)
            winb_ref[:, c:c + w] = wp
            return _dot(h, wp)
    else:
        piece = lambda c, w: _dot(h, win_ref[:, c:c + w])
    hc = piece(0, dc)
    cg = piece(2 * dc, dc)
    bg = piece(dc, dc)
    k = piece(3 * dc + D_ATT, D_ATT)
    v = piece(3 * dc + 2 * D_ATT, D_ATT)
    if sample:
        k_ref[...] = k
        v_ref[...] = v

    u = cg * hc
    ubuf[8:tm + 8, :] = u
    um1 = ubuf[7:tm + 7, :]
    um2 = ubuf[6:tm + 6, :]
    if sample:
        t = lax.broadcasted_iota(jnp.int32, (tm, dc), 0) % u_ref.shape[1]
        um1 = jnp.where(t >= 1, um1, s1_ref[...])
        um2 = jnp.where(t >= 2, um2, s2_ref[...])
    cw = cw_ref[...]
    conv = um2 * cw[0:1, :] + um1 * cw[1:2, :] + u * cw[2:3, :]
    yc_ref[...] = _group_rms(bg * conv, cn_ref[...], bd_ref).astype(BF16)
    q_ref[...] = piece(3 * dc, D_ATT)

    if sample:
        u_ref[...] = u.reshape(u_ref.shape)
    else:
        vt = v.T
        kt_ref[0] = k.T
        vtf_ref[0] = vt
        kb_ref[...] = k.astype(BF16)
        nblk = tm // MOBA_BLOCK
        for i in range(nblk):
            vt_ref[i] = vt[:, i * MOBA_BLOCK:(i + 1) * MOBA_BLOCK].astype(BF16)
        mean_ref[0] = jnp.sum(k.reshape(nblk, MOBA_BLOCK, D_ATT), axis=1) * (1.0 / MOBA_BLOCK)
        cnew_ref[0] = ubuf[tm + 6:tm + 8, :]


def _inproj_prompt(x, g, w_in, conv_w, conv_norm, bd, *, batch, tm):
    n, d = x.shape
    dc = conv_w.shape[1]
    seq = n // batch
    tps = seq // tm
    nblk = tm // MOBA_BLOCK
    row = lambda w: pl.BlockSpec((tm, w), lambda i: (i, 0))
    tok_minor = pl.BlockSpec((1, D_ATT, tm), lambda i: (i // tps, 0, i % tps))
    f = lambda w, dt: jax.ShapeDtypeStruct((n, w), dt)
    body = functools.partial(_inproj_body, tm=tm, tiles_per_seq=tps, sample=False)
    return pl.pallas_call(
        body,
        grid=(n // tm,),
        in_specs=[row(d), _const_spec(g.shape), _const_spec(w_in.shape), _const_spec(conv_w.shape),
                  _const_spec(conv_norm.shape), _const_spec(bd.shape)],
        out_specs=[row(D_ATT), tok_minor, tok_minor, row(D_ATT),
                   pl.BlockSpec((nblk, D_ATT, MOBA_BLOCK), lambda i: (i, 0, 0)), row(dc),
                   pl.BlockSpec((1, nblk, D_ATT), lambda i: (i, 0, 0)),
                   pl.BlockSpec((1, CONV_W - 1, dc), lambda i: (i // tps, 0, 0))],
        out_shape=[f(D_ATT, F32), jax.ShapeDtypeStruct((batch, D_ATT, seq), F32),
                   jax.ShapeDtypeStruct((batch, D_ATT, seq), F32), f(D_ATT, BF16),
                   jax.ShapeDtypeStruct((n // MOBA_BLOCK, D_ATT, MOBA_BLOCK), BF16), f(dc, BF16),
                   jax.ShapeDtypeStruct((n // tm, nblk, D_ATT), F32),
                   jax.ShapeDtypeStruct((batch, CONV_W - 1, dc), F32)],
        scratch_shapes=[pltpu.VMEM((tm + 8, dc), F32)],
        compiler_params=pltpu.CompilerParams(dimension_semantics=("arbitrary",),
                                             vmem_limit_bytes=VMEM_LIMIT),
        name="inproj_prompt",
    )(x, g, w_in, conv_w, conv_norm, bd)


def _inproj_sample(x, g, w_in, conv_w, conv_norm, bd, s1, s2, *, seq):
    n, d = x.shape
    dc = conv_w.shape[1]
    tm = n
    full = lambda shape: pl.BlockSpec(shape, lambda i: (0,) * len(shape))
    f = lambda w, dt: jax.ShapeDtypeStruct((n, w), dt)
    body = functools.partial(_inproj_body, tm=tm, tiles_per_seq=1, sample=True)
    return pl.pallas_call(
        body,
        grid=(1,),
        in_specs=[full((tm, d)), full(g.shape), full(w_in.shape), full(conv_w.shape),
                  full(conv_norm.shape), full(bd.shape), full((tm, dc)), full((tm, dc))],
        out_specs=[full((tm, D_ATT)), full((tm, D_ATT)), full((tm, D_ATT)), full((tm, dc)),
                   full((n // seq, seq, dc)), full(w_in.shape)],
        out_shape=[f(D_ATT, F32), f(D_ATT, F32), f(D_ATT, F32), f(dc, BF16),
                   jax.ShapeDtypeStruct((n // seq, seq, dc), F32), jax.ShapeDtypeStruct(w_in.shape, BF16)],
        scratch_shapes=[pltpu.VMEM((tm + 8, dc), F32)],
        compiler_params=pltpu.CompilerParams(dimension_semantics=("arbitrary",),
                                             vmem_limit_bytes=VMEM_LIMIT),
        name="inproj_sample",
    )(x, g, w_in, conv_w, conv_norm, bd, s1, s2)


def _split3(x):
    hi = x.astype(BF16).astype(F32)
    mid = (x - hi).astype(BF16).astype(F32)
    lo = (x - hi - mid).astype(BF16).astype(F32)
    return hi, mid, lo


def _attn_prompt_body(q_ref, kb_ref, vt_ref, mean_ref, gain_ref, o_ref,
                      causal_ref, featk_ref, qabt_ref, colb_ref, so_ref, seta_ref, setb_ref,
                      m_ref, l_ref, acc_ref):
    blk = MOBA_BLOCK
    b = pl.program_id(0)
    j = pl.program_id(1)
    nb = mean_ref.shape[1]
    group = LANES // N_HEADS
    qcols = HEADS_PER_SLAB * blk
    lane_q = lax.broadcasted_iota(jnp.int32, (1, qcols), 1)

    def slope_row(p):
        return jnp.where(lane_q < blk, LOG2E * _slope(HEADS_PER_SLAB * p), LOG2E * _slope(HEADS_PER_SLAB * p + 1))

    @pl.when((b == 0) & (j == 0))
    def _init_tables():
        kk = lax.broadcasted_iota(jnp.int32, (blk, qcols), 0)
        qq = lax.broadcasted_iota(jnp.int32, (blk, qcols), 1)
        causal_ref[...] = jnp.where((qq % blk) >= kk, 0.0, NEG_INF)
        ki = lax.broadcasted_iota(jnp.int32, (blk, LANES), 0).astype(F32)
        kl = lax.broadcasted_iota(jnp.int32, (blk, LANES), 1)
        featk_ref[...] = jnp.where(kl < 3, ki, jnp.where(kl < 6, 1.0, 0.0)).astype(BF16)
        fr = lax.broadcasted_iota(jnp.int32, (LANES, qcols), 0)
        for p in range(N_SLABS):
            a = slope_row(p)
            terms = _split3(a) + _split3(-a * (lane_q % blk).astype(F32))
            feat = jnp.zeros((LANES, qcols), F32)
            for r, t in enumerate(terms):
                feat = jnp.where(fr == r, t, feat)
            qabt_ref[p, LANES:, :] = feat.astype(BF16)

    qt = q_ref[...].T
    means = mean_ref[0]
    if nb < group:
        means = jnp.concatenate([means, jnp.zeros((group - nb, D_ATT), F32)], axis=0)
    mt = jnp.concatenate([means] * N_HEADS, axis=0)
    rh = lax.broadcasted_iota(jnp.int32, mt.shape, 0) // group
    ch = lax.broadcasted_iota(jnp.int32, mt.shape, 1) // HEAD_DIM
    mbd = jnp.where(rh == ch, mt, 0.0)
    m_hi = mbd.astype(BF16)
    q_hi = qt.astype(BF16)
    m2 = jnp.concatenate([m_hi, (mbd - m_hi.astype(F32)).astype(BF16)], axis=0)
    q2 = jnp.concatenate([q_hi, (qt - q_hi.astype(F32)).astype(BF16)], axis=1)
    g4 = _dot(m2, q2)
    hg = N_HEADS * group
    gate_t = (g4[:hg, :blk] + g4[hg:, blk:]) + (g4[:hg, blk:] + g4[hg:, :blk])
    gate = jnp.concatenate([gate_t[h * group:(h + 1) * group, :] for h in range(N_HEADS)], axis=1)

    n_idx = lax.broadcasted_iota(jnp.int32, gate.shape, 0)
    n_f = n_idx.astype(F32)
    valid = n_idx < j
    work = jnp.where(valid, gate, NEG_INF)
    picked = jnp.zeros(gate.shape, F32)
    for _ in range(MOBA_TOPK):
        top = jnp.max(work, axis=0, keepdims=True)
        first = jnp.min(jnp.where(work == top, n_f, float(group)), axis=0, keepdims=True)
        pick = n_f == first
        picked = jnp.where(pick, 1.0, picked)
        work = jnp.where(pick, REMOVED, work)
    colb_ref[...] = jnp.where((picked > 0.0) & valid, 0.0, NEG_INF)

    row_d = lax.broadcasted_iota(jnp.int32, (LANES, blk), 0)
    for p in range(N_SLABS):
        qs = qt[p * LANES:(p + 1) * LANES, :] * (SCALE * LOG2E)
        qa = jnp.where(row_d < HEAD_DIM, qs, 0.0)
        qb = jnp.where(row_d >= HEAD_DIM, qs, 0.0)
        qabt_ref[p, :LANES, :] = jnp.concatenate([qa, qb], axis=1).astype(BF16)

    slabs = [slice(p * LANES, (p + 1) * LANES) for p in range(N_SLABS)]

    def scores(n, p):
        off = pl.multiple_of(n * blk, blk)
        keys = jnp.concatenate([kb_ref[pl.ds(off, blk), slabs[p]], featk_ref[...]], axis=1)
        return _dot(keys, qabt_ref[p])

    sets = (seta_ref, setb_ref)

    def park_unit(n_first, count, p):
        for g in range(count):
            sets[p % 2][g] = scores(jnp.minimum(n_first + g, j), p)

    def weighted_values(blocks, p, e):
        vt = jnp.concatenate([vt_ref[n, slabs[p], :] for n in blocks], axis=1)
        ones = jnp.ones((SUM_ROWS, vt.shape[1]), BF16)
        pvs = [_dot(jnp.concatenate([vt[h * HEAD_DIM:(h + 1) * HEAD_DIM], ones], axis=0),
                    e[:, h * blk:(h + 1) * blk]) for h in range(HEADS_PER_SLAB)]
        return (jnp.concatenate([pv[:HEAD_DIM] for pv in pvs], axis=0),
                jnp.concatenate([pv[HEAD_DIM:HEAD_DIM + 1] for pv in pvs], axis=1))

    def by_head(row):
        return jnp.where(row_d < HEAD_DIM, row[:, :blk], row[:, blk:])

    def reduce_unit(n_first, count, p, track_max):
        cs = slice(p * qcols, (p + 1) * qcols)
        src = sets[p % 2]
        blocks = [n_first + g for g in range(count)]
        crows = [colb_ref[pl.ds(n, 1), cs] - slope_row(p) * ((j - n) * blk).astype(F32) for n in blocks]
        m_prev = m_ref[p]
        if track_max:
            m_new = m_prev
            for g, crow in enumerate(crows):
                m_new = jnp.maximum(m_new, jnp.max(src[g], axis=0, keepdims=True) + crow)
            alpha = jnp.exp2(m_prev - m_new)
            m_ref[p] = m_new
        else:
            m_new = m_prev
        e = jnp.concatenate([jnp.exp2(src[g] - (m_new - crow)).astype(BF16) for g, crow in enumerate(crows)],
                            axis=0)
        pv, esum = weighted_values(blocks, p, e)
        if track_max:
            l_ref[p] = alpha * l_ref[p] + esum
            acc_ref[p] = by_head(alpha) * acc_ref[p] + pv
        else:
            l_ref[p] = l_ref[p] + esum
            acc_ref[p] = acc_ref[p] + pv

    def sweep(n_first, count, n_after, track_max):
        for p in range(N_SLABS):
            if p + 1 < N_SLABS:
                park_unit(n_first, count, p + 1)
            elif n_after is not None:
                park_unit(n_after, UPDATE_BLOCKS, 0)
            reduce_unit(n_first, count, p, track_max)

    def attend(track_max):
        for p in range(N_SLABS):
            so_ref[p] = scores(j, p)
        park_unit(0, UPDATE_BLOCKS, 0)
        for p in range(N_SLABS):
            sb = so_ref[p] + causal_ref[...]
            m = jnp.max(sb, axis=0, keepdims=True)
            pv, esum = weighted_values([j], p, jnp.exp2(sb - m).astype(BF16))
            m_ref[p] = m
            l_ref[p] = esum
            acc_ref[p] = pv

        def trip(t, carry):
            sweep(UPDATE_BLOCKS * t, UPDATE_BLOCKS, UPDATE_BLOCKS * (t + 1), track_max)
            return carry

        lax.fori_loop(0, j // UPDATE_BLOCKS, trip, 0)

        nr = (j // UPDATE_BLOCKS) * UPDATE_BLOCKS
        for r in range(1, UPDATE_BLOCKS):
            @pl.when(j - nr == r)
            def _(r=r):
                sweep(nr, r, None, track_max)

    attend(track_max=False)
    l_top = l_ref[0]
    a_top = jnp.abs(acc_ref[0])
    for p in range(1, N_SLABS):
        l_top = jnp.maximum(l_top, l_ref[p])
        a_top = jnp.maximum(a_top, jnp.abs(acc_ref[p]))
    finite = (jnp.max(l_top) < FINITE_MAX) & (jnp.max(a_top) < FINITE_MAX)

    @pl.when(jnp.logical_not(finite))
    def _():
        attend(track_max=True)

    for p in range(N_SLABS):
        o2 = acc_ref[p] / by_head(l_ref[p])
        sq = o2 * o2
        ms_a = jnp.sum(sq[:HEAD_DIM], axis=0, keepdims=True) * (1.0 / HEAD_DIM)
        ms_b = jnp.sum(sq[HEAD_DIM:], axis=0, keepdims=True) * (1.0 / HEAD_DIM)
        inv = jnp.where(row_d < HEAD_DIM, lax.rsqrt(ms_a + EPS), lax.rsqrt(ms_b + EPS))
        ls = slice(p * LANES, (p + 1) * LANES)
        o_ref[:, ls] = ((o2 * inv).T * gain_ref[:, ls]).astype(BF16)


def _attn_prompt(q, kb, vt, means, gain, *, batch):
    n = q.shape[0]
    seq = n // batch
    nb = seq // MOBA_BLOCK
    group = LANES // N_HEADS
    assert nb <= group and seq % MOBA_BLOCK == 0
    means = means.reshape(batch, nb, D_ATT)
    qcols = HEADS_PER_SLAB * MOBA_BLOCK
    return pl.pallas_call(
        _attn_prompt_body,
        grid=(batch, nb),
        in_specs=[pl.BlockSpec((MOBA_BLOCK, D_ATT), lambda b, j: (b * nb + j, 0)),
                  pl.BlockSpec((seq, D_ATT), lambda b, j: (b, 0)),
                  pl.BlockSpec((nb, D_ATT, MOBA_BLOCK), lambda b, j: (b, 0, 0)),
                  pl.BlockSpec((1, nb, D_ATT), lambda b, j: (b, 0, 0)),
                  pl.BlockSpec(gain.shape, lambda b, j: (0, 0))],
        out_specs=pl.BlockSpec((MOBA_BLOCK, D_ATT), lambda b, j: (b * nb + j, 0)),
        out_shape=jax.ShapeDtypeStruct((n, D_ATT), BF16),
        scratch_shapes=[pltpu.VMEM((MOBA_BLOCK, qcols), F32),
                        pltpu.VMEM((MOBA_BLOCK, LANES), BF16),
                        pltpu.VMEM((N_SLABS, 2 * LANES, qcols), BF16),
                        pltpu.VMEM((group, N_HEADS * MOBA_BLOCK), F32),
                        pltpu.VMEM((N_SLABS, MOBA_BLOCK, qcols), F32),
                        pltpu.VMEM((UPDATE_BLOCKS, MOBA_BLOCK, qcols), F32),
                        pltpu.VMEM((UPDATE_BLOCKS, MOBA_BLOCK, qcols), F32),
                        pltpu.VMEM((N_SLABS, 1, qcols), F32),
                        pltpu.VMEM((N_SLABS, 1, qcols), F32),
                        pltpu.VMEM((N_SLABS, LANES, MOBA_BLOCK), F32)],
        compiler_params=pltpu.CompilerParams(dimension_semantics=("arbitrary", "arbitrary"),
                                             vmem_limit_bytes=VMEM_LIMIT),
        name="attn_prompt",
    )(q, kb, vt, means, gain)


def _attn_sample_t_body(pt_ref, q_ref, kn_ref, vn_ref, gain_ref, bd_ref, ckt_hbm, cvt_hbm, o_ref,
                        kbuf, vbuf, s_ref, acc_ref, ksem, vsem, *, past_len, page, page_base):
    b = pl.program_id(0)
    nbat = pl.num_programs(0)
    n_pages = past_len // page
    ppb = MOBA_BLOCK // page
    nb = past_len // MOBA_BLOCK
    t_new = q_ref.shape[1]
    rows = N_HEADS * t_new

    def page_copy(hbm, buf, sem, bb, pg):
        return pltpu.make_async_copy(hbm.at[page_base + pt_ref[bb, pg]], buf.at[pg], sem.at[pg])

    def start_all(hbm, buf, sem, bb):
        def body(i, c):
            for k in range(DMA_THREADS):
                page_copy(hbm, buf, sem, bb, i * DMA_THREADS + k).start(priority=k)
            return c
        lax.fori_loop(0, n_pages // DMA_THREADS, body, 0)

    def refill(hbm, buf, sem, pgs):
        @pl.when(b + 1 < nbat)
        def _():
            for k, pg in enumerate(pgs):
                page_copy(hbm, buf, sem, b + 1, pg).start(priority=k % DMA_THREADS)

    @pl.when(b == 0)
    def _():
        start_all(ckt_hbm, kbuf, ksem, b)
        start_all(cvt_hbm, vbuf, vsem, b)

    qt = jnp.concatenate([q_ref[0]] * N_HEADS, axis=0)
    rh = lax.broadcasted_iota(jnp.int32, qt.shape, 0) // t_new
    ch = lax.broadcasted_iota(jnp.int32, qt.shape, 1) // HEAD_DIM
    qs = jnp.where(rh == ch, qt, 0.0) * SCALE
    q_hi = qs.astype(BF16)
    q_lo = (qs - q_hi.astype(F32)).astype(BF16)
    qq = jnp.concatenate([q_hi, q_lo], axis=0)

    def k_tile(i, c):
        pgs = [i * K_TILE + k for k in range(K_TILE)]
        for pg in pgs:
            page_copy(ckt_hbm, kbuf, ksem, b, pg).wait()
        for pg in pgs:
            s2 = _dot(qq, kbuf[pg].reshape(D_ATT, page).astype(BF16))
            s_ref[pg] = s2[:rows] + s2[rows:]
        refill(ckt_hbm, kbuf, ksem, pgs)
        return c
    lax.fori_loop(0, n_pages // K_TILE, k_tile, 0)

    lane = lax.broadcasted_iota(jnp.int32, (rows, LANES), 1)
    gate = jnp.zeros((rows, LANES), F32)
    gcols = []
    for n in range(nb):
        tot = s_ref[n * ppb]
        for i in range(1, ppb):
            tot = tot + s_ref[n * ppb + i]
        g = jnp.sum(tot, axis=1, keepdims=True)
        gcols.append(g)
        gate = jnp.where(lane == n, g, gate)
    rank = jnp.zeros(gate.shape, jnp.int32)
    for m in range(nb):
        beats = (gcols[m] > gate) | ((gcols[m] == gate) & (lane > m))
        rank = rank + beats.astype(jnp.int32)
    colb = jnp.where(rank < MOBA_TOPK, 0.0, NEG_INF)

    r1 = lax.broadcasted_iota(jnp.int32, (rows, 1), 0)
    tq = r1 % t_new
    slope = jnp.zeros((rows, 1), F32)
    for h in range(N_HEADS):
        slope = jnp.where(r1 // t_new == h, _slope(h), slope)
    in_page = slope * (tq - lane).astype(F32)

    zpad = jnp.zeros((LANES - t_new, D_ATT), F32)
    kn = jnp.concatenate([kn_ref[0], zpad], axis=0).astype(BF16)
    vn = jnp.concatenate([vn_ref[0], zpad], axis=0).astype(BF16)
    s2 = _dot_nt(qq, kn)
    s_own = jnp.where(lane <= tq, s2[:rows] + s2[rows:] - in_page, NEG_INF)

    mrun = s_own
    for n in range(nb):
        mask_n = jnp.sum(jnp.where(lane == n, colb, 0.0), axis=1, keepdims=True)
        for i in range(ppb):
            pg = n * ppb + i
            sn = s_ref[pg] - in_page + (mask_n - slope * float(past_len - pg * page))
            s_ref[pg] = sn
            mrun = jnp.maximum(mrun, sn)
    m = jnp.max(mrun, axis=1, keepdims=True)

    e_own = jnp.exp(s_own - m)
    lrun = e_own
    for pg in range(n_pages):
        e = jnp.exp(s_ref[pg] - m)
        s_ref[pg] = e
        lrun = lrun + e
    l = jnp.sum(lrun, axis=1, keepdims=True)

    acc_ref[...] = jnp.zeros(acc_ref.shape, F32)
    zrows = jnp.zeros((LANES - rows, V_TILE * page), BF16)

    def v_tile(i, c):
        pgs = [i * V_TILE + k for k in range(V_TILE)]
        for pg in pgs:
            page_copy(cvt_hbm, vbuf, vsem, b, pg).wait()
        vt = jnp.concatenate([vbuf[pg].reshape(D_ATT, page) for pg in pgs], axis=1).astype(BF16)
        p = jnp.concatenate([s_ref[pg] for pg in pgs], axis=1).astype(BF16)
        acc_ref[...] += _dot_nt(vt, jnp.concatenate([p, zrows], axis=0))
        refill(cvt_hbm, vbuf, vsem, pgs)
        return c
    lax.fori_loop(0, n_pages // V_TILE, v_tile, 0)

    acc = acc_ref[...].T[:rows] + _dot(e_own.astype(BF16), vn)
    accn = acc / l
    ch8 = lax.broadcasted_iota(jnp.int32, (t_new, D_ATT), 1) // HEAD_DIM
    out = jnp.zeros((t_new, D_ATT), F32)
    for h in range(N_HEADS):
        out = jnp.where(ch8 == h, accn[h * t_new:(h + 1) * t_new, :], out)
    o_ref[0] = _group_rms(out, gain_ref[...], bd_ref).astype(BF16)


def _attn_sample_t(page_table, q, kn, vn, gain, bd, cache_kt, cache_vt, *, page_base, past_len):
    nbat, t_new, _ = q.shape
    page = cache_kt.shape[3]
    n_pages = past_len // page
    rows = N_HEADS * t_new
    assert past_len % MOBA_BLOCK == 0 and MOBA_BLOCK % page == 0 and page == LANES
    assert rows <= LANES and t_new % 8 == 0 and past_len // MOBA_BLOCK <= LANES
    assert n_pages % V_TILE == 0 and n_pages % K_TILE == 0
    body = functools.partial(_attn_sample_t_body, past_len=past_len, page=page, page_base=page_base)
    per_b = pl.BlockSpec((1, t_new, D_ATT), lambda b, pt: (b, 0, 0))
    grid_spec = pltpu.PrefetchScalarGridSpec(
        num_scalar_prefetch=1,
        grid=(nbat,),
        in_specs=[per_b, per_b, per_b,
                  pl.BlockSpec(gain.shape, lambda b, pt: (0, 0)),
                  pl.BlockSpec(bd.shape, lambda b, pt: (0, 0)),
                  pl.BlockSpec(memory_space=pl.ANY),
                  pl.BlockSpec(memory_space=pl.ANY)],
        out_specs=per_b,
        scratch_shapes=[pltpu.VMEM((n_pages, N_HEADS, HEAD_DIM, page), F32),
                        pltpu.VMEM((n_pages, N_HEADS, HEAD_DIM, page), F32),
                        pltpu.VMEM((n_pages, rows, page), F32),
                        pltpu.VMEM((D_ATT, LANES), F32),
                        pltpu.SemaphoreType.DMA((n_pages,)),
                        pltpu.SemaphoreType.DMA((n_pages,))],
    )
    return pl.pallas_call(
        body,
        grid_spec=grid_spec,
        out_shape=jax.ShapeDtypeStruct((nbat, t_new, D_ATT), BF16),
        compiler_params=pltpu.CompilerParams(dimension_semantics=("arbitrary",),
                                             vmem_limit_bytes=VMEM_LIMIT),
        name="attn_sample",
    )(page_table, q, kn, vn, gain, bd, cache_kt, cache_vt)


def kernel(x_prompt, x_sample, cache_k, cache_v, state_conv, page_table, ffn1_norm, ffn1_w_gu, ffn1_w_down,
           mix_norm, w_in, conv_w, conv_out_norm, attn_out_norm, w_out, ffn2_norm, ffn2_w_gu, ffn2_w_down,
           final_norm):
    bp, seq, d = x_prompt.shape
    bs, dseq, _ = x_sample.shape
    depth, n_pool, page = cache_k.shape[:3]
    dc = conv_w.shape[2]
    past_len = page_table.shape[1] * page

    ck = jnp.transpose(cache_k, (0, 1, 3, 4, 2)).reshape(depth * n_pool, N_HEADS, HEAD_DIM, page)
    cv = jnp.transpose(cache_v, (0, 1, 3, 4, 2)).reshape(depth * n_pool, N_HEADS, HEAD_DIM, page)
    gi = lax.broadcasted_iota(jnp.int32, (D_ATT, D_ATT), 0) // HEAD_DIM
    gj = lax.broadcasted_iota(jnp.int32, (D_ATT, D_ATT), 1) // HEAD_DIM
    bd = (gi == gj).astype(BF16)

    xp = x_prompt.reshape(bp * seq, d)
    xs = x_sample.reshape(bs * dseq, d)
    tm_p = PROMPT_ROW_TILE
    assert seq % tm_p == 0 and seq % INPROJ_ROW_TILE == 0 and INPROJ_ROW_TILE % MOBA_BLOCK == 0
    assert dc // N_CONV_GROUPS == HEAD_DIM and dc == D_ATT
    row = lambda a: a.reshape(1, -1)
    outs = [[] for _ in range(6)]
    for l in range(depth):
        g1, gm, g2 = row(ffn1_norm[l]), row(mix_norm[l]), row(ffn2_norm[l])
        gc, ga = row(conv_out_norm[l]), row(attn_out_norm[l])
        last = l == depth - 1
        gfin = row(final_norm) if last else None

        st = state_conv[l]
        zpad = jnp.zeros((bs, dseq - (CONV_W - 1), dc), F32)
        s2 = jnp.concatenate([st, zpad], axis=1).reshape(bs * dseq, dc)
        s1 = jnp.concatenate([st[:, 1:2], jnp.zeros((bs, dseq - 1, dc), F32)], axis=1).reshape(bs * dseq, dc)
        x1s, wg1, wu1, wd1 = _ffn_stream_call(xs, g1, ffn1_w_gu[l], ffn1_w_down[l], name="ffn1_sample")
        qs, ks, vs, ycs, us, win = _inproj_sample(x1s, gm, w_in[l], conv_w[l], gc, bd, s1, s2, seq=dseq)
        r3 = lambda a: a.reshape(bs, dseq, D_ATT)
        yas = _attn_sample_t(page_table, r3(qs), r3(ks), r3(vs), ga, bd, ck, cv,
                             page_base=l * n_pool, past_len=past_len)
        xs, wg2, wu2, wd2, woc, woa = _ffn_stream_call(
            x1s, g2, ffn2_w_gu[l], ffn2_w_down[l], mix=(ycs, yas.reshape(bs * dseq, D_ATT), w_out[l]),
            final=gfin, name="ffn2_sample")
        outs[3].append(ks.reshape(bs, dseq, N_HEADS, HEAD_DIM))
        outs[4].append(vs.reshape(bs, dseq, N_HEADS, HEAD_DIM))
        outs[5].append(us[:, dseq - (CONV_W - 1):, :])

        x1 = _ffn_call(xp, g1, wg1, wu1, wd1, tm=tm_p, name="ffn1_prompt")
        q, kt, vtf, kb, vt, yc, means, cnew = _inproj_prompt(x1, gm, win, conv_w[l], gc, bd, batch=bp,
                                                             tm=INPROJ_ROW_TILE)
        ya = _attn_prompt(q, kb, vt, means, ga, batch=bp)
        xp = _ffn_call(x1, g2, wg2, wu2, wd2, tm=tm_p, mix=(yc, ya, woc, woa), final=gfin, name="ffn2_prompt")
        tok_major = lambda a: a.reshape(bp, N_HEADS, HEAD_DIM, seq).transpose(0, 3, 1, 2)
        outs[0].append(tok_major(kt))
        outs[1].append(tok_major(vtf))
        outs[2].append(cnew)

    y_prompt = xp.reshape(bp, seq, d)
    y_sample = xs.reshape(bs, dseq, d)
    kp, vp, cp, ksn, vsn, csn = (jnp.stack(o) for o in outs)
    return (y_prompt, y_sample, kp, vp, cp, ksn, vsn, csn)
```

```python
import functools

import jax
import jax.numpy as jnp
from jax import lax
from jax.experimental import pallas as pl
from jax.experimental.pallas import tpu as pltpu

F32 = jnp.float32
BF16 = jnp.bfloat16

N_HEADS = 8
HEAD_DIM = 64
D_ATT = N_HEADS * HEAD_DIM
N_CONV_GROUPS = 8
CONV_W = 3
MOBA_BLOCK = 256
MOBA_TOPK = 3
EPS = 1e-5
NEG_INF = -1e30
FINITE_MAX = 3.0e38
REMOVED = -3e38
SCALE = HEAD_DIM ** -0.5
LOG2E = 1.4426950408889634

LANES = 128
MXU_WIDTH = 256
HEADS_PER_SLAB = LANES // HEAD_DIM
N_SLABS = D_ATT // LANES
VMEM_LIMIT = 56 * 1024 * 1024
ATTN_VMEM_LIMIT = 62 * 1024 * 1024
PROMPT_ROW_TILE = 512
INPROJ_ROW_TILE = 1024
UPDATE_BLOCKS = 2
SUM_ROWS = 16
DMA_THREADS = 2
K_TILE = 16
V_TILE = 16

NT_DIMS = (((1,), (1,)), ((), ()))


def _slope(h):
    return 2.0 ** (-(8.0 / N_HEADS) * (h + 1))


def _dot(a, b):
    return jnp.dot(a, b, preferred_element_type=F32)


def _dot_nt(a, b):
    return lax.dot_general(a, b, NT_DIMS, preferred_element_type=F32)


def _rms(x, g):
    ms = jnp.mean(x * x, axis=-1, keepdims=True)
    return x * lax.rsqrt(ms + EPS) * g


def _group_sumsq(y, bd_ref):
    y2 = y * y
    hi = y2.astype(BF16)
    lo = (y2 - hi.astype(F32)).astype(BF16)
    bd = bd_ref[...]
    return _dot(hi, bd) + _dot(lo, bd)


def _group_rms(y, g, bd_ref):
    ms = _group_sumsq(y, bd_ref) * (1.0 / HEAD_DIM)
    return y * lax.rsqrt(ms + EPS) * g


def _const_spec(shape):
    nd = len(shape)
    return pl.BlockSpec(shape, lambda *_: (0,) * nd, pipeline_mode=pl.Buffered(1))


def _ffn_body(*refs, mix, final, bounds):
    it = iter(refs)
    x_ref = next(it)
    if mix:
        yc_ref, ya_ref, woc_ref, woa_ref = next(it), next(it), next(it), next(it)
    g_ref, wg_ref, wu_ref, wd_ref = next(it), next(it), next(it), next(it)
    gf_ref = next(it) if final else None
    o_ref = next(it)

    x = x_ref[...]
    if mix:
        x = x + _dot(yc_ref[...], woc_ref[...]) + _dot(ya_ref[...], woa_ref[...])
    h = _rms(x, g_ref[...]).astype(BF16)
    acc = jnp.zeros(x.shape, F32)
    for lo, hi in zip(bounds[:-1], bounds[1:]):
        gate = _dot(h, wg_ref[:, lo:hi])
        up = _dot(h, wu_ref[:, lo:hi])
        act = (gate * jax.nn.sigmoid(gate) * up).astype(BF16)
        acc = acc + _dot(act, wd_ref[lo:hi, :])
    x = x + 0.5 * acc
    if final:
        x = _rms(x, gf_ref[...])
    o_ref[...] = x


def _ffn_call(x, g, wg, wu, wd, *, tm, mix=None, final=None, name):
    n, d = x.shape
    d_ff = wd.shape[0]
    row = lambda w: pl.BlockSpec((tm, w), lambda i: (i, 0))
    ins, specs = [x], [row(d)]
    if mix is not None:
        yc, ya, woc, woa = mix
        ins += [yc, ya, woc, woa]
        specs += [row(yc.shape[1]), row(ya.shape[1]), _const_spec(woc.shape), _const_spec(woa.shape)]
    ins += [g, wg, wu, wd]
    specs += [_const_spec(g.shape), _const_spec(wg.shape), _const_spec(wu.shape), _const_spec(wd.shape)]
    if final is not None:
        ins.append(final)
        specs.append(_const_spec(final.shape))
    assert d_ff % MXU_WIDTH == 0
    tiles = d_ff // MXU_WIDTH
    bounds = (0, (tiles + 1) // 2 * MXU_WIDTH, d_ff)
    body = functools.partial(_ffn_body, mix=mix is not None, final=final is not None, bounds=bounds)
    return pl.pallas_call(
        body,
        grid=(n // tm,),
        in_specs=specs,
        out_specs=row(d),
        out_shape=jax.ShapeDtypeStruct((n, d), F32),
        compiler_params=pltpu.CompilerParams(dimension_semantics=("arbitrary",),
                                             vmem_limit_bytes=VMEM_LIMIT),
        name=name,
    )(*ins)


def _ffn_stream_body(*refs, mix, final):
    it = iter(refs)
    x_ref = next(it)
    if mix:
        yc_ref, ya_ref, woc_ref, woa_ref = next(it), next(it), next(it), next(it)
    g_ref, wg_ref, wu_ref, wd_ref = next(it), next(it), next(it), next(it)
    gf_ref = next(it) if final else None
    o_ref, wgb_ref, wub_ref, wdb_ref = next(it), next(it), next(it), next(it)
    if mix:
        wocb_ref, woab_ref = next(it), next(it)
    x_scr, h_scr, acc_scr = next(it), next(it), next(it)
    c = pl.program_id(0)

    @pl.when(c == 0)
    def _():
        x = x_ref[...]
        if mix:
            woc, woa = woc_ref[...].astype(BF16), woa_ref[...].astype(BF16)
            wocb_ref[...] = woc
            woab_ref[...] = woa
            x = x + _dot(yc_ref[...], woc) + _dot(ya_ref[...], woa)
        x_scr[...] = x
        h_scr[...] = _rms(x, g_ref[...]).astype(BF16)
        acc_scr[...] = jnp.zeros(acc_scr.shape, F32)

    wg, wu, wd = wg_ref[...].astype(BF16), wu_ref[...].astype(BF16), wd_ref[...].astype(BF16)
    wgb_ref[...] = wg
    wub_ref[...] = wu
    wdb_ref[...] = wd
    h = h_scr[...]
    gate = _dot(h, wg)
    act = (gate * jax.nn.sigmoid(gate) * _dot(h, wu)).astype(BF16)
    acc_scr[...] += _dot(act, wd)

    @pl.when(c == pl.num_programs(0) - 1)
    def _():
        x = x_scr[...] + 0.5 * acc_scr[...]
        if final:
            x = _rms(x, gf_ref[...])
        o_ref[...] = x


def _ffn_stream_call(x, g, w_gu, w_down, *, mix=None, final=None, name):
    n, d = x.shape
    d_ff = w_down.shape[0]
    tw = MXU_WIDTH
    assert d_ff % tw == 0 and w_gu.shape == (d, 2 * d_ff)
    nt = d_ff // tw
    full = lambda shape: pl.BlockSpec(shape, lambda c: (0,) * len(shape))
    ins, specs = [x], [full((n, d))]
    outs = [jax.ShapeDtypeStruct((n, d), F32), jax.ShapeDtypeStruct((d, d_ff), BF16),
            jax.ShapeDtypeStruct((d, d_ff), BF16), jax.ShapeDtypeStruct((d_ff, d), BF16)]
    out_specs = [full((n, d)), pl.BlockSpec((d, tw), lambda c: (0, c)), pl.BlockSpec((d, tw), lambda c: (0, c)),
                 pl.BlockSpec((tw, d), lambda c: (c, 0))]
    if mix is not None:
        yc, ya, w_out = mix
        dm = yc.shape[1]
        assert w_out.shape == (dm + ya.shape[1], d) and ya.shape[1] == dm
        ins += [yc, ya, w_out, w_out]
        specs += [full(yc.shape), full(ya.shape), pl.BlockSpec((dm, d), lambda c: (0, 0)),
                  pl.BlockSpec((dm, d), lambda c: (1, 0))]
        outs += [jax.ShapeDtypeStruct((dm, d), BF16)] * 2
        out_specs += [full((dm, d))] * 2
    ins += [g, w_gu, w_gu, w_down]
    specs += [full(g.shape), pl.BlockSpec((d, tw), lambda c: (0, c)), pl.BlockSpec((d, tw), lambda c: (0, nt + c)),
              pl.BlockSpec((tw, d), lambda c: (c, 0))]
    if final is not None:
        ins.append(final)
        specs.append(full(final.shape))
    body = functools.partial(_ffn_stream_body, mix=mix is not None, final=final is not None)
    return pl.pallas_call(
        body,
        grid=(nt,),
        in_specs=specs,
        out_specs=out_specs,
        out_shape=outs,
        scratch_shapes=[pltpu.VMEM((n, d), F32), pltpu.VMEM((n, d), BF16), pltpu.VMEM((n, d), F32)],
        compiler_params=pltpu.CompilerParams(dimension_semantics=("arbitrary",),
                                             vmem_limit_bytes=VMEM_LIMIT),
        name=name,
    )(*ins)


def _inproj_body(*refs, tm, tiles_per_seq, sample):
    it = iter(refs)
    x_ref, g_ref, win_ref, cw_ref, cn_ref, bd_ref = (next(it) for _ in range(6))
    if sample:
        s1_ref, s2_ref = next(it), next(it)
        q_ref, k_ref, v_ref, yc_ref, u_ref, winb_ref = (next(it) for _ in range(6))
    else:
        q_ref, kt_ref, vtf_ref, kb_ref, vt_ref, yc_ref, mean_ref, cnew_ref = (next(it) for _ in range(8))
    ubuf = next(it)

    dc = yc_ref.shape[1]
    if sample:
        ubuf[0:8, :] = jnp.zeros((8, dc), F32)
    else:
        first = (pl.program_id(0) % tiles_per_seq) == 0

        @pl.when(first)
        def _():
            ubuf[0:8, :] = jnp.zeros((8, dc), F32)

        @pl.when(jnp.logical_not(first))
        def _():
            ubuf[0:8, :] = ubuf[tm:tm + 8, :]

    h = _rms(x_ref[...], g_ref[...]).astype(BF16)
    if sample:
        def piece(c, w):
            wp = win_ref[:, c:c + w].astype(BF16)
            winb_ref[:, c:c + w] = wp
            return _dot(h, wp)
    else:
        piece = lambda c, w: _dot(h, win_ref[:, c:c + w])
    hc = piece(0, dc)
    cg = piece(2 * dc, dc)
    bg = piece(dc, dc)
    k = piece(3 * dc + D_ATT, D_ATT)
    v = piece(3 * dc + 2 * D_ATT, D_ATT)
    if sample:
        k_ref[...] = k
        v_ref[...] = v

    u = cg * hc
    ubuf[8:tm + 8, :] = u
    um1 = ubuf[7:tm + 7, :]
    um2 = ubuf[6:tm + 6, :]
    if sample:
        t = lax.broadcasted_iota(jnp.int32, (tm, dc), 0) % u_ref.shape[1]
        um1 = jnp.where(t >= 1, um1, s1_ref[...])
        um2 = jnp.where(t >= 2, um2, s2_ref[...])
    cw = cw_ref[...]
    conv = um2 * cw[0:1, :] + um1 * cw[1:2, :] + u * cw[2:3, :]
    yc_ref[...] = _group_rms(bg * conv, cn_ref[...], bd_ref).astype(BF16)
    q_ref[...] = piece(3 * dc, D_ATT)

    if sample:
        u_ref[...] = u.reshape(u_ref.shape)
    else:
        vt = v.T
        kt_ref[0] = k.T
        vtf_ref[0] = vt
        kb_ref[...] = k.astype(BF16)
        nblk = tm // MOBA_BLOCK
        for i in range(nblk):
            vt_ref[i] = vt[:, i * MOBA_BLOCK:(i + 1) * MOBA_BLOCK].astype(BF16)
        mean_ref[0] = jnp.sum(k.reshape(nblk, MOBA_BLOCK, D_ATT), axis=1) * (1.0 / MOBA_BLOCK)
        cnew_ref[0] = ubuf[tm + 6:tm + 8, :]


def _inproj_prompt(x, g, w_in, conv_w, conv_norm, bd, *, batch, tm):
    n, d = x.shape
    dc = conv_w.shape[1]
    seq = n // batch
    tps = seq // tm
    nblk = tm // MOBA_BLOCK
    row = lambda w: pl.BlockSpec((tm, w), lambda i: (i, 0))
    tok_minor = pl.BlockSpec((1, D_ATT, tm), lambda i: (i // tps, 0, i % tps))
    f = lambda w, dt: jax.ShapeDtypeStruct((n, w), dt)
    body = functools.partial(_inproj_body, tm=tm, tiles_per_seq=tps, sample=False)
    return pl.pallas_call(
        body,
        grid=(n // tm,),
        in_specs=[row(d), _const_spec(g.shape), _const_spec(w_in.shape), _const_spec(conv_w.shape),
                  _const_spec(conv_norm.shape), _const_spec(bd.shape)],
        out_specs=[row(D_ATT), tok_minor, tok_minor, row(D_ATT),
                   pl.BlockSpec((nblk, D_ATT, MOBA_BLOCK), lambda i: (i, 0, 0)), row(dc),
                   pl.BlockSpec((1, nblk, D_ATT), lambda i: (i, 0, 0)),
                   pl.BlockSpec((1, CONV_W - 1, dc), lambda i: (i // tps, 0, 0))],
        out_shape=[f(D_ATT, F32), jax.ShapeDtypeStruct((batch, D_ATT, seq), F32),
                   jax.ShapeDtypeStruct((batch, D_ATT, seq), F32), f(D_ATT, BF16),
                   jax.ShapeDtypeStruct((n // MOBA_BLOCK, D_ATT, MOBA_BLOCK), BF16), f(dc, BF16),
                   jax.ShapeDtypeStruct((n // tm, nblk, D_ATT), F32),
                   jax.ShapeDtypeStruct((batch, CONV_W - 1, dc), F32)],
        scratch_shapes=[pltpu.VMEM((tm + 8, dc), F32)],
        compiler_params=pltpu.CompilerParams(dimension_semantics=("arbitrary",),
                                             vmem_limit_bytes=VMEM_LIMIT),
        name="inproj_prompt",
    )(x, g, w_in, conv_w, conv_norm, bd)


def _inproj_sample(x, g, w_in, conv_w, conv_norm, bd, s1, s2, *, seq):
    n, d = x.shape
    dc = conv_w.shape[1]
    tm = n
    full = lambda shape: pl.BlockSpec(shape, lambda i: (0,) * len(shape))
    f = lambda w, dt: jax.ShapeDtypeStruct((n, w), dt)
    body = functools.partial(_inproj_body, tm=tm, tiles_per_seq=1, sample=True)
    return pl.pallas_call(
        body,
        grid=(1,),
        in_specs=[full((tm, d)), full(g.shape), full(w_in.shape), full(conv_w.shape),
                  full(conv_norm.shape), full(bd.shape), full((tm, dc)), full((tm, dc))],
        out_specs=[full((tm, D_ATT)), full((tm, D_ATT)), full((tm, D_ATT)), full((tm, dc)),
                   full((n // seq, seq, dc)), full(w_in.shape)],
        out_shape=[f(D_ATT, F32), f(D_ATT, F32), f(D_ATT, F32), f(dc, BF16),
                   jax.ShapeDtypeStruct((n // seq, seq, dc), F32), jax.ShapeDtypeStruct(w_in.shape, BF16)],
        scratch_shapes=[pltpu.VMEM((tm + 8, dc), F32)],
        compiler_params=pltpu.CompilerParams(dimension_semantics=("arbitrary",),
                                             vmem_limit_bytes=VMEM_LIMIT),
        name="inproj_sample",
    )(x, g, w_in, conv_w, conv_norm, bd, s1, s2)


def _split3(x):
    hi = x.astype(BF16).astype(F32)
    mid = (x - hi).astype(BF16).astype(F32)
    lo = (x - hi - mid).astype(BF16).astype(F32)
    return hi, mid, lo


def _attn_prompt_body(q_ref, kb_ref, vt_ref, mean_ref, gain_ref, o_ref,
                      causal_ref, featk_ref, qabt_ref, colb_ref, so_ref, seta_ref, setb_ref,
                      m_ref, l_ref, acc_ref):
    blk = MOBA_BLOCK
    b = pl.program_id(0)
    j = pl.program_id(1)
    nb = mean_ref.shape[1]
    group = LANES // N_HEADS
    qcols = HEADS_PER_SLAB * blk
    lane_q = lax.broadcasted_iota(jnp.int32, (1, qcols), 1)

    def slope_row(p):
        return jnp.where(lane_q < blk, LOG2E * _slope(HEADS_PER_SLAB * p), LOG2E * _slope(HEADS_PER_SLAB * p + 1))

    @pl.when((b == 0) & (j == 0))
    def _init_tables():
        kk = lax.broadcasted_iota(jnp.int32, (blk, qcols), 0)
        qq = lax.broadcasted_iota(jnp.int32, (blk, qcols), 1)
        causal_ref[...] = jnp.where((qq % blk) >= kk, 0.0, NEG_INF)
        ki = lax.broadcasted_iota(jnp.int32, (blk, LANES), 0).astype(F32)
        kl = lax.broadcasted_iota(jnp.int32, (blk, LANES), 1)
        featk_ref[...] = jnp.where(kl < 3, ki, jnp.where(kl < 6, 1.0, 0.0)).astype(BF16)
        fr = lax.broadcasted_iota(jnp.int32, (LANES, qcols), 0)
        for p in range(N_SLABS):
            a = slope_row(p)
            terms = _split3(a) + _split3(-a * (lane_q % blk).astype(F32))
            feat = jnp.zeros((LANES, qcols), F32)
            for r, t in enumerate(terms):
                feat = jnp.where(fr == r, t, feat)
            qabt_ref[p, LANES:, :] = feat.astype(BF16)

    qt = q_ref[...].T
    means = mean_ref[0]
    if nb < group:
        means = jnp.concatenate([means, jnp.zeros((group - nb, D_ATT), F32)], axis=0)
    mt = jnp.concatenate([means] * N_HEADS, axis=0)
    rh = lax.broadcasted_iota(jnp.int32, mt.shape, 0) // group
    ch = lax.broadcasted_iota(jnp.int32, mt.shape, 1) // HEAD_DIM
    mbd = jnp.where(rh == ch, mt, 0.0)
    m_hi = mbd.astype(BF16)
    q_hi = qt.astype(BF16)
    m2 = jnp.concatenate([m_hi, (mbd - m_hi.astype(F32)).astype(BF16)], axis=0)
    q2 = jnp.concatenate([q_hi, (qt - q_hi.astype(F32)).astype(BF16)], axis=1)
    g4 = _dot(m2, q2)
    hg = N_HEADS * group
    gate_t = (g4[:hg, :blk] + g4[hg:, blk:]) + (g4[:hg, blk:] + g4[hg:, :blk])
    gate = jnp.concatenate([gate_t[h * group:(h + 1) * group, :] for h in range(N_HEADS)], axis=1)

    n_idx = lax.broadcasted_iota(jnp.int32, gate.shape, 0)
    n_f = n_idx.astype(F32)
    valid = n_idx < j
    work = jnp.where(valid, gate, NEG_INF)
    picked = jnp.zeros(gate.shape, F32)
    for _ in range(MOBA_TOPK):
        top = jnp.max(work, axis=0, keepdims=True)
        first = jnp.min(jnp.where(work == top, n_f, float(group)), axis=0, keepdims=True)
        pick = n_f == first
        picked = jnp.where(pick, 1.0, picked)
        work = jnp.where(pick, REMOVED, work)
    colb_ref[...] = jnp.where((picked > 0.0) & valid, 0.0, NEG_INF)

    row_d = lax.broadcasted_iota(jnp.int32, (LANES, blk), 0)
    for p in range(N_SLABS):
        qs = qt[p * LANES:(p + 1) * LANES, :] * (SCALE * LOG2E)
        qa = jnp.where(row_d < HEAD_DIM, qs, 0.0)
        qb = jnp.where(row_d >= HEAD_DIM, qs, 0.0)
        qabt_ref[p, :LANES, :] = jnp.concatenate([qa, qb], axis=1).astype(BF16)

    slabs = [slice(p * LANES, (p + 1) * LANES) for p in range(N_SLABS)]

    def scores(n, p):
        off = pl.multiple_of(n * blk, blk)
        keys = jnp.concatenate([kb_ref[pl.ds(off, blk), slabs[p]], featk_ref[...]], axis=1)
        return _dot(keys, qabt_ref[p])

    sets = (seta_ref, setb_ref)

    def park_unit(n_first, count, p):
        for g in range(count):
            sets[p % 2][g] = scores(jnp.minimum(n_first + g, j), p)

    def weighted_values(blocks, p, e):
        vt = jnp.concatenate([vt_ref[n, slabs[p], :] for n in blocks], axis=1)
        ones = jnp.ones((SUM_ROWS, vt.shape[1]), BF16)
        pvs = [_dot(jnp.concatenate([vt[h * HEAD_DIM:(h + 1) * HEAD_DIM], ones], axis=0),
                    e[:, h * blk:(h + 1) * blk]) for h in range(HEADS_PER_SLAB)]
        return (jnp.concatenate([pv[:HEAD_DIM] for pv in pvs], axis=0),
                jnp.concatenate([pv[HEAD_DIM:HEAD_DIM + 1] for pv in pvs], axis=1))

    def by_head(row):
        return jnp.where(row_d < HEAD_DIM, row[:, :blk], row[:, blk:])

    def reduce_unit(n_first, count, p, track_max):
        cs = slice(p * qcols, (p + 1) * qcols)
        src = sets[p % 2]
        blocks = [n_first + g for g in range(count)]
        crows = [colb_ref[pl.ds(n, 1), cs] - slope_row(p) * ((j - n) * blk).astype(F32) for n in blocks]
        m_prev = m_ref[p]
        if track_max:
            m_new = m_prev
            for g, crow in enumerate(crows):
                m_new = jnp.maximum(m_new, jnp.max(src[g], axis=0, keepdims=True) + crow)
            alpha = jnp.exp2(m_prev - m_new)
            m_ref[p] = m_new
        else:
            m_new = m_prev
        e = jnp.concatenate([jnp.exp2(src[g] - (m_new - crow)).astype(BF16) for g, crow in enumerate(crows)],
                            axis=0)
        pv, esum = weighted_values(blocks, p, e)
        if track_max:
            l_ref[p] = alpha * l_ref[p] + esum
            acc_ref[p] = by_head(alpha) * acc_ref[p] + pv
        else:
            l_ref[p] = l_ref[p] + esum
            acc_ref[p] = acc_ref[p] + pv

    def sweep(n_first, count, n_after, track_max):
        for p in range(N_SLABS):
            if p + 1 < N_SLABS:
                park_unit(n_first, count, p + 1)
            elif n_after is not None:
                park_unit(n_after, UPDATE_BLOCKS, 0)
            reduce_unit(n_first, count, p, track_max)

    def attend(track_max):
        for p in range(N_SLABS):
            so_ref[p] = scores(j, p)
        park_unit(0, UPDATE_BLOCKS, 0)
        for p in range(N_SLABS):
            sb = so_ref[p] + causal_ref[...]
            m = jnp.max(sb, axis=0, keepdims=True)
            pv, esum = weighted_values([j], p, jnp.exp2(sb - m).astype(BF16))
            m_ref[p] = m
            l_ref[p] = esum
            acc_ref[p] = pv

        def trip(t, carry):
            sweep(UPDATE_BLOCKS * t, UPDATE_BLOCKS, UPDATE_BLOCKS * (t + 1), track_max)
            return carry

        lax.fori_loop(0, j // UPDATE_BLOCKS, trip, 0)

        nr = (j // UPDATE_BLOCKS) * UPDATE_BLOCKS
        for r in range(1, UPDATE_BLOCKS):
            @pl.when(j - nr == r)
            def _(r=r):
                sweep(nr, r, None, track_max)

    attend(track_max=False)
    l_top = l_ref[0]
    a_top = jnp.abs(acc_ref[0])
    for p in range(1, N_SLABS):
        l_top = jnp.maximum(l_top, l_ref[p])
        a_top = jnp.maximum(a_top, jnp.abs(acc_ref[p]))
    finite = (jnp.max(l_top) < FINITE_MAX) & (jnp.max(a_top) < FINITE_MAX)

    @pl.when(jnp.logical_not(finite))
    def _():
        attend(track_max=True)

    for p in range(N_SLABS):
        o2 = acc_ref[p] / by_head(l_ref[p])
        sq = o2 * o2
        ms_a = jnp.sum(sq[:HEAD_DIM], axis=0, keepdims=True) * (1.0 / HEAD_DIM)
        ms_b = jnp.sum(sq[HEAD_DIM:], axis=0, keepdims=True) * (1.0 / HEAD_DIM)
        inv = jnp.where(row_d < HEAD_DIM, lax.rsqrt(ms_a + EPS), lax.rsqrt(ms_b + EPS))
        ls = slice(p * LANES, (p + 1) * LANES)
        o_ref[:, ls] = ((o2 * inv).T * gain_ref[:, ls]).astype(BF16)


def _attn_prompt(q, kb, vt, means, gain, *, batch):
    n = q.shape[0]
    seq = n // batch
    nb = seq // MOBA_BLOCK
    group = LANES // N_HEADS
    assert nb <= group and seq % MOBA_BLOCK == 0
    means = means.reshape(batch, nb, D_ATT)
    qcols = HEADS_PER_SLAB * MOBA_BLOCK
    return pl.pallas_call(
        _attn_prompt_body,
        grid=(batch, nb),
        in_specs=[pl.BlockSpec((MOBA_BLOCK, D_ATT), lambda b, j: (b * nb + j, 0)),
                  pl.BlockSpec((seq, D_ATT), lambda b, j: (b, 0)),
                  pl.BlockSpec((nb, D_ATT, MOBA_BLOCK), lambda b, j: (b, 0, 0)),
                  pl.BlockSpec((1, nb, D_ATT), lambda b, j: (b, 0, 0)),
                  pl.BlockSpec(gain.shape, lambda b, j: (0, 0))],
        out_specs=pl.BlockSpec((MOBA_BLOCK, D_ATT), lambda b, j: (b * nb + j, 0)),
        out_shape=jax.ShapeDtypeStruct((n, D_ATT), BF16),
        scratch_shapes=[pltpu.VMEM((MOBA_BLOCK, qcols), F32),
                        pltpu.VMEM((MOBA_BLOCK, LANES), BF16),
                        pltpu.VMEM((N_SLABS, 2 * LANES, qcols), BF16),
                        pltpu.VMEM((group, N_HEADS * MOBA_BLOCK), F32),
                        pltpu.VMEM((N_SLABS, MOBA_BLOCK, qcols), F32),
                        pltpu.VMEM((UPDATE_BLOCKS, MOBA_BLOCK, qcols), F32),
                        pltpu.VMEM((UPDATE_BLOCKS, MOBA_BLOCK, qcols), F32),
                        pltpu.VMEM((N_SLABS, 1, qcols), F32),
                        pltpu.VMEM((N_SLABS, 1, qcols), F32),
                        pltpu.VMEM((N_SLABS, LANES, MOBA_BLOCK), F32)],
        compiler_params=pltpu.CompilerParams(dimension_semantics=("arbitrary", "arbitrary"),
                                             vmem_limit_bytes=VMEM_LIMIT),
        name="attn_prompt",
    )(q, kb, vt, means, gain)


def _attn_sample_half(pt_ref, q_ref, kn_ref, vn_ref, gain_ref, bd_ref, ckt_hbm, cvt_hbm, o_ref,
                      kbuf, vbuf, s_ref, acc_ref, stat_ref, ksem, vsem, *,
                      b, nbat, first, k_half, v_half, past_len, page, page_base):
    n_pages = past_len // page
    ppb = MOBA_BLOCK // page
    nb = past_len // MOBA_BLOCK
    t_new = q_ref.shape[1]
    rows = N_HEADS * t_new

    def page_copy(hbm, buf, sem, bb, pg):
        return pltpu.make_async_copy(hbm.at[page_base + pt_ref[bb, pg]], buf.at[pg], sem.at[pg])

    def start_all(hbm, buf, sem, bb):
        def body(i, c):
            for k in range(DMA_THREADS):
                page_copy(hbm, buf, sem, bb, i * DMA_THREADS + k).start(priority=k)
            return c
        lax.fori_loop(0, n_pages // DMA_THREADS, body, 0)

    def refill(hbm, buf, sem, pgs):
        @pl.when(b + 1 < nbat)
        def _():
            for k, pg in enumerate(pgs):
                page_copy(hbm, buf, sem, b + 1, pg).start(priority=k % DMA_THREADS)

    @pl.when(first)
    def _():
        start_all(ckt_hbm, kbuf, ksem, b)
        start_all(cvt_hbm, vbuf, vsem, b)

    shared = dict(b=b, page_copy=page_copy, refill=refill, t_new=t_new, rows=rows, n_pages=n_pages, nb=nb,
                  ppb=ppb, page=page, past_len=past_len, s_ref=s_ref, stat_ref=stat_ref)
    pl.when(k_half)(functools.partial(_sample_k_half, q_ref, kn_ref, kbuf, ksem, ckt_hbm, **shared))
    pl.when(v_half)(functools.partial(_sample_v_half, vn_ref, gain_ref, bd_ref, o_ref, vbuf, vsem, cvt_hbm, acc_ref,
                                      **shared))


def _sample_k_half(q_ref, kn_ref, kbuf, ksem, ckt_hbm, *, b, page_copy, refill, t_new, rows, n_pages, nb, ppb, page,
                   past_len, s_ref, stat_ref):
    qt = jnp.concatenate([q_ref[0]] * N_HEADS, axis=0)
    rh = lax.broadcasted_iota(jnp.int32, qt.shape, 0) // t_new
    ch = lax.broadcasted_iota(jnp.int32, qt.shape, 1) // HEAD_DIM
    qs = jnp.where(rh == ch, qt, 0.0) * SCALE
    q_hi = qs.astype(BF16)
    q_lo = (qs - q_hi.astype(F32)).astype(BF16)
    qq = jnp.concatenate([q_hi, q_lo], axis=0)

    def k_tile(i, c):
        pgs = [i * K_TILE + k for k in range(K_TILE)]
        for pg in pgs:
            page_copy(ckt_hbm, kbuf, ksem, b, pg).wait()
        for pg in pgs:
            s2 = _dot(qq, kbuf[pg].reshape(D_ATT, page).astype(BF16))
            s_ref[pg] = s2[:rows] + s2[rows:]
        refill(ckt_hbm, kbuf, ksem, pgs)
        return c
    lax.fori_loop(0, n_pages // K_TILE, k_tile, 0)

    lane = lax.broadcasted_iota(jnp.int32, (rows, LANES), 1)
    gate = jnp.zeros((rows, LANES), F32)
    gcols = []
    for n in range(nb):
        tot = s_ref[n * ppb]
        for i in range(1, ppb):
            tot = tot + s_ref[n * ppb + i]
        g = jnp.sum(tot, axis=1, keepdims=True)
        gcols.append(g)
        gate = jnp.where(lane == n, g, gate)
    rank = jnp.zeros(gate.shape, jnp.int32)
    for m in range(nb):
        beats = (gcols[m] > gate) | ((gcols[m] == gate) & (lane > m))
        rank = rank + beats.astype(jnp.int32)
    colb = jnp.where(rank < MOBA_TOPK, 0.0, NEG_INF)

    r1 = lax.broadcasted_iota(jnp.int32, (rows, 1), 0)
    tq = r1 % t_new
    slope = jnp.zeros((rows, 1), F32)
    for h in range(N_HEADS):
        slope = jnp.where(r1 // t_new == h, _slope(h), slope)
    in_page = slope * (tq - lane).astype(F32)

    zpad = jnp.zeros((LANES - t_new, D_ATT), F32)
    kn = jnp.concatenate([kn_ref[0], zpad], axis=0).astype(BF16)
    s2 = _dot_nt(qq, kn)
    s_own = jnp.where(lane <= tq, s2[:rows] + s2[rows:] - in_page, NEG_INF)

    mrun = s_own
    for n in range(nb):
        mask_n = jnp.sum(jnp.where(lane == n, colb, 0.0), axis=1, keepdims=True)
        for i in range(ppb):
            pg = n * ppb + i
            sn = s_ref[pg] - in_page + (mask_n - slope * float(past_len - pg * page))
            s_ref[pg] = sn
            mrun = jnp.maximum(mrun, sn)
    m = jnp.max(mrun, axis=1, keepdims=True)

    e_own = jnp.exp(s_own - m)
    lrun = e_own
    for pg in range(n_pages):
        e = jnp.exp(s_ref[pg] - m)
        s_ref[pg] = e
        lrun = lrun + e
    stat_ref[0] = jnp.broadcast_to(jnp.sum(lrun, axis=1, keepdims=True), (rows, LANES))
    stat_ref[1] = e_own


def _sample_v_half(vn_ref, gain_ref, bd_ref, o_ref, vbuf, vsem, cvt_hbm, acc_ref, *, b, page_copy, refill, t_new, rows,
                   n_pages, nb, ppb, page, past_len, s_ref, stat_ref):
    l = stat_ref[0][:, :1]
    e_own = stat_ref[1]
    zpad = jnp.zeros((LANES - t_new, D_ATT), F32)
    vn = jnp.concatenate([vn_ref[0], zpad], axis=0).astype(BF16)

    acc_ref[...] = jnp.zeros(acc_ref.shape, F32)
    zrows = jnp.zeros((LANES - rows, V_TILE * page), BF16)

    def v_tile(i, c):
        pgs = [i * V_TILE + k for k in range(V_TILE)]
        for pg in pgs:
            page_copy(cvt_hbm, vbuf, vsem, b, pg).wait()
        vt = jnp.concatenate([vbuf[pg].reshape(D_ATT, page) for pg in pgs], axis=1).astype(BF16)
        p = jnp.concatenate([s_ref[pg] for pg in pgs], axis=1).astype(BF16)
        acc_ref[...] += _dot_nt(vt, jnp.concatenate([p, zrows], axis=0))
        refill(cvt_hbm, vbuf, vsem, pgs)
        return c
    lax.fori_loop(0, n_pages // V_TILE, v_tile, 0)

    acc = acc_ref[...].T[:rows] + _dot(e_own.astype(BF16), vn)
    accn = acc / l
    ch8 = lax.broadcasted_iota(jnp.int32, (t_new, D_ATT), 1) // HEAD_DIM
    out = jnp.zeros((t_new, D_ATT), F32)
    for h in range(N_HEADS):
        out = jnp.where(ch8 == h, accn[h * t_new:(h + 1) * t_new, :], out)
    o_ref[0] = _group_rms(out, gain_ref[...], bd_ref).astype(BF16)


N_PROMPT_SCRATCH = 10


def _attn_both_body(pt_ref, q_ref, kb_ref, vt_ref, mean_ref, gain_ref, qs_ref, kn_ref, vn_ref, bd_ref, ckt_hbm, cvt_hbm,
                    o_ref, os_ref, *scratch, nseq, past_len, page, page_base):
    _attn_prompt_body(q_ref, kb_ref, vt_ref, mean_ref, gain_ref, o_ref, *scratch[:N_PROMPT_SCRATCH])
    step = pl.program_id(0) * pl.num_programs(1) + pl.program_id(1)
    seq = step // 2
    live = seq < nseq
    _attn_sample_half(pt_ref, qs_ref, kn_ref, vn_ref, gain_ref, bd_ref, ckt_hbm, cvt_hbm, os_ref,
                      *scratch[N_PROMPT_SCRATCH:], b=jnp.minimum(seq, nseq - 1), nbat=nseq, first=step == 0,
                      k_half=live & (step % 2 == 0), v_half=live & (step % 2 == 1),
                      past_len=past_len, page=page, page_base=page_base)


def _attn_both(page_table, q, kb, vt, means, gain, qs, kn, vn, bd, cache_kt, cache_vt, *, batch, page_base, past_len):
    n = q.shape[0]
    seq = n // batch
    nb = seq // MOBA_BLOCK
    group = LANES // N_HEADS
    assert nb <= group and seq % MOBA_BLOCK == 0
    means = means.reshape(batch, nb, D_ATT)
    qcols = HEADS_PER_SLAB * MOBA_BLOCK
    nseq, t_new, _ = qs.shape
    page = cache_kt.shape[3]
    n_pages = past_len // page
    rows = N_HEADS * t_new
    assert past_len % MOBA_BLOCK == 0 and MOBA_BLOCK % page == 0 and page == LANES
    assert rows <= LANES and t_new % 8 == 0 and past_len // MOBA_BLOCK <= LANES
    assert n_pages % V_TILE == 0 and n_pages % K_TILE == 0
    assert 2 * nseq <= batch * nb
    body = functools.partial(_attn_both_body, nseq=nseq, past_len=past_len, page=page, page_base=page_base)
    blk_row = pl.BlockSpec((MOBA_BLOCK, D_ATT), lambda b, j, pt: (b * nb + j, 0))
    per_seq = pl.BlockSpec((1, t_new, D_ATT), lambda b, j, pt: (jnp.minimum((b * nb + j) // 2, nseq - 1), 0, 0))
    once = lambda shape: pl.BlockSpec(shape, lambda b, j, pt: (0,) * len(shape))
    grid_spec = pltpu.PrefetchScalarGridSpec(
        num_scalar_prefetch=1,
        grid=(batch, nb),
        in_specs=[blk_row,
                  pl.BlockSpec((seq, D_ATT), lambda b, j, pt: (b, 0)),
                  pl.BlockSpec((nb, D_ATT, MOBA_BLOCK), lambda b, j, pt: (b, 0, 0)),
                  pl.BlockSpec((1, nb, D_ATT), lambda b, j, pt: (b, 0, 0)),
                  once(gain.shape),
                  per_seq, per_seq, per_seq, once(bd.shape),
                  pl.BlockSpec(memory_space=pl.ANY),
                  pl.BlockSpec(memory_space=pl.ANY)],
        out_specs=[blk_row, per_seq],
        scratch_shapes=[pltpu.VMEM((MOBA_BLOCK, qcols), F32),
                        pltpu.VMEM((MOBA_BLOCK, LANES), BF16),
                        pltpu.VMEM((N_SLABS, 2 * LANES, qcols), BF16),
                        pltpu.VMEM((group, N_HEADS * MOBA_BLOCK), F32),
                        pltpu.VMEM((N_SLABS, MOBA_BLOCK, qcols), F32),
                        pltpu.VMEM((UPDATE_BLOCKS, MOBA_BLOCK, qcols), F32),
                        pltpu.VMEM((UPDATE_BLOCKS, MOBA_BLOCK, qcols), F32),
                        pltpu.VMEM((N_SLABS, 1, qcols), F32),
                        pltpu.VMEM((N_SLABS, 1, qcols), F32),
                        pltpu.VMEM((N_SLABS, LANES, MOBA_BLOCK), F32),
                        pltpu.VMEM((n_pages, N_HEADS, HEAD_DIM, page), F32),
                        pltpu.VMEM((n_pages, N_HEADS, HEAD_DIM, page), F32),
                        pltpu.VMEM((n_pages, rows, page), F32),
                        pltpu.VMEM((D_ATT, LANES), F32),
                        pltpu.VMEM((2, rows, LANES), F32),
                        pltpu.SemaphoreType.DMA((n_pages,)),
                        pltpu.SemaphoreType.DMA((n_pages,))],
    )
    return pl.pallas_call(
        body,
        grid_spec=grid_spec,
        out_shape=[jax.ShapeDtypeStruct((n, D_ATT), BF16), jax.ShapeDtypeStruct((nseq, t_new, D_ATT), BF16)],
        compiler_params=pltpu.CompilerParams(dimension_semantics=("arbitrary", "arbitrary"),
                                             vmem_limit_bytes=ATTN_VMEM_LIMIT),
        name="attn_both",
    )(page_table, q, kb, vt, means, gain, qs, kn, vn, bd, cache_kt, cache_vt)


def kernel(x_prompt, x_sample, cache_k, cache_v, state_conv, page_table, ffn1_norm, ffn1_w_gu, ffn1_w_down,
           mix_norm, w_in, conv_w, conv_out_norm, attn_out_norm, w_out, ffn2_norm, ffn2_w_gu, ffn2_w_down,
           final_norm):
    bp, seq, d = x_prompt.shape
    bs, dseq, _ = x_sample.shape
    depth, n_pool, page = cache_k.shape[:3]
    dc = conv_w.shape[2]
    past_len = page_table.shape[1] * page

    ck = jnp.transpose(cache_k, (0, 1, 3, 4, 2)).reshape(depth * n_pool, N_HEADS, HEAD_DIM, page)
    cv = jnp.transpose(cache_v, (0, 1, 3, 4, 2)).reshape(depth * n_pool, N_HEADS, HEAD_DIM, page)
    gi = lax.broadcasted_iota(jnp.int32, (D_ATT, D_ATT), 0) // HEAD_DIM
    gj = lax.broadcasted_iota(jnp.int32, (D_ATT, D_ATT), 1) // HEAD_DIM
    bd = (gi == gj).astype(BF16)

    xp = x_prompt.reshape(bp * seq, d)
    xs = x_sample.reshape(bs * dseq, d)
    tm_p = PROMPT_ROW_TILE
    assert seq % tm_p == 0 and seq % INPROJ_ROW_TILE == 0 and INPROJ_ROW_TILE % MOBA_BLOCK == 0
    assert dc // N_CONV_GROUPS == HEAD_DIM and dc == D_ATT
    row = lambda a: a.reshape(1, -1)
    outs = [[] for _ in range(6)]
    for l in range(depth):
        g1, gm, g2 = row(ffn1_norm[l]), row(mix_norm[l]), row(ffn2_norm[l])
        gc, ga = row(conv_out_norm[l]), row(attn_out_norm[l])
        last = l == depth - 1
        gfin = row(final_norm) if last else None

        st = state_conv[l]
        zpad = jnp.zeros((bs, dseq - (CONV_W - 1), dc), F32)
        s2 = jnp.concatenate([st, zpad], axis=1).reshape(bs * dseq, dc)
        s1 = jnp.concatenate([st[:, 1:2], jnp.zeros((bs, dseq - 1, dc), F32)], axis=1).reshape(bs * dseq, dc)
        x1s, wg1, wu1, wd1 = _ffn_stream_call(xs, g1, ffn1_w_gu[l], ffn1_w_down[l], name="ffn1_sample")
        qs, ks, vs, ycs, us, win = _inproj_sample(x1s, gm, w_in[l], conv_w[l], gc, bd, s1, s2, seq=dseq)
        outs[3].append(ks.reshape(bs, dseq, N_HEADS, HEAD_DIM))
        outs[4].append(vs.reshape(bs, dseq, N_HEADS, HEAD_DIM))
        outs[5].append(us[:, dseq - (CONV_W - 1):, :])

        x1 = _ffn_call(xp, g1, wg1, wu1, wd1, tm=tm_p, name="ffn1_prompt")
        q, kt, vtf, kb, vt, yc, means, cnew = _inproj_prompt(x1, gm, win, conv_w[l], gc, bd, batch=bp,
                                                             tm=INPROJ_ROW_TILE)
        r3 = lambda a: a.reshape(bs, dseq, D_ATT)
        ya, yas = _attn_both(page_table, q, kb, vt, means, ga, r3(qs), r3(ks), r3(vs), bd, ck, cv,
                             batch=bp, page_base=l * n_pool, past_len=past_len)
        xs, wg2, wu2, wd2, woc, woa = _ffn_stream_call(
            x1s, g2, ffn2_w_gu[l], ffn2_w_down[l], mix=(ycs, yas.reshape(bs * dseq, D_ATT), w_out[l]),
            final=gfin, name="ffn2_sample")
        xp = _ffn_call(x1, g2, wg2, wu2, wd2, tm=tm_p, mix=(yc, ya, woc, woa), final=gfin, name="ffn2_prompt")
        tok_major = lambda a: a.reshape(bp, N_HEADS, HEAD_DIM, seq).transpose(0, 3, 1, 2)
        outs[0].append(tok_major(kt))
        outs[1].append(tok_major(vtf))
        outs[2].append(cnew)

    y_prompt = xp.reshape(bp, seq, d)
    y_sample = xs.reshape(bs, dseq, d)
    kp, vp, cp, ksn, vsn, csn = (jnp.stack(o) for o in outs)
    return (y_prompt, y_sample, kp, vp, cp, ksn, vsn, csn)
```

```python
import functools

import jax
import jax.numpy as jnp
from jax import lax
from jax.experimental import pallas as pl
from jax.experimental.pallas import tpu as pltpu

F32 = jnp.float32
BF16 = jnp.bfloat16

N_HEADS = 8
HEAD_DIM = 64
D_ATT = N_HEADS * HEAD_DIM
N_CONV_GROUPS = 8
CONV_W = 3
MOBA_BLOCK = 256
MOBA_TOPK = 3
EPS = 1e-5
NEG_INF = -1e30
FINITE_MAX = 3.0e38
REMOVED = -3e38
SCALE = HEAD_DIM ** -0.5
LOG2E = 1.4426950408889634

LANES = 128
MXU_WIDTH = 256
HEADS_PER_SLAB = LANES // HEAD_DIM
N_SLABS = D_ATT // LANES
VMEM_LIMIT = 56 * 1024 * 1024
ATTN_VMEM_LIMIT = 62 * 1024 * 1024
PROMPT_ROW_TILE = 512
INPROJ_ROW_TILE = 1024
UPDATE_BLOCKS = 2
SUM_ROWS = 16
DMA_THREADS = 2
K_TILE = 32
V_TILE = 32

NT_DIMS = (((1,), (1,)), ((), ()))


def _slope(h):
    return 2.0 ** (-(8.0 / N_HEADS) * (h + 1))


def _dot(a, b):
    return jnp.dot(a, b, preferred_element_type=F32)


def _dot_nt(a, b):
    return lax.dot_general(a, b, NT_DIMS, preferred_element_type=F32)


def _rms(x, g):
    ms = jnp.mean(x * x, axis=-1, keepdims=True)
    return x * lax.rsqrt(ms + EPS) * g


def _group_sumsq(y, bd_ref):
    y2 = y * y
    hi = y2.astype(BF16)
    lo = (y2 - hi.astype(F32)).astype(BF16)
    bd = bd_ref[...]
    return _dot(hi, bd) + _dot(lo, bd)


def _group_rms(y, g, bd_ref):
    ms = _group_sumsq(y, bd_ref) * (1.0 / HEAD_DIM)
    return y * lax.rsqrt(ms + EPS) * g


def _const_spec(shape):
    nd = len(shape)
    return pl.BlockSpec(shape, lambda *_: (0,) * nd, pipeline_mode=pl.Buffered(1))


def _ffn_body(*refs, mix, final, bounds):
    it = iter(refs)
    x_ref = next(it)
    if mix:
        yc_ref, ya_ref, woc_ref, woa_ref = next(it), next(it), next(it), next(it)
    g_ref, wg_ref, wu_ref, wd_ref = next(it), next(it), next(it), next(it)
    gf_ref = next(it) if final else None
    o_ref = next(it)

    x = x_ref[...]
    if mix:
        x = x + _dot(yc_ref[...], woc_ref[...]) + _dot(ya_ref[...], woa_ref[...])
    h = _rms(x, g_ref[...]).astype(BF16)
    acc = jnp.zeros(x.shape, F32)
    for lo, hi in zip(bounds[:-1], bounds[1:]):
        gate = _dot(h, wg_ref[:, lo:hi])
        up = _dot(h, wu_ref[:, lo:hi])
        act = (gate * jax.nn.sigmoid(gate) * up).astype(BF16)
        acc = acc + _dot(act, wd_ref[lo:hi, :])
    x = x + 0.5 * acc
    if final:
        x = _rms(x, gf_ref[...])
    o_ref[...] = x


def _ffn_call(x, g, wg, wu, wd, *, tm, mix=None, final=None, name):
    n, d = x.shape
    d_ff = wd.shape[0]
    row = lambda w: pl.BlockSpec((tm, w), lambda i: (i, 0))
    ins, specs = [x], [row(d)]
    if mix is not None:
        yc, ya, woc, woa = mix
        ins += [yc, ya, woc, woa]
        specs += [row(yc.shape[1]), row(ya.shape[1]), _const_spec(woc.shape), _const_spec(woa.shape)]
    ins += [g, wg, wu, wd]
    specs += [_const_spec(g.shape), _const_spec(wg.shape), _const_spec(wu.shape), _const_spec(wd.shape)]
    if final is not None:
        ins.append(final)
        specs.append(_const_spec(final.shape))
    assert d_ff % MXU_WIDTH == 0
    tiles = d_ff // MXU_WIDTH
    bounds = (0, (tiles + 1) // 2 * MXU_WIDTH, d_ff)
    body = functools.partial(_ffn_body, mix=mix is not None, final=final is not None, bounds=bounds)
    return pl.pallas_call(
        body,
        grid=(n // tm,),
        in_specs=specs,
        out_specs=row(d),
        out_shape=jax.ShapeDtypeStruct((n, d), F32),
        compiler_params=pltpu.CompilerParams(dimension_semantics=("arbitrary",),
                                             vmem_limit_bytes=VMEM_LIMIT),
        name=name,
    )(*ins)


def _ffn_stream_body(*refs, mix, final):
    it = iter(refs)
    x_ref = next(it)
    if mix:
        yc_ref, ya_ref, woc_ref, woa_ref = next(it), next(it), next(it), next(it)
    g_ref, wg_ref, wu_ref, wd_ref = next(it), next(it), next(it), next(it)
    gf_ref = next(it) if final else None
    o_ref, wgb_ref, wub_ref, wdb_ref = next(it), next(it), next(it), next(it)
    if mix:
        wocb_ref, woab_ref = next(it), next(it)
    x_scr, h_scr, acc_scr = next(it), next(it), next(it)
    c = pl.program_id(0)

    @pl.when(c == 0)
    def _():
        x = x_ref[...]
        if mix:
            woc, woa = woc_ref[...].astype(BF16), woa_ref[...].astype(BF16)
            wocb_ref[...] = woc
            woab_ref[...] = woa
            x = x + _dot(yc_ref[...], woc) + _dot(ya_ref[...], woa)
        x_scr[...] = x
        h_scr[...] = _rms(x, g_ref[...]).astype(BF16)
        acc_scr[...] = jnp.zeros(acc_scr.shape, F32)

    wg, wu, wd = wg_ref[...].astype(BF16), wu_ref[...].astype(BF16), wd_ref[...].astype(BF16)
    wgb_ref[...] = wg
    wub_ref[...] = wu
    wdb_ref[...] = wd
    h = h_scr[...]
    gate = _dot(h, wg)
    act = (gate * jax.nn.sigmoid(gate) * _dot(h, wu)).astype(BF16)
    acc_scr[...] += _dot(act, wd)

    @pl.when(c == pl.num_programs(0) - 1)
    def _():
        x = x_scr[...] + 0.5 * acc_scr[...]
        if final:
            x = _rms(x, gf_ref[...])
        o_ref[...] = x


def _ffn_stream_call(x, g, w_gu, w_down, *, mix=None, final=None, name):
    n, d = x.shape
    d_ff = w_down.shape[0]
    tw = MXU_WIDTH
    assert d_ff % tw == 0 and w_gu.shape == (d, 2 * d_ff)
    nt = d_ff // tw
    full = lambda shape: pl.BlockSpec(shape, lambda c: (0,) * len(shape))
    ins, specs = [x], [full((n, d))]
    outs = [jax.ShapeDtypeStruct((n, d), F32), jax.ShapeDtypeStruct((d, d_ff), BF16),
            jax.ShapeDtypeStruct((d, d_ff), BF16), jax.ShapeDtypeStruct((d_ff, d), BF16)]
    out_specs = [full((n, d)), pl.BlockSpec((d, tw), lambda c: (0, c)), pl.BlockSpec((d, tw), lambda c: (0, c)),
                 pl.BlockSpec((tw, d), lambda c: (c, 0))]
    if mix is not None:
        yc, ya, w_out = mix
        dm = yc.shape[1]
        assert w_out.shape == (dm + ya.shape[1], d) and ya.shape[1] == dm
        ins += [yc, ya, w_out, w_out]
        specs += [full(yc.shape), full(ya.shape), pl.BlockSpec((dm, d), lambda c: (0, 0)),
                  pl.BlockSpec((dm, d), lambda c: (1, 0))]
        outs += [jax.ShapeDtypeStruct((dm, d), BF16)] * 2
        out_specs += [full((dm, d))] * 2
    ins += [g, w_gu, w_gu, w_down]
    specs += [full(g.shape), pl.BlockSpec((d, tw), lambda c: (0, c)), pl.BlockSpec((d, tw), lambda c: (0, nt + c)),
              pl.BlockSpec((tw, d), lambda c: (c, 0))]
    if final is not None:
        ins.append(final)
        specs.append(full(final.shape))
    body = functools.partial(_ffn_stream_body, mix=mix is not None, final=final is not None)
    return pl.pallas_call(
        body,
        grid=(nt,),
        in_specs=specs,
        out_specs=out_specs,
        out_shape=outs,
        scratch_shapes=[pltpu.VMEM((n, d), F32), pltpu.VMEM((n, d), BF16), pltpu.VMEM((n, d), F32)],
        compiler_params=pltpu.CompilerParams(dimension_semantics=("arbitrary",),
                                             vmem_limit_bytes=VMEM_LIMIT),
        name=name,
    )(*ins)


def _inproj_body(*refs, tm, tiles_per_seq, sample):
    it = iter(refs)
    x_ref, g_ref, win_ref, cw_ref, cn_ref, bd_ref = (next(it) for _ in range(6))
    if sample:
        s1_ref, s2_ref = next(it), next(it)
        q_ref, k_ref, v_ref, yc_ref, u_ref, winb_ref = (next(it) for _ in range(6))
    else:
        q_ref, kt_ref, vtf_ref, kb_ref, vt_ref, yc_ref, mean_ref, cnew_ref = (next(it) for _ in range(8))
    ubuf = next(it)

    dc = yc_ref.shape[1]
    if sample:
        ubuf[0:8, :] = jnp.zeros((8, dc), F32)
    else:
        first = (pl.program_id(0) % tiles_per_seq) == 0

        @pl.when(first)
        def _():
            ubuf[0:8, :] = jnp.zeros((8, dc), F32)

        @pl.when(jnp.logical_not(first))
        def _():
            ubuf[0:8, :] = ubuf[tm:tm + 8, :]

    h = _rms(x_ref[...], g_ref[...]).astype(BF16)
    if sample:
        def piece(c, w):
            wp = win_ref[:, c:c + w].astype(BF16)
            winb_ref[:, c:c + w] = wp
            return _dot(h, wp)
    else:
        piece = lambda c, w: _dot(h, win_ref[:, c:c + w])
    hc = piece(0, dc)
    cg = piece(2 * dc, dc)
    bg = piece(dc, dc)
    k = piece(3 * dc + D_ATT, D_ATT)
    v = piece(3 * dc + 2 * D_ATT, D_ATT)
    if sample:
        k_ref[...] = k
        v_ref[...] = v

    u = cg * hc
    ubuf[8:tm + 8, :] = u
    um1 = ubuf[7:tm + 7, :]
    um2 = ubuf[6:tm + 6, :]
    if sample:
        t = lax.broadcasted_iota(jnp.int32, (tm, dc), 0) % u_ref.shape[1]
        um1 = jnp.where(t >= 1, um1, s1_ref[...])
        um2 = jnp.where(t >= 2, um2, s2_ref[...])
    cw = cw_ref[...]
    conv = um2 * cw[0:1, :] + um1 * cw[1:2, :] + u * cw[2:3, :]
    yc_ref[...] = _group_rms(bg * conv, cn_ref[...], bd_ref).astype(BF16)
    q_ref[...] = piece(3 * dc, D_ATT)

    if sample:
        u_ref[...] = u.reshape(u_ref.shape)
    else:
        vt = v.T
        kt_ref[0] = k.T
        vtf_ref[0] = vt
        kb_ref[...] = k.astype(BF16)
        nblk = tm // MOBA_BLOCK
        for i in range(nblk):
            vt_ref[i] = vt[:, i * MOBA_BLOCK:(i + 1) * MOBA_BLOCK].astype(BF16)
        mean_ref[0] = jnp.sum(k.reshape(nblk, MOBA_BLOCK, D_ATT), axis=1) * (1.0 / MOBA_BLOCK)
        cnew_ref[0] = ubuf[tm + 6:tm + 8, :]


def _inproj_prompt(x, g, w_in, conv_w, conv_norm, bd, *, batch, tm):
    n, d = x.shape
    dc = conv_w.shape[1]
    seq = n // batch
    tps = seq // tm
    nblk = tm // MOBA_BLOCK
    row = lambda w: pl.BlockSpec((tm, w), lambda i: (i, 0))
    tok_minor = pl.BlockSpec((1, D_ATT, tm), lambda i: (i // tps, 0, i % tps))
    f = lambda w, dt: jax.ShapeDtypeStruct((n, w), dt)
    body = functools.partial(_inproj_body, tm=tm, tiles_per_seq=tps, sample=False)
    return pl.pallas_call(
        body,
        grid=(n // tm,),
        in_specs=[row(d), _const_spec(g.shape), _const_spec(w_in.shape), _const_spec(conv_w.shape),
                  _const_spec(conv_norm.shape), _const_spec(bd.shape)],
        out_specs=[row(D_ATT), tok_minor, tok_minor, row(D_ATT),
                   pl.BlockSpec((nblk, D_ATT, MOBA_BLOCK), lambda i: (i, 0, 0)), row(dc),
                   pl.BlockSpec((1, nblk, D_ATT), lambda i: (i, 0, 0)),
                   pl.BlockSpec((1, CONV_W - 1, dc), lambda i: (i // tps, 0, 0))],
        out_shape=[f(D_ATT, F32), jax.ShapeDtypeStruct((batch, D_ATT, seq), F32),
                   jax.ShapeDtypeStruct((batch, D_ATT, seq), F32), f(D_ATT, BF16),
                   jax.ShapeDtypeStruct((n // MOBA_BLOCK, D_ATT, MOBA_BLOCK), BF16), f(dc, BF16),
                   jax.ShapeDtypeStruct((n // tm, nblk, D_ATT), F32),
                   jax.ShapeDtypeStruct((batch, CONV_W - 1, dc), F32)],
        scratch_shapes=[pltpu.VMEM((tm + 8, dc), F32)],
        compiler_params=pltpu.CompilerParams(dimension_semantics=("arbitrary",),
                                             vmem_limit_bytes=VMEM_LIMIT),
        name="inproj_prompt",
    )(x, g, w_in, conv_w, conv_norm, bd)


def _inproj_sample(x, g, w_in, conv_w, conv_norm, bd, s1, s2, *, seq):
    n, d = x.shape
    dc = conv_w.shape[1]
    tm = n
    full = lambda shape: pl.BlockSpec(shape, lambda i: (0,) * len(shape))
    f = lambda w, dt: jax.ShapeDtypeStruct((n, w), dt)
    body = functools.partial(_inproj_body, tm=tm, tiles_per_seq=1, sample=True)
    return pl.pallas_call(
        body,
        grid=(1,),
        in_specs=[full((tm, d)), full(g.shape), full(w_in.shape), full(conv_w.shape),
                  full(conv_norm.shape), full(bd.shape), full((tm, dc)), full((tm, dc))],
        out_specs=[full((tm, D_ATT)), full((tm, D_ATT)), full((tm, D_ATT)), full((tm, dc)),
                   full((n // seq, seq, dc)), full(w_in.shape)],
        out_shape=[f(D_ATT, F32), f(D_ATT, F32), f(D_ATT, F32), f(dc, BF16),
                   jax.ShapeDtypeStruct((n // seq, seq, dc), F32), jax.ShapeDtypeStruct(w_in.shape, BF16)],
        scratch_shapes=[pltpu.VMEM((tm + 8, dc), F32)],
        compiler_params=pltpu.CompilerParams(dimension_semantics=("arbitrary",),
                                             vmem_limit_bytes=VMEM_LIMIT),
        name="inproj_sample",
    )(x, g, w_in, conv_w, conv_norm, bd, s1, s2)


def _split3(x):
    hi = x.astype(BF16).astype(F32)
    mid = (x - hi).astype(BF16).astype(F32)
    lo = (x - hi - mid).astype(BF16).astype(F32)
    return hi, mid, lo


def _attn_prompt_body(q_ref, kb_ref, vt_ref, mean_ref, gain_ref, o_ref,
                      causal_ref, featk_ref, qabt_ref, colb_ref, so_ref, seta_ref, setb_ref,
                      m_ref, l_ref, acc_ref):
    blk = MOBA_BLOCK
    b = pl.program_id(0)
    j = pl.program_id(1)
    nb = mean_ref.shape[1]
    group = LANES // N_HEADS
    qcols = HEADS_PER_SLAB * blk
    lane_q = lax.broadcasted_iota(jnp.int32, (1, qcols), 1)

    def slope_row(p):
        return jnp.where(lane_q < blk, LOG2E * _slope(HEADS_PER_SLAB * p), LOG2E * _slope(HEADS_PER_SLAB * p + 1))

    @pl.when((b == 0) & (j == 0))
    def _init_tables():
        kk = lax.broadcasted_iota(jnp.int32, (blk, qcols), 0)
        qq = lax.broadcasted_iota(jnp.int32, (blk, qcols), 1)
        causal_ref[...] = jnp.where((qq % blk) >= kk, 0.0, NEG_INF)
        ki = lax.broadcasted_iota(jnp.int32, (blk, LANES), 0).astype(F32)
        kl = lax.broadcasted_iota(jnp.int32, (blk, LANES), 1)
        featk_ref[...] = jnp.where(kl < 3, ki, jnp.where(kl < 6, 1.0, 0.0)).astype(BF16)
        fr = lax.broadcasted_iota(jnp.int32, (LANES, qcols), 0)
        for p in range(N_SLABS):
            a = slope_row(p)
            terms = _split3(a) + _split3(-a * (lane_q % blk).astype(F32))
            feat = jnp.zeros((LANES, qcols), F32)
            for r, t in enumerate(terms):
                feat = jnp.where(fr == r, t, feat)
            qabt_ref[p, LANES:, :] = feat.astype(BF16)

    qt = q_ref[...].T
    means = mean_ref[0]
    if nb < group:
        means = jnp.concatenate([means, jnp.zeros((group - nb, D_ATT), F32)], axis=0)
    mt = jnp.concatenate([means] * N_HEADS, axis=0)
    rh = lax.broadcasted_iota(jnp.int32, mt.shape, 0) // group
    ch = lax.broadcasted_iota(jnp.int32, mt.shape, 1) // HEAD_DIM
    mbd = jnp.where(rh == ch, mt, 0.0)
    m_hi = mbd.astype(BF16)
    q_hi = qt.astype(BF16)
    m2 = jnp.concatenate([m_hi, (mbd - m_hi.astype(F32)).astype(BF16)], axis=0)
    q2 = jnp.concatenate([q_hi, (qt - q_hi.astype(F32)).astype(BF16)], axis=1)
    g4 = _dot(m2, q2)
    hg = N_HEADS * group
    gate_t = (g4[:hg, :blk] + g4[hg:, blk:]) + (g4[:hg, blk:] + g4[hg:, :blk])
    gate = jnp.concatenate([gate_t[h * group:(h + 1) * group, :] for h in range(N_HEADS)], axis=1)

    n_idx = lax.broadcasted_iota(jnp.int32, gate.shape, 0)
    n_f = n_idx.astype(F32)
    valid = n_idx < j
    work = jnp.where(valid, gate, NEG_INF)
    picked = jnp.zeros(gate.shape, F32)
    for _ in range(MOBA_TOPK):
        top = jnp.max(work, axis=0, keepdims=True)
        first = jnp.min(jnp.where(work == top, n_f, float(group)), axis=0, keepdims=True)
        pick = n_f == first
        picked = jnp.where(pick, 1.0, picked)
        work = jnp.where(pick, REMOVED, work)
    colb_ref[...] = jnp.where((picked > 0.0) & valid, 0.0, NEG_INF)

    row_d = lax.broadcasted_iota(jnp.int32, (LANES, blk), 0)
    for p in range(N_SLABS):
        qs = qt[p * LANES:(p + 1) * LANES, :] * (SCALE * LOG2E)
        qa = jnp.where(row_d < HEAD_DIM, qs, 0.0)
        qb = jnp.where(row_d >= HEAD_DIM, qs, 0.0)
        qabt_ref[p, :LANES, :] = jnp.concatenate([qa, qb], axis=1).astype(BF16)

    slabs = [slice(p * LANES, (p + 1) * LANES) for p in range(N_SLABS)]

    def scores(n, p):
        off = pl.multiple_of(n * blk, blk)
        keys = jnp.concatenate([kb_ref[pl.ds(off, blk), slabs[p]], featk_ref[...]], axis=1)
        return _dot(keys, qabt_ref[p])

    sets = (seta_ref, setb_ref)

    def park_unit(n_first, count, p):
        for g in range(count):
            sets[p % 2][g] = scores(jnp.minimum(n_first + g, j), p)

    def weighted_values(blocks, p, e):
        vt = jnp.concatenate([vt_ref[n, slabs[p], :] for n in blocks], axis=1)
        ones = jnp.ones((SUM_ROWS, vt.shape[1]), BF16)
        pvs = [_dot(jnp.concatenate([vt[h * HEAD_DIM:(h + 1) * HEAD_DIM], ones], axis=0),
                    e[:, h * blk:(h + 1) * blk]) for h in range(HEADS_PER_SLAB)]
        return (jnp.concatenate([pv[:HEAD_DIM] for pv in pvs], axis=0),
                jnp.concatenate([pv[HEAD_DIM:HEAD_DIM + 1] for pv in pvs], axis=1))

    def by_head(row):
        return jnp.where(row_d < HEAD_DIM, row[:, :blk], row[:, blk:])

    def reduce_unit(n_first, count, p, track_max):
        cs = slice(p * qcols, (p + 1) * qcols)
        src = sets[p % 2]
        blocks = [n_first + g for g in range(count)]
        crows = [colb_ref[pl.ds(n, 1), cs] - slope_row(p) * ((j - n) * blk).astype(F32) for n in blocks]
        m_prev = m_ref[p]
        if track_max:
            m_new = m_prev
            for g, crow in enumerate(crows):
                m_new = jnp.maximum(m_new, jnp.max(src[g], axis=0, keepdims=True) + crow)
            alpha = jnp.exp2(m_prev - m_new)
            m_ref[p] = m_new
        else:
            m_new = m_prev
        e = jnp.concatenate([jnp.exp2(src[g] - (m_new - crow)).astype(BF16) for g, crow in enumerate(crows)],
                            axis=0)
        pv, esum = weighted_values(blocks, p, e)
        if track_max:
            l_ref[p] = alpha * l_ref[p] + esum
            acc_ref[p] = by_head(alpha) * acc_ref[p] + pv
        else:
            l_ref[p] = l_ref[p] + esum
            acc_ref[p] = acc_ref[p] + pv

    def sweep(n_first, count, n_after, track_max):
        for p in range(N_SLABS):
            if p + 1 < N_SLABS:
                park_unit(n_first, count, p + 1)
            elif n_after is not None:
                park_unit(n_after, UPDATE_BLOCKS, 0)
            reduce_unit(n_first, count, p, track_max)

    def attend(track_max):
        for p in range(N_SLABS):
            so_ref[p] = scores(j, p)
        park_unit(0, UPDATE_BLOCKS, 0)
        for p in range(N_SLABS):
            sb = so_ref[p] + causal_ref[...]
            m = jnp.max(sb, axis=0, keepdims=True)
            pv, esum = weighted_values([j], p, jnp.exp2(sb - m).astype(BF16))
            m_ref[p] = m
            l_ref[p] = esum
            acc_ref[p] = pv

        def trip(t, carry):
            sweep(UPDATE_BLOCKS * t, UPDATE_BLOCKS, UPDATE_BLOCKS * (t + 1), track_max)
            return carry

        lax.fori_loop(0, j // UPDATE_BLOCKS, trip, 0)

        nr = (j // UPDATE_BLOCKS) * UPDATE_BLOCKS
        for r in range(1, UPDATE_BLOCKS):
            @pl.when(j - nr == r)
            def _(r=r):
                sweep(nr, r, None, track_max)

    attend(track_max=False)
    l_top = l_ref[0]
    a_top = jnp.abs(acc_ref[0])
    for p in range(1, N_SLABS):
        l_top = jnp.maximum(l_top, l_ref[p])
        a_top = jnp.maximum(a_top, jnp.abs(acc_ref[p]))
    finite = (jnp.max(l_top) < FINITE_MAX) & (jnp.max(a_top) < FINITE_MAX)

    @pl.when(jnp.logical_not(finite))
    def _():
        attend(track_max=True)

    for p in range(N_SLABS):
        o2 = acc_ref[p] / by_head(l_ref[p])
        sq = o2 * o2
        ms_a = jnp.sum(sq[:HEAD_DIM], axis=0, keepdims=True) * (1.0 / HEAD_DIM)
        ms_b = jnp.sum(sq[HEAD_DIM:], axis=0, keepdims=True) * (1.0 / HEAD_DIM)
        inv = jnp.where(row_d < HEAD_DIM, lax.rsqrt(ms_a + EPS), lax.rsqrt(ms_b + EPS))
        ls = slice(p * LANES, (p + 1) * LANES)
        o_ref[:, ls] = ((o2 * inv).T * gain_ref[:, ls]).astype(BF16)


def _attn_sample_half(pt_ref, q_ref, kn_ref, vn_ref, gain_ref, bd_ref, ckt_hbm, cvt_hbm, o_ref,
                      kbuf, vbuf, s_ref, acc_ref, stat_ref, ksem, vsem, *,
                      b, nbat, first, k_half, v_half, past_len, page, page_base):
    n_pages = past_len // page
    ppb = MOBA_BLOCK // page
    nb = past_len // MOBA_BLOCK
    t_new = q_ref.shape[1]
    rows = N_HEADS * t_new

    def page_copy(hbm, buf, sem, bb, pg):
        return pltpu.make_async_copy(hbm.at[page_base + pt_ref[bb, pg]], buf.at[pg], sem.at[pg])

    def start_all(hbm, buf, sem, bb):
        def body(i, c):
            for k in range(DMA_THREADS):
                page_copy(hbm, buf, sem, bb, i * DMA_THREADS + k).start(priority=k)
            return c
        lax.fori_loop(0, n_pages // DMA_THREADS, body, 0)

    def refill(hbm, buf, sem, pgs):
        @pl.when(b + 1 < nbat)
        def _():
            for k, pg in enumerate(pgs):
                page_copy(hbm, buf, sem, b + 1, pg).start(priority=k % DMA_THREADS)

    @pl.when(first)
    def _():
        start_all(ckt_hbm, kbuf, ksem, b)
        start_all(cvt_hbm, vbuf, vsem, b)

    shared = dict(b=b, page_copy=page_copy, refill=refill, t_new=t_new, rows=rows, n_pages=n_pages, nb=nb,
                  ppb=ppb, page=page, past_len=past_len, s_ref=s_ref, stat_ref=stat_ref)
    pl.when(k_half)(functools.partial(_sample_k_half, q_ref, kn_ref, kbuf, ksem, ckt_hbm, **shared))
    pl.when(v_half)(functools.partial(_sample_v_half, vn_ref, gain_ref, bd_ref, o_ref, vbuf, vsem, cvt_hbm, acc_ref,
                                      **shared))


def _sample_k_half(q_ref, kn_ref, kbuf, ksem, ckt_hbm, *, b, page_copy, refill, t_new, rows, n_pages, nb, ppb, page,
                   past_len, s_ref, stat_ref):
    qt = jnp.concatenate([q_ref[0]] * N_HEADS, axis=0)
    rh = lax.broadcasted_iota(jnp.int32, qt.shape, 0) // t_new
    ch = lax.broadcasted_iota(jnp.int32, qt.shape, 1) // HEAD_DIM
    qs = jnp.where(rh == ch, qt, 0.0) * SCALE
    q_hi = qs.astype(BF16)
    q_lo = (qs - q_hi.astype(F32)).astype(BF16)
    qq = jnp.concatenate([q_hi, q_lo], axis=0)

    def k_tile(i, c):
        pgs = [i * K_TILE + k for k in range(K_TILE)]
        for pg in pgs:
            page_copy(ckt_hbm, kbuf, ksem, b, pg).wait()
        for pg in pgs:
            s2 = _dot(qq, kbuf[pg].reshape(D_ATT, page).astype(BF16))
            s_ref[pg] = s2[:rows] + s2[rows:]
        refill(ckt_hbm, kbuf, ksem, pgs)
        return c
    lax.fori_loop(0, n_pages // K_TILE, k_tile, 0)

    lane = lax.broadcasted_iota(jnp.int32, (rows, LANES), 1)
    gate = jnp.full((rows, LANES), REMOVED, F32)
    for n in range(nb):
        tot = s_ref[n * ppb]
        for i in range(1, ppb):
            tot = tot + s_ref[n * ppb + i]
        gate = jnp.where(lane == n, jnp.sum(tot, axis=1, keepdims=True), gate)
    lane_f = lane.astype(F32)
    picked = jnp.zeros(gate.shape, F32)
    for _ in range(MOBA_TOPK):
        top = jnp.max(gate, axis=1, keepdims=True)
        first_lane = jnp.min(jnp.where(gate == top, lane_f, float(LANES)), axis=1, keepdims=True)
        pick = lane_f == first_lane
        picked = jnp.where(pick, 1.0, picked)
        gate = jnp.where(pick, REMOVED, gate)
    colb = jnp.where(picked > 0.0, 0.0, NEG_INF)

    r1 = lax.broadcasted_iota(jnp.int32, (rows, 1), 0)
    tq = r1 % t_new
    slope = jnp.zeros((rows, 1), F32)
    for h in range(N_HEADS):
        slope = jnp.where(r1 // t_new == h, _slope(h), slope)
    in_page = slope * (tq - lane).astype(F32)

    zpad = jnp.zeros((LANES - t_new, D_ATT), F32)
    kn = jnp.concatenate([kn_ref[0], zpad], axis=0).astype(BF16)
    s2 = _dot_nt(qq, kn)
    s_own = jnp.where(lane <= tq, s2[:rows] + s2[rows:] - in_page, NEG_INF)

    mrun = s_own
    for n in range(nb):
        mask_n = jnp.sum(jnp.where(lane == n, colb, 0.0), axis=1, keepdims=True)
        for i in range(ppb):
            pg = n * ppb + i
            sn = s_ref[pg] - in_page + (mask_n - slope * float(past_len - pg * page))
            s_ref[pg] = sn
            mrun = jnp.maximum(mrun, sn)
    m = jnp.max(mrun, axis=1, keepdims=True)

    e_own = jnp.exp(s_own - m)
    lrun = e_own
    for pg in range(n_pages):
        e = jnp.exp(s_ref[pg] - m)
        s_ref[pg] = e
        lrun = lrun + e
    stat_ref[0] = jnp.broadcast_to(jnp.sum(lrun, axis=1, keepdims=True), (rows, LANES))
    stat_ref[1] = e_own


def _sample_v_half(vn_ref, gain_ref, bd_ref, o_ref, vbuf, vsem, cvt_hbm, acc_ref, *, b, page_copy, refill, t_new, rows,
                   n_pages, nb, ppb, page, past_len, s_ref, stat_ref):
    l = stat_ref[0][:, :1]
    e_own = stat_ref[1]
    zpad = jnp.zeros((LANES - t_new, D_ATT), F32)
    vn = jnp.concatenate([vn_ref[0], zpad], axis=0).astype(BF16)

    acc_ref[...] = jnp.zeros(acc_ref.shape, F32)
    zrows = jnp.zeros((LANES - rows, V_TILE * page), BF16)

    def v_tile(i, c):
        pgs = [i * V_TILE + k for k in range(V_TILE)]
        for pg in pgs:
            page_copy(cvt_hbm, vbuf, vsem, b, pg).wait()
        vt = jnp.concatenate([vbuf[pg].reshape(D_ATT, page) for pg in pgs], axis=1).astype(BF16)
        p = jnp.concatenate([s_ref[pg] for pg in pgs], axis=1).astype(BF16)
        acc_ref[...] += _dot_nt(vt, jnp.concatenate([p, zrows], axis=0))
        refill(cvt_hbm, vbuf, vsem, pgs)
        return c
    lax.fori_loop(0, n_pages // V_TILE, v_tile, 0)

    acc = acc_ref[...].T[:rows] + _dot(e_own.astype(BF16), vn)
    accn = acc / l
    ch8 = lax.broadcasted_iota(jnp.int32, (t_new, D_ATT), 1) // HEAD_DIM
    out = jnp.zeros((t_new, D_ATT), F32)
    for h in range(N_HEADS):
        out = jnp.where(ch8 == h, accn[h * t_new:(h + 1) * t_new, :], out)
    o_ref[0] = _group_rms(out, gain_ref[...], bd_ref).astype(BF16)


N_PROMPT_SCRATCH = 10


def _attn_both_body(pt_ref, q_ref, kb_ref, vt_ref, mean_ref, gain_ref, qs_ref, kn_ref, vn_ref, bd_ref, ckt_hbm, cvt_hbm,
                    o_ref, os_ref, *scratch, nseq, past_len, page, page_base):
    _attn_prompt_body(q_ref, kb_ref, vt_ref, mean_ref, gain_ref, o_ref, *scratch[:N_PROMPT_SCRATCH])
    step = pl.program_id(0) * pl.num_programs(1) + pl.program_id(1)
    seq = step // 2
    live = seq < nseq
    _attn_sample_half(pt_ref, qs_ref, kn_ref, vn_ref, gain_ref, bd_ref, ckt_hbm, cvt_hbm, os_ref,
                      *scratch[N_PROMPT_SCRATCH:], b=jnp.minimum(seq, nseq - 1), nbat=nseq, first=step == 0,
                      k_half=live & (step % 2 == 0), v_half=live & (step % 2 == 1),
                      past_len=past_len, page=page, page_base=page_base)


def _attn_both(page_table, q, kb, vt, means, gain, qs, kn, vn, bd, cache_kt, cache_vt, *, batch, page_base, past_len):
    n = q.shape[0]
    seq = n // batch
    nb = seq // MOBA_BLOCK
    group = LANES // N_HEADS
    assert nb <= group and seq % MOBA_BLOCK == 0
    means = means.reshape(batch, nb, D_ATT)
    qcols = HEADS_PER_SLAB * MOBA_BLOCK
    nseq, t_new, _ = qs.shape
    page = cache_kt.shape[3]
    n_pages = past_len // page
    rows = N_HEADS * t_new
    assert past_len % MOBA_BLOCK == 0 and MOBA_BLOCK % page == 0 and page == LANES
    assert rows <= LANES and t_new % 8 == 0 and past_len // MOBA_BLOCK <= LANES
    assert n_pages % V_TILE == 0 and n_pages % K_TILE == 0
    assert 2 * nseq <= batch * nb
    body = functools.partial(_attn_both_body, nseq=nseq, past_len=past_len, page=page, page_base=page_base)
    blk_row = pl.BlockSpec((MOBA_BLOCK, D_ATT), lambda b, j, pt: (b * nb + j, 0))
    per_seq = pl.BlockSpec((1, t_new, D_ATT), lambda b, j, pt: (jnp.minimum((b * nb + j) // 2, nseq - 1), 0, 0))
    once = lambda shape: pl.BlockSpec(shape, lambda b, j, pt: (0,) * len(shape))
    grid_spec = pltpu.PrefetchScalarGridSpec(
        num_scalar_prefetch=1,
        grid=(batch, nb),
        in_specs=[blk_row,
                  pl.BlockSpec((seq, D_ATT), lambda b, j, pt: (b, 0)),
                  pl.BlockSpec((nb, D_ATT, MOBA_BLOCK), lambda b, j, pt: (b, 0, 0)),
                  pl.BlockSpec((1, nb, D_ATT), lambda b, j, pt: (b, 0, 0)),
                  once(gain.shape),
                  per_seq, per_seq, per_seq, once(bd.shape),
                  pl.BlockSpec(memory_space=pl.ANY),
                  pl.BlockSpec(memory_space=pl.ANY)],
        out_specs=[blk_row, per_seq],
        scratch_shapes=[pltpu.VMEM((MOBA_BLOCK, qcols), F32),
                        pltpu.VMEM((MOBA_BLOCK, LANES), BF16),
                        pltpu.VMEM((N_SLABS, 2 * LANES, qcols), BF16),
                        pltpu.VMEM((group, N_HEADS * MOBA_BLOCK), F32),
                        pltpu.VMEM((N_SLABS, MOBA_BLOCK, qcols), F32),
                        pltpu.VMEM((UPDATE_BLOCKS, MOBA_BLOCK, qcols), F32),
                        pltpu.VMEM((UPDATE_BLOCKS, MOBA_BLOCK, qcols), F32),
                        pltpu.VMEM((N_SLABS, 1, qcols), F32),
                        pltpu.VMEM((N_SLABS, 1, qcols), F32),
                        pltpu.VMEM((N_SLABS, LANES, MOBA_BLOCK), F32),
                        pltpu.VMEM((n_pages, N_HEADS, HEAD_DIM, page), F32),
                        pltpu.VMEM((n_pages, N_HEADS, HEAD_DIM, page), F32),
                        pltpu.VMEM((n_pages, rows, page), F32),
                        pltpu.VMEM((D_ATT, LANES), F32),
                        pltpu.VMEM((2, rows, LANES), F32),
                        pltpu.SemaphoreType.DMA((n_pages,)),
                        pltpu.SemaphoreType.DMA((n_pages,))],
    )
    return pl.pallas_call(
        body,
        grid_spec=grid_spec,
        out_shape=[jax.ShapeDtypeStruct((n, D_ATT), BF16), jax.ShapeDtypeStruct((nseq, t_new, D_ATT), BF16)],
        compiler_params=pltpu.CompilerParams(dimension_semantics=("arbitrary", "arbitrary"),
                                             vmem_limit_bytes=ATTN_VMEM_LIMIT),
        name="attn_both",
    )(page_table, q, kb, vt, means, gain, qs, kn, vn, bd, cache_kt, cache_vt)


def kernel(x_prompt, x_sample, cache_k, cache_v, state_conv, page_table, ffn1_norm, ffn1_w_gu, ffn1_w_down,
           mix_norm, w_in, conv_w, conv_out_norm, attn_out_norm, w_out, ffn2_norm, ffn2_w_gu, ffn2_w_down,
           final_norm):
    bp, seq, d = x_prompt.shape
    bs, dseq, _ = x_sample.shape
    depth, n_pool, page = cache_k.shape[:3]
    dc = conv_w.shape[2]
    past_len = page_table.shape[1] * page

    ck = jnp.transpose(cache_k, (0, 1, 3, 4, 2)).reshape(depth * n_pool, N_HEADS, HEAD_DIM, page)
    cv = jnp.transpose(cache_v, (0, 1, 3, 4, 2)).reshape(depth * n_pool, N_HEADS, HEAD_DIM, page)
    gi = lax.broadcasted_iota(jnp.int32, (D_ATT, D_ATT), 0) // HEAD_DIM
    gj = lax.broadcasted_iota(jnp.int32, (D_ATT, D_ATT), 1) // HEAD_DIM
    bd = (gi == gj).astype(BF16)

    xp = x_prompt.reshape(bp * seq, d)
    xs = x_sample.reshape(bs * dseq, d)
    tm_p = PROMPT_ROW_TILE
    assert seq % tm_p == 0 and seq % INPROJ_ROW_TILE == 0 and INPROJ_ROW_TILE % MOBA_BLOCK == 0
    assert dc // N_CONV_GROUPS == HEAD_DIM and dc == D_ATT
    row = lambda a: a.reshape(1, -1)
    outs = [[] for _ in range(6)]
    for l in range(depth):
        g1, gm, g2 = row(ffn1_norm[l]), row(mix_norm[l]), row(ffn2_norm[l])
        gc, ga = row(conv_out_norm[l]), row(attn_out_norm[l])
        last = l == depth - 1
        gfin = row(final_norm) if last else None

        st = state_conv[l]
        zpad = jnp.zeros((bs, dseq - (CONV_W - 1), dc), F32)
        s2 = jnp.concatenate([st, zpad], axis=1).reshape(bs * dseq, dc)
        s1 = jnp.concatenate([st[:, 1:2], jnp.zeros((bs, dseq - 1, dc), F32)], axis=1).reshape(bs * dseq, dc)
        x1s, wg1, wu1, wd1 = _ffn_stream_call(xs, g1, ffn1_w_gu[l], ffn1_w_down[l], name="ffn1_sample")
        qs, ks, vs, ycs, us, win = _inproj_sample(x1s, gm, w_in[l], conv_w[l], gc, bd, s1, s2, seq=dseq)
        outs[3].append(ks.reshape(bs, dseq, N_HEADS, HEAD_DIM))
        outs[4].append(vs.reshape(bs, dseq, N_HEADS, HEAD_DIM))
        outs[5].append(us[:, dseq - (CONV_W - 1):, :])

        x1 = _ffn_call(xp, g1, wg1, wu1, wd1, tm=tm_p, name="ffn1_prompt")
        q, kt, vtf, kb, vt, yc, means, cnew = _inproj_prompt(x1, gm, win, conv_w[l], gc, bd, batch=bp,
                                                             tm=INPROJ_ROW_TILE)
        r3 = lambda a: a.reshape(bs, dseq, D_ATT)
        ya, yas = _attn_both(page_table, q, kb, vt, means, ga, r3(qs), r3(ks), r3(vs), bd, ck, cv,
                             batch=bp, page_base=l * n_pool, past_len=past_len)
        xs, wg2, wu2, wd2, woc, woa = _ffn_stream_call(
            x1s, g2, ffn2_w_gu[l], ffn2_w_down[l], mix=(ycs, yas.reshape(bs * dseq, D_ATT), w_out[l]),
            final=gfin, name="ffn2_sample")
        xp = _ffn_call(x1, g2, wg2, wu2, wd2, tm=tm_p, mix=(yc, ya, woc, woa), final=gfin, name="ffn2_prompt")
        tok_major = lambda a: a.reshape(bp, N_HEADS, HEAD_DIM, seq).transpose(0, 3, 1, 2)
        outs[0].append(tok_major(kt))
        outs[1].append(tok_major(vtf))
        outs[2].append(cnew)

    y_prompt = xp.reshape(bp, seq, d)
    y_sample = xs.reshape(bs, dseq, d)
    kp, vp, cp, ksn, vsn, csn = (jnp.stack(o) for o in outs)
    return (y_prompt, y_sample, kp, vp, cp, ksn, vsn, csn)
```

```python
import functools

import jax
import jax.numpy as jnp
from jax import lax
from jax.experimental import pallas as pl
from jax.experimental.pallas import tpu as pltpu

F32 = jnp.float32
BF16 = jnp.bfloat16

N_HEADS = 8
HEAD_DIM = 64
D_ATT = N_HEADS * HEAD_DIM
N_CONV_GROUPS = 8
CONV_W = 3
MOBA_BLOCK = 256
MOBA_TOPK = 3
EPS = 1e-5
NEG_INF = -1e30
FINITE_MAX = 3.0e38
REMOVED = -3e38
SCALE = HEAD_DIM ** -0.5
LOG2E = 1.4426950408889634

LANES = 128
MXU_WIDTH = 256
HEADS_PER_SLAB = LANES // HEAD_DIM
N_SLABS = D_ATT // LANES
VMEM_LIMIT = 56 * 1024 * 1024
ATTN_VMEM_LIMIT = 62 * 1024 * 1024
PROMPT_ROW_TILE = 512
INPROJ_ROW_TILE = 1024
UPDATE_BLOCKS = 2
SUM_ROWS = 16
DMA_THREADS = 2
K_TILE = 32
V_TILE = 32

NT_DIMS = (((1,), (1,)), ((), ()))


def _slope(h):
    return 2.0 ** (-(8.0 / N_HEADS) * (h + 1))


def _dot(a, b):
    return jnp.dot(a, b, preferred_element_type=F32)


def _dot_nt(a, b):
    return lax.dot_general(a, b, NT_DIMS, preferred_element_type=F32)


def _rms(x, g):
    ms = jnp.mean(x * x, axis=-1, keepdims=True)
    return x * lax.rsqrt(ms + EPS) * g


def _group_sumsq(y, bd_ref):
    y2 = y * y
    hi = y2.astype(BF16)
    lo = (y2 - hi.astype(F32)).astype(BF16)
    bd = bd_ref[...]
    return _dot(hi, bd) + _dot(lo, bd)


def _group_rms(y, g, bd_ref):
    ms = _group_sumsq(y, bd_ref) * (1.0 / HEAD_DIM)
    return y * lax.rsqrt(ms + EPS) * g


def _const_spec(shape):
    nd = len(shape)
    return pl.BlockSpec(shape, lambda *_: (0,) * nd, pipeline_mode=pl.Buffered(1))


def _ffn_body(*refs, mix, final, bounds):
    it = iter(refs)
    x_ref = next(it)
    if mix:
        yc_ref, ya_ref, woc_ref, woa_ref = next(it), next(it), next(it), next(it)
    g_ref, wg_ref, wu_ref, wd_ref = next(it), next(it), next(it), next(it)
    gf_ref = next(it) if final else None
    o_ref = next(it)

    x = x_ref[...]
    if mix:
        x = x + _dot(yc_ref[...], woc_ref[...]) + _dot(ya_ref[...], woa_ref[...])
    h = _rms(x, g_ref[...]).astype(BF16)
    acc = jnp.zeros(x.shape, F32)
    for lo, hi in zip(bounds[:-1], bounds[1:]):
        gate = _dot(h, wg_ref[:, lo:hi])
        up = _dot(h, wu_ref[:, lo:hi])
        act = (gate * jax.nn.sigmoid(gate) * up).astype(BF16)
        acc = acc + _dot(act, wd_ref[lo:hi, :])
    x = x + 0.5 * acc
    if final:
        x = _rms(x, gf_ref[...])
    o_ref[...] = x


def _ffn_call(x, g, wg, wu, wd, *, tm, mix=None, final=None, name):
    n, d = x.shape
    d_ff = wd.shape[0]
    row = lambda w: pl.BlockSpec((tm, w), lambda i: (i, 0))
    ins, specs = [x], [row(d)]
    if mix is not None:
        yc, ya, woc, woa = mix
        ins += [yc, ya, woc, woa]
        specs += [row(yc.shape[1]), row(ya.shape[1]), _const_spec(woc.shape), _const_spec(woa.shape)]
    ins += [g, wg, wu, wd]
    specs += [_const_spec(g.shape), _const_spec(wg.shape), _const_spec(wu.shape), _const_spec(wd.shape)]
    if final is not None:
        ins.append(final)
        specs.append(_const_spec(final.shape))
    assert d_ff % MXU_WIDTH == 0
    tiles = d_ff // MXU_WIDTH
    bounds = (0, (tiles + 1) // 2 * MXU_WIDTH, d_ff)
    body = functools.partial(_ffn_body, mix=mix is not None, final=final is not None, bounds=bounds)
    return pl.pallas_call(
        body,
        grid=(n // tm,),
        in_specs=specs,
        out_specs=row(d),
        out_shape=jax.ShapeDtypeStruct((n, d), F32),
        compiler_params=pltpu.CompilerParams(dimension_semantics=("arbitrary",),
                                             vmem_limit_bytes=VMEM_LIMIT),
        name=name,
    )(*ins)


def _ffn_stream_body(*refs, mix, final, proj):
    it = iter(refs)
    x_ref = next(it)
    if mix:
        yc_ref, ya_ref, woc_ref, woa_ref = next(it), next(it), next(it), next(it)
    g_ref, wg_ref, wu_ref, wd_ref = next(it), next(it), next(it), next(it)
    gf_ref = next(it) if final else None
    proj_in = [next(it) for _ in range(7)] if proj else None
    o_ref, wgb_ref, wub_ref, wdb_ref = next(it), next(it), next(it), next(it)
    if mix:
        wocb_ref, woab_ref = next(it), next(it)
    proj_out = [next(it) for _ in range(6)] if proj else None
    x_scr, h_scr, acc_scr = next(it), next(it), next(it)
    ubuf = next(it) if proj else None
    c = pl.program_id(0)

    @pl.when(c == 0)
    def _():
        x = x_ref[...]
        if mix:
            woc, woa = woc_ref[...].astype(BF16), woa_ref[...].astype(BF16)
            wocb_ref[...] = woc
            woab_ref[...] = woa
            x = x + _dot(yc_ref[...], woc) + _dot(ya_ref[...], woa)
        x_scr[...] = x
        h_scr[...] = _rms(x, g_ref[...]).astype(BF16)
        acc_scr[...] = jnp.zeros(acc_scr.shape, F32)

    wg, wu, wd = wg_ref[...].astype(BF16), wu_ref[...].astype(BF16), wd_ref[...].astype(BF16)
    wgb_ref[...] = wg
    wub_ref[...] = wu
    wdb_ref[...] = wd
    h = h_scr[...]
    gate = _dot(h, wg)
    act = (gate * jax.nn.sigmoid(gate) * _dot(h, wu)).astype(BF16)
    acc_scr[...] += _dot(act, wd)

    @pl.when(c == pl.num_programs(0) - 1)
    def _():
        x = x_scr[...] + 0.5 * acc_scr[...]
        if final:
            x = _rms(x, gf_ref[...])
        o_ref[...] = x
        if proj:
            x_scr[...] = x
            _inproj_body(x_scr, *proj_in, *proj_out, ubuf, tm=x_scr.shape[0], tiles_per_seq=1, sample=True)


def _ffn_stream_call(x, g, w_gu, w_down, *, mix=None, final=None, proj=None, name):
    n, d = x.shape
    d_ff = w_down.shape[0]
    tw = MXU_WIDTH
    assert d_ff % tw == 0 and w_gu.shape == (d, 2 * d_ff)
    nt = d_ff // tw
    full = lambda shape: pl.BlockSpec(shape, lambda c: (0,) * len(shape))
    ins, specs = [x], [full((n, d))]
    outs = [jax.ShapeDtypeStruct((n, d), F32), jax.ShapeDtypeStruct((d, d_ff), BF16),
            jax.ShapeDtypeStruct((d, d_ff), BF16), jax.ShapeDtypeStruct((d_ff, d), BF16)]
    out_specs = [full((n, d)), pl.BlockSpec((d, tw), lambda c: (0, c)), pl.BlockSpec((d, tw), lambda c: (0, c)),
                 pl.BlockSpec((tw, d), lambda c: (c, 0))]
    if mix is not None:
        yc, ya, w_out = mix
        dm = yc.shape[1]
        assert w_out.shape == (dm + ya.shape[1], d) and ya.shape[1] == dm
        ins += [yc, ya, w_out, w_out]
        specs += [full(yc.shape), full(ya.shape), pl.BlockSpec((dm, d), lambda c: (0, 0)),
                  pl.BlockSpec((dm, d), lambda c: (1, 0))]
        outs += [jax.ShapeDtypeStruct((dm, d), BF16)] * 2
        out_specs += [full((dm, d))] * 2
    ins += [g, w_gu, w_gu, w_down]
    specs += [full(g.shape), pl.BlockSpec((d, tw), lambda c: (0, c)), pl.BlockSpec((d, tw), lambda c: (0, nt + c)),
              pl.BlockSpec((tw, d), lambda c: (c, 0))]
    if final is not None:
        ins.append(final)
        specs.append(full(final.shape))
    scratch = [pltpu.VMEM((n, d), F32), pltpu.VMEM((n, d), BF16), pltpu.VMEM((n, d), F32)]
    if proj is not None:
        assert final is None
        gm, w_in, conv_w, conv_norm, bd, s1, s2, seq = proj
        dc = conv_w.shape[1]
        proj_ins = [gm, w_in, conv_w, conv_norm, bd, s1, s2]
        ins += proj_ins
        specs += [full(a.shape) for a in proj_ins]
        proj_shapes = [(n, D_ATT), (n, D_ATT), (n, D_ATT), (n, dc), (n // seq, seq, dc), w_in.shape]
        proj_types = [F32, F32, F32, BF16, F32, BF16]
        outs += [jax.ShapeDtypeStruct(s, t) for s, t in zip(proj_shapes, proj_types)]
        out_specs += [full(s) for s in proj_shapes]
        scratch.append(pltpu.VMEM((n + 8, dc), F32))
    body = functools.partial(_ffn_stream_body, mix=mix is not None, final=final is not None, proj=proj is not None)
    return pl.pallas_call(
        body,
        grid=(nt,),
        in_specs=specs,
        out_specs=out_specs,
        out_shape=outs,
        scratch_shapes=scratch,
        compiler_params=pltpu.CompilerParams(dimension_semantics=("arbitrary",),
                                             vmem_limit_bytes=VMEM_LIMIT),
        name=name,
    )(*ins)


def _inproj_body(*refs, tm, tiles_per_seq, sample):
    it = iter(refs)
    x_ref, g_ref, win_ref, cw_ref, cn_ref, bd_ref = (next(it) for _ in range(6))
    if sample:
        s1_ref, s2_ref = next(it), next(it)
        q_ref, k_ref, v_ref, yc_ref, u_ref, winb_ref = (next(it) for _ in range(6))
    else:
        q_ref, kt_ref, vtf_ref, kb_ref, vt_ref, yc_ref, mean_ref, cnew_ref = (next(it) for _ in range(8))
    ubuf = next(it)

    dc = yc_ref.shape[1]
    if sample:
        ubuf[0:8, :] = jnp.zeros((8, dc), F32)
    else:
        first = (pl.program_id(0) % tiles_per_seq) == 0

        @pl.when(first)
        def _():
            ubuf[0:8, :] = jnp.zeros((8, dc), F32)

        @pl.when(jnp.logical_not(first))
        def _():
            ubuf[0:8, :] = ubuf[tm:tm + 8, :]

    h = _rms(x_ref[...], g_ref[...]).astype(BF16)
    if sample:
        def piece(c, w):
            wp = win_ref[:, c:c + w].astype(BF16)
            winb_ref[:, c:c + w] = wp
            return _dot(h, wp)
    else:
        piece = lambda c, w: _dot(h, win_ref[:, c:c + w])
    hc = piece(0, dc)
    cg = piece(2 * dc, dc)
    bg = piece(dc, dc)
    k = piece(3 * dc + D_ATT, D_ATT)
    v = piece(3 * dc + 2 * D_ATT, D_ATT)
    if sample:
        k_ref[...] = k
        v_ref[...] = v

    u = cg * hc
    ubuf[8:tm + 8, :] = u
    um1 = ubuf[7:tm + 7, :]
    um2 = ubuf[6:tm + 6, :]
    if sample:
        t = lax.broadcasted_iota(jnp.int32, (tm, dc), 0) % u_ref.shape[1]
        um1 = jnp.where(t >= 1, um1, s1_ref[...])
        um2 = jnp.where(t >= 2, um2, s2_ref[...])
    cw = cw_ref[...]
    conv = um2 * cw[0:1, :] + um1 * cw[1:2, :] + u * cw[2:3, :]
    yc_ref[...] = _group_rms(bg * conv, cn_ref[...], bd_ref).astype(BF16)
    q_ref[...] = piece(3 * dc, D_ATT)

    if sample:
        u_ref[...] = u.reshape(u_ref.shape)
    else:
        vt = v.T
        kt_ref[0] = k.T
        vtf_ref[0] = vt
        kb_ref[...] = k.astype(BF16)
        nblk = tm // MOBA_BLOCK
        for i in range(nblk):
            vt_ref[i] = vt[:, i * MOBA_BLOCK:(i + 1) * MOBA_BLOCK].astype(BF16)
        mean_ref[0] = jnp.sum(k.reshape(nblk, MOBA_BLOCK, D_ATT), axis=1) * (1.0 / MOBA_BLOCK)
        cnew_ref[0] = ubuf[tm + 6:tm + 8, :]


def _inproj_prompt(x, g, w_in, conv_w, conv_norm, bd, *, batch, tm):
    n, d = x.shape
    dc = conv_w.shape[1]
    seq = n // batch
    tps = seq // tm
    nblk = tm // MOBA_BLOCK
    row = lambda w: pl.BlockSpec((tm, w), lambda i: (i, 0))
    tok_minor = pl.BlockSpec((1, D_ATT, tm), lambda i: (i // tps, 0, i % tps))
    f = lambda w, dt: jax.ShapeDtypeStruct((n, w), dt)
    body = functools.partial(_inproj_body, tm=tm, tiles_per_seq=tps, sample=False)
    return pl.pallas_call(
        body,
        grid=(n // tm,),
        in_specs=[row(d), _const_spec(g.shape), _const_spec(w_in.shape), _const_spec(conv_w.shape),
                  _const_spec(conv_norm.shape), _const_spec(bd.shape)],
        out_specs=[row(D_ATT), tok_minor, tok_minor, row(D_ATT),
                   pl.BlockSpec((nblk, D_ATT, MOBA_BLOCK), lambda i: (i, 0, 0)), row(dc),
                   pl.BlockSpec((1, nblk, D_ATT), lambda i: (i, 0, 0)),
                   pl.BlockSpec((1, CONV_W - 1, dc), lambda i: (i // tps, 0, 0))],
        out_shape=[f(D_ATT, F32), jax.ShapeDtypeStruct((batch, D_ATT, seq), F32),
                   jax.ShapeDtypeStruct((batch, D_ATT, seq), F32), f(D_ATT, BF16),
                   jax.ShapeDtypeStruct((n // MOBA_BLOCK, D_ATT, MOBA_BLOCK), BF16), f(dc, BF16),
                   jax.ShapeDtypeStruct((n // tm, nblk, D_ATT), F32),
                   jax.ShapeDtypeStruct((batch, CONV_W - 1, dc), F32)],
        scratch_shapes=[pltpu.VMEM((tm + 8, dc), F32)],
        compiler_params=pltpu.CompilerParams(dimension_semantics=("arbitrary",),
                                             vmem_limit_bytes=VMEM_LIMIT),
        name="inproj_prompt",
    )(x, g, w_in, conv_w, conv_norm, bd)


def _split3(x):
    hi = x.astype(BF16).astype(F32)
    mid = (x - hi).astype(BF16).astype(F32)
    lo = (x - hi - mid).astype(BF16).astype(F32)
    return hi, mid, lo


def _attn_prompt_body(q_ref, kb_ref, vt_ref, mean_ref, gain_ref, o_ref,
                      causal_ref, featk_ref, qabt_ref, colb_ref, so_ref, seta_ref, setb_ref,
                      m_ref, l_ref, acc_ref):
    blk = MOBA_BLOCK
    b = pl.program_id(0)
    j = pl.program_id(1)
    nb = mean_ref.shape[1]
    group = LANES // N_HEADS
    qcols = HEADS_PER_SLAB * blk
    lane_q = lax.broadcasted_iota(jnp.int32, (1, qcols), 1)

    def slope_row(p):
        return jnp.where(lane_q < blk, LOG2E * _slope(HEADS_PER_SLAB * p), LOG2E * _slope(HEADS_PER_SLAB * p + 1))

    @pl.when((b == 0) & (j == 0))
    def _init_tables():
        kk = lax.broadcasted_iota(jnp.int32, (blk, qcols), 0)
        qq = lax.broadcasted_iota(jnp.int32, (blk, qcols), 1)
        causal_ref[...] = jnp.where((qq % blk) >= kk, 0.0, NEG_INF)
        ki = lax.broadcasted_iota(jnp.int32, (blk, LANES), 0).astype(F32)
        kl = lax.broadcasted_iota(jnp.int32, (blk, LANES), 1)
        featk_ref[...] = jnp.where(kl < 3, ki, jnp.where(kl < 6, 1.0, 0.0)).astype(BF16)
        fr = lax.broadcasted_iota(jnp.int32, (LANES, qcols), 0)
        for p in range(N_SLABS):
            a = slope_row(p)
            terms = _split3(a) + _split3(-a * (lane_q % blk).astype(F32))
            feat = jnp.zeros((LANES, qcols), F32)
            for r, t in enumerate(terms):
                feat = jnp.where(fr == r, t, feat)
            qabt_ref[p, LANES:, :] = feat.astype(BF16)

    qt = q_ref[...].T
    means = mean_ref[0]
    if nb < group:
        means = jnp.concatenate([means, jnp.zeros((group - nb, D_ATT), F32)], axis=0)
    mt = jnp.concatenate([means] * N_HEADS, axis=0)
    rh = lax.broadcasted_iota(jnp.int32, mt.shape, 0) // group
    ch = lax.broadcasted_iota(jnp.int32, mt.shape, 1) // HEAD_DIM
    mbd = jnp.where(rh == ch, mt, 0.0)
    m_hi = mbd.astype(BF16)
    q_hi = qt.astype(BF16)
    m2 = jnp.concatenate([m_hi, (mbd - m_hi.astype(F32)).astype(BF16)], axis=0)
    q2 = jnp.concatenate([q_hi, (qt - q_hi.astype(F32)).astype(BF16)], axis=1)
    g4 = _dot(m2, q2)
    hg = N_HEADS * group
    gate_t = (g4[:hg, :blk] + g4[hg:, blk:]) + (g4[:hg, blk:] + g4[hg:, :blk])
    gate = jnp.concatenate([gate_t[h * group:(h + 1) * group, :] for h in range(N_HEADS)], axis=1)

    n_idx = lax.broadcasted_iota(jnp.int32, gate.shape, 0)
    n_f = n_idx.astype(F32)
    valid = n_idx < j
    work = jnp.where(valid, gate, NEG_INF)
    picked = jnp.zeros(gate.shape, F32)
    for _ in range(MOBA_TOPK):
        top = jnp.max(work, axis=0, keepdims=True)
        first = jnp.min(jnp.where(work == top, n_f, float(group)), axis=0, keepdims=True)
        pick = n_f == first
        picked = jnp.where(pick, 1.0, picked)
        work = jnp.where(pick, REMOVED, work)
    colb_ref[...] = jnp.where((picked > 0.0) & valid, 0.0, NEG_INF)

    row_d = lax.broadcasted_iota(jnp.int32, (LANES, blk), 0)
    for p in range(N_SLABS):
        qs = qt[p * LANES:(p + 1) * LANES, :] * (SCALE * LOG2E)
        qa = jnp.where(row_d < HEAD_DIM, qs, 0.0)
        qb = jnp.where(row_d >= HEAD_DIM, qs, 0.0)
        qabt_ref[p, :LANES, :] = jnp.concatenate([qa, qb], axis=1).astype(BF16)

    slabs = [slice(p * LANES, (p + 1) * LANES) for p in range(N_SLABS)]

    def scores(n, p):
        off = pl.multiple_of(n * blk, blk)
        keys = jnp.concatenate([kb_ref[pl.ds(off, blk), slabs[p]], featk_ref[...]], axis=1)
        return _dot(keys, qabt_ref[p])

    sets = (seta_ref, setb_ref)

    def park_unit(n_first, count, p):
        for g in range(count):
            sets[p % 2][g] = scores(jnp.minimum(n_first + g, j), p)

    def weighted_values(blocks, p, e):
        vt = jnp.concatenate([vt_ref[n, slabs[p], :] for n in blocks], axis=1)
        ones = jnp.ones((SUM_ROWS, vt.shape[1]), BF16)
        pvs = [_dot(jnp.concatenate([vt[h * HEAD_DIM:(h + 1) * HEAD_DIM], ones], axis=0),
                    e[:, h * blk:(h + 1) * blk]) for h in range(HEADS_PER_SLAB)]
        return (jnp.concatenate([pv[:HEAD_DIM] for pv in pvs], axis=0),
                jnp.concatenate([pv[HEAD_DIM:HEAD_DIM + 1] for pv in pvs], axis=1))

    def by_head(row):
        return jnp.where(row_d < HEAD_DIM, row[:, :blk], row[:, blk:])

    def reduce_unit(n_first, count, p, track_max):
        cs = slice(p * qcols, (p + 1) * qcols)
        src = sets[p % 2]
        blocks = [n_first + g for g in range(count)]
        crows = [colb_ref[pl.ds(n, 1), cs] - slope_row(p) * ((j - n) * blk).astype(F32) for n in blocks]
        m_prev = m_ref[p]
        if track_max:
            m_new = m_prev
            for g, crow in enumerate(crows):
                m_new = jnp.maximum(m_new, jnp.max(src[g], axis=0, keepdims=True) + crow)
            alpha = jnp.exp2(m_prev - m_new)
            m_ref[p] = m_new
        else:
            m_new = m_prev
        e = jnp.concatenate([jnp.exp2(src[g] - (m_new - crow)).astype(BF16) for g, crow in enumerate(crows)],
                            axis=0)
        pv, esum = weighted_values(blocks, p, e)
        if track_max:
            l_ref[p] = alpha * l_ref[p] + esum
            acc_ref[p] = by_head(alpha) * acc_ref[p] + pv
        else:
            l_ref[p] = l_ref[p] + esum
            acc_ref[p] = acc_ref[p] + pv

    def sweep(n_first, count, n_after, track_max):
        for p in range(N_SLABS):
            if p + 1 < N_SLABS:
                park_unit(n_first, count, p + 1)
            elif n_after is not None:
                park_unit(n_after, UPDATE_BLOCKS, 0)
            reduce_unit(n_first, count, p, track_max)

    def attend(track_max):
        for p in range(N_SLABS):
            so_ref[p] = scores(j, p)
        park_unit(0, UPDATE_BLOCKS, 0)
        for p in range(N_SLABS):
            sb = so_ref[p] + causal_ref[...]
            m = jnp.max(sb, axis=0, keepdims=True)
            pv, esum = weighted_values([j], p, jnp.exp2(sb - m).astype(BF16))
            m_ref[p] = m
            l_ref[p] = esum
            acc_ref[p] = pv

        def trip(t, carry):
            sweep(UPDATE_BLOCKS * t, UPDATE_BLOCKS, UPDATE_BLOCKS * (t + 1), track_max)
            return carry

        lax.fori_loop(0, j // UPDATE_BLOCKS, trip, 0)

        nr = (j // UPDATE_BLOCKS) * UPDATE_BLOCKS
        for r in range(1, UPDATE_BLOCKS):
            @pl.when(j - nr == r)
            def _(r=r):
                sweep(nr, r, None, track_max)

    attend(track_max=False)
    l_top = l_ref[0]
    a_top = jnp.abs(acc_ref[0])
    for p in range(1, N_SLABS):
        l_top = jnp.maximum(l_top, l_ref[p])
        a_top = jnp.maximum(a_top, jnp.abs(acc_ref[p]))
    finite = (jnp.max(l_top) < FINITE_MAX) & (jnp.max(a_top) < FINITE_MAX)

    @pl.when(jnp.logical_not(finite))
    def _():
        attend(track_max=True)

    for p in range(N_SLABS):
        o2 = acc_ref[p] / by_head(l_ref[p])
        sq = o2 * o2
        ms_a = jnp.sum(sq[:HEAD_DIM], axis=0, keepdims=True) * (1.0 / HEAD_DIM)
        ms_b = jnp.sum(sq[HEAD_DIM:], axis=0, keepdims=True) * (1.0 / HEAD_DIM)
        inv = jnp.where(row_d < HEAD_DIM, lax.rsqrt(ms_a + EPS), lax.rsqrt(ms_b + EPS))
        ls = slice(p * LANES, (p + 1) * LANES)
        o_ref[:, ls] = ((o2 * inv).T * gain_ref[:, ls]).astype(BF16)


def _attn_sample_half(pt_ref, q_ref, kn_ref, vn_ref, gain_ref, bd_ref, ckt_hbm, cvt_hbm, o_ref,
                      kbuf, vbuf, s_ref, acc_ref, stat_ref, ksem, vsem, *,
                      b, nbat, first, k_half, v_half, past_len, page, page_base):
    n_pages = past_len // page
    ppb = MOBA_BLOCK // page
    nb = past_len // MOBA_BLOCK
    t_new = q_ref.shape[1]
    rows = N_HEADS * t_new

    def page_copy(hbm, buf, sem, bb, pg):
        return pltpu.make_async_copy(hbm.at[page_base + pt_ref[bb, pg]], buf.at[pg], sem.at[pg])

    def start_all(hbm, buf, sem, bb):
        def body(i, c):
            for k in range(DMA_THREADS):
                page_copy(hbm, buf, sem, bb, i * DMA_THREADS + k).start(priority=k)
            return c
        lax.fori_loop(0, n_pages // DMA_THREADS, body, 0)

    def refill(hbm, buf, sem, pgs):
        @pl.when(b + 1 < nbat)
        def _():
            for k, pg in enumerate(pgs):
                page_copy(hbm, buf, sem, b + 1, pg).start(priority=k % DMA_THREADS)

    @pl.when(first)
    def _():
        start_all(ckt_hbm, kbuf, ksem, b)
        start_all(cvt_hbm, vbuf, vsem, b)

    shared = dict(b=b, page_copy=page_copy, refill=refill, t_new=t_new, rows=rows, n_pages=n_pages, nb=nb,
                  ppb=ppb, page=page, past_len=past_len, s_ref=s_ref, stat_ref=stat_ref)
    pl.when(k_half)(functools.partial(_sample_k_half, q_ref, kn_ref, kbuf, ksem, ckt_hbm, **shared))
    pl.when(v_half)(functools.partial(_sample_v_half, vn_ref, gain_ref, bd_ref, o_ref, vbuf, vsem, cvt_hbm, acc_ref,
                                      **shared))


def _sample_k_half(q_ref, kn_ref, kbuf, ksem, ckt_hbm, *, b, page_copy, refill, t_new, rows, n_pages, nb, ppb, page,
                   past_len, s_ref, stat_ref):
    qt = jnp.concatenate([q_ref[0]] * N_HEADS, axis=0)
    rh = lax.broadcasted_iota(jnp.int32, qt.shape, 0) // t_new
    ch = lax.broadcasted_iota(jnp.int32, qt.shape, 1) // HEAD_DIM
    qs = jnp.where(rh == ch, qt, 0.0) * SCALE
    q_hi = qs.astype(BF16)
    q_lo = (qs - q_hi.astype(F32)).astype(BF16)
    qq = jnp.concatenate([q_hi, q_lo], axis=0)

    def k_tile(i, c):
        pgs = [i * K_TILE + k for k in range(K_TILE)]
        for pg in pgs:
            page_copy(ckt_hbm, kbuf, ksem, b, pg).wait()
        for pg in pgs:
            s2 = _dot(qq, kbuf[pg].reshape(D_ATT, page).astype(BF16))
            s_ref[pg] = s2[:rows] + s2[rows:]
        refill(ckt_hbm, kbuf, ksem, pgs)
        return c
    lax.fori_loop(0, n_pages // K_TILE, k_tile, 0)

    lane = lax.broadcasted_iota(jnp.int32, (rows, LANES), 1)
    gate = jnp.full((rows, LANES), REMOVED, F32)
    for n in range(nb):
        tot = s_ref[n * ppb]
        for i in range(1, ppb):
            tot = tot + s_ref[n * ppb + i]
        gate = jnp.where(lane == n, jnp.sum(tot, axis=1, keepdims=True), gate)
    lane_f = lane.astype(F32)
    picked = jnp.zeros(gate.shape, F32)
    for _ in range(MOBA_TOPK):
        top = jnp.max(gate, axis=1, keepdims=True)
        first_lane = jnp.min(jnp.where(gate == top, lane_f, float(LANES)), axis=1, keepdims=True)
        pick = lane_f == first_lane
        picked = jnp.where(pick, 1.0, picked)
        gate = jnp.where(pick, REMOVED, gate)
    colb = jnp.where(picked > 0.0, 0.0, NEG_INF)

    r1 = lax.broadcasted_iota(jnp.int32, (rows, 1), 0)
    tq = r1 % t_new
    slope = jnp.zeros((rows, 1), F32)
    for h in range(N_HEADS):
        slope = jnp.where(r1 // t_new == h, _slope(h), slope)
    in_page = slope * (tq - lane).astype(F32)

    zpad = jnp.zeros((LANES - t_new, D_ATT), F32)
    kn = jnp.concatenate([kn_ref[0], zpad], axis=0).astype(BF16)
    s2 = _dot_nt(qq, kn)
    s_own = jnp.where(lane <= tq, s2[:rows] + s2[rows:] - in_page, NEG_INF)

    mrun = s_own
    for n in range(nb):
        mask_n = jnp.sum(jnp.where(lane == n, colb, 0.0), axis=1, keepdims=True)
        for i in range(ppb):
            pg = n * ppb + i
            sn = s_ref[pg] - in_page + (mask_n - slope * float(past_len - pg * page))
            s_ref[pg] = sn
            mrun = jnp.maximum(mrun, sn)
    m = jnp.max(mrun, axis=1, keepdims=True)

    e_own = jnp.exp(s_own - m)
    lrun = e_own
    for pg in range(n_pages):
        e = jnp.exp(s_ref[pg] - m)
        s_ref[pg] = e
        lrun = lrun + e
    stat_ref[0] = jnp.broadcast_to(jnp.sum(lrun, axis=1, keepdims=True), (rows, LANES))
    stat_ref[1] = e_own


def _sample_v_half(vn_ref, gain_ref, bd_ref, o_ref, vbuf, vsem, cvt_hbm, acc_ref, *, b, page_copy, refill, t_new, rows,
                   n_pages, nb, ppb, page, past_len, s_ref, stat_ref):
    l = stat_ref[0][:, :1]
    e_own = stat_ref[1]
    zpad = jnp.zeros((LANES - t_new, D_ATT), F32)
    vn = jnp.concatenate([vn_ref[0], zpad], axis=0).astype(BF16)

    acc_ref[...] = jnp.zeros(acc_ref.shape, F32)
    zrows = jnp.zeros((LANES - rows, V_TILE * page), BF16)

    def v_tile(i, c):
        pgs = [i * V_TILE + k for k in range(V_TILE)]
        for pg in pgs:
            page_copy(cvt_hbm, vbuf, vsem, b, pg).wait()
        vt = jnp.concatenate([vbuf[pg].reshape(D_ATT, page) for pg in pgs], axis=1).astype(BF16)
        p = jnp.concatenate([s_ref[pg] for pg in pgs], axis=1).astype(BF16)
        acc_ref[...] += _dot_nt(vt, jnp.concatenate([p, zrows], axis=0))
        refill(cvt_hbm, vbuf, vsem, pgs)
        return c
    lax.fori_loop(0, n_pages // V_TILE, v_tile, 0)

    acc = acc_ref[...].T[:rows] + _dot(e_own.astype(BF16), vn)
    accn = acc / l
    ch8 = lax.broadcasted_iota(jnp.int32, (t_new, D_ATT), 1) // HEAD_DIM
    out = jnp.zeros((t_new, D_ATT), F32)
    for h in range(N_HEADS):
        out = jnp.where(ch8 == h, accn[h * t_new:(h + 1) * t_new, :], out)
    o_ref[0] = _group_rms(out, gain_ref[...], bd_ref).astype(BF16)


N_PROMPT_SCRATCH = 10


def _attn_both_body(pt_ref, q_ref, kb_ref, vt_ref, mean_ref, gain_ref, qs_ref, kn_ref, vn_ref, bd_ref, ckt_hbm, cvt_hbm,
                    o_ref, os_ref, *scratch, nseq, past_len, page, page_base):
    _attn_prompt_body(q_ref, kb_ref, vt_ref, mean_ref, gain_ref, o_ref, *scratch[:N_PROMPT_SCRATCH])
    step = pl.program_id(0) * pl.num_programs(1) + pl.program_id(1)
    seq = step // 2
    live = seq < nseq
    _attn_sample_half(pt_ref, qs_ref, kn_ref, vn_ref, gain_ref, bd_ref, ckt_hbm, cvt_hbm, os_ref,
                      *scratch[N_PROMPT_SCRATCH:], b=jnp.minimum(seq, nseq - 1), nbat=nseq, first=step == 0,
                      k_half=live & (step % 2 == 0), v_half=live & (step % 2 == 1),
                      past_len=past_len, page=page, page_base=page_base)


def _attn_both(page_table, q, kb, vt, means, gain, qs, kn, vn, bd, cache_kt, cache_vt, *, batch, page_base, past_len):
    n = q.shape[0]
    seq = n // batch
    nb = seq // MOBA_BLOCK
    group = LANES // N_HEADS
    assert nb <= group and seq % MOBA_BLOCK == 0
    means = means.reshape(batch, nb, D_ATT)
    qcols = HEADS_PER_SLAB * MOBA_BLOCK
    nseq, t_new, _ = qs.shape
    page = cache_kt.shape[3]
    n_pages = past_len // page
    rows = N_HEADS * t_new
    assert past_len % MOBA_BLOCK == 0 and MOBA_BLOCK % page == 0 and page == LANES
    assert rows <= LANES and t_new % 8 == 0 and past_len // MOBA_BLOCK <= LANES
    assert n_pages % V_TILE == 0 and n_pages % K_TILE == 0
    assert 2 * nseq <= batch * nb
    body = functools.partial(_attn_both_body, nseq=nseq, past_len=past_len, page=page, page_base=page_base)
    blk_row = pl.BlockSpec((MOBA_BLOCK, D_ATT), lambda b, j, pt: (b * nb + j, 0))
    per_seq = pl.BlockSpec((1, t_new, D_ATT), lambda b, j, pt: (jnp.minimum((b * nb + j) // 2, nseq - 1), 0, 0))
    once = lambda shape: pl.BlockSpec(shape, lambda b, j, pt: (0,) * len(shape))
    grid_spec = pltpu.PrefetchScalarGridSpec(
        num_scalar_prefetch=1,
        grid=(batch, nb),
        in_specs=[blk_row,
                  pl.BlockSpec((seq, D_ATT), lambda b, j, pt: (b, 0)),
                  pl.BlockSpec((nb, D_ATT, MOBA_BLOCK), lambda b, j, pt: (b, 0, 0)),
                  pl.BlockSpec((1, nb, D_ATT), lambda b, j, pt: (b, 0, 0)),
                  once(gain.shape),
                  per_seq, per_seq, per_seq, once(bd.shape),
                  pl.BlockSpec(memory_space=pl.ANY),
                  pl.BlockSpec(memory_space=pl.ANY)],
        out_specs=[blk_row, per_seq],
        scratch_shapes=[pltpu.VMEM((MOBA_BLOCK, qcols), F32),
                        pltpu.VMEM((MOBA_BLOCK, LANES), BF16),
                        pltpu.VMEM((N_SLABS, 2 * LANES, qcols), BF16),
                        pltpu.VMEM((group, N_HEADS * MOBA_BLOCK), F32),
                        pltpu.VMEM((N_SLABS, MOBA_BLOCK, qcols), F32),
                        pltpu.VMEM((UPDATE_BLOCKS, MOBA_BLOCK, qcols), F32),
                        pltpu.VMEM((UPDATE_BLOCKS, MOBA_BLOCK, qcols), F32),
                        pltpu.VMEM((N_SLABS, 1, qcols), F32),
                        pltpu.VMEM((N_SLABS, 1, qcols), F32),
                        pltpu.VMEM((N_SLABS, LANES, MOBA_BLOCK), F32),
                        pltpu.VMEM((n_pages, N_HEADS, HEAD_DIM, page), F32),
                        pltpu.VMEM((n_pages, N_HEADS, HEAD_DIM, page), F32),
                        pltpu.VMEM((n_pages, rows, page), F32),
                        pltpu.VMEM((D_ATT, LANES), F32),
                        pltpu.VMEM((2, rows, LANES), F32),
                        pltpu.SemaphoreType.DMA((n_pages,)),
                        pltpu.SemaphoreType.DMA((n_pages,))],
    )
    return pl.pallas_call(
        body,
        grid_spec=grid_spec,
        out_shape=[jax.ShapeDtypeStruct((n, D_ATT), BF16), jax.ShapeDtypeStruct((nseq, t_new, D_ATT), BF16)],
        compiler_params=pltpu.CompilerParams(dimension_semantics=("arbitrary", "arbitrary"),
                                             vmem_limit_bytes=ATTN_VMEM_LIMIT),
        name="attn_both",
    )(page_table, q, kb, vt, means, gain, qs, kn, vn, bd, cache_kt, cache_vt)


def kernel(x_prompt, x_sample, cache_k, cache_v, state_conv, page_table, ffn1_norm, ffn1_w_gu, ffn1_w_down,
           mix_norm, w_in, conv_w, conv_out_norm, attn_out_norm, w_out, ffn2_norm, ffn2_w_gu, ffn2_w_down,
           final_norm):
    bp, seq, d = x_prompt.shape
    bs, dseq, _ = x_sample.shape
    depth, n_pool, page = cache_k.shape[:3]
    dc = conv_w.shape[2]
    past_len = page_table.shape[1] * page

    ck = jnp.transpose(cache_k, (0, 1, 3, 4, 2)).reshape(depth * n_pool, N_HEADS, HEAD_DIM, page)
    cv = jnp.transpose(cache_v, (0, 1, 3, 4, 2)).reshape(depth * n_pool, N_HEADS, HEAD_DIM, page)
    gi = lax.broadcasted_iota(jnp.int32, (D_ATT, D_ATT), 0) // HEAD_DIM
    gj = lax.broadcasted_iota(jnp.int32, (D_ATT, D_ATT), 1) // HEAD_DIM
    bd = (gi == gj).astype(BF16)

    xp = x_prompt.reshape(bp * seq, d)
    xs = x_sample.reshape(bs * dseq, d)
    tm_p = PROMPT_ROW_TILE
    assert seq % tm_p == 0 and seq % INPROJ_ROW_TILE == 0 and INPROJ_ROW_TILE % MOBA_BLOCK == 0
    assert dc // N_CONV_GROUPS == HEAD_DIM and dc == D_ATT
    row = lambda a: a.reshape(1, -1)
    outs = [[] for _ in range(6)]
    for l in range(depth):
        g1, gm, g2 = row(ffn1_norm[l]), row(mix_norm[l]), row(ffn2_norm[l])
        gc, ga = row(conv_out_norm[l]), row(attn_out_norm[l])
        last = l == depth - 1
        gfin = row(final_norm) if last else None

        st = state_conv[l]
        zpad = jnp.zeros((bs, dseq - (CONV_W - 1), dc), F32)
        s2 = jnp.concatenate([st, zpad], axis=1).reshape(bs * dseq, dc)
        s1 = jnp.concatenate([st[:, 1:2], jnp.zeros((bs, dseq - 1, dc), F32)], axis=1).reshape(bs * dseq, dc)
        x1s, wg1, wu1, wd1, qs, ks, vs, ycs, us, win = _ffn_stream_call(
            xs, g1, ffn1_w_gu[l], ffn1_w_down[l], proj=(gm, w_in[l], conv_w[l], gc, bd, s1, s2, dseq),
            name="ffn1_inproj_sample")
        outs[3].append(ks.reshape(bs, dseq, N_HEADS, HEAD_DIM))
        outs[4].append(vs.reshape(bs, dseq, N_HEADS, HEAD_DIM))
        outs[5].append(us[:, dseq - (CONV_W - 1):, :])

        x1 = _ffn_call(xp, g1, wg1, wu1, wd1, tm=tm_p, name="ffn1_prompt")
        q, kt, vtf, kb, vt, yc, means, cnew = _inproj_prompt(x1, gm, win, conv_w[l], gc, bd, batch=bp,
                                                             tm=INPROJ_ROW_TILE)
        r3 = lambda a: a.reshape(bs, dseq, D_ATT)
        ya, yas = _attn_both(page_table, q, kb, vt, means, ga, r3(qs), r3(ks), r3(vs), bd, ck, cv,
                             batch=bp, page_base=l * n_pool, past_len=past_len)
        xs, wg2, wu2, wd2, woc, woa = _ffn_stream_call(
            x1s, g2, ffn2_w_gu[l], ffn2_w_down[l], mix=(ycs, yas.reshape(bs * dseq, D_ATT), w_out[l]),
            final=gfin, name="ffn2_sample")
        xp = _ffn_call(x1, g2, wg2, wu2, wd2, tm=tm_p, mix=(yc, ya, woc, woa), final=gfin, name="ffn2_prompt")
        tok_major = lambda a: a.reshape(bp, N_HEADS, HEAD_DIM, seq).transpose(0, 3, 1, 2)
        outs[0].append(tok_major(kt))
        outs[1].append(tok_major(vtf))
        outs[2].append(cnew)

    y_prompt = xp.reshape(bp, seq, d)
    y_sample = xs.reshape(bs, dseq, d)
    kp, vp, cp, ksn, vsn, csn = (jnp.stack(o) for o in outs)
    return (y_prompt, y_sample, kp, vp, cp, ksn, vsn, csn)
```
